```python
import math
import jax, jax.numpy as jnp
from jax import lax
import numpy as np

D_MODEL = 4096
BATCH = 4
SEQ = 4096
DEPTH = 1

D_MIX = D_MODEL
D_ATTN = D_MIX // 2
D_SSM = D_MIX - D_ATTN
HEAD_DIM = 128
N_HEADS = D_ATTN // HEAD_DIM
N_KV_HEADS = 4
GQA_GROUP = N_HEADS // N_KV_HEADS
ROT_DIM = HEAD_DIM // 4
ROPE_THETA = 500000.0
CMP_STRIDE = 16
CMP_BLOCK = 2 * CMP_STRIDE
CMP_HIDDEN = 512
SLC_BLOCK = 64
N_SELECT = 16
WINDOW = 512
Q_BLOCK = 128
N_BRANCHES = 3
FORCE_BONUS = 1.0e4
NEG_INF = -1.0e30
SSM_GROUP = 16
N_SSM_GROUPS = D_SSM // SSM_GROUP
SSM_STATE = 64
DT_MIN = 1.0e-3
DT_MAX = 1.0e-1
N_EXPERT_GROUPS = 8
EXPERTS_PER_GROUP = 8
N_EXPERTS = N_EXPERT_GROUPS * EXPERTS_PER_GROUP
TOP_K = 2
D_EXPERT = 1024
MOE_BLOCK = 128
EPS = 1.0e-6

Q_WIDTH = N_HEADS * HEAD_DIM
KV_WIDTH = N_KV_HEADS * HEAD_DIM
GATE_WIDTH = N_HEADS * N_BRANCHES
D_IN = Q_WIDTH + 6 * KV_WIDTH + GATE_WIDTH + D_SSM

kernel_name = 'hymba_nsa_s5_hier_moe_block'


def rmsnorm(x, g):
    xf = x.astype(jnp.float32)
    return xf * lax.rsqrt(jnp.mean(xf * xf, axis=-1, keepdims=True) + EPS) * g.astype(jnp.float32)


def partial_rope(t, pos):
    half = ROT_DIM // 2
    inv_freq = ROPE_THETA ** (-jnp.arange(half, dtype=jnp.float32) / half)
    ang = pos.astype(jnp.float32)[:, None] * inv_freq[None, :]
    cos = jnp.cos(ang)[None, :, None, :]
    sin = jnp.sin(ang)[None, :, None, :]
    t = t.astype(jnp.float32)
    t1, t2, rest = t[..., :half], t[..., half:ROT_DIM], t[..., ROT_DIM:]
    return jnp.concatenate([t1 * cos - t2 * sin, t2 * cos + t1 * sin, rest], axis=-1)


def compress_blocks(t, pe, w1, b1, w2, b2):
    bsz, seq, hk, d = t.shape
    chunks = t.reshape(bsz, seq // CMP_STRIDE, CMP_STRIDE, hk, d)
    blocks = jnp.concatenate([chunks[:, :-1], chunks[:, 1:]], axis=2) + pe[None, None, :, None, :]
    flat = blocks.transpose(0, 1, 3, 2, 4).reshape(bsz, seq // CMP_STRIDE - 1, hk, CMP_BLOCK * d)
    return jax.nn.gelu(flat @ w1 + b1) @ w2 + b2


def masked_softmax(s, mask):
    return jax.nn.softmax(jnp.where(mask, s, NEG_INF), axis=-1)


def nsa_attention(q, kv, gate_logits, pe_k, w1_k, b1_k, w2_k, b2_k, pe_v, w1_v, b1_v, w2_v, b2_v):
    bsz, seq = q.shape[0], q.shape[1]
    n_cmp = seq // CMP_STRIDE - 1
    n_slc = seq // SLC_BLOCK
    n_sel = min(N_SELECT, n_slc)
    scale = HEAD_DIM ** -0.5
    pos = jnp.arange(seq)
    q = q.reshape(bsz, seq, N_HEADS, HEAD_DIM)
    k_cmp, v_cmp, k_slc, v_slc, k_win, v_win = [kv[:, :, i] for i in range(6)]

    def to_q_heads(t):
        return t.reshape(bsz, seq, N_KV_HEADS, GQA_GROUP, HEAD_DIM).transpose(0, 2, 3, 1, 4).astype(jnp.float32)

    def to_kv_heads(t):
        return t.transpose(0, 2, 1, 3).astype(jnp.float32)

    q_plain = to_q_heads(q)
    q_rot = to_q_heads(partial_rope(q, pos))
    kc = to_kv_heads(compress_blocks(k_cmp, pe_k, w1_k, b1_k, w2_k, b2_k))
    vc = to_kv_heads(compress_blocks(v_cmp, pe_v, w1_v, b1_v, w2_v, b2_v))
    ks = to_kv_heads(partial_rope(k_slc, pos)).reshape(bsz, N_KV_HEADS, n_slc, SLC_BLOCK, HEAD_DIM)
    vs = to_kv_heads(v_slc).reshape(bsz, N_KV_HEADS, n_slc, SLC_BLOCK, HEAD_DIM)
    pad = ((0, 0), (0, 0), (WINDOW, 0), (0, 0))
    kw = jnp.pad(to_kv_heads(partial_rope(k_win, pos)), pad)
    vw = jnp.pad(to_kv_heads(v_win), pad)
    gates = jax.nn.sigmoid(gate_logits.astype(jnp.float32)).reshape(
        bsz, seq, N_KV_HEADS, GQA_GROUP, N_BRANCHES).transpose(0, 2, 3, 1, 4)

    cmp_start = jnp.arange(n_cmp) * CMP_STRIDE
    cmp_last = cmp_start + CMP_BLOCK - 1
    slc_start = jnp.arange(n_slc) * SLC_BLOCK
    overlap = jnp.clip(jnp.minimum(cmp_start[:, None] + CMP_BLOCK, slc_start[None, :] + SLC_BLOCK)
                       - jnp.maximum(cmp_start[:, None], slc_start[None, :]), 0, None)
    cmp_to_slc = overlap.astype(jnp.float32) / CMP_BLOCK
    slc_ids = jnp.arange(n_slc)
    b_ix = jnp.arange(bsz)[:, None, None, None]
    h_ix = jnp.arange(N_KV_HEADS)[None, :, None, None]
    tok_in_blk = jnp.arange(SLC_BLOCK)
    win_off = jnp.arange(WINDOW + Q_BLOCK)

    def query_block(qb):
        s0 = qb * Q_BLOCK
        t = s0 + jnp.arange(Q_BLOCK)
        qp = lax.dynamic_slice_in_dim(q_plain, s0, Q_BLOCK, axis=3)
        qr = lax.dynamic_slice_in_dim(q_rot, s0, Q_BLOCK, axis=3)
        g = lax.dynamic_slice_in_dim(gates, s0, Q_BLOCK, axis=3)
        cmask = cmp_last[None, :] <= t[:, None]
        p_c = masked_softmax(jnp.einsum('bhgtd,bhnd->bhgtn', qp, kc) * scale, cmask)
        p_c = p_c * jnp.any(cmask, axis=-1)[:, None]
        o_c = jnp.einsum('bhgtn,bhnd->bhgtd', p_c, vc)
        imp = jnp.einsum('bhgtn,ns->bhts', p_c, cmp_to_slc)
        cur = t // SLC_BLOCK
        causal_blk = slc_ids[None, :] <= cur[:, None]
        forced = (slc_ids[None, :] == 0) | (slc_ids[None, :] == cur[:, None]) | (slc_ids[None, :] == cur[:, None] - 1)
        sel_score = jnp.where(causal_blk, imp + FORCE_BONUS * forced, NEG_INF)
        top_score, top_idx = lax.top_k(sel_score, n_sel)
        k_g = ks[b_ix, h_ix, top_idx]
        v_g = vs[b_ix, h_ix, top_idx]
        kpos = top_idx[..., None] * SLC_BLOCK + tok_in_blk
        smask = (top_score > 0.5 * NEG_INF)[..., None] & (kpos <= t[None, None, :, None, None])
        s_s = jnp.einsum('bhgtd,bhtkjd->bhgtkj', qr, k_g) * scale
        p_s = masked_softmax(s_s.reshape(bsz, N_KV_HEADS, GQA_GROUP, Q_BLOCK, n_sel * SLC_BLOCK),
                             smask.reshape(bsz, N_KV_HEADS, 1, Q_BLOCK, n_sel * SLC_BLOCK))
        o_s = jnp.einsum('bhgtkj,bhtkjd->bhgtd', p_s.reshape(s_s.shape), v_g)
        kwb = lax.dynamic_slice_in_dim(kw, s0, WINDOW + Q_BLOCK, axis=2)
        vwb = lax.dynamic_slice_in_dim(vw, s0, WINDOW + Q_BLOCK, axis=2)
        kpos_w = s0 - WINDOW + win_off
        dist = t[:, None] - kpos_w[None, :]
        wmask = (dist >= 0) & (dist < WINDOW) & (kpos_w[None, :] >= 0)
        p_w = masked_softmax(jnp.einsum('bhgtd,bhkd->bhgtk', qr, kwb) * scale, wmask)
        o_w = jnp.einsum('bhgtk,bhkd->bhgtd', p_w, vwb)
        return g[..., 0:1] * o_c + g[..., 1:2] * o_s + g[..., 2:3] * o_w

    out = lax.map(query_block, jnp.arange(seq // Q_BLOCK))
    return out.transpose(1, 0, 4, 2, 3, 5).reshape(bsz, seq, D_ATTN)


def _ssm_combine(e1, e2):
    a1, b1 = e1
    a2, b2 = e2
    return a1 * a2, a2 * b1 + b2


def s5_mixer(u, a_re, a_im, log_dt, b_re, b_im, c_re, c_im, d_skip, w_glu, b_glu):
    bsz, seq, _ = u.shape
    uf = u.astype(jnp.float32).reshape(bsz, seq, N_SSM_GROUPS, SSM_GROUP)
    lam = lax.complex(a_re.astype(jnp.float32), a_im.astype(jnp.float32))
    dt = jnp.exp(log_dt.astype(jnp.float32))[:, None]
    a_bar = jnp.exp(lam * dt)
    b_mat = lax.complex(b_re.astype(jnp.float32), b_im.astype(jnp.float32))
    b_bar = ((a_bar - 1.0) / lam)[:, :, None] * b_mat
    c_mat = lax.complex(c_re.astype(jnp.float32), c_im.astype(jnp.float32))
    bu = jnp.einsum('gph,blgh->blgp', b_bar, uf.astype(jnp.complex64))
    a_seq = jnp.broadcast_to(a_bar, bu.shape)
    _, state = lax.associative_scan(_ssm_combine, (a_seq, bu), axis=1)
    y = jnp.einsum('ghp,blgp->blgh', c_mat, state).real \
        + d_skip.astype(jnp.float32).reshape(N_SSM_GROUPS, SSM_GROUP) * uf
    y = jax.nn.gelu(y.reshape(bsz, seq, D_SSM))
    return y * jax.nn.sigmoid(y @ w_glu + b_glu)


def hier_moe(hn, w_coarse, b_coarse, w_fine, b_fine, w_gate, w_up, w_down):
    bsz, seq, d = hn.shape
    n_tok = bsz * seq
    n_assign = n_tok * TOP_K
    n_blocks = -(-(n_assign + N_EXPERTS * (MOE_BLOCK - 1)) // MOE_BLOCK)
    xf = hn.reshape(n_tok, d)
    tok = jnp.arange(n_tok)
    p_grp = jax.nn.softmax((xf @ w_coarse + b_coarse).astype(jnp.float32), axis=-1)
    grp = jnp.argmax(p_grp, axis=-1)
    w_grp = p_grp[tok, grp]
    fine = (xf @ w_fine + b_fine).astype(jnp.float32).reshape(n_tok, N_EXPERT_GROUPS, EXPERTS_PER_GROUP)[tok, grp]
    p_fine = jax.nn.softmax(fine, axis=-1)
    top_p, top_e = lax.top_k(p_fine, TOP_K)
    gate = w_grp[:, None] * top_p / jnp.sum(top_p, axis=-1, keepdims=True)
    expert = (grp[:, None] * EXPERTS_PER_GROUP + top_e).reshape(n_assign)
    order = jnp.argsort(expert)
    e_s = expert[order]
    tok_s = (order // TOP_K).astype(jnp.int32)
    gate_s = gate.reshape(n_assign)[order]
    counts = jnp.zeros((N_EXPERTS,), jnp.int32).at[expert].add(1)
    start = jnp.cumsum(counts) - counts
    padded = (counts + MOE_BLOCK - 1) // MOE_BLOCK * MOE_BLOCK
    pad_end = jnp.cumsum(padded)
    dest = pad_end[e_s] - padded[e_s] + jnp.arange(n_assign) - start[e_s]
    buf_tok = jnp.zeros((n_blocks * MOE_BLOCK,), jnp.int32).at[dest].set(tok_s)
    blk_expert = jnp.minimum(jnp.searchsorted(pad_end, jnp.arange(n_blocks) * MOE_BLOCK, side='right'),
                             N_EXPERTS - 1)

    def expert_block(args):
        toks, e = args
        xb = xf[toks]
        return (jax.nn.silu(xb @ w_gate[e]) * (xb @ w_up[e])) @ w_down[e]

    y_buf = lax.map(expert_block, (buf_tok.reshape(n_blocks, MOE_BLOCK), blk_expert))
    y_buf = y_buf.reshape(n_blocks * MOE_BLOCK, d)
    y = jnp.zeros((n_tok, d), jnp.float32).at[tok_s].add(y_buf[dest].astype(jnp.float32) * gate_s[:, None])
    return y.reshape(bsz, seq, d)


def setup_inputs(seed: int = 0) -> dict:
    key = jax.random.key(seed)
    keys = iter(jax.random.split(key, 48))

    def nrm(shape, scale):
        return jax.random.normal(next(keys), shape, jnp.float32) * scale

    def gain(shape):
        return 1.0 + nrm(shape, 0.05)

    n_idx = jnp.arange(SSM_STATE, dtype=jnp.float32)
    return {
        'x': nrm((BATCH, SEQ, D_MODEL), 1.0),
        'norm_mix': gain((DEPTH, D_MODEL)),
        'w_in': nrm((DEPTH, D_MODEL, D_IN), D_MODEL ** -0.5),
        'cmp_pe_k': nrm((DEPTH, CMP_BLOCK, HEAD_DIM), 0.1),
        'cmp_w1_k': nrm((DEPTH, CMP_BLOCK * HEAD_DIM, CMP_HIDDEN), (CMP_BLOCK * HEAD_DIM) ** -0.5),
        'cmp_b1_k': nrm((DEPTH, CMP_HIDDEN), 0.02),
        'cmp_w2_k': nrm((DEPTH, CMP_HIDDEN, HEAD_DIM), CMP_HIDDEN ** -0.5),
        'cmp_b2_k': nrm((DEPTH, HEAD_DIM), 0.02),
        'cmp_pe_v': nrm((DEPTH, CMP_BLOCK, HEAD_DIM), 0.1),
        'cmp_w1_v': nrm((DEPTH, CMP_BLOCK * HEAD_DIM, CMP_HIDDEN), (CMP_BLOCK * HEAD_DIM) ** -0.5),
        'cmp_b1_v': nrm((DEPTH, CMP_HIDDEN), 0.02),
        'cmp_w2_v': nrm((DEPTH, CMP_HIDDEN, HEAD_DIM), CMP_HIDDEN ** -0.5),
        'cmp_b2_v': nrm((DEPTH, HEAD_DIM), 0.02),
        'ssm_a_re': -0.5 + nrm((DEPTH, N_SSM_GROUPS, SSM_STATE), 0.01),
        'ssm_a_im': jnp.broadcast_to(math.pi * n_idx, (DEPTH, N_SSM_GROUPS, SSM_STATE)),
        'ssm_log_dt': jax.random.uniform(next(keys), (DEPTH, N_SSM_GROUPS), jnp.float32,
                                         minval=math.log(DT_MIN), maxval=math.log(DT_MAX)),
        'ssm_b_re': nrm((DEPTH, N_SSM_GROUPS, SSM_STATE, SSM_GROUP), (2 * SSM_GROUP) ** -0.5),
        'ssm_b_im': nrm((DEPTH, N_SSM_GROUPS, SSM_STATE, SSM_GROUP), (2 * SSM_GROUP) ** -0.5),
        'ssm_c_re': nrm((DEPTH, N_SSM_GROUPS, SSM_GROUP, SSM_STATE), (2 * SSM_STATE) ** -0.5),
        'ssm_c_im': nrm((DEPTH, N_SSM_GROUPS, SSM_GROUP, SSM_STATE), (2 * SSM_STATE) ** -0.5),
        'ssm_d': nrm((DEPTH, D_SSM), 1.0),
        'ssm_w_glu': nrm((DEPTH, D_SSM, D_SSM), D_SSM ** -0.5),
        'ssm_b_glu': nrm((DEPTH, D_SSM), 0.02),
        'norm_attn_out': gain((DEPTH, D_ATTN)),
        'norm_ssm_out': gain((DEPTH, D_SSM)),
        'w_out': nrm((DEPTH, D_MIX, D_MODEL), D_MIX ** -0.5),
        'norm_ffn': gain((DEPTH, D_MODEL)),
        'router_w_coarse': nrm((DEPTH, D_MODEL, N_EXPERT_GROUPS), D_MODEL ** -0.5),
        'router_b_coarse': nrm((DEPTH, N_EXPERT_GROUPS), 0.01),
        'router_w_fine': nrm((DEPTH, D_MODEL, N_EXPERTS), D_MODEL ** -0.5),
        'router_b_fine': nrm((DEPTH, N_EXPERTS), 0.01),
        'w_gate': nrm((DEPTH, N_EXPERTS, D_MODEL, D_EXPERT), D_MODEL ** -0.5),
        'w_up': nrm((DEPTH, N_EXPERTS, D_MODEL, D_EXPERT), D_MODEL ** -0.5),
        'w_down': nrm((DEPTH, N_EXPERTS, D_EXPERT, D_MODEL), D_EXPERT ** -0.5),
        'norm_final': gain((D_MODEL,)),
    }


def reference(x, norm_mix, w_in, cmp_pe_k, cmp_w1_k, cmp_b1_k, cmp_w2_k, cmp_b2_k,
              cmp_pe_v, cmp_w1_v, cmp_b1_v, cmp_w2_v, cmp_b2_v,
              ssm_a_re, ssm_a_im, ssm_log_dt, ssm_b_re, ssm_b_im, ssm_c_re, ssm_c_im,
              ssm_d, ssm_w_glu, ssm_b_glu, norm_attn_out, norm_ssm_out, w_out,
              norm_ffn, router_w_coarse, router_b_coarse, router_w_fine, router_b_fine,
              w_gate, w_up, w_down, norm_final):
    bsz, seq, _ = x.shape
    h = x.astype(jnp.float32)
    cut1 = Q_WIDTH
    cut2 = cut1 + 6 * KV_WIDTH
    cut3 = cut2 + GATE_WIDTH
    for layer in range(DEPTH):
        hn = rmsnorm(h, norm_mix[layer])
        proj = hn @ w_in[layer]
        q = proj[..., :cut1]
        kv = proj[..., cut1:cut2].reshape(bsz, seq, 6, N_KV_HEADS, HEAD_DIM)
        gate_logits = proj[..., cut2:cut3]
        u = proj[..., cut3:]
        attn = nsa_attention(q, kv, gate_logits,
                             cmp_pe_k[layer], cmp_w1_k[layer], cmp_b1_k[layer], cmp_w2_k[layer], cmp_b2_k[layer],
                             cmp_pe_v[layer], cmp_w1_v[layer], cmp_b1_v[layer], cmp_w2_v[layer], cmp_b2_v[layer])
        ssm = s5_mixer(u, ssm_a_re[layer], ssm_a_im[layer], ssm_log_dt[layer], ssm_b_re[layer], ssm_b_im[layer],
                       ssm_c_re[layer], ssm_c_im[layer], ssm_d[layer], ssm_w_glu[layer], ssm_b_glu[layer])
        mixed = jnp.concatenate([rmsnorm(attn, norm_attn_out[layer]), rmsnorm(ssm, norm_ssm_out[layer])], axis=-1)
        h = h + mixed @ w_out[layer]
        h = h + hier_moe(rmsnorm(h, norm_ffn[layer]), router_w_coarse[layer], router_b_coarse[layer],
                         router_w_fine[layer], router_b_fine[layer], w_gate[layer], w_up[layer], w_down[layer])
    return rmsnorm(h, norm_final).astype(x.dtype)
```

```python
import functools
import math

import jax
import jax.numpy as jnp
from jax import lax
from jax.experimental import pallas as pl
from jax.experimental.pallas import tpu as pltpu

F32 = jnp.float32
BF16 = jnp.bfloat16

HEAD_DIM = 128
N_HEADS = 16
N_KV_HEADS = 4
GQA_GROUP = N_HEADS // N_KV_HEADS
ROT_DIM = HEAD_DIM // 4
ROPE_THETA = 500000.0
CMP_STRIDE = 16
CMP_BLOCK = 32
SLC_BLOCK = 64
N_SELECT = 16
WINDOW = 512
Q_TILE = 128
N_BRANCHES = 3
FORCE_BONUS = 1.0e4
NEG_INF = -1.0e30
SSM_GROUP = 16
SSM_STATE = 64
SSM_CHUNK = 16
SSM_GROUPS_PER_BLOCK = 128 // SSM_GROUP
N_EXPERT_GROUPS = 8
EXPERTS_PER_GROUP = 8
N_EXPERTS = N_EXPERT_GROUPS * EXPERTS_PER_GROUP
TOP_K = 2
MOE_BLOCK = 128
EPS = 1.0e-6

V7X_VMEM_LIMIT = 56 * 1024 * 1024
LANES = 128


def _cparams(sem, vmem=V7X_VMEM_LIMIT):
    return pltpu.CompilerParams(dimension_semantics=sem, vmem_limit_bytes=vmem)


def _gelu_tanh(x):
    return 0.5 * x * (1.0 + jnp.tanh(math.sqrt(2.0 / math.pi) * (x + 0.044715 * (x * x * x))))


def _sigmoid(x):
    return 1.0 / (1.0 + jnp.exp(-x))


def _rmsnorm_kernel(x_ref, g_ref, o_ref):
    x = x_ref[...]
    ms = jnp.mean(x * x, axis=-1, keepdims=True)
    o_ref[...] = (x * lax.rsqrt(ms + EPS) * g_ref[...]).astype(o_ref.dtype)


def _rmsnorm(x, g, out_dtype, tm=256):
    m, d = x.shape
    return pl.pallas_call(
        _rmsnorm_kernel,
        out_shape=jax.ShapeDtypeStruct((m, d), out_dtype),
        grid=(m // tm,),
        in_specs=[pl.BlockSpec((tm, d), lambda i: (i, 0)),
                  pl.BlockSpec((1, d), lambda i: (0, 0))],
        out_specs=pl.BlockSpec((tm, d), lambda i: (i, 0)),
        compiler_params=_cparams(("parallel",)),
        name="rmsnorm",
    )(x, g.reshape(1, d))


def _mm_kernel(a_ref, w_ref, o_ref):
    o_ref[...] = jnp.dot(a_ref[...], w_ref[...], preferred_element_type=F32).astype(o_ref.dtype)


def _mm_glu_kernel(a_ref, w_ref, b_ref, y_ref, o_ref):
    z = jnp.dot(a_ref[...], w_ref[...], preferred_element_type=F32) + b_ref[...]
    o_ref[...] = y_ref[...] * _sigmoid(z)


def _mm_residual_kernel(a_ref, w_ref, r_ref, o_ref):
    o_ref[...] = r_ref[...] + jnp.dot(a_ref[...], w_ref[...], preferred_element_type=F32)


def _matmul(a, w, *, out_dtype=F32, bias=None, glu_y=None, residual=None, tm=512, tn=1024, name="matmul"):
    m, k = a.shape
    n = w.shape[1]
    tn = min(tn, n)
    assert m % tm == 0 and n % tn == 0
    a_spec = pl.BlockSpec((tm, k), lambda j, i: (i, 0))
    w_spec = pl.BlockSpec((k, tn), lambda j, i: (0, j))
    o_spec = pl.BlockSpec((tm, tn), lambda j, i: (i, j))
    if glu_y is not None:
        kern, ins = _mm_glu_kernel, (a, w, bias.reshape(1, n), glu_y)
        specs = [a_spec, w_spec, pl.BlockSpec((1, tn), lambda j, i: (0, j)), o_spec]
    elif residual is not None:
        kern, ins, specs = _mm_residual_kernel, (a, w, residual), [a_spec, w_spec, o_spec]
    else:
        kern, ins, specs = _mm_kernel, (a, w), [a_spec, w_spec]
    return pl.pallas_call(
        kern,
        out_shape=jax.ShapeDtypeStruct((m, n), out_dtype),
        grid=(n // tn, m // tm),
        in_specs=specs,
        out_specs=o_spec,
        compiler_params=_cparams(("parallel", "parallel")),
        name=name,
    )(*ins)


def _rope_tables(seq, reps):
    half = ROT_DIM // 2
    inv_freq = ROPE_THETA ** (-jnp.arange(half, dtype=F32) / half)
    ang = jnp.arange(seq).astype(F32)[:, None] * inv_freq[None, :]
    cos, sin = jnp.cos(ang), jnp.sin(ang)
    ones = jnp.ones((seq, HEAD_DIM - ROT_DIM), F32)
    cos_t = jnp.concatenate([cos, cos, ones], axis=-1)
    sin_t = jnp.concatenate([-sin, sin, 0.0 * ones], axis=-1)
    return jnp.tile(cos_t, (1, reps)), jnp.tile(sin_t, (1, reps))


def _rope(x, cos_t, sin_t):
    half = ROT_DIM // 2
    n = x.shape[-1]
    lane = lax.broadcasted_iota(jnp.int32, x.shape, x.ndim - 1) % HEAD_DIM
    partner = jnp.where(lane < half, pltpu.roll(x, n - half, x.ndim - 1), pltpu.roll(x, half, x.ndim - 1))
    return x * cos_t + partner * sin_t


def _kv_prep_kernel(slc_ref, win_ref, cos_ref, sin_ref, ks_ref, vs_ref, kw_ref, vw_ref):
    w = ks_ref.shape[-1]
    cos_t, sin_t = cos_ref[...], sin_ref[...]
    slc = slc_ref[...]
    win = win_ref[...]
    ks_ref[...] = _rope(slc[:, :w], cos_t, sin_t).astype(BF16)
    vs_ref[...] = slc[:, w:].astype(BF16)
    kw_ref[...] = _rope(win[:, :w], cos_t, sin_t).astype(BF16)
    vw_ref[...] = win[:, w:].astype(BF16)


def _kv_prep(kv, seq, tm=256):
    m = kv.shape[0]
    w = N_KV_HEADS * HEAD_DIM
    cos_t, sin_t = _rope_tables(seq, N_KV_HEADS)
    tpb = seq // tm
    out = jax.ShapeDtypeStruct((m, w), BF16)
    o_spec = pl.BlockSpec((tm, w), lambda i: (i, 0))
    t_spec = pl.BlockSpec((tm, w), lambda i: (i % tpb, 0))
    return pl.pallas_call(
        _kv_prep_kernel,
        out_shape=(out, out, out, out),
        grid=(m // tm,),
        in_specs=[pl.BlockSpec((tm, 2 * w), lambda i: (i, 1)),
                  pl.BlockSpec((tm, 2 * w), lambda i: (i, 2)),
                  t_spec, t_spec],
        out_specs=(o_spec, o_spec, o_spec, o_spec),
        compiler_params=_cparams(("parallel",)),
        name="kv_prep",
    )(kv, kv, cos_t, sin_t)


def _compress_kernel(x_ref, pe_ref, w1_ref, b1_ref, w2_ref, b2_ref, o_ref):
    nc = x_ref.shape[0]
    hid = w1_ref.shape[-1]
    top = jnp.zeros((nc, hid), F32)
    bot = jnp.zeros((nc, hid), F32)
    for j in range(CMP_STRIDE):
        xj = x_ref[:, j, :]
        top += jnp.dot((xj + pe_ref[0, j:j + 1, :]).astype(BF16), w1_ref[0, j],
                       preferred_element_type=F32)
        bot += jnp.dot((xj + pe_ref[0, CMP_STRIDE + j:CMP_STRIDE + j + 1, :]).astype(BF16),
                       w1_ref[0, CMP_STRIDE + j], preferred_element_type=F32)
    h = _gelu_tanh(top + pltpu.roll(bot, nc - 1, 0) + b1_ref[0])
    o_ref[0, 0, 0] = (jnp.dot(h.astype(BF16), w2_ref[0], preferred_element_type=F32) + b2_ref[0]).astype(BF16)


def _compress(kv, bsz, seq, pe, w1, b1, w2, b2):
    nc = seq // CMP_STRIDE
    hid = w1.shape[-1]
    kv3 = kv.reshape(bsz * nc, CMP_STRIDE, kv.shape[-1])
    w1r = w1.reshape(2, CMP_BLOCK, HEAD_DIM, hid)
    return pl.pallas_call(
        _compress_kernel,
        out_shape=jax.ShapeDtypeStruct((2, bsz, N_KV_HEADS, nc, HEAD_DIM), BF16),
        grid=(2, bsz, N_KV_HEADS),
        in_specs=[pl.BlockSpec((nc, CMP_STRIDE, HEAD_DIM), lambda s, b, h: (b, 0, s * N_KV_HEADS + h)),
                  pl.BlockSpec((1, CMP_BLOCK, HEAD_DIM), lambda s, b, h: (s, 0, 0)),
                  pl.BlockSpec((1, CMP_BLOCK, HEAD_DIM, hid), lambda s, b, h: (s, 0, 0, 0)),
                  pl.BlockSpec((1, 1, hid), lambda s, b, h: (s, 0, 0)),
                  pl.BlockSpec((1, hid, HEAD_DIM), lambda s, b, h: (s, 0, 0)),
                  pl.BlockSpec((1, 1, HEAD_DIM), lambda s, b, h: (s, 0, 0))],
        out_specs=pl.BlockSpec((1, 1, 1, nc, HEAD_DIM), lambda s, b, h: (s, b, h, 0, 0)),
        compiler_params=_cparams(("parallel", "parallel", "parallel")),
        name="nsa_compress",
    )(kv3, pe, w1r, b1.reshape(2, 1, hid), w2, b2.reshape(2, 1, HEAD_DIM))


def _softmax_rows(s, valid_bias):
    s = s + valid_bias
    m = jnp.max(s, axis=-1, keepdims=True)
    e = jnp.exp(s - m)
    return e, jnp.sum(e, axis=-1, keepdims=True)


def _attn_kernel(q_ref, gate_ref, cos_ref, sin_ref, kc_ref, vc_ref, c2s_ref,
                 ks_ref, vs_ref, kw_ref, vw_ref, o_ref, *, seq):
    qt = pl.program_id(2)
    s0 = qt * Q_TILE
    n_slc = seq // SLC_BLOCK
    n_cmp = kc_ref.shape[-2]
    scale = HEAD_DIM ** -0.5
    rows = GQA_GROUP * Q_TILE
    nt = (((1,), (1,)), ((), ()))

    q4 = q_ref[...]
    qp = jnp.concatenate([q4[:, g * HEAD_DIM:(g + 1) * HEAD_DIM] for g in range(GQA_GROUP)], axis=0) * scale
    cos4 = jnp.concatenate([cos_ref[...]] * GQA_GROUP, axis=0)
    sin4 = jnp.concatenate([sin_ref[...]] * GQA_GROUP, axis=0)
    qr_b = _rope(qp, cos4, sin4).astype(BF16)
    qp_b = qp.astype(BF16)
    t_q = s0 + lax.broadcasted_iota(jnp.int32, (Q_TILE, 1), 0)

    def tile4(x):
        return jnp.concatenate([x] * GQA_GROUP, axis=0)

    s_c = lax.dot_general(qp_b, kc_ref[0, 0], nt, preferred_element_type=F32)
    n_idx = lax.broadcasted_iota(jnp.int32, (Q_TILE, n_cmp), 1)
    c_ok = (n_idx * CMP_STRIDE + (CMP_BLOCK - 1)) <= t_q
    c_ok4 = tile4(jnp.where(c_ok, 1.0, 0.0))
    e_c, l_c = _softmax_rows(s_c, tile4(jnp.where(c_ok, 0.0, NEG_INF)))
    e_c = e_c * c_ok4
    l_c = jnp.sum(e_c, axis=-1, keepdims=True)
    p_c = e_c / jnp.where(l_c > 0.0, l_c, 1.0)
    o_c = jnp.dot(p_c.astype(BF16), vc_ref[0, 0], preferred_element_type=F32)
    p_sum = (p_c[0:Q_TILE] + p_c[Q_TILE:2 * Q_TILE]) + (p_c[2 * Q_TILE:3 * Q_TILE] + p_c[3 * Q_TILE:])
    p_hi = p_sum.astype(BF16)
    r1 = p_sum - p_hi.astype(F32)
    p_mid = r1.astype(BF16)
    p_lo = (r1 - p_mid.astype(F32)).astype(BF16)
    c2s = c2s_ref[...]
    imp_t = (lax.dot_general(c2s, p_hi, nt, preferred_element_type=F32)
             + lax.dot_general(c2s, p_mid, nt, preferred_element_type=F32)
             + lax.dot_general(c2s, p_lo, nt, preferred_element_type=F32))

    sid = lax.broadcasted_iota(jnp.int32, (n_slc, Q_TILE), 0)
    t_l = s0 + lax.broadcasted_iota(jnp.int32, (n_slc, Q_TILE), 1)
    cur = lax.shift_right_logical(t_l, int(math.log2(SLC_BLOCK)))
    causal_blk = sid <= cur
    forced = jnp.where(sid == 0, 1.0, jnp.where(sid == cur, 1.0, jnp.where(sid == cur - 1, 1.0, 0.0)))
    work = jnp.where(causal_blk, imp_t + FORCE_BONUS * forced, NEG_INF)
    sid_f = sid.astype(F32)
    sel_t = jnp.zeros((n_slc, Q_TILE), F32)
    for _ in range(min(N_SELECT, n_slc)):
        mx = jnp.max(work, axis=0, keepdims=True)
        first = jnp.min(jnp.where(work == mx, sid_f, float(n_slc)), axis=0, keepdims=True)
        pick = sid_f == first
        sel_t = jnp.where(pick, 1.0, sel_t)
        work = jnp.where(pick, -3.0e38, work)
    sel_t = jnp.where(causal_blk, sel_t, 0.0)
    sel_b = sel_t.T.astype(BF16)

    kt = 512
    blk_per_kt = kt // SLC_BLOCK

    def slc_body(j, carry):
        m, l, acc = carry
        k0 = pl.multiple_of(j * kt, kt)
        kj = ks_ref[pl.ds(k0, kt), :]
        vj = vs_ref[pl.ds(k0, kt), :]
        s = lax.dot_general(qr_b, kj, nt, preferred_element_type=F32)
        e_s = lax.broadcasted_iota(jnp.int32, (n_slc, kt), 0)
        e_k = lax.broadcasted_iota(jnp.int32, (n_slc, kt), 1)
        expand = jnp.where(e_s == j * blk_per_kt + lax.shift_right_logical(e_k, int(math.log2(SLC_BLOCK))),
                           1.0, 0.0).astype(BF16)
        chosen = jnp.dot(sel_b, expand, preferred_element_type=F32)
        kpos = k0 + lax.broadcasted_iota(jnp.int32, (Q_TILE, kt), 1)
        bias = jnp.where(kpos <= t_q, jnp.where(chosen > 0.5, 0.0, NEG_INF), NEG_INF)
        s = s + tile4(bias)
        m_new = jnp.maximum(m, jnp.max(s, axis=-1, keepdims=True))
        alpha = jnp.exp(m - m_new)
        p = jnp.exp(s - m_new)
        l = alpha * l + jnp.sum(p, axis=-1, keepdims=True)
        acc = alpha * acc + jnp.dot(p.astype(BF16), vj, preferred_element_type=F32)
        return m_new, l, acc

    n_kt = lax.shift_right_logical(s0 + Q_TILE - 1, int(math.log2(kt))) + 1
    init = (jnp.full((rows, 1), NEG_INF, F32), jnp.zeros((rows, 1), F32), jnp.zeros((rows, HEAD_DIM), F32))
    _, l_s, acc_s = lax.fori_loop(0, n_kt, slc_body, init)
    o_s = acc_s / l_s

    slab = WINDOW + Q_TILE
    w0 = pl.multiple_of(jnp.maximum(s0 - WINDOW, 0), Q_TILE)
    s_w = lax.dot_general(qr_b, kw_ref[pl.ds(w0, slab), :], nt, preferred_element_type=F32)
    dist = t_q - (w0 + lax.broadcasted_iota(jnp.int32, (Q_TILE, slab), 1))
    w_bias = jnp.where(dist >= 0, jnp.where(dist < WINDOW, 0.0, NEG_INF), NEG_INF)
    e_w, l_w = _softmax_rows(s_w, tile4(w_bias))
    o_w = jnp.dot(e_w.astype(BF16), vw_ref[pl.ds(w0, slab), :], preferred_element_type=F32) / l_w

    sg = _sigmoid(gate_ref[...])
    outs = []
    for g in range(GQA_GROUP):
        r = slice(g * Q_TILE, (g + 1) * Q_TILE)
        outs.append(sg[:, g:g + 1] * o_c[r]
                    + sg[:, GQA_GROUP + g:GQA_GROUP + g + 1] * o_s[r]
                    + sg[:, 2 * GQA_GROUP + g:2 * GQA_GROUP + g + 1] * o_w[r])
    o_ref[...] = jnp.concatenate(outs, axis=-1)


def _cmp_to_slc_t(seq):
    n_cmp = seq // CMP_STRIDE - 1
    n_slc = seq // SLC_BLOCK
    cmp_start = jnp.arange(n_cmp) * CMP_STRIDE
    slc_start = jnp.arange(n_slc) * SLC_BLOCK
    overlap = jnp.clip(jnp.minimum(cmp_start[:, None] + CMP_BLOCK, slc_start[None, :] + SLC_BLOCK)
                       - jnp.maximum(cmp_start[:, None], slc_start[None, :]), 0, None)
    c2s = overlap.astype(F32) / CMP_BLOCK
    return jnp.pad(c2s, ((0, 1), (0, 0))).T.astype(BF16)


def _nsa_attention(q, gates, kc, vc, ks, vs, kw, vw, bsz, seq):
    m = q.shape[0]
    nq = seq // Q_TILE
    nc = seq // CMP_STRIDE
    cos_t, sin_t = _rope_tables(seq, 1)
    c2s_t = _cmp_to_slc_t(seq)
    gw = GQA_GROUP * HEAD_DIM
    kv_spec = pl.BlockSpec((seq, HEAD_DIM), lambda b, h, t: (b, h))
    cmp_spec = pl.BlockSpec((1, 1, nc, HEAD_DIM), lambda b, h, t: (b, h, 0, 0))
    tab_spec = pl.BlockSpec((Q_TILE, HEAD_DIM), lambda b, h, t: (t, 0))
    return pl.pallas_call(
        functools.partial(_attn_kernel, seq=seq),
        out_shape=jax.ShapeDtypeStruct((m, N_HEADS * HEAD_DIM), F32),
        grid=(bsz, N_KV_HEADS, nq),
        in_specs=[pl.BlockSpec((Q_TILE, gw), lambda b, h, t: (b * nq + t, h)),
                  pl.BlockSpec((Q_TILE, LANES), lambda b, h, t: (b * nq + t, h)),
                  tab_spec, tab_spec, cmp_spec, cmp_spec,
                  pl.BlockSpec(c2s_t.shape, lambda b, h, t: (0, 0)),
                  kv_spec, kv_spec, kv_spec, kv_spec],
        out_specs=pl.BlockSpec((Q_TILE, gw), lambda b, h, t: (b * nq + t, h)),
        compiler_params=_cparams(("parallel", "parallel", "arbitrary")),
        name="nsa_attention",
    )(q, gates, cos_t, sin_t, kc, vc, c2s_t, ks, vs, kw, vw)


def _s5_operators(a_re, a_im, log_dt, b_re, b_im, c_re, c_im):
    hp = lax.Precision.HIGHEST
    n_g = a_re.shape[0]
    gpb = SSM_GROUPS_PER_BLOCK
    n_gb = n_g // gpb
    t_c = SSM_CHUNK
    lam = lax.complex(a_re.astype(F32), a_im.astype(F32))
    dt = jnp.exp(log_dt.astype(F32))[:, None]
    a_bar = jnp.exp(lam * dt)
    b_bar = ((a_bar - 1.0) / lam)[:, :, None] * lax.complex(b_re.astype(F32), b_im.astype(F32))
    c_mat = lax.complex(c_re.astype(F32), c_im.astype(F32))
    steps = jnp.arange(t_c + 1, dtype=F32)[:, None, None]
    pw = jnp.exp((lam * dt)[None] * steps)
    eye = jnp.eye(gpb, dtype=F32)

    def blockdiag(x):
        r, c = x.shape[-2:]
        return jnp.einsum('bgrc,gk->bgrkc', x, eye).reshape(n_gb, gpb * r, gpb * c)

    k_lag = jnp.einsum('ghp,kgp,gpi->kgih', c_mat, pw[:t_c], b_bar, precision=hp).real
    w_intra = jnp.stack([blockdiag(k_lag[k].reshape(n_gb, gpb, SSM_GROUP, SSM_GROUP)) for k in range(t_c)], axis=2)
    w_intra = w_intra.reshape(n_gb, LANES, t_c * LANES)
    p_in = pw[t_c - 1::-1][:t_c, :, :, None] * b_bar[None]
    p_in = jnp.swapaxes(p_in, -1, -2)

    def s_in_plane(x):
        return jnp.stack([blockdiag(x[k].reshape(n_gb, gpb, SSM_GROUP, SSM_STATE)) for k in range(t_c)], axis=1)

    s_in = jnp.concatenate([s_in_plane(p_in.real), s_in_plane(p_in.imag)], axis=-1)
    m_out = c_mat[None] * pw[1:, :, None, :]
    m_out = jnp.swapaxes(m_out, -1, -2)

    def s_out_plane(x):
        return jnp.stack([blockdiag(x[k].reshape(n_gb, gpb, SSM_STATE, SSM_GROUP)) for k in range(t_c)], axis=2)

    s_out = jnp.concatenate([s_out_plane(m_out.real), s_out_plane(-m_out.imag)], axis=1)
    s_out = s_out.reshape(n_gb, 2 * gpb * SSM_STATE, t_c * LANES)
    a_chunk = pw[t_c].reshape(n_gb, 1, gpb * SSM_STATE)
    return (w_intra.astype(BF16), s_in.astype(BF16), s_out.astype(BF16),
            a_chunk.real.astype(F32), a_chunk.imag.astype(F32))


def _s5_kernel(u_ref, wi_ref, sin_ref, sout_ref, ar_ref, ai_ref, d_ref, o_ref, xin_ref, xprev_ref, y_ref):
    nc = u_ref.shape[0]
    t_c = SSM_CHUNK
    half = ar_ref.shape[-1]
    xin = jnp.zeros((nc, 2 * half), F32)
    for tau in range(t_c):
        xin += jnp.dot(u_ref[:, tau, :].astype(BF16), sin_ref[0, tau], preferred_element_type=F32)
    xin_ref[...] = xin
    a_r, a_i = ar_ref[0], ai_ref[0]

    def step(c, carry):
        x_r, x_i = carry
        xprev_ref[pl.ds(c, 1), :] = jnp.concatenate([x_r, x_i], axis=-1)
        row = xin_ref[pl.ds(c, 1), :]
        n_r = a_r * x_r - a_i * x_i + row[:, :half]
        n_i = a_r * x_i + a_i * x_r + row[:, half:]
        return n_r, n_i

    zero = jnp.zeros((1, half), F32)
    lax.fori_loop(0, nc, step, (zero, zero))
    y_ref[...] = jnp.dot(xprev_ref[...].astype(BF16), sout_ref[0], preferred_element_type=F32)
    for tau in range(t_c):
        width = (t_c - tau) * LANES
        y_ref[:, tau * LANES:] += jnp.dot(u_ref[:, tau, :].astype(BF16), wi_ref[0, :, :width],
                                          preferred_element_type=F32)
    for t in range(t_c):
        y = y_ref[:, t * LANES:(t + 1) * LANES] + d_ref[0] * u_ref[:, t, :]
        o_ref[:, t, :] = _gelu_tanh(y)


def _s5_scan(u, bsz, seq, ops, d_skip):
    m, d_ssm = u.shape
    w_intra, s_in, s_out, a_r, a_i = ops
    n_gb = d_ssm // LANES
    nc = seq // SSM_CHUNK
    t_c = SSM_CHUNK
    st = s_in.shape[-1]
    u3 = u.reshape(m // t_c, t_c, d_ssm)
    out = pl.pallas_call(
        _s5_kernel,
        out_shape=jax.ShapeDtypeStruct(u3.shape, F32),
        grid=(n_gb, bsz),
        in_specs=[pl.BlockSpec((nc, t_c, LANES), lambda g, b: (b, 0, g)),
                  pl.BlockSpec((1, LANES, t_c * LANES), lambda g, b: (g, 0, 0)),
                  pl.BlockSpec((1, t_c, LANES, st), lambda g, b: (g, 0, 0, 0)),
                  pl.BlockSpec((1, st, t_c * LANES), lambda g, b: (g, 0, 0)),
                  pl.BlockSpec((1, 1, st // 2), lambda g, b: (g, 0, 0)),
                  pl.BlockSpec((1, 1, st // 2), lambda g, b: (g, 0, 0)),
                  pl.BlockSpec((1, 1, LANES), lambda g, b: (g, 0, 0))],
        out_specs=pl.BlockSpec((nc, t_c, LANES), lambda g, b: (b, 0, g)),
        scratch_shapes=[pltpu.VMEM((nc, st), F32), pltpu.VMEM((nc, st), F32),
                        pltpu.VMEM((nc, t_c * LANES), F32)],
        compiler_params=_cparams(("parallel", "parallel")),
        name="s5_scan",
    )(u3, w_intra, s_in, s_out, a_r, a_i, d_skip.reshape(n_gb, 1, LANES))
    return out.reshape(m, d_ssm)


def _mixnorm_kernel(a_ref, s_ref, ga_ref, gs_ref, o_ref):
    da = a_ref.shape[-1]
    a = a_ref[...]
    s = s_ref[...]
    o_ref[:, :da] = (a * lax.rsqrt(jnp.mean(a * a, axis=-1, keepdims=True) + EPS) * ga_ref[...]).astype(BF16)
    o_ref[:, da:] = (s * lax.rsqrt(jnp.mean(s * s, axis=-1, keepdims=True) + EPS) * gs_ref[...]).astype(BF16)


def _mixnorm(attn, ssm, g_attn, g_ssm, tm=256):
    m, da = attn.shape
    ds = ssm.shape[1]
    return pl.pallas_call(
        _mixnorm_kernel,
        out_shape=jax.ShapeDtypeStruct((m, da + ds), BF16),
        grid=(m // tm,),
        in_specs=[pl.BlockSpec((tm, da), lambda i: (i, 0)), pl.BlockSpec((tm, ds), lambda i: (i, 0)),
                  pl.BlockSpec((1, da), lambda i: (0, 0)), pl.BlockSpec((1, ds), lambda i: (0, 0))],
        out_specs=pl.BlockSpec((tm, da + ds), lambda i: (i, 0)),
        compiler_params=_cparams(("parallel",)),
        name="mixnorm",
    )(attn, ssm, g_attn.reshape(1, da), g_ssm.reshape(1, ds))


def _split_bf16(x):
    hi = x.astype(BF16)
    return hi, (x - hi.astype(F32)).astype(BF16)


def _ffn_norm_router_kernel(h_ref, g_ref, whi_ref, wlo_ref, b_ref, hn_ref, logit_ref):
    x = h_ref[...]
    hn = x * lax.rsqrt(jnp.mean(x * x, axis=-1, keepdims=True) + EPS) * g_ref[...]
    hn_ref[...] = hn
    hi, lo = _split_bf16(hn)
    logit_ref[...] = (jnp.dot(hi, whi_ref[...], preferred_element_type=F32)
                      + jnp.dot(lo, whi_ref[...], preferred_element_type=F32)
                      + jnp.dot(hi, wlo_ref[...], preferred_element_type=F32)) + b_ref[...]


def _ffn_norm_router(h, g, w_router, b_router, tm=256):
    m, d = h.shape
    n = w_router.shape[1]
    w_hi, w_lo = _split_bf16(w_router)
    return pl.pallas_call(
        _ffn_norm_router_kernel,
        out_shape=(jax.ShapeDtypeStruct((m, d), F32), jax.ShapeDtypeStruct((m, n), F32)),
        grid=(m // tm,),
        in_specs=[pl.BlockSpec((tm, d), lambda i: (i, 0)), pl.BlockSpec((1, d), lambda i: (0, 0)),
                  pl.BlockSpec((d, n), lambda i: (0, 0)), pl.BlockSpec((d, n), lambda i: (0, 0)),
                  pl.BlockSpec((1, n), lambda i: (0, 0))],
        out_specs=(pl.BlockSpec((tm, d), lambda i: (i, 0)), pl.BlockSpec((tm, n), lambda i: (i, 0))),
        compiler_params=_cparams(("parallel",)),
        name="ffn_norm_router",
    )(h, g.reshape(1, d), w_hi, w_lo, b_router.reshape(1, n))


def _gather_rows_kernel(tok_ref, x_hbm, o_ref, sem):
    base = pl.program_id(0) * MOE_BLOCK

    def row_copy(r):
        return pltpu.make_async_copy(x_hbm.at[pl.ds(tok_ref[base + r], 1), :], o_ref.at[pl.ds(r, 1), :], sem)

    def start(r, c):
        row_copy(r).start()
        return c

    def wait(r, c):
        row_copy(r).wait()
        return c

    lax.fori_loop(0, MOE_BLOCK, start, 0)
    lax.fori_loop(0, MOE_BLOCK, wait, 0)


def _gather_rows(x, tok):
    n_rows = tok.shape[0]
    d = x.shape[1]
    return pl.pallas_call(
        _gather_rows_kernel,
        out_shape=jax.ShapeDtypeStruct((n_rows, d), x.dtype),
        grid_spec=pltpu.PrefetchScalarGridSpec(
            num_scalar_prefetch=1,
            grid=(n_rows // MOE_BLOCK,),
            in_specs=[pl.BlockSpec(memory_space=pl.ANY)],
            out_specs=pl.BlockSpec((MOE_BLOCK, d), lambda i, tok: (i, 0)),
            scratch_shapes=[pltpu.SemaphoreType.DMA(())]),
        compiler_params=_cparams(("arbitrary",)),
        name="moe_gather",
    )(tok, x)


def _moe_up_kernel(meta_ref, x_ref, wg_ref, wu_ref, o_ref, wg_b, wu_b):
    i = pl.program_id(0)
    n_items = meta_ref.shape[0] // MOE_META_ROWS

    @pl.when(i < meta_ref[4 * n_items])
    def _():
        @pl.when(meta_ref[3 * n_items + i] == 1)
        def _():
            wg_b[...] = wg_ref[0].astype(BF16)
            wu_b[...] = wu_ref[0].astype(BF16)

        x = x_ref[...].astype(BF16)
        g = jnp.dot(x, wg_b[...], preferred_element_type=F32)
        u = jnp.dot(x, wu_b[...], preferred_element_type=F32)
        o_ref[...] = (g * _sigmoid(g) * u).astype(o_ref.dtype)

    @pl.when(i >= meta_ref[4 * n_items])
    def _():
        o_ref[...] = jnp.zeros(o_ref.shape, o_ref.dtype)


def _moe_down_kernel(meta_ref, a_ref, wd_ref, o_ref, wd_b):
    i = pl.program_id(0)
    n_items = meta_ref.shape[0] // MOE_META_ROWS

    @pl.when(i < meta_ref[4 * n_items])
    def _():
        @pl.when(meta_ref[3 * n_items + i] == 1)
        def _():
            wd_b[...] = wd_ref[0].astype(BF16)

        o_ref[...] = jnp.dot(a_ref[...], wd_b[...], preferred_element_type=F32)

    @pl.when(i >= meta_ref[4 * n_items])
    def _():
        o_ref[...] = jnp.zeros(o_ref.shape, o_ref.dtype)


MOE_META_ROWS = 6


def _moe_items(blk_expert, n_used, n_chunks):
    n_blocks = blk_expert.shape[0]
    n_items = n_blocks * n_chunks
    blk = jnp.arange(n_blocks, dtype=jnp.int32)
    used = blk < n_used
    e_key = jnp.where(used, blk_expert, N_EXPERTS)
    run_start = jnp.searchsorted(e_key, e_key, side='left').astype(jnp.int32)
    run_len = (jnp.searchsorted(e_key, e_key, side='right') - run_start).astype(jnp.int32)
    pos = blk - run_start
    chunk = jnp.arange(n_chunks, dtype=jnp.int32)
    item_pos = (run_start[:, None] * n_chunks + chunk[None, :] * run_len[:, None] + pos[:, None])
    item_pos = jnp.where(used[:, None], item_pos, n_items)
    flat = lambda x: jnp.broadcast_to(x, (n_blocks, n_chunks)).reshape(-1)
    tgt = item_pos.reshape(-1)
    it_blk = jnp.zeros((n_items,), jnp.int32).at[tgt].set(flat(blk[:, None]), mode='drop')
    it_chunk = jnp.zeros((n_items,), jnp.int32).at[tgt].set(flat(chunk[None, :]), mode='drop')
    it_exp = jnp.zeros((n_items,), jnp.int32).at[tgt].set(flat(blk_expert[:, None]), mode='drop')
    it_first = jnp.zeros((n_items,), jnp.int32).at[tgt].set(flat((pos == 0)[:, None].astype(jnp.int32)), mode='drop')
    n_items_used = n_used * n_chunks
    idx = jnp.arange(n_items, dtype=jnp.int32)
    last = jnp.maximum(n_items_used - 1, 0)
    keep = idx < n_items_used
    spare = idx - n_items_used
    it_wchunk = jnp.where(keep, it_chunk, it_chunk[last])
    it_blk = jnp.where(keep, it_blk, n_used + spare // n_chunks)
    it_chunk = jnp.where(keep, it_chunk, spare % n_chunks)
    it_exp = jnp.where(keep, it_exp, it_exp[last])
    it_first = jnp.where(keep, it_first, 0)
    return jnp.concatenate([it_blk, it_chunk, it_exp, it_first,
                            jnp.full((n_items,), n_items_used, jnp.int32), it_wchunk]).astype(jnp.int32)


def _moe_up(x_sorted, w_gate, w_up, meta, n_chunks):
    n_rows, d = x_sorted.shape
    d_e = w_gate.shape[-1]
    tn = d_e // n_chunks
    n_items = meta.shape[0] // MOE_META_ROWS
    w_spec = pl.BlockSpec((1, d, tn), lambda i, mt: (mt[2 * n_items + i], 0, mt[5 * n_items + i]))
    return pl.pallas_call(
        _moe_up_kernel,
        out_shape=jax.ShapeDtypeStruct((n_rows, d_e), BF16),
        grid_spec=pltpu.PrefetchScalarGridSpec(
            num_scalar_prefetch=1,
            grid=(n_items,),
            in_specs=[pl.BlockSpec((MOE_BLOCK, d), lambda i, mt: (mt[i], 0)), w_spec, w_spec],
            out_specs=pl.BlockSpec((MOE_BLOCK, tn), lambda i, mt: (mt[i], mt[n_items + i])),
            scratch_shapes=[pltpu.VMEM((d, tn), BF16), pltpu.VMEM((d, tn), BF16)]),
        compiler_params=_cparams(("arbitrary",)),
        name="moe_up",
    )(meta, x_sorted, w_gate, w_up)


def _moe_down(act, w_down, meta, n_chunks):
    n_rows, d_e = act.shape
    d = w_down.shape[-1]
    tn = d // n_chunks
    n_items = meta.shape[0] // MOE_META_ROWS
    return pl.pallas_call(
        _moe_down_kernel,
        out_shape=jax.ShapeDtypeStruct((n_rows, d), F32),
        grid_spec=pltpu.PrefetchScalarGridSpec(
            num_scalar_prefetch=1,
            grid=(n_items,),
            in_specs=[pl.BlockSpec((MOE_BLOCK, d_e), lambda i, mt: (mt[i], 0)),
                      pl.BlockSpec((1, d_e, tn), lambda i, mt: (mt[2 * n_items + i], 0, mt[5 * n_items + i]))],
            out_specs=pl.BlockSpec((MOE_BLOCK, tn), lambda i, mt: (mt[i], mt[n_items + i])),
            scratch_shapes=[pltpu.VMEM((d_e, tn), BF16)]),
        compiler_params=_cparams(("arbitrary",)),
        name="moe_down",
    )(meta, act, w_down)


def _combine_kernel(dest_ref, h_ref, gate_ref, gn_ref, y_hbm, o_ref, rows_ref, sem):
    tm = h_ref.shape[0]
    base = pl.program_id(0) * tm * TOP_K

    def row_copy(r):
        return pltpu.make_async_copy(y_hbm.at[pl.ds(dest_ref[base + r], 1), :], rows_ref.at[pl.ds(r, 1), :], sem)

    def start(r, c):
        row_copy(r).start()
        return c

    def wait(r, c):
        row_copy(r).wait()
        return c

    lax.fori_loop(0, tm * TOP_K, start, 0)
    lax.fori_loop(0, tm * TOP_K, wait, 0)
    gate = gate_ref[...]
    h = h_ref[...] + gate[:, 0:1] * rows_ref[0:tm, :] + gate[:, 1:2] * rows_ref[tm:2 * tm, :]
    o_ref[...] = h * lax.rsqrt(jnp.mean(h * h, axis=-1, keepdims=True) + EPS) * gn_ref[...]


def _combine_final(h, gate, dest, y_buf, g_final, tm=128):
    m, d = h.shape
    return pl.pallas_call(
        _combine_kernel,
        out_shape=jax.ShapeDtypeStruct((m, d), F32),
        grid_spec=pltpu.PrefetchScalarGridSpec(
            num_scalar_prefetch=1,
            grid=(m // tm,),
            in_specs=[pl.BlockSpec((tm, d), lambda i, dst: (i, 0)),
                      pl.BlockSpec((tm, TOP_K), lambda i, dst: (i, 0)),
                      pl.BlockSpec((1, d), lambda i, dst: (0, 0)),
                      pl.BlockSpec(memory_space=pl.ANY)],
            out_specs=pl.BlockSpec((tm, d), lambda i, dst: (i, 0)),
            scratch_shapes=[pltpu.VMEM((TOP_K * tm, d), F32), pltpu.SemaphoreType.DMA(())]),
        compiler_params=_cparams(("arbitrary",)),
        name="moe_combine_final",
    )(dest, h, gate, g_final.reshape(1, d), y_buf)


def _route(logits, b_unused=None):
    n_tok = logits.shape[0]
    tok = jnp.arange(n_tok)
    p_grp = jax.nn.softmax(logits[:, :N_EXPERT_GROUPS], axis=-1)
    grp = jnp.argmax(p_grp, axis=-1)
    w_grp = p_grp[tok, grp]
    fine = logits[:, N_EXPERT_GROUPS:N_EXPERT_GROUPS + N_EXPERTS].reshape(
        n_tok, N_EXPERT_GROUPS, EXPERTS_PER_GROUP)[tok, grp]
    p_fine = jax.nn.softmax(fine, axis=-1)
    top_p, top_e = lax.top_k(p_fine, TOP_K)
    gate = w_grp[:, None] * top_p / jnp.sum(top_p, axis=-1, keepdims=True)
    expert = (grp[:, None] * EXPERTS_PER_GROUP + top_e).astype(jnp.int32)
    return gate, expert


def _dispatch_plan(expert, tm):
    n_tok = expert.shape[0]
    n_assign = n_tok * TOP_K
    n_blocks = -(-(n_assign + N_EXPERTS * (MOE_BLOCK - 1)) // MOE_BLOCK)
    flat = expert.reshape(n_assign)
    order = jnp.argsort(flat)
    e_s = flat[order]
    counts = jnp.zeros((N_EXPERTS,), jnp.int32).at[flat].add(1)
    start = jnp.cumsum(counts) - counts
    padded = (counts + MOE_BLOCK - 1) // MOE_BLOCK * MOE_BLOCK
    pad_end = jnp.cumsum(padded)
    dest_s = (pad_end[e_s] - padded[e_s] + jnp.arange(n_assign, dtype=jnp.int32) - start[e_s]).astype(jnp.int32)
    buf_tok = jnp.zeros((n_blocks * MOE_BLOCK,), jnp.int32).at[dest_s].set((order // TOP_K).astype(jnp.int32))
    blk_expert = jnp.minimum(jnp.searchsorted(pad_end, jnp.arange(n_blocks) * MOE_BLOCK, side='right'),
                             N_EXPERTS - 1).astype(jnp.int32)
    n_used = (pad_end[-1] // MOE_BLOCK).astype(jnp.int32)
    dest = jnp.zeros((n_assign,), jnp.int32).at[order].set(dest_s)
    dest = dest.reshape(n_tok // tm, tm, TOP_K).transpose(0, 2, 1).reshape(-1)
    return buf_tok, blk_expert, n_used, dest


def _gate_column_order():
    cols = []
    for hk in range(N_KV_HEADS):
        blk = [-1] * LANES
        for br in range(N_BRANCHES):
            for g in range(GQA_GROUP):
                blk[br * GQA_GROUP + g] = (hk * GQA_GROUP + g) * N_BRANCHES + br
        cols.extend(blk)
    return cols


def kernel(x, norm_mix, w_in, cmp_pe_k, cmp_w1_k, cmp_b1_k, cmp_w2_k, cmp_b2_k, cmp_pe_v, cmp_w1_v, cmp_b1_v, cmp_w2_v, cmp_b2_v, ssm_a_re, ssm_a_im, ssm_log_dt, ssm_b_re, ssm_b_im, ssm_c_re, ssm_c_im, ssm_d, ssm_w_glu, ssm_b_glu, norm_attn_out, norm_ssm_out, w_out, norm_ffn, router_w_coarse, router_b_coarse, router_w_fine, router_b_fine, w_gate, w_up, w_down, norm_final):
    bsz, seq, d_model = x.shape
    m = bsz * seq
    depth = w_in.shape[0]
    q_w = N_HEADS * HEAD_DIM
    kv_w = N_KV_HEADS * HEAD_DIM
    gate_w = N_HEADS * N_BRANCHES
    cut1, cut2 = q_w, q_w + 6 * kv_w
    cut3 = cut2 + gate_w
    h = x.astype(F32).reshape(m, d_model)
    gate_cols = jnp.array([c if c >= 0 else 0 for c in _gate_column_order()], jnp.int32)
    gate_keep = jnp.array([1.0 if c >= 0 else 0.0 for c in _gate_column_order()], F32)

    for layer in range(depth):
        w_in_b = w_in[layer].astype(BF16)
        hn = _rmsnorm(h, norm_mix[layer], BF16)
        q = _matmul(hn, w_in_b[:, :cut1], name="proj_q")
        kv = _matmul(hn, w_in_b[:, cut1:cut2], name="proj_kv")
        w_gate_cols = (w_in[layer][:, cut2:cut3][:, gate_cols] * gate_keep[None, :]).astype(BF16)
        gates = _matmul(hn, w_gate_cols, tn=512, name="proj_gate")
        u = _matmul(hn, w_in_b[:, cut3:], name="proj_u")

        stack2 = lambda a, b: jnp.stack([a[layer], b[layer]])
        cmp_kv = _compress(kv, bsz, seq, stack2(cmp_pe_k, cmp_pe_v),
                           stack2(cmp_w1_k, cmp_w1_v).astype(BF16), stack2(cmp_b1_k, cmp_b1_v),
                           stack2(cmp_w2_k, cmp_w2_v).astype(BF16), stack2(cmp_b2_k, cmp_b2_v))
        ks, vs, kw, vw = _kv_prep(kv, seq)
        attn = _nsa_attention(q, gates, cmp_kv[0], cmp_kv[1], ks, vs, kw, vw, bsz, seq)

        ops = _s5_operators(ssm_a_re[layer], ssm_a_im[layer], ssm_log_dt[layer], ssm_b_re[layer],
                            ssm_b_im[layer], ssm_c_re[layer], ssm_c_im[layer])
        y = _s5_scan(u, bsz, seq, ops, ssm_d[layer].astype(F32))
        ssm = _matmul(y.astype(BF16), ssm_w_glu[layer].astype(BF16), bias=ssm_b_glu[layer].astype(F32),
                      glu_y=y, name="ssm_glu")

        mixed = _mixnorm(attn, ssm, norm_attn_out[layer], norm_ssm_out[layer])
        h = _matmul(mixed, w_out[layer].astype(BF16), residual=h, name="out_proj")

        n_r = N_EXPERT_GROUPS + N_EXPERTS
        w_router = jnp.pad(jnp.concatenate([router_w_coarse[layer], router_w_fine[layer]], axis=1),
                           ((0, 0), (0, LANES - n_r))).astype(F32)
        b_router = jnp.pad(jnp.concatenate([router_b_coarse[layer], router_b_fine[layer]]), (0, LANES - n_r))
        hn2, logits = _ffn_norm_router(h, norm_ffn[layer], w_router, b_router.astype(F32))
        gate, expert = _route(logits)
        tm_c = 128
        buf_tok, blk_expert, n_used, dest = _dispatch_plan(expert, tm_c)
        x_sorted = _gather_rows(hn2, buf_tok)
        act = _moe_up(x_sorted, w_gate[layer], w_up[layer], _moe_items(blk_expert, n_used, 2), 2)
        y_buf = _moe_down(act, w_down[layer], _moe_items(blk_expert, n_used, 4), 4)
        if layer == depth - 1:
            out = _combine_final(h, gate, dest, y_buf, norm_final, tm=tm_c)
        else:
            raise NotImplementedError("stacked layers need a combine without the final norm")
    return out.reshape(bsz, seq, d_model).astype(x.dtype)
```

```python
import functools
import math

import jax
import jax.numpy as jnp
from jax import lax
from jax.experimental import pallas as pl
from jax.experimental.pallas import tpu as pltpu

F32 = jnp.float32
BF16 = jnp.bfloat16

HEAD_DIM = 128
N_HEADS = 16
N_KV_HEADS = 4
GQA_GROUP = N_HEADS // N_KV_HEADS
ROT_DIM = HEAD_DIM // 4
ROPE_THETA = 500000.0
CMP_STRIDE = 16
CMP_BLOCK = 32
SLC_BLOCK = 64
N_SELECT = 16
WINDOW = 512
Q_TILE = 128
N_BRANCHES = 3
FORCE_BONUS = 1.0e4
NEG_INF = -1.0e30
SSM_GROUP = 16
SSM_STATE = 64
SSM_CHUNK = 16
SSM_GROUPS_PER_BLOCK = 128 // SSM_GROUP
N_EXPERT_GROUPS = 8
EXPERTS_PER_GROUP = 8
N_EXPERTS = N_EXPERT_GROUPS * EXPERTS_PER_GROUP
TOP_K = 2
MOE_BLOCK = 128
EPS = 1.0e-6
LOG2_E = 1.4426950408889634

V7X_VMEM_LIMIT = 56 * 1024 * 1024
LANES = 128
ROW_FOLD = 16
DMA_UNROLL = 8


def _cparams(sem, vmem=V7X_VMEM_LIMIT):
    return pltpu.CompilerParams(dimension_semantics=sem, vmem_limit_bytes=vmem)


def _gelu_tanh(x):
    return 0.5 * x * (1.0 + jnp.tanh(math.sqrt(2.0 / math.pi) * (x + 0.044715 * (x * x * x))))


def _sigmoid(x):
    return 1.0 / (1.0 + jnp.exp(-x))


def _rmsnorm_kernel(x_ref, g_ref, o_ref):
    x = x_ref[...]
    ms = jnp.mean(x * x, axis=-1, keepdims=True)
    o_ref[...] = (x * lax.rsqrt(ms + EPS) * g_ref[...]).astype(o_ref.dtype)


def _rmsnorm(x, g, out_dtype, tm=256):
    m, d = x.shape
    return pl.pallas_call(
        _rmsnorm_kernel,
        out_shape=jax.ShapeDtypeStruct((m, d), out_dtype),
        grid=(m // tm,),
        in_specs=[pl.BlockSpec((tm, d), lambda i: (i, 0)),
                  pl.BlockSpec((1, d), lambda i: (0, 0))],
        out_specs=pl.BlockSpec((tm, d), lambda i: (i, 0)),
        compiler_params=_cparams(("parallel",)),
        name="rmsnorm",
    )(x, g.reshape(1, d))


def _mm_kernel(a_ref, w_ref, o_ref):
    acc = jnp.dot(a_ref[...], w_ref[...], preferred_element_type=F32)
    o_ref[...] = acc.reshape(o_ref.shape).astype(o_ref.dtype)


def _mm_glu_kernel(y_ref, w_ref, b_ref, yg_ref, o_ref):
    tm = o_ref.shape[0]
    a = y_ref[...].reshape(tm, y_ref.shape[-1]).astype(BF16)
    z = jnp.dot(a, w_ref[...], preferred_element_type=F32) + b_ref[...]
    o_ref[...] = yg_ref[...].reshape(o_ref.shape) * _sigmoid(z)


def _mm_residual_kernel(a_ref, w_ref, r_ref, o_ref):
    o_ref[...] = r_ref[...] + jnp.dot(a_ref[...], w_ref[...], preferred_element_type=F32)


def _matmul(a, w, *, fold_out=False, residual=None, tm=512, tn=1024, name="matmul"):
    m, k = a.shape
    n = w.shape[1]
    tn = min(tn, n)
    assert m % tm == 0 and n % tn == 0
    a_spec = pl.BlockSpec((tm, k), lambda j, i: (i, 0))
    w_spec = pl.BlockSpec((k, tn), lambda j, i: (0, j))
    o_spec = pl.BlockSpec((tm, tn), lambda j, i: (i, j))
    out_shape = (m, n)
    if residual is not None:
        kern, ins, specs = _mm_residual_kernel, (a, w, residual), [a_spec, w_spec, o_spec]
    else:
        kern, ins, specs = _mm_kernel, (a, w), [a_spec, w_spec]
        if fold_out:
            out_shape = (m // ROW_FOLD, ROW_FOLD, n)
            o_spec = pl.BlockSpec((tm // ROW_FOLD, ROW_FOLD, tn), lambda j, i: (i, 0, j))
    return pl.pallas_call(
        kern,
        out_shape=jax.ShapeDtypeStruct(out_shape, F32),
        grid=(n // tn, m // tm),
        in_specs=specs,
        out_specs=o_spec,
        compiler_params=_cparams(("parallel", "parallel")),
        name=name,
    )(*ins)


def _glu_matmul(y3, w, bias, tm=512, tn=1024):
    mf, fold, k = y3.shape
    n = w.shape[1]
    m = mf * fold
    return pl.pallas_call(
        _mm_glu_kernel,
        out_shape=jax.ShapeDtypeStruct((m, n), F32),
        grid=(n // tn, m // tm),
        in_specs=[pl.BlockSpec((tm // fold, fold, k), lambda j, i: (i, 0, 0)),
                  pl.BlockSpec((k, tn), lambda j, i: (0, j)),
                  pl.BlockSpec((1, tn), lambda j, i: (0, j)),
                  pl.BlockSpec((tm // fold, fold, tn), lambda j, i: (i, 0, j))],
        out_specs=pl.BlockSpec((tm, tn), lambda j, i: (i, j)),
        compiler_params=_cparams(("parallel", "parallel")),
        name="ssm_glu",
    )(y3, w, bias.reshape(1, n), y3)


def _rope_tables(seq, reps):
    half = ROT_DIM // 2
    inv_freq = ROPE_THETA ** (-jnp.arange(half, dtype=F32) / half)
    ang = jnp.arange(seq).astype(F32)[:, None] * inv_freq[None, :]
    cos, sin = jnp.cos(ang), jnp.sin(ang)
    ones = jnp.ones((seq, HEAD_DIM - ROT_DIM), F32)
    cos_t = jnp.concatenate([cos, cos, ones], axis=-1)
    sin_t = jnp.concatenate([-sin, sin, 0.0 * ones], axis=-1)
    return jnp.tile(cos_t, (1, reps)), jnp.tile(sin_t, (1, reps))


def _rope(x, cos_t, sin_t):
    half = ROT_DIM // 2
    n = x.shape[-1]
    lane = jnp.bitwise_and(lax.broadcasted_iota(jnp.int32, x.shape, x.ndim - 1), HEAD_DIM - 1)
    partner = jnp.where(lane < half, pltpu.roll(x, n - half, x.ndim - 1), pltpu.roll(x, half, x.ndim - 1))
    return x * cos_t + partner * sin_t


def _kv_prep_kernel(slc_ref, win_ref, cos_ref, sin_ref, ks_ref, vs_ref, kw_ref, vw_ref):
    tm, w = ks_ref.shape
    cos_t, sin_t = cos_ref[...], sin_ref[...]
    slc = slc_ref[...].reshape(tm, 2 * w)
    win = win_ref[...].reshape(tm, 2 * w)
    ks_ref[...] = _rope(slc[:, :w], cos_t, sin_t).astype(BF16)
    vs_ref[...] = slc[:, w:].astype(BF16)
    kw_ref[...] = _rope(win[:, :w], cos_t, sin_t).astype(BF16)
    vw_ref[...] = win[:, w:].astype(BF16)


def _kv_prep(kv3, seq, tm=256):
    m = kv3.shape[0] * ROW_FOLD
    w = N_KV_HEADS * HEAD_DIM
    cos_t, sin_t = _rope_tables(seq, N_KV_HEADS)
    tpb = seq // tm
    out = jax.ShapeDtypeStruct((m, w), BF16)
    o_spec = pl.BlockSpec((tm, w), lambda i: (i, 0))
    t_spec = pl.BlockSpec((tm, w), lambda i: (i % tpb, 0))
    return pl.pallas_call(
        _kv_prep_kernel,
        out_shape=(out, out, out, out),
        grid=(m // tm,),
        in_specs=[pl.BlockSpec((tm // ROW_FOLD, ROW_FOLD, 2 * w), lambda i: (i, 0, 1)),
                  pl.BlockSpec((tm // ROW_FOLD, ROW_FOLD, 2 * w), lambda i: (i, 0, 2)),
                  t_spec, t_spec],
        out_specs=(o_spec, o_spec, o_spec, o_spec),
        compiler_params=_cparams(("parallel",)),
        name="kv_prep",
    )(kv3, kv3, cos_t, sin_t)


def _compress_kernel(x_ref, pe_ref, w1_ref, b1_ref, w2_ref, b2_ref, o_ref):
    nc = x_ref.shape[0]
    hid = w1_ref.shape[-1]
    top = jnp.zeros((nc, hid), F32)
    bot = jnp.zeros((nc, hid), F32)
    for j in range(CMP_STRIDE):
        xj = x_ref[:, j, :]
        top += jnp.dot((xj + pe_ref[0, j:j + 1, :]).astype(BF16), w1_ref[0, j],
                       preferred_element_type=F32)
        bot += jnp.dot((xj + pe_ref[0, CMP_STRIDE + j:CMP_STRIDE + j + 1, :]).astype(BF16),
                       w1_ref[0, CMP_STRIDE + j], preferred_element_type=F32)
    h = _gelu_tanh(top + pltpu.roll(bot, nc - 1, 0) + b1_ref[0])
    o_ref[0, 0, 0] = (jnp.dot(h.astype(BF16), w2_ref[0], preferred_element_type=F32) + b2_ref[0]).astype(BF16)


def _compress(kv3, bsz, seq, pe, w1, b1, w2, b2):
    nc = seq // CMP_STRIDE
    hid = w1.shape[-1]
    w1r = w1.reshape(2, CMP_BLOCK, HEAD_DIM, hid)
    return pl.pallas_call(
        _compress_kernel,
        out_shape=jax.ShapeDtypeStruct((2, bsz, N_KV_HEADS, nc, HEAD_DIM), BF16),
        grid=(2, bsz, N_KV_HEADS),
        in_specs=[pl.BlockSpec((nc, CMP_STRIDE, HEAD_DIM), lambda s, b, h: (b, 0, s * N_KV_HEADS + h)),
                  pl.BlockSpec((1, CMP_BLOCK, HEAD_DIM), lambda s, b, h: (s, 0, 0)),
                  pl.BlockSpec((1, CMP_BLOCK, HEAD_DIM, hid), lambda s, b, h: (s, 0, 0, 0)),
                  pl.BlockSpec((1, 1, hid), lambda s, b, h: (s, 0, 0)),
                  pl.BlockSpec((1, hid, HEAD_DIM), lambda s, b, h: (s, 0, 0)),
                  pl.BlockSpec((1, 1, HEAD_DIM), lambda s, b, h: (s, 0, 0))],
        out_specs=pl.BlockSpec((1, 1, 1, nc, HEAD_DIM), lambda s, b, h: (s, b, h, 0, 0)),
        compiler_params=_cparams(("parallel", "parallel", "parallel")),
        name="nsa_compress",
    )(kv3, pe, w1r, b1.reshape(2, 1, hid), w2, b2.reshape(2, 1, HEAD_DIM))


def _attn_kernel(q_ref, gate_ref, cos_ref, sin_ref, kc_ref, vc_ref, c2s_ref,
                 ks_ref, vs_ref, kw_ref, vw_ref, o_ref, *, seq):
    qt = pl.program_id(2)
    s0 = qt * Q_TILE
    n_slc = seq // SLC_BLOCK
    n_cmp = kc_ref.shape[-2]
    scale = HEAD_DIM ** -0.5 * LOG2_E
    nt = (((1,), (1,)), ((), ()))
    heads = range(GQA_GROUP)

    cos_t, sin_t = cos_ref[...], sin_ref[...]
    qp_b, qr_b = [], []
    for g in heads:
        qg = q_ref[:, g * HEAD_DIM:(g + 1) * HEAD_DIM] * scale
        qp_b.append(qg.astype(BF16))
        qr_b.append(_rope(qg, cos_t, sin_t).astype(BF16))
    t_q = s0 + lax.broadcasted_iota(jnp.int32, (Q_TILE, 1), 0)

    n_idx = lax.broadcasted_iota(jnp.int32, (Q_TILE, n_cmp), 1)
    c_ok = (n_idx * CMP_STRIDE + (CMP_BLOCK - 1)) <= t_q
    c_bias = jnp.where(c_ok, 0.0, NEG_INF)
    c_keep = jnp.where(c_ok, 1.0, 0.0)
    o_c, p_sum = [], None
    for g in heads:
        s_c = lax.dot_general(qp_b[g], kc_ref[0, 0], nt, preferred_element_type=F32) + c_bias
        e_c = jnp.exp2(s_c - jnp.max(s_c, axis=-1, keepdims=True)) * c_keep
        l_c = jnp.sum(e_c, axis=-1, keepdims=True)
        p_c = e_c / jnp.where(l_c > 0.0, l_c, 1.0)
        o_c.append(jnp.dot(p_c.astype(BF16), vc_ref[0, 0], preferred_element_type=F32))
        p_sum = p_c if p_sum is None else p_sum + p_c
    p_hi = p_sum.astype(BF16)
    r1 = p_sum - p_hi.astype(F32)
    p_mid = r1.astype(BF16)
    p_lo = (r1 - p_mid.astype(F32)).astype(BF16)
    c2s = c2s_ref[...]
    imp_t = (lax.dot_general(c2s, p_hi, nt, preferred_element_type=F32)
             + lax.dot_general(c2s, p_mid, nt, preferred_element_type=F32)
             + lax.dot_general(c2s, p_lo, nt, preferred_element_type=F32))

    sid = lax.broadcasted_iota(jnp.int32, (n_slc, Q_TILE), 0)
    t_l = s0 + lax.broadcasted_iota(jnp.int32, (n_slc, Q_TILE), 1)
    cur = lax.shift_right_logical(t_l, int(math.log2(SLC_BLOCK)))
    causal_blk = sid <= cur
    forced = jnp.where(sid == 0, 1.0, jnp.where(sid == cur, 1.0, jnp.where(sid == cur - 1, 1.0, 0.0)))
    work = jnp.where(causal_blk, imp_t + FORCE_BONUS * forced, NEG_INF)
    sid_f = sid.astype(F32)
    sel_t = jnp.zeros((n_slc, Q_TILE), F32)
    for _ in range(min(N_SELECT, n_slc)):
        mx = jnp.max(work, axis=0, keepdims=True)
        first = jnp.min(jnp.where(work == mx, sid_f, float(n_slc)), axis=0, keepdims=True)
        pick = sid_f == first
        sel_t = jnp.where(pick, 1.0, sel_t)
        work = jnp.where(pick, -3.0e38, work)
    sel_t = jnp.where(causal_blk, sel_t, 0.0)
    sel_b = sel_t.T.astype(BF16)

    kt = 512
    blk_per_kt = kt // SLC_BLOCK

    def slc_body(j, carry):
        k0 = pl.multiple_of(j * kt, kt)
        kj = ks_ref[pl.ds(k0, kt), :]
        vj = vs_ref[pl.ds(k0, kt), :]
        e_s = lax.broadcasted_iota(jnp.int32, (n_slc, kt), 0)
        e_k = lax.broadcasted_iota(jnp.int32, (n_slc, kt), 1)
        expand = jnp.where(e_s == j * blk_per_kt + lax.shift_right_logical(e_k, int(math.log2(SLC_BLOCK))),
                           1.0, 0.0).astype(BF16)
        chosen = jnp.dot(sel_b, expand, preferred_element_type=F32)
        kpos = k0 + lax.broadcasted_iota(jnp.int32, (Q_TILE, kt), 1)
        bias = jnp.where(kpos <= t_q, jnp.where(chosen > 0.5, 0.0, NEG_INF), NEG_INF)
        out = []
        for g in heads:
            m, l, acc = carry[g]
            s = lax.dot_general(qr_b[g], kj, nt, preferred_element_type=F32) + bias
            m_new = jnp.maximum(m, jnp.max(s, axis=-1, keepdims=True))
            alpha = jnp.exp2(m - m_new)
            p = jnp.exp2(s - m_new)
            l = alpha * l + jnp.sum(p, axis=-1, keepdims=True)
            acc = alpha * acc + jnp.dot(p.astype(BF16), vj, preferred_element_type=F32)
            out.append((m_new, l, acc))
        return tuple(out)

    n_kt = lax.shift_right_logical(s0 + Q_TILE - 1, int(math.log2(kt))) + 1
    init = tuple((jnp.full((Q_TILE, 1), NEG_INF, F32), jnp.zeros((Q_TILE, 1), F32),
                  jnp.zeros((Q_TILE, HEAD_DIM), F32)) for _ in heads)
    slc = lax.fori_loop(0, n_kt, slc_body, init)

    slab = WINDOW + Q_TILE
    w0 = pl.multiple_of(jnp.maximum(s0 - WINDOW, 0), Q_TILE)
    k_w = kw_ref[pl.ds(w0, slab), :]
    v_w = vw_ref[pl.ds(w0, slab), :]
    dist = t_q - (w0 + lax.broadcasted_iota(jnp.int32, (Q_TILE, slab), 1))
    w_bias = jnp.where(dist >= 0, jnp.where(dist < WINDOW, 0.0, NEG_INF), NEG_INF)

    sg = _sigmoid(gate_ref[...])
    for g in heads:
        s_w = lax.dot_general(qr_b[g], k_w, nt, preferred_element_type=F32) + w_bias
        e_w = jnp.exp2(s_w - jnp.max(s_w, axis=-1, keepdims=True))
        o_w = (jnp.dot(e_w.astype(BF16), v_w, preferred_element_type=F32)
               / jnp.sum(e_w, axis=-1, keepdims=True))
        o_s = slc[g][2] / slc[g][1]
        o_ref[:, g * HEAD_DIM:(g + 1) * HEAD_DIM] = (
            sg[:, g:g + 1] * o_c[g]
            + sg[:, GQA_GROUP + g:GQA_GROUP + g + 1] * o_s
            + sg[:, 2 * GQA_GROUP + g:2 * GQA_GROUP + g + 1] * o_w)


def _cmp_to_slc_t(seq):
    n_cmp = seq // CMP_STRIDE - 1
    n_slc = seq // SLC_BLOCK
    cmp_start = jnp.arange(n_cmp) * CMP_STRIDE
    slc_start = jnp.arange(n_slc) * SLC_BLOCK
    overlap = jnp.clip(jnp.minimum(cmp_start[:, None] + CMP_BLOCK, slc_start[None, :] + SLC_BLOCK)
                       - jnp.maximum(cmp_start[:, None], slc_start[None, :]), 0, None)
    c2s = overlap.astype(F32) / CMP_BLOCK
    return jnp.pad(c2s, ((0, 1), (0, 0))).T.astype(BF16)


def _nsa_attention(q, gates, kc, vc, ks, vs, kw, vw, bsz, seq):
    m = q.shape[0]
    nq = seq // Q_TILE
    nc = seq // CMP_STRIDE
    cos_t, sin_t = _rope_tables(seq, 1)
    c2s_t = _cmp_to_slc_t(seq)
    gw = GQA_GROUP * HEAD_DIM
    kv_spec = pl.BlockSpec((seq, HEAD_DIM), lambda b, h, t: (b, h))
    cmp_spec = pl.BlockSpec((1, 1, nc, HEAD_DIM), lambda b, h, t: (b, h, 0, 0))
    tab_spec = pl.BlockSpec((Q_TILE, HEAD_DIM), lambda b, h, t: (t, 0))
    return pl.pallas_call(
        functools.partial(_attn_kernel, seq=seq),
        out_shape=jax.ShapeDtypeStruct((m, N_HEADS * HEAD_DIM), F32),
        grid=(bsz, N_KV_HEADS, nq),
        in_specs=[pl.BlockSpec((Q_TILE, gw), lambda b, h, t: (b * nq + t, h)),
                  pl.BlockSpec((Q_TILE, LANES), lambda b, h, t: (b * nq + t, h)),
                  tab_spec, tab_spec, cmp_spec, cmp_spec,
                  pl.BlockSpec(c2s_t.shape, lambda b, h, t: (0, 0)),
                  kv_spec, kv_spec, kv_spec, kv_spec],
        out_specs=pl.BlockSpec((Q_TILE, gw), lambda b, h, t: (b * nq + t, h)),
        compiler_params=_cparams(("parallel", "parallel", "arbitrary")),
        name="nsa_attention",
    )(q, gates, cos_t, sin_t, kc, vc, c2s_t, ks, vs, kw, vw)


def _s5_operators(a_re, a_im, log_dt, b_re, b_im, c_re, c_im):
    hp = lax.Precision.HIGHEST
    n_g = a_re.shape[0]
    gpb = SSM_GROUPS_PER_BLOCK
    n_gb = n_g // gpb
    t_c, hh, pp = SSM_CHUNK, SSM_GROUP, SSM_STATE
    lam = lax.complex(a_re.astype(F32), a_im.astype(F32))
    dt = jnp.exp(log_dt.astype(F32))[:, None]
    a_bar = jnp.exp(lam * dt)
    b_bar = ((a_bar - 1.0) / lam)[:, :, None] * lax.complex(b_re.astype(F32), b_im.astype(F32))
    c_mat = lax.complex(c_re.astype(F32), c_im.astype(F32))
    steps = jnp.arange(t_c + 1, dtype=F32)[:, None, None]
    pw = jnp.exp((lam * dt)[None] * steps)
    eye = jnp.eye(gpb, dtype=F32)
    k_lag = jnp.einsum('ghp,kgp,gpi->kgih', c_mat, pw[:t_c], b_bar, precision=hp).real
    w_intra = jnp.einsum('kbgio,gq->bgikqo', k_lag.reshape(t_c, n_gb, gpb, hh, hh), eye)
    w_intra = w_intra.reshape(n_gb, LANES, t_c * LANES)
    p_in = (pw[t_c - 1::-1][:, :, :, None] * b_bar[None]).reshape(t_c, n_gb, gpb, pp, hh)

    def s_in_plane(x):
        return jnp.einsum('tbgph,gq->btghqp', x, eye).reshape(n_gb, t_c, LANES, gpb * pp)

    s_in = jnp.concatenate([s_in_plane(p_in.real), s_in_plane(p_in.imag)], axis=-1)
    m_out = (c_mat[None] * pw[1:, :, None, :]).reshape(t_c, n_gb, gpb, hh, pp)

    def s_out_plane(x):
        return jnp.einsum('tbghp,gq->bgptqh', x, eye).reshape(n_gb, gpb * pp, t_c * LANES)

    s_out = jnp.concatenate([s_out_plane(m_out.real), s_out_plane(-m_out.imag)], axis=1)
    a_chunk = pw[t_c].reshape(n_gb, 1, gpb * pp)
    return (w_intra.astype(BF16), s_in.astype(BF16), s_out.astype(BF16),
            a_chunk.real.astype(F32), a_chunk.imag.astype(F32))


def _s5_kernel(u_ref, wi_ref, sin_ref, sout_ref, ar_ref, ai_ref, d_ref, o_ref, xin_ref, xprev_ref, y_ref):
    nc = u_ref.shape[0]
    t_c = SSM_CHUNK
    half = ar_ref.shape[-1]
    xin = jnp.zeros((nc, 2 * half), F32)
    for tau in range(t_c):
        xin += jnp.dot(u_ref[:, tau, :].astype(BF16), sin_ref[0, tau], preferred_element_type=F32)
    xin_ref[...] = xin
    a_r, a_i = ar_ref[0], ai_ref[0]

    def step(c, carry):
        x_r, x_i = carry
        xprev_ref[pl.ds(c, 1), :] = jnp.concatenate([x_r, x_i], axis=-1)
        row = xin_ref[pl.ds(c, 1), :]
        n_r = a_r * x_r - a_i * x_i + row[:, :half]
        n_i = a_r * x_i + a_i * x_r + row[:, half:]
        return n_r, n_i

    zero = jnp.zeros((1, half), F32)
    lax.fori_loop(0, nc, step, (zero, zero))
    y_ref[...] = jnp.dot(xprev_ref[...].astype(BF16), sout_ref[0], preferred_element_type=F32)
    for tau in range(t_c):
        width = (t_c - tau) * LANES
        y_ref[:, tau * LANES:] += jnp.dot(u_ref[:, tau, :].astype(BF16), wi_ref[0, :, :width],
                                          preferred_element_type=F32)
    for t in range(t_c):
        y = y_ref[:, t * LANES:(t + 1) * LANES] + d_ref[0] * u_ref[:, t, :]
        o_ref[:, t, :] = _gelu_tanh(y)


def _s5_scan(u3, bsz, seq, ops, d_skip):
    _, t_c, d_ssm = u3.shape
    w_intra, s_in, s_out, a_r, a_i = ops
    n_gb = d_ssm // LANES
    nc = seq // t_c
    st = s_in.shape[-1]
    return pl.pallas_call(
        _s5_kernel,
        out_shape=jax.ShapeDtypeStruct(u3.shape, F32),
        grid=(n_gb, bsz),
        in_specs=[pl.BlockSpec((nc, t_c, LANES), lambda g, b: (b, 0, g)),
                  pl.BlockSpec((1, LANES, t_c * LANES), lambda g, b: (g, 0, 0)),
                  pl.BlockSpec((1, t_c, LANES, st), lambda g, b: (g, 0, 0, 0)),
                  pl.BlockSpec((1, st, t_c * LANES), lambda g, b: (g, 0, 0)),
                  pl.BlockSpec((1, 1, st // 2), lambda g, b: (g, 0, 0)),
                  pl.BlockSpec((1, 1, st // 2), lambda g, b: (g, 0, 0)),
                  pl.BlockSpec((1, 1, LANES), lambda g, b: (g, 0, 0))],
        out_specs=pl.BlockSpec((nc, t_c, LANES), lambda g, b: (b, 0, g)),
        scratch_shapes=[pltpu.VMEM((nc, st), F32), pltpu.VMEM((nc, st), F32),
                        pltpu.VMEM((nc, t_c * LANES), F32)],
        compiler_params=_cparams(("parallel", "parallel")),
        name="s5_scan",
    )(u3, w_intra, s_in, s_out, a_r, a_i, d_skip.reshape(n_gb, 1, LANES))


def _mixnorm_kernel(a_ref, s_ref, ga_ref, gs_ref, o_ref):
    da = a_ref.shape[-1]
    a = a_ref[...]
    s = s_ref[...]
    o_ref[:, :da] = (a * lax.rsqrt(jnp.mean(a * a, axis=-1, keepdims=True) + EPS) * ga_ref[...]).astype(BF16)
    o_ref[:, da:] = (s * lax.rsqrt(jnp.mean(s * s, axis=-1, keepdims=True) + EPS) * gs_ref[...]).astype(BF16)


def _mixnorm(attn, ssm, g_attn, g_ssm, tm=256):
    m, da = attn.shape
    ds = ssm.shape[1]
    return pl.pallas_call(
        _mixnorm_kernel,
        out_shape=jax.ShapeDtypeStruct((m, da + ds), BF16),
        grid=(m // tm,),
        in_specs=[pl.BlockSpec((tm, da), lambda i: (i, 0)), pl.BlockSpec((tm, ds), lambda i: (i, 0)),
                  pl.BlockSpec((1, da), lambda i: (0, 0)), pl.BlockSpec((1, ds), lambda i: (0, 0))],
        out_specs=pl.BlockSpec((tm, da + ds), lambda i: (i, 0)),
        compiler_params=_cparams(("parallel",)),
        name="mixnorm",
    )(attn, ssm, g_attn.reshape(1, da), g_ssm.reshape(1, ds))


def _split_bf16(x):
    hi = x.astype(BF16)
    return hi, (x - hi.astype(F32)).astype(BF16)


def _lane_max(x):
    return jnp.max(x, axis=-1, keepdims=True)


def _first_lane_of(cond_val, lane_f):
    return jnp.min(jnp.where(cond_val > 0.0, lane_f, float(LANES)), axis=-1, keepdims=True)


def _ffn_norm_router_kernel(h_ref, g_ref, whi_ref, wlo_ref, b_ref, hn_ref, route_ref):
    x = h_ref[...]
    hn = x * lax.rsqrt(jnp.mean(x * x, axis=-1, keepdims=True) + EPS) * g_ref[...]
    hn_ref[...] = hn
    hi, lo = _split_bf16(hn)
    logits = (jnp.dot(hi, whi_ref[...], preferred_element_type=F32)
              + jnp.dot(lo, whi_ref[...], preferred_element_type=F32)
              + jnp.dot(hi, wlo_ref[...], preferred_element_type=F32)) + b_ref[...]
    lane = lax.broadcasted_iota(jnp.int32, logits.shape, 1)
    lane_f = lane.astype(F32)
    in_c = jnp.where(lane < N_EXPERT_GROUPS, 1.0, 0.0)
    c_l = jnp.where(in_c > 0.0, logits, NEG_INF)
    c_e = jnp.exp(c_l - _lane_max(c_l)) * in_c
    p_c = c_e / jnp.sum(c_e, axis=-1, keepdims=True)
    w_grp = _lane_max(p_c)
    grp = _first_lane_of(jnp.where(p_c == w_grp, in_c, 0.0), lane_f)
    lo_lane = float(N_EXPERT_GROUPS) + grp * float(EXPERTS_PER_GROUP)
    in_f = jnp.where(lane_f >= lo_lane, jnp.where(lane_f < lo_lane + float(EXPERTS_PER_GROUP), 1.0, 0.0), 0.0)
    f_l = jnp.where(in_f > 0.0, logits, NEG_INF)
    f_e = jnp.exp(f_l - _lane_max(f_l)) * in_f
    p_f = f_e / jnp.sum(f_e, axis=-1, keepdims=True)
    p_f = jnp.where(in_f > 0.0, p_f, -1.0)
    p1 = _lane_max(p_f)
    i1 = _first_lane_of(jnp.where(p_f == p1, in_f, 0.0), lane_f)
    rest = jnp.where(lane_f == i1, -1.0, p_f)
    p2 = _lane_max(rest)
    i2 = _first_lane_of(jnp.where(rest == p2, in_f, 0.0) * jnp.where(lane_f == i1, 0.0, 1.0), lane_f)
    denom = p1 + p2
    vals = (w_grp * p1 / denom, w_grp * p2 / denom,
            i1 - float(N_EXPERT_GROUPS), i2 - float(N_EXPERT_GROUPS))
    out = jnp.zeros(logits.shape, F32)
    for k, v in enumerate(vals):
        out = jnp.where(lane == k, v, out)
    route_ref[...] = out


def _ffn_norm_router(h, g, w_router, b_router, tm=256):
    m, d = h.shape
    n = w_router.shape[1]
    w_hi, w_lo = _split_bf16(w_router)
    return pl.pallas_call(
        _ffn_norm_router_kernel,
        out_shape=(jax.ShapeDtypeStruct((m, d), F32), jax.ShapeDtypeStruct((m, n), F32)),
        grid=(m // tm,),
        in_specs=[pl.BlockSpec((tm, d), lambda i: (i, 0)), pl.BlockSpec((1, d), lambda i: (0, 0)),
                  pl.BlockSpec((d, n), lambda i: (0, 0)), pl.BlockSpec((d, n), lambda i: (0, 0)),
                  pl.BlockSpec((1, n), lambda i: (0, 0))],
        out_specs=(pl.BlockSpec((tm, d), lambda i: (i, 0)), pl.BlockSpec((tm, n), lambda i: (i, 0))),
        compiler_params=_cparams(("parallel",)),
        name="ffn_norm_router",
    )(h, g.reshape(1, d), w_hi, w_lo, b_router.reshape(1, n))


def _dispatch_plan(expert, tm):
    n_tok = expert.shape[0]
    n_assign = n_tok * TOP_K
    n_blocks = -(-(n_assign + N_EXPERTS * (MOE_BLOCK - 1)) // MOE_BLOCK)
    n_rows = n_blocks * MOE_BLOCK
    flat = expert.reshape(n_assign)
    ids = jnp.arange(n_assign, dtype=jnp.int32)
    _, order = lax.sort_key_val(flat, ids)
    _, inv = lax.sort_key_val(order, ids)
    e_ids = jnp.arange(N_EXPERTS, dtype=jnp.int32)
    counts = jnp.sum((flat[:, None] == e_ids[None, :]).astype(jnp.int32), axis=0)
    start = jnp.cumsum(counts) - counts
    padded = (counts + MOE_BLOCK - 1) // MOE_BLOCK * MOE_BLOCK
    pad_end = jnp.cumsum(padded)
    pad_start = pad_end - padded
    rows = jnp.arange(n_rows, dtype=jnp.int32)
    row_e = jnp.minimum(jnp.sum((pad_end[None, :] <= rows[:, None]).astype(jnp.int32), axis=1), N_EXPERTS - 1)
    slot = rows - pad_start[row_e]
    valid = (slot < counts[row_e]) & (rows < pad_end[-1])
    src = jnp.clip(start[row_e] + slot, 0, n_assign - 1)
    buf_tok = jnp.where(valid, order[src] // TOP_K, 0).astype(jnp.int32)
    dest = (pad_start[flat] + inv - start[flat]).astype(jnp.int32)
    dest = dest.reshape(n_tok // tm, tm, TOP_K).transpose(0, 2, 1).reshape(-1)
    plan = jnp.concatenate([pad_start // MOE_BLOCK, padded // MOE_BLOCK,
                            pad_end[-1:] // MOE_BLOCK]).astype(jnp.int32)
    return plan, buf_tok, dest, n_blocks


def _zero_fill_spare(plan_ref, zero_vmem, out_hbm, col0, width, sem, n_blocks):
    n_used = plan_ref[2 * N_EXPERTS]

    def cp(blk):
        return pltpu.make_async_copy(
            zero_vmem, out_hbm.at[pl.ds(pl.multiple_of(blk * MOE_BLOCK, MOE_BLOCK), MOE_BLOCK), pl.ds(col0, width)],
            sem)

    def start(blk, c):
        cp(blk).start()
        return c

    def wait(blk, c):
        cp(blk).wait()
        return c

    lax.fori_loop(n_used, n_blocks, start, 0)
    lax.fori_loop(n_used, n_blocks, wait, 0)


def _moe_up_kernel(plan_ref, tok_ref, x_hbm, wg_ref, wu_ref, act_hbm, wg_b, wu_b, xbuf, obuf, gsem, osem,
                   *, n_blocks):
    e = pl.program_id(0)
    c = pl.program_id(1)
    tn = wg_b.shape[1]
    col0 = pl.multiple_of(c * tn, tn)
    b0 = plan_ref[e]
    nb = plan_ref[N_EXPERTS + e]

    def row_copy(blk, r, slot):
        tok = tok_ref[blk * MOE_BLOCK + r]
        return pltpu.make_async_copy(x_hbm.at[pl.ds(tok, 1), :], xbuf.at[slot, pl.ds(r, 1), :], gsem.at[slot])

    def gather_start(blk, slot):
        lax.fori_loop(0, MOE_BLOCK, lambda r, cr: (row_copy(blk, r, slot).start(), cr)[1], 0, unroll=DMA_UNROLL)

    def gather_wait(blk, slot):
        lax.fori_loop(0, MOE_BLOCK, lambda r, cr: (row_copy(blk, r, slot).wait(), cr)[1], 0, unroll=DMA_UNROLL)

    def out_copy(blk, slot):
        rows = pl.ds(pl.multiple_of(blk * MOE_BLOCK, MOE_BLOCK), MOE_BLOCK)
        return pltpu.make_async_copy(obuf.at[slot], act_hbm.at[rows, pl.ds(col0, tn)], osem.at[slot])

    @pl.when(nb > 0)
    def _():
        gather_start(b0, 0)
        wg_b[...] = wg_ref[0].astype(BF16)
        wu_b[...] = wu_ref[0].astype(BF16)

    def body(j, carry):
        slot = lax.rem(j, 2)
        gather_wait(b0 + j, slot)

        @pl.when(j + 1 < nb)
        def _():
            gather_start(b0 + j + 1, 1 - slot)

        x = xbuf[slot].astype(BF16)
        g = jnp.dot(x, wg_b[...], preferred_element_type=F32)
        u = jnp.dot(x, wu_b[...], preferred_element_type=F32)

        @pl.when(j >= 2)
        def _():
            out_copy(b0 + j - 2, slot).wait()

        obuf[slot] = (g * _sigmoid(g) * u).astype(BF16)
        out_copy(b0 + j, slot).start()
        return carry

    lax.fori_loop(0, nb, body, 0)

    @pl.when(nb >= 2)
    def _():
        out_copy(b0 + nb - 2, lax.rem(nb, 2)).wait()

    @pl.when(nb >= 1)
    def _():
        out_copy(b0 + nb - 1, lax.rem(nb + 1, 2)).wait()

    @pl.when(e == N_EXPERTS - 1)
    def _():
        obuf[0] = jnp.zeros(obuf.shape[1:], obuf.dtype)
        _zero_fill_spare(plan_ref, obuf.at[0], act_hbm, col0, tn, osem.at[0], n_blocks)


def _moe_up(hn, w_gate, w_up, plan, buf_tok, n_blocks, n_chunks=2):
    d = hn.shape[1]
    d_e = w_gate.shape[-1]
    tn = d_e // n_chunks
    w_spec = pl.BlockSpec((1, d, tn), lambda e, c, plan, tok: (e, 0, c))
    return pl.pallas_call(
        functools.partial(_moe_up_kernel, n_blocks=n_blocks),
        out_shape=jax.ShapeDtypeStruct((n_blocks * MOE_BLOCK, d_e), BF16),
        grid_spec=pltpu.PrefetchScalarGridSpec(
            num_scalar_prefetch=2,
            grid=(N_EXPERTS, n_chunks),
            in_specs=[pl.BlockSpec(memory_space=pl.ANY), w_spec, w_spec],
            out_specs=pl.BlockSpec(memory_space=pl.ANY),
            scratch_shapes=[pltpu.VMEM((d, tn), BF16), pltpu.VMEM((d, tn), BF16),
                            pltpu.VMEM((2, MOE_BLOCK, d), F32), pltpu.VMEM((2, MOE_BLOCK, tn), BF16),
                            pltpu.SemaphoreType.DMA((2,)), pltpu.SemaphoreType.DMA((2,))]),
        compiler_params=_cparams(("arbitrary", "arbitrary")),
        name="moe_up",
    )(plan, buf_tok, hn, w_gate, w_up)


def _moe_down_kernel(plan_ref, act_hbm, wd_ref, y_hbm, wd_b, abuf, obuf, asem, osem, *, n_blocks):
    e = pl.program_id(0)
    c = pl.program_id(1)
    tn = wd_b.shape[1]
    col0 = pl.multiple_of(c * tn, tn)
    b0 = plan_ref[e]
    nb = plan_ref[N_EXPERTS + e]

    def rows_of(blk):
        return pl.ds(pl.multiple_of(blk * MOE_BLOCK, MOE_BLOCK), MOE_BLOCK)

    def in_copy(blk, slot):
        return pltpu.make_async_copy(act_hbm.at[rows_of(blk), :], abuf.at[slot], asem.at[slot])

    def out_copy(blk, slot):
        return pltpu.make_async_copy(obuf.at[slot], y_hbm.at[rows_of(blk), pl.ds(col0, tn)], osem.at[slot])

    @pl.when(nb > 0)
    def _():
        in_copy(b0, 0).start()
        wd_b[...] = wd_ref[0].astype(BF16)

    def body(j, carry):
        slot = lax.rem(j, 2)
        in_copy(b0 + j, slot).wait()

        @pl.when(j + 1 < nb)
        def _():
            in_copy(b0 + j + 1, 1 - slot).start()

        y = jnp.dot(abuf[slot], wd_b[...], preferred_element_type=F32)

        @pl.when(j >= 2)
        def _():
            out_copy(b0 + j - 2, slot).wait()

        obuf[slot] = y
        out_copy(b0 + j, slot).start()
        return carry

    lax.fori_loop(0, nb, body, 0)

    @pl.when(nb >= 2)
    def _():
        out_copy(b0 + nb - 2, lax.rem(nb, 2)).wait()

    @pl.when(nb >= 1)
    def _():
        out_copy(b0 + nb - 1, lax.rem(nb + 1, 2)).wait()

    @pl.when(e == N_EXPERTS - 1)
    def _():
        obuf[0] = jnp.zeros(obuf.shape[1:], obuf.dtype)
        _zero_fill_spare(plan_ref, obuf.at[0], y_hbm, col0, tn, osem.at[0], n_blocks)


def _moe_down(act, w_down, plan, n_blocks, n_chunks=2):
    n_rows, d_e = act.shape
    d = w_down.shape[-1]
    tn = d // n_chunks
    return pl.pallas_call(
        functools.partial(_moe_down_kernel, n_blocks=n_blocks),
        out_shape=jax.ShapeDtypeStruct((n_rows, d), F32),
        grid_spec=pltpu.PrefetchScalarGridSpec(
            num_scalar_prefetch=1,
            grid=(N_EXPERTS, n_chunks),
            in_specs=[pl.BlockSpec(memory_space=pl.ANY),
                      pl.BlockSpec((1, d_e, tn), lambda e, c, plan: (e, 0, c))],
            out_specs=pl.BlockSpec(memory_space=pl.ANY),
            scratch_shapes=[pltpu.VMEM((d_e, tn), BF16),
                            pltpu.VMEM((2, MOE_BLOCK, d_e), BF16), pltpu.VMEM((2, MOE_BLOCK, tn), F32),
                            pltpu.SemaphoreType.DMA((2,)), pltpu.SemaphoreType.DMA((2,))]),
        compiler_params=_cparams(("arbitrary", "arbitrary")),
        name="moe_down",
    )(plan, act, w_down)


def _combine_kernel(dest_ref, h_ref, route_ref, gn_ref, y_hbm, o_ref, rows_ref, sem):
    tm = h_ref.shape[0]
    n_rows = tm * TOP_K
    i = pl.program_id(0)
    n = pl.num_programs(0)

    def row_copy(step, r, slot):
        src = dest_ref[step * n_rows + r]
        return pltpu.make_async_copy(y_hbm.at[pl.ds(src, 1), :], rows_ref.at[slot, pl.ds(r, 1), :], sem.at[slot])

    def gather_start(step, slot):
        lax.fori_loop(0, n_rows, lambda r, cr: (row_copy(step, r, slot).start(), cr)[1], 0, unroll=DMA_UNROLL)

    def gather_wait(step, slot):
        lax.fori_loop(0, n_rows, lambda r, cr: (row_copy(step, r, slot).wait(), cr)[1], 0, unroll=DMA_UNROLL)

    slot = lax.rem(i, 2)

    @pl.when(i == 0)
    def _():
        gather_start(0, 0)

    @pl.when(i + 1 < n)
    def _():
        gather_start(i + 1, 1 - slot)

    gather_wait(i, slot)
    route = route_ref[...]
    rows = rows_ref[slot]
    h = h_ref[...] + route[:, 0:1] * rows[0:tm, :] + route[:, 1:2] * rows[tm:2 * tm, :]
    o_ref[...] = h * lax.rsqrt(jnp.mean(h * h, axis=-1, keepdims=True) + EPS) * gn_ref[...]


def _combine_final(h, route, dest, y_buf, g_final, tm=128):
    m, d = h.shape
    return pl.pallas_call(
        _combine_kernel,
        out_shape=jax.ShapeDtypeStruct((m, d), F32),
        grid_spec=pltpu.PrefetchScalarGridSpec(
            num_scalar_prefetch=1,
            grid=(m // tm,),
            in_specs=[pl.BlockSpec((tm, d), lambda i, dst: (i, 0)),
                      pl.BlockSpec((tm, LANES), lambda i, dst: (i, 0)),
                      pl.BlockSpec((1, d), lambda i, dst: (0, 0)),
                      pl.BlockSpec(memory_space=pl.ANY)],
            out_specs=pl.BlockSpec((tm, d), lambda i, dst: (i, 0)),
            scratch_shapes=[pltpu.VMEM((2, TOP_K * tm, d), F32), pltpu.SemaphoreType.DMA((2,))]),
        compiler_params=_cparams(("arbitrary",)),
        name="moe_combine_final",
    )(dest, h, route, g_final.reshape(1, d), y_buf)


def _gate_column_order():
    cols = []
    for hk in range(N_KV_HEADS):
        blk = [-1] * LANES
        for br in range(N_BRANCHES):
            for g in range(GQA_GROUP):
                blk[br * GQA_GROUP + g] = (hk * GQA_GROUP + g) * N_BRANCHES + br
        cols.extend(blk)
    return cols


def kernel(x, norm_mix, w_in, cmp_pe_k, cmp_w1_k, cmp_b1_k, cmp_w2_k, cmp_b2_k, cmp_pe_v, cmp_w1_v, cmp_b1_v, cmp_w2_v, cmp_b2_v, ssm_a_re, ssm_a_im, ssm_log_dt, ssm_b_re, ssm_b_im, ssm_c_re, ssm_c_im, ssm_d, ssm_w_glu, ssm_b_glu, norm_attn_out, norm_ssm_out, w_out, norm_ffn, router_w_coarse, router_b_coarse, router_w_fine, router_b_fine, w_gate, w_up, w_down, norm_final):
    bsz, seq, d_model = x.shape
    m = bsz * seq
    depth = w_in.shape[0]
    assert depth == 1, "the combine kernel fuses the final norm; stacked layers are not supported"
    layer = 0
    q_w = N_HEADS * HEAD_DIM
    kv_w = N_KV_HEADS * HEAD_DIM
    gate_w = N_HEADS * N_BRANCHES
    cut1, cut2 = q_w, q_w + 6 * kv_w
    cut3 = cut2 + gate_w
    h = x.astype(F32).reshape(m, d_model)
    gate_cols = jnp.array([c if c >= 0 else 0 for c in _gate_column_order()], jnp.int32)
    gate_keep = jnp.array([1.0 if c >= 0 else 0.0 for c in _gate_column_order()], F32)

    w_in_b = w_in[layer].astype(BF16)
    hn = _rmsnorm(h, norm_mix[layer], BF16)
    q = _matmul(hn, w_in_b[:, :cut1], name="proj_q")
    kv3 = _matmul(hn, w_in_b[:, cut1:cut2], fold_out=True, name="proj_kv")
    w_gate_cols = (w_in[layer][:, cut2:cut3][:, gate_cols] * gate_keep[None, :]).astype(BF16)
    gates = _matmul(hn, w_gate_cols, tn=512, name="proj_gate")
    u3 = _matmul(hn, w_in_b[:, cut3:], fold_out=True, name="proj_u")

    stack2 = lambda a, b: jnp.stack([a[layer], b[layer]])
    cmp_kv = _compress(kv3, bsz, seq, stack2(cmp_pe_k, cmp_pe_v),
                       stack2(cmp_w1_k, cmp_w1_v).astype(BF16), stack2(cmp_b1_k, cmp_b1_v),
                       stack2(cmp_w2_k, cmp_w2_v).astype(BF16), stack2(cmp_b2_k, cmp_b2_v))
    ks, vs, kw, vw = _kv_prep(kv3, seq)
    attn = _nsa_attention(q, gates, cmp_kv[0], cmp_kv[1], ks, vs, kw, vw, bsz, seq)

    ops = _s5_operators(ssm_a_re[layer], ssm_a_im[layer], ssm_log_dt[layer], ssm_b_re[layer],
                        ssm_b_im[layer], ssm_c_re[layer], ssm_c_im[layer])
    y3 = _s5_scan(u3, bsz, seq, ops, ssm_d[layer].astype(F32))
    ssm = _glu_matmul(y3, ssm_w_glu[layer].astype(BF16), ssm_b_glu[layer].astype(F32))

    mixed = _mixnorm(attn, ssm, norm_attn_out[layer], norm_ssm_out[layer])
    h = _matmul(mixed, w_out[layer].astype(BF16), residual=h, name="out_proj")

    n_r = N_EXPERT_GROUPS + N_EXPERTS
    w_router = jnp.pad(jnp.concatenate([router_w_coarse[layer], router_w_fine[layer]], axis=1),
                       ((0, 0), (0, LANES - n_r))).astype(F32)
    b_router = jnp.pad(jnp.concatenate([router_b_coarse[layer], router_b_fine[layer]]), (0, LANES - n_r))
    hn2, route = _ffn_norm_router(h, norm_ffn[layer], w_router, b_router.astype(F32))
    expert = route[:, 2:2 + TOP_K].astype(jnp.int32)
    tm_c = 128
    plan, buf_tok, dest, n_blocks = _dispatch_plan(expert, tm_c)
    act = _moe_up(hn2, w_gate[layer], w_up[layer], plan, buf_tok, n_blocks)
    y_buf = _moe_down(act, w_down[layer], plan, n_blocks)
    out = _combine_final(h, route, dest, y_buf, norm_final, tm=tm_c)
    return out.reshape(bsz, seq, d_model).astype(x.dtype)
```

```python
import functools
import math

import jax
import jax.numpy as jnp
from jax import lax
from jax.experimental import pallas as pl
from jax.experimental.pallas import tpu as pltpu

F32 = jnp.float32
BF16 = jnp.bfloat16

HEAD_DIM = 128
N_HEADS = 16
N_KV_HEADS = 4
GQA_GROUP = N_HEADS // N_KV_HEADS
ROT_DIM = HEAD_DIM // 4
ROPE_THETA = 500000.0
CMP_STRIDE = 16
CMP_BLOCK = 32
SLC_BLOCK = 64
N_SELECT = 16
WINDOW = 512
Q_TILE = 128
N_BRANCHES = 3
FORCE_BONUS = 1.0e4
NEG_INF = -1.0e30
MASK_BIG = 1.0e30
SSM_GROUP = 16
SSM_STATE = 64
SSM_CHUNK = 16
SSM_GROUPS_PER_BLOCK = 128 // SSM_GROUP
N_EXPERT_GROUPS = 8
EXPERTS_PER_GROUP = 8
N_EXPERTS = N_EXPERT_GROUPS * EXPERTS_PER_GROUP
TOP_K = 2
MOE_BLOCK = 128
EPS = 1.0e-6
LOG2_E = 1.4426950408889634

V7X_VMEM_LIMIT = 56 * 1024 * 1024
LANES = 128
SCAN_GROUP = 8
SCAN_TABLE_ROWS = 16
DMA_UNROLL = 8
SIDE_DMA_PRIORITY = 1


def _cparams(sem, vmem=V7X_VMEM_LIMIT):
    return pltpu.CompilerParams(dimension_semantics=sem, vmem_limit_bytes=vmem)


def _gelu_tanh(x):
    return 0.5 * x * (1.0 + jnp.tanh(math.sqrt(2.0 / math.pi) * (x + 0.044715 * (x * x * x))))


def _sigmoid(x):
    return 1.0 / (1.0 + jnp.exp(-x))


def _rmsnorm_kernel(x_ref, g_ref, o_ref):
    x = x_ref[...]
    ms = jnp.mean(x * x, axis=-1, keepdims=True)
    o_ref[...] = (x * lax.rsqrt(ms + EPS) * g_ref[...]).astype(o_ref.dtype)


def _rmsnorm(x, g, out_dtype, tm=256):
    m, d = x.shape
    return pl.pallas_call(
        _rmsnorm_kernel,
        out_shape=jax.ShapeDtypeStruct((m, d), out_dtype),
        grid=(m // tm,),
        in_specs=[pl.BlockSpec((tm, d), lambda i: (i, 0)),
                  pl.BlockSpec((1, d), lambda i: (0, 0))],
        out_specs=pl.BlockSpec((tm, d), lambda i: (i, 0)),
        compiler_params=_cparams(("parallel",)),
        name="rmsnorm",
    )(x, g.reshape(1, d))


def _mm_kernel(a_ref, w_ref, o_ref):
    o_ref[...] = jnp.dot(a_ref[...], w_ref[...], preferred_element_type=F32)


def _mm_glu_kernel(y_ref, w_ref, b_ref, yg_ref, o_ref):
    z = jnp.dot(y_ref[...].astype(BF16), w_ref[...], preferred_element_type=F32) + b_ref[...]
    o_ref[...] = yg_ref[...] * _sigmoid(z)


def _mm_residual_kernel(a_ref, w_ref, r_ref, o_ref):
    o_ref[...] = r_ref[...] + jnp.dot(a_ref[...], w_ref[...], preferred_element_type=F32)


def _matmul(a, w, *, residual=None, tm=512, tn=1024, name="matmul"):
    m, k = a.shape
    n = w.shape[1]
    tn = min(tn, n)
    assert m % tm == 0 and n % tn == 0
    a_spec = pl.BlockSpec((tm, k), lambda j, i: (i, 0))
    w_spec = pl.BlockSpec((k, tn), lambda j, i: (0, j))
    o_spec = pl.BlockSpec((tm, tn), lambda j, i: (i, j))
    if residual is not None:
        kern, ins, specs = _mm_residual_kernel, (a, w, residual), [a_spec, w_spec, o_spec]
    else:
        kern, ins, specs = _mm_kernel, (a, w), [a_spec, w_spec]
    return pl.pallas_call(
        kern,
        out_shape=jax.ShapeDtypeStruct((m, n), F32),
        grid=(n // tn, m // tm),
        in_specs=specs,
        out_specs=o_spec,
        compiler_params=_cparams(("parallel", "parallel")),
        name=name,
    )(*ins)


def _glu_matmul(y, w, bias, tm=512, tn=1024):
    m, k = y.shape
    n = w.shape[1]
    return pl.pallas_call(
        _mm_glu_kernel,
        out_shape=jax.ShapeDtypeStruct((m, n), F32),
        grid=(n // tn, m // tm),
        in_specs=[pl.BlockSpec((tm, k), lambda j, i: (i, 0)),
                  pl.BlockSpec((k, tn), lambda j, i: (0, j)),
                  pl.BlockSpec((1, tn), lambda j, i: (0, j)),
                  pl.BlockSpec((tm, tn), lambda j, i: (i, j))],
        out_specs=pl.BlockSpec((tm, tn), lambda j, i: (i, j)),
        compiler_params=_cparams(("parallel", "parallel")),
        name="ssm_glu",
    )(y, w, bias.reshape(1, n), y)


def _rope_tables(seq, reps):
    half = ROT_DIM // 2
    inv_freq = ROPE_THETA ** (-jnp.arange(half, dtype=F32) / half)
    ang = jnp.arange(seq).astype(F32)[:, None] * inv_freq[None, :]
    cos, sin = jnp.cos(ang), jnp.sin(ang)
    ones = jnp.ones((seq, HEAD_DIM - ROT_DIM), F32)
    cos_t = jnp.concatenate([cos, cos, ones], axis=-1)
    sin_t = jnp.concatenate([-sin, sin, 0.0 * ones], axis=-1)
    return jnp.tile(cos_t, (1, reps)), jnp.tile(sin_t, (1, reps))


def _rope(x, cos_t, sin_t):
    half = ROT_DIM // 2
    n = x.shape[-1]
    lane = jnp.bitwise_and(lax.broadcasted_iota(jnp.int32, x.shape, x.ndim - 1), HEAD_DIM - 1)
    partner = jnp.where(lane < half, pltpu.roll(x, n - half, x.ndim - 1), pltpu.roll(x, half, x.ndim - 1))
    return x * cos_t + partner * sin_t


def _kv_prep_kernel(slc_ref, win_ref, cos_ref, sin_ref, ks_ref, vs_ref, kw_ref, vw_ref, *, tiles_per_seq):
    tm, w = vs_ref.shape
    cos_t, sin_t = cos_ref[...], sin_ref[...]
    slc = slc_ref[...]
    win = win_ref[...]
    k_slc = _rope(slc[:, :w], cos_t, sin_t).astype(BF16)
    pos = lax.rem(pl.program_id(0), tiles_per_seq) * tm + lax.broadcasted_iota(jnp.int32, (tm, HEAD_DIM), 0)
    lane = lax.broadcasted_iota(jnp.int32, (tm, HEAD_DIM), 1)
    own_blk = jnp.where(lane == lax.shift_right_logical(pos, int(math.log2(SLC_BLOCK))), MASK_BIG, 0.0).astype(BF16)
    for hk in range(N_KV_HEADS):
        ks_ref[:, 2 * hk * HEAD_DIM:(2 * hk + 1) * HEAD_DIM] = k_slc[:, hk * HEAD_DIM:(hk + 1) * HEAD_DIM]
        ks_ref[:, (2 * hk + 1) * HEAD_DIM:(2 * hk + 2) * HEAD_DIM] = own_blk
    vs_ref[...] = slc[:, w:].astype(BF16)
    kw_ref[...] = _rope(win[:, :w], cos_t, sin_t).astype(BF16)
    vw_ref[...] = win[:, w:].astype(BF16)


def _kv_prep(kv, seq, tm=256):
    assert seq // SLC_BLOCK <= HEAD_DIM
    m = kv.shape[0]
    w = N_KV_HEADS * HEAD_DIM
    cos_t, sin_t = _rope_tables(seq, N_KV_HEADS)
    tpb = seq // tm
    out = jax.ShapeDtypeStruct((m, w), BF16)
    o_spec = pl.BlockSpec((tm, w), lambda i: (i, 0))
    t_spec = pl.BlockSpec((tm, w), lambda i: (i % tpb, 0))
    return pl.pallas_call(
        functools.partial(_kv_prep_kernel, tiles_per_seq=tpb),
        out_shape=(jax.ShapeDtypeStruct((m, 2 * w), BF16), out, out, out),
        grid=(m // tm,),
        in_specs=[pl.BlockSpec((tm, 2 * w), lambda i: (i, 1)),
                  pl.BlockSpec((tm, 2 * w), lambda i: (i, 2)),
                  t_spec, t_spec],
        out_specs=(pl.BlockSpec((tm, 2 * w), lambda i: (i, 0)), o_spec, o_spec, o_spec),
        compiler_params=_cparams(("parallel",)),
        name="kv_prep",
    )(kv, kv, cos_t, sin_t)


def _compress_kernel(x_ref, pe_ref, w1_ref, b1_ref, w2_ref, b2_ref, o_ref):
    nc = x_ref.shape[0] // CMP_STRIDE
    hid = w1_ref.shape[-1]
    top = jnp.zeros((nc, hid), F32)
    bot = jnp.zeros((nc, hid), F32)
    for j in range(CMP_STRIDE):
        xj = x_ref[pl.ds(j, nc, stride=CMP_STRIDE), :]
        top += jnp.dot((xj + pe_ref[0, j:j + 1, :]).astype(BF16), w1_ref[0, j],
                       preferred_element_type=F32)
        bot += jnp.dot((xj + pe_ref[0, CMP_STRIDE + j:CMP_STRIDE + j + 1, :]).astype(BF16),
                       w1_ref[0, CMP_STRIDE + j], preferred_element_type=F32)
    h = _gelu_tanh(top + pltpu.roll(bot, nc - 1, 0) + b1_ref[0])
    o_ref[0, 0, 0] = (jnp.dot(h.astype(BF16), w2_ref[0], preferred_element_type=F32) + b2_ref[0]).astype(BF16)


def _compress(kv, bsz, seq, pe, w1, b1, w2, b2):
    nc = seq // CMP_STRIDE
    hid = w1.shape[-1]
    w1r = w1.reshape(2, CMP_BLOCK, HEAD_DIM, hid)
    return pl.pallas_call(
        _compress_kernel,
        out_shape=jax.ShapeDtypeStruct((2, bsz, N_KV_HEADS, nc, HEAD_DIM), BF16),
        grid=(2, bsz, N_KV_HEADS),
        in_specs=[pl.BlockSpec((seq, HEAD_DIM), lambda s, b, h: (b, s * N_KV_HEADS + h)),
                  pl.BlockSpec((1, CMP_BLOCK, HEAD_DIM), lambda s, b, h: (s, 0, 0)),
                  pl.BlockSpec((1, CMP_BLOCK, HEAD_DIM, hid), lambda s, b, h: (s, 0, 0, 0)),
                  pl.BlockSpec((1, 1, hid), lambda s, b, h: (s, 0, 0)),
                  pl.BlockSpec((1, hid, HEAD_DIM), lambda s, b, h: (s, 0, 0)),
                  pl.BlockSpec((1, 1, HEAD_DIM), lambda s, b, h: (s, 0, 0))],
        out_specs=pl.BlockSpec((1, 1, 1, nc, HEAD_DIM), lambda s, b, h: (s, b, h, 0, 0)),
        compiler_params=_cparams(("parallel", "parallel", "parallel")),
        name="nsa_compress",
    )(kv, pe, w1r, b1.reshape(2, 1, hid), w2, b2.reshape(2, 1, HEAD_DIM))


def _attn_kernel(q_ref, gate_ref, cos_ref, sin_ref, kc_ref, vc_ref, c2s_ref,
                 ks_ref, vs_ref, kw_ref, vw_ref, o_ref, *, seq):
    qt = pl.program_id(2)
    s0 = qt * Q_TILE
    n_slc = seq // SLC_BLOCK
    n_cmp = kc_ref.shape[-2]
    scale = HEAD_DIM ** -0.5 * LOG2_E
    rows = GQA_GROUP * Q_TILE
    nt = (((1,), (1,)), ((), ()))

    def tile4(x):
        return jnp.concatenate([x] * GQA_GROUP, axis=0)

    q4 = q_ref[...]
    qp = jnp.concatenate([q4[:, g * HEAD_DIM:(g + 1) * HEAD_DIM] for g in range(GQA_GROUP)], axis=0) * scale
    qr_b = _rope(qp, tile4(cos_ref[...]), tile4(sin_ref[...])).astype(BF16)
    qp_b = qp.astype(BF16)
    t_q = s0 + lax.broadcasted_iota(jnp.int32, (Q_TILE, 1), 0)

    n_idx = lax.broadcasted_iota(jnp.int32, (Q_TILE, n_cmp), 1)
    c_ok = (n_idx * CMP_STRIDE + (CMP_BLOCK - 1)) <= t_q
    s_c = lax.dot_general(qp_b, kc_ref[0, 0], nt, preferred_element_type=F32) + tile4(jnp.where(c_ok, 0.0, NEG_INF))
    e_c = jnp.exp2(s_c - jnp.max(s_c, axis=-1, keepdims=True)) * tile4(jnp.where(c_ok, 1.0, 0.0))
    l_c = jnp.sum(e_c, axis=-1, keepdims=True)
    p_c = e_c / jnp.where(l_c > 0.0, l_c, 1.0)
    o_c = jnp.dot(p_c.astype(BF16), vc_ref[0, 0], preferred_element_type=F32)
    p_sum = (p_c[0:Q_TILE] + p_c[Q_TILE:2 * Q_TILE]) + (p_c[2 * Q_TILE:3 * Q_TILE] + p_c[3 * Q_TILE:])
    p_hi = p_sum.astype(BF16)
    r1 = p_sum - p_hi.astype(F32)
    p_mid = r1.astype(BF16)
    p_lo = (r1 - p_mid.astype(F32)).astype(BF16)
    c2s = c2s_ref[...]
    imp_t = (lax.dot_general(c2s, p_hi, nt, preferred_element_type=F32)
             + lax.dot_general(c2s, p_mid, nt, preferred_element_type=F32)
             + lax.dot_general(c2s, p_lo, nt, preferred_element_type=F32))

    sid = lax.broadcasted_iota(jnp.int32, (n_slc, Q_TILE), 0)
    t_l = s0 + lax.broadcasted_iota(jnp.int32, (n_slc, Q_TILE), 1)
    cur = lax.shift_right_logical(t_l, int(math.log2(SLC_BLOCK)))
    causal_blk = sid <= cur
    forced = jnp.where(sid == 0, 1.0, jnp.where(sid == cur, 1.0, jnp.where(sid == cur - 1, 1.0, 0.0)))
    work = jnp.where(causal_blk, imp_t + FORCE_BONUS * forced, NEG_INF)
    sid_f = sid.astype(F32)
    sel_t = jnp.zeros((n_slc, Q_TILE), F32)
    for _ in range(min(N_SELECT, n_slc)):
        mx = jnp.max(work, axis=0, keepdims=True)
        first = jnp.min(jnp.where(work == mx, sid_f, float(n_slc)), axis=0, keepdims=True)
        pick = sid_f == first
        sel_t = jnp.where(pick, 1.0, sel_t)
        work = jnp.where(pick, -3.0e38, work)
    sel_t = jnp.where(causal_blk, sel_t, 0.0)
    unsel = jnp.concatenate([sel_t - 1.0, jnp.zeros((HEAD_DIM - n_slc, Q_TILE), F32)], axis=0).T
    q_aug = jnp.concatenate([qr_b, tile4(unsel.astype(BF16))], axis=1)

    kt = 512

    def slc_tile(j, carry, diag_bias):
        m, l, acc = carry
        k0 = pl.multiple_of(j * kt, kt)
        s = lax.dot_general(q_aug, ks_ref[pl.ds(k0, kt), :], nt, preferred_element_type=F32)
        if diag_bias is not None:
            s = s + diag_bias
        m_new = jnp.maximum(m, jnp.max(s, axis=-1, keepdims=True))
        alpha = jnp.exp2(m - m_new)
        p = jnp.exp2(s - m_new)
        l = alpha * l + jnp.sum(p, axis=-1, keepdims=True)
        acc = alpha * acc + jnp.dot(p.astype(BF16), vs_ref[pl.ds(k0, kt), :], preferred_element_type=F32)
        return m_new, l, acc

    slab = WINDOW + Q_TILE
    w0 = pl.multiple_of(jnp.maximum(s0 - WINDOW, 0), Q_TILE)
    dist = t_q - (w0 + lax.broadcasted_iota(jnp.int32, (Q_TILE, slab), 1))
    w_bias = jnp.where(dist >= 0, jnp.where(dist < WINDOW, 0.0, NEG_INF), NEG_INF)
    s_w = lax.dot_general(qr_b, kw_ref[pl.ds(w0, slab), :], nt, preferred_element_type=F32) + tile4(w_bias)
    e_w = jnp.exp2(s_w - jnp.max(s_w, axis=-1, keepdims=True))
    o_w = (jnp.dot(e_w.astype(BF16), vw_ref[pl.ds(w0, slab), :], preferred_element_type=F32)
           / jnp.sum(e_w, axis=-1, keepdims=True))

    j_diag = lax.shift_right_logical(s0, int(math.log2(kt)))
    kpos = j_diag * kt + lax.broadcasted_iota(jnp.int32, (Q_TILE, kt), 1)
    init = (jnp.full((rows, 1), NEG_INF, F32), jnp.zeros((rows, 1), F32), jnp.zeros((rows, HEAD_DIM), F32))
    carry = slc_tile(j_diag, init, tile4(jnp.where(kpos <= t_q, 0.0, NEG_INF)))
    carry = lax.fori_loop(0, lax.shift_right_logical(j_diag, 1),
                          lambda p, c: slc_tile(2 * p + 1, slc_tile(2 * p, c, None), None), carry)
    _, l_s, acc_s = lax.cond(jnp.bitwise_and(j_diag, 1) == 1,
                             lambda c: slc_tile(j_diag - 1, c, None), lambda c: c, carry)
    o_s = acc_s / l_s

    sg = _sigmoid(gate_ref[...])
    for g in range(GQA_GROUP):
        r = slice(g * Q_TILE, (g + 1) * Q_TILE)
        o_ref[:, g * HEAD_DIM:(g + 1) * HEAD_DIM] = (
            sg[:, g:g + 1] * o_c[r]
            + sg[:, GQA_GROUP + g:GQA_GROUP + g + 1] * o_s[r]
            + sg[:, 2 * GQA_GROUP + g:2 * GQA_GROUP + g + 1] * o_w[r])


def _cmp_to_slc_t(seq):
    n_cmp = seq // CMP_STRIDE - 1
    n_slc = seq // SLC_BLOCK
    cmp_start = jnp.arange(n_cmp) * CMP_STRIDE
    slc_start = jnp.arange(n_slc) * SLC_BLOCK
    overlap = jnp.clip(jnp.minimum(cmp_start[:, None] + CMP_BLOCK, slc_start[None, :] + SLC_BLOCK)
                       - jnp.maximum(cmp_start[:, None], slc_start[None, :]), 0, None)
    c2s = overlap.astype(F32) / CMP_BLOCK
    return jnp.pad(c2s, ((0, 1), (0, 0))).T.astype(BF16)


def _nsa_attention(q, gates, kc, vc, ks, vs, kw, vw, bsz, seq):
    m = q.shape[0]
    nq = seq // Q_TILE
    nc = seq // CMP_STRIDE
    cos_t, sin_t = _rope_tables(seq, 1)
    c2s_t = _cmp_to_slc_t(seq)
    gw = GQA_GROUP * HEAD_DIM
    kv_spec = pl.BlockSpec((seq, HEAD_DIM), lambda b, h, t: (b, h))
    cmp_spec = pl.BlockSpec((1, 1, nc, HEAD_DIM), lambda b, h, t: (b, h, 0, 0))
    tab_spec = pl.BlockSpec((Q_TILE, HEAD_DIM), lambda b, h, t: (t, 0))
    return pl.pallas_call(
        functools.partial(_attn_kernel, seq=seq),
        out_shape=jax.ShapeDtypeStruct((m, N_HEADS * HEAD_DIM), F32),
        grid=(bsz, N_KV_HEADS, nq),
        in_specs=[pl.BlockSpec((Q_TILE, gw), lambda b, h, t: (b * nq + t, h)),
                  pl.BlockSpec((Q_TILE, LANES), lambda b, h, t: (b * nq + t, h)),
                  tab_spec, tab_spec, cmp_spec, cmp_spec,
                  pl.BlockSpec(c2s_t.shape, lambda b, h, t: (0, 0)),
                  pl.BlockSpec((seq, 2 * HEAD_DIM), lambda b, h, t: (b, h)), kv_spec, kv_spec, kv_spec],
        out_specs=pl.BlockSpec((Q_TILE, gw), lambda b, h, t: (b * nq + t, h)),
        compiler_params=_cparams(("parallel", "parallel", "arbitrary")),
        name="nsa_attention",
    )(q, gates, cos_t, sin_t, kc, vc, c2s_t, ks, vs, kw, vw)


def _s5_operators(a_re, a_im, log_dt, b_re, b_im, c_re, c_im):
    hp = lax.Precision.HIGHEST
    n_g = a_re.shape[0]
    gpb = SSM_GROUPS_PER_BLOCK
    n_gb = n_g // gpb
    t_c, hh, pp = SSM_CHUNK, SSM_GROUP, SSM_STATE
    lam = lax.complex(a_re.astype(F32), a_im.astype(F32))
    dt = jnp.exp(log_dt.astype(F32))[:, None]
    a_bar = jnp.exp(lam * dt)
    b_bar = ((a_bar - 1.0) / lam)[:, :, None] * lax.complex(b_re.astype(F32), b_im.astype(F32))
    c_mat = lax.complex(c_re.astype(F32), c_im.astype(F32))
    steps = jnp.arange(t_c + 1, dtype=F32)[:, None, None]
    pw = jnp.exp((lam * dt)[None] * steps)

    def blockdiag(x, r, c):
        rows = jnp.arange(gpb * r)[:, None] // r
        cols = jnp.arange(gpb * c)[None, :] // c
        return jnp.concatenate([x] * gpb, axis=-1) * (rows == cols).astype(F32)

    k_lag = jnp.einsum('ghp,kgp,gpi->kgih', c_mat, pw[:t_c], b_bar, precision=hp).real
    k_rows = k_lag.reshape(t_c, n_gb, gpb * hh, hh)
    w_intra = jnp.concatenate([blockdiag(k_rows[k], hh, hh) for k in range(t_c)], axis=-1)
    p_in = jnp.swapaxes(pw[t_c - 1::-1][:, :, :, None] * b_bar[None], -1, -2)
    p_in = p_in.reshape(t_c, n_gb, gpb * hh, pp)
    s_in = jnp.concatenate([blockdiag(p_in.real, hh, pp), blockdiag(p_in.imag, hh, pp)], axis=-1)
    m_out = jnp.swapaxes(c_mat[None] * pw[1:, :, None, :], -1, -2)
    m_out = m_out.reshape(t_c, n_gb, gpb * pp, hh)
    s_out = jnp.concatenate(
        [jnp.concatenate([blockdiag(plane[t], pp, hh) for t in range(t_c)], axis=-1)
         for plane in (m_out.real, -m_out.imag)], axis=1)
    k_steps = jnp.arange(SCAN_TABLE_ROWS, dtype=F32)[:, None, None] * float(t_c)
    a_pow = jnp.exp((lam * dt)[None] * k_steps).reshape(SCAN_TABLE_ROWS, n_gb, gpb * pp).swapaxes(0, 1)
    w_intra = w_intra.astype(BF16)
    toeplitz = [jnp.pad(w_intra[:, :, :(t_c - tau) * LANES], ((0, 0), (0, 0), (tau * LANES, 0)))
                for tau in range(t_c)]
    w_out = jnp.concatenate(toeplitz + [s_out.astype(BF16)], axis=1)
    s_in = jnp.swapaxes(s_in.astype(BF16), 0, 1).reshape(n_gb, t_c * LANES, 2 * gpb * pp)
    return s_in, w_out, a_pow.real.astype(F32), a_pow.imag.astype(F32)


def _s5_kernel(u_ref, sin_ref, wout_ref, ar_ref, ai_ref, d_ref, o_ref, xin_ref, xprev_ref):
    t_c = SSM_CHUNK
    nc = u_ref.shape[0] // t_c
    half = ar_ref.shape[-1]

    def step_rows(ref, t):
        return ref.at[pl.ds(t, nc, stride=t_c), :]

    u_flat = jnp.concatenate([step_rows(u_ref, tau)[...].astype(BF16) for tau in range(t_c)], axis=1)
    xin = jnp.dot(u_flat, sin_ref[0], preferred_element_type=F32)
    x_r, x_i = xin[:, :half], xin[:, half:]
    r_in_grp = jnp.bitwise_and(lax.broadcasted_iota(jnp.int32, (nc, 1), 0), SCAN_GROUP - 1)
    d = 1
    while d < SCAN_GROUP:
        keep = jnp.where(r_in_grp >= d, 1.0, 0.0)
        s_r, s_i = pltpu.roll(x_r, d, 0) * keep, pltpu.roll(x_i, d, 0) * keep
        ad_r, ad_i = ar_ref[0, d:d + 1, :], ai_ref[0, d:d + 1, :]
        x_r, x_i = x_r + ad_r * s_r - ad_i * s_i, x_i + ad_r * s_i + ad_i * s_r
        d *= 2
    keep = jnp.where(r_in_grp >= 1, 1.0, 0.0)
    xin_ref[:, :half] = x_r
    xin_ref[:, half:] = x_i
    xprev_ref[:, :half] = pltpu.roll(x_r, 1, 0) * keep
    xprev_ref[:, half:] = pltpu.roll(x_i, 1, 0) * keep
    ak_r, ak_i = ar_ref[0, 0:SCAN_GROUP, :], ai_ref[0, 0:SCAN_GROUP, :]
    ag_r, ag_i = ar_ref[0, SCAN_GROUP:SCAN_GROUP + 1, :], ai_ref[0, SCAN_GROUP:SCAN_GROUP + 1, :]

    def group(g, carry):
        c_r, c_i = carry
        rows = pl.ds(pl.multiple_of(g * SCAN_GROUP, SCAN_GROUP), SCAN_GROUP)
        xprev_ref[rows, :half] = xprev_ref[rows, :half] + ak_r * c_r - ak_i * c_i
        xprev_ref[rows, half:] = xprev_ref[rows, half:] + ak_r * c_i + ak_i * c_r
        last = xin_ref[pl.ds(g * SCAN_GROUP + SCAN_GROUP - 1, 1), :]
        return (last[:, :half] + ag_r * c_r - ag_i * c_i, last[:, half:] + ag_r * c_i + ag_i * c_r)

    zero = jnp.zeros((1, half), F32)
    lax.fori_loop(0, nc // SCAN_GROUP, group, (zero, zero))
    lhs = jnp.concatenate([u_flat, xprev_ref[...].astype(BF16)], axis=1)
    y = jnp.dot(lhs, wout_ref[0], preferred_element_type=F32)
    for t in range(t_c):
        step_rows(o_ref, t)[...] = _gelu_tanh(y[:, t * LANES:(t + 1) * LANES] + d_ref[0] * step_rows(u_ref, t)[...])


def _s5_scan(u, bsz, seq, ops, d_skip):
    d_ssm = u.shape[1]
    t_c = SSM_CHUNK
    s_in, w_out, a_r, a_i = ops
    n_gb = d_ssm // LANES
    nc = seq // t_c
    st = s_in.shape[-1]
    return pl.pallas_call(
        _s5_kernel,
        out_shape=jax.ShapeDtypeStruct(u.shape, F32),
        grid=(n_gb, bsz),
        in_specs=[pl.BlockSpec((seq, LANES), lambda g, b: (b, g)),
                  pl.BlockSpec((1,) + s_in.shape[1:], lambda g, b: (g, 0, 0)),
                  pl.BlockSpec((1,) + w_out.shape[1:], lambda g, b: (g, 0, 0)),
                  pl.BlockSpec((1, SCAN_TABLE_ROWS, st // 2), lambda g, b: (g, 0, 0)),
                  pl.BlockSpec((1, SCAN_TABLE_ROWS, st // 2), lambda g, b: (g, 0, 0)),
                  pl.BlockSpec((1, 1, LANES), lambda g, b: (g, 0, 0))],
        out_specs=pl.BlockSpec((seq, LANES), lambda g, b: (b, g)),
        scratch_shapes=[pltpu.VMEM((nc, st), F32), pltpu.VMEM((nc, st), F32)],
        compiler_params=_cparams(("parallel", "parallel")),
        name="s5_scan",
    )(u, s_in, w_out, a_r, a_i, d_skip.reshape(n_gb, 1, LANES))


def _mixnorm_kernel(a_ref, s_ref, ga_ref, gs_ref, o_ref):
    da = a_ref.shape[-1]
    a = a_ref[...]
    s = s_ref[...]
    o_ref[:, :da] = (a * lax.rsqrt(jnp.mean(a * a, axis=-1, keepdims=True) + EPS) * ga_ref[...]).astype(BF16)
    o_ref[:, da:] = (s * lax.rsqrt(jnp.mean(s * s, axis=-1, keepdims=True) + EPS) * gs_ref[...]).astype(BF16)


def _mixnorm(attn, ssm, g_attn, g_ssm, tm=256):
    m, da = attn.shape
    ds = ssm.shape[1]
    return pl.pallas_call(
        _mixnorm_kernel,
        out_shape=jax.ShapeDtypeStruct((m, da + ds), BF16),
        grid=(m // tm,),
        in_specs=[pl.BlockSpec((tm, da), lambda i: (i, 0)), pl.BlockSpec((tm, ds), lambda i: (i, 0)),
                  pl.BlockSpec((1, da), lambda i: (0, 0)), pl.BlockSpec((1, ds), lambda i: (0, 0))],
        out_specs=pl.BlockSpec((tm, da + ds), lambda i: (i, 0)),
        compiler_params=_cparams(("parallel",)),
        name="mixnorm",
    )(attn, ssm, g_attn.reshape(1, da), g_ssm.reshape(1, ds))


def _split_bf16(x):
    hi = x.astype(BF16)
    return hi, (x - hi.astype(F32)).astype(BF16)


def _lane_max(x):
    return jnp.max(x, axis=-1, keepdims=True)


def _first_lane_of(cond_val, lane_f):
    return jnp.min(jnp.where(cond_val > 0.0, lane_f, float(LANES)), axis=-1, keepdims=True)


def _ffn_norm_router_kernel(h_ref, g_ref, whi_ref, wlo_ref, b_ref, hn_ref, route_ref):
    x = h_ref[...]
    hn = x * lax.rsqrt(jnp.mean(x * x, axis=-1, keepdims=True) + EPS) * g_ref[...]
    hn_ref[...] = hn
    hi, lo = _split_bf16(hn)
    logits = (jnp.dot(hi, whi_ref[...], preferred_element_type=F32)
              + jnp.dot(lo, whi_ref[...], preferred_element_type=F32)
              + jnp.dot(hi, wlo_ref[...], preferred_element_type=F32)) + b_ref[...]
    lane = lax.broadcasted_iota(jnp.int32, logits.shape, 1)
    lane_f = lane.astype(F32)
    in_c = jnp.where(lane < N_EXPERT_GROUPS, 1.0, 0.0)
    c_l = jnp.where(in_c > 0.0, logits, NEG_INF)
    c_e = jnp.exp(c_l - _lane_max(c_l)) * in_c
    p_c = c_e / jnp.sum(c_e, axis=-1, keepdims=True)
    w_grp = _lane_max(p_c)
    grp = _first_lane_of(jnp.where(p_c == w_grp, in_c, 0.0), lane_f)
    lo_lane = float(N_EXPERT_GROUPS) + grp * float(EXPERTS_PER_GROUP)
    in_f = jnp.where(lane_f >= lo_lane, jnp.where(lane_f < lo_lane + float(EXPERTS_PER_GROUP), 1.0, 0.0), 0.0)
    f_l = jnp.where(in_f > 0.0, logits, NEG_INF)
    f_e = jnp.exp(f_l - _lane_max(f_l)) * in_f
    p_f = f_e / jnp.sum(f_e, axis=-1, keepdims=True)
    p_f = jnp.where(in_f > 0.0, p_f, -1.0)
    p1 = _lane_max(p_f)
    i1 = _first_lane_of(jnp.where(p_f == p1, in_f, 0.0), lane_f)
    rest = jnp.where(lane_f == i1, -1.0, p_f)
    p2 = _lane_max(rest)
    i2 = _first_lane_of(jnp.where(rest == p2, in_f, 0.0) * jnp.where(lane_f == i1, 0.0, 1.0), lane_f)
    denom = p1 + p2
    vals = (w_grp * p1 / denom, w_grp * p2 / denom,
            i1 - float(N_EXPERT_GROUPS), i2 - float(N_EXPERT_GROUPS))
    out = jnp.zeros(logits.shape, F32)
    for k, v in enumerate(vals):
        out = jnp.where(lane == k, v, out)
    route_ref[...] = out


def _ffn_norm_router(h, g, w_router, b_router, tm=256):
    m, d = h.shape
    n = w_router.shape[1]
    w_hi, w_lo = _split_bf16(w_router)
    return pl.pallas_call(
        _ffn_norm_router_kernel,
        out_shape=(jax.ShapeDtypeStruct((m, d), F32), jax.ShapeDtypeStruct((m, n), F32)),
        grid=(m // tm,),
        in_specs=[pl.BlockSpec((tm, d), lambda i: (i, 0)), pl.BlockSpec((1, d), lambda i: (0, 0)),
                  pl.BlockSpec((d, n), lambda i: (0, 0)), pl.BlockSpec((d, n), lambda i: (0, 0)),
                  pl.BlockSpec((1, n), lambda i: (0, 0))],
        out_specs=(pl.BlockSpec((tm, d), lambda i: (i, 0)), pl.BlockSpec((tm, n), lambda i: (i, 0))),
        compiler_params=_cparams(("parallel",)),
        name="ffn_norm_router",
    )(h, g.reshape(1, d), w_hi, w_lo, b_router.reshape(1, n))


def _dispatch_plan(expert, tm):
    n_tok = expert.shape[0]
    n_assign = n_tok * TOP_K
    n_blocks = -(-(n_assign + N_EXPERTS * (MOE_BLOCK - 1)) // MOE_BLOCK)
    n_rows = n_blocks * MOE_BLOCK
    flat = expert.reshape(n_assign)
    ids = jnp.arange(n_assign, dtype=jnp.int32)
    _, order = lax.sort_key_val(flat, ids)
    _, inv = lax.sort_key_val(order, ids)
    e_ids = jnp.arange(N_EXPERTS, dtype=jnp.int32)
    counts = jnp.sum((flat[:, None] == e_ids[None, :]).astype(jnp.int32), axis=0)
    start = jnp.cumsum(counts) - counts
    padded = (counts + MOE_BLOCK - 1) // MOE_BLOCK * MOE_BLOCK
    pad_end = jnp.cumsum(padded)
    pad_start = pad_end - padded
    rows = jnp.arange(n_rows, dtype=jnp.int32)
    row_e = jnp.minimum(jnp.sum((pad_end[None, :] <= rows[:, None]).astype(jnp.int32), axis=1), N_EXPERTS - 1)
    slot = rows - pad_start[row_e]
    valid = (slot < counts[row_e]) & (rows < pad_end[-1])
    src = jnp.clip(start[row_e] + slot, 0, n_assign - 1)
    buf_tok = jnp.where(valid, order[src] // TOP_K, 0).astype(jnp.int32)
    dest = (pad_start[flat] + inv - start[flat]).astype(jnp.int32)
    dest = dest.reshape(n_tok // tm, tm, TOP_K).transpose(0, 2, 1).reshape(-1)
    plan = jnp.concatenate([pad_start // MOE_BLOCK, padded // MOE_BLOCK,
                            pad_end[-1:] // MOE_BLOCK]).astype(jnp.int32)
    return plan, buf_tok, dest, n_blocks


def _zero_fill_spare(plan_ref, zero_vmem, out_hbm, col0, width, sem, n_blocks):
    n_used = plan_ref[2 * N_EXPERTS]

    def cp(blk):
        return pltpu.make_async_copy(
            zero_vmem, out_hbm.at[pl.ds(pl.multiple_of(blk * MOE_BLOCK, MOE_BLOCK), MOE_BLOCK), pl.ds(col0, width)],
            sem)

    def start(blk, c):
        cp(blk).start(priority=SIDE_DMA_PRIORITY)
        return c

    def wait(blk, c):
        cp(blk).wait()
        return c

    lax.fori_loop(n_used, n_blocks, start, 0)
    lax.fori_loop(n_used, n_blocks, wait, 0)


def _moe_up_kernel(plan_ref, tok_ref, x_hbm, wg_ref, wu_ref, act_hbm, wg_b, wu_b, xbuf, obuf, gsem, osem,
                   *, n_blocks):
    e = pl.program_id(0)
    c = pl.program_id(1)
    tn = wg_b.shape[1]
    col0 = pl.multiple_of(c * tn, tn)
    b0 = plan_ref[e]
    nb = plan_ref[N_EXPERTS + e]

    def row_copy(blk, r, slot):
        tok = tok_ref[blk * MOE_BLOCK + r]
        return pltpu.make_async_copy(x_hbm.at[pl.ds(tok, 1), :], xbuf.at[slot, pl.ds(r, 1), :], gsem.at[slot])

    def gather_start(blk, slot):
        lax.fori_loop(0, MOE_BLOCK, lambda r, cr: (row_copy(blk, r, slot).start(priority=SIDE_DMA_PRIORITY), cr)[1], 0, unroll=DMA_UNROLL)

    def gather_wait(blk, slot):
        lax.fori_loop(0, MOE_BLOCK, lambda r, cr: (row_copy(blk, r, slot).wait(), cr)[1], 0, unroll=DMA_UNROLL)

    def out_copy(blk, slot):
        rows = pl.ds(pl.multiple_of(blk * MOE_BLOCK, MOE_BLOCK), MOE_BLOCK)
        return pltpu.make_async_copy(obuf.at[slot], act_hbm.at[rows, pl.ds(col0, tn)], osem.at[slot])

    @pl.when(nb > 0)
    def _():
        gather_start(b0, 0)
        wg_b[...] = wg_ref[0].astype(BF16)
        wu_b[...] = wu_ref[0].astype(BF16)

    def body(j, carry):
        slot = lax.rem(j, 2)
        gather_wait(b0 + j, slot)

        @pl.when(j + 1 < nb)
        def _():
            gather_start(b0 + j + 1, 1 - slot)

        x = xbuf[slot].astype(BF16)
        g = jnp.dot(x, wg_b[...], preferred_element_type=F32)
        u = jnp.dot(x, wu_b[...], preferred_element_type=F32)

        @pl.when(j >= 2)
        def _():
            out_copy(b0 + j - 2, slot).wait()

        obuf[slot] = (g * _sigmoid(g) * u).astype(BF16)
        out_copy(b0 + j, slot).start(priority=SIDE_DMA_PRIORITY)
        return carry

    lax.fori_loop(0, nb, body, 0)

    @pl.when(nb >= 2)
    def _():
        out_copy(b0 + nb - 2, lax.rem(nb, 2)).wait()

    @pl.when(nb >= 1)
    def _():
        out_copy(b0 + nb - 1, lax.rem(nb + 1, 2)).wait()

    @pl.when(e == N_EXPERTS - 1)
    def _():
        obuf[0] = jnp.zeros(obuf.shape[1:], obuf.dtype)
        _zero_fill_spare(plan_ref, obuf.at[0], act_hbm, col0, tn, osem.at[0], n_blocks)


def _moe_up(hn, w_gate, w_up, plan, buf_tok, n_blocks, n_chunks=2):
    d = hn.shape[1]
    d_e = w_gate.shape[-1]
    tn = d_e // n_chunks
    w_spec = pl.BlockSpec((1, d, tn), lambda e, c, plan, tok: (e, 0, c))
    return pl.pallas_call(
        functools.partial(_moe_up_kernel, n_blocks=n_blocks),
        out_shape=jax.ShapeDtypeStruct((n_blocks * MOE_BLOCK, d_e), BF16),
        grid_spec=pltpu.PrefetchScalarGridSpec(
            num_scalar_prefetch=2,
            grid=(N_EXPERTS, n_chunks),
            in_specs=[pl.BlockSpec(memory_space=pl.ANY), w_spec, w_spec],
            out_specs=pl.BlockSpec(memory_space=pl.ANY),
            scratch_shapes=[pltpu.VMEM((d, tn), BF16), pltpu.VMEM((d, tn), BF16),
                            pltpu.VMEM((2, MOE_BLOCK, d), F32), pltpu.VMEM((2, MOE_BLOCK, tn), BF16),
                            pltpu.SemaphoreType.DMA((2,)), pltpu.SemaphoreType.DMA((2,))]),
        compiler_params=_cparams(("arbitrary", "arbitrary")),
        name="moe_up",
    )(plan, buf_tok, hn, w_gate, w_up)


def _moe_down_kernel(plan_ref, act_hbm, wd_ref, y_hbm, wd_b, abuf, obuf, asem, osem, *, n_blocks):
    e = pl.program_id(0)
    c = pl.program_id(1)
    tn = wd_b.shape[1]
    col0 = pl.multiple_of(c * tn, tn)
    b0 = plan_ref[e]
    nb = plan_ref[N_EXPERTS + e]

    def rows_of(blk):
        return pl.ds(pl.multiple_of(blk * MOE_BLOCK, MOE_BLOCK), MOE_BLOCK)

    def in_copy(blk, slot):
        return pltpu.make_async_copy(act_hbm.at[rows_of(blk), :], abuf.at[slot], asem.at[slot])

    def out_copy(blk, slot):
        return pltpu.make_async_copy(obuf.at[slot], y_hbm.at[rows_of(blk), pl.ds(col0, tn)], osem.at[slot])

    @pl.when(nb > 0)
    def _():
        in_copy(b0, 0).start(priority=SIDE_DMA_PRIORITY)
        wd_b[...] = wd_ref[0].astype(BF16)

    def body(j, carry):
        slot = lax.rem(j, 2)
        in_copy(b0 + j, slot).wait()

        @pl.when(j + 1 < nb)
        def _():
            in_copy(b0 + j + 1, 1 - slot).start(priority=SIDE_DMA_PRIORITY)

        y = jnp.dot(abuf[slot], wd_b[...], preferred_element_type=F32)

        @pl.when(j >= 2)
        def _():
            out_copy(b0 + j - 2, slot).wait()

        obuf[slot] = y
        out_copy(b0 + j, slot).start(priority=SIDE_DMA_PRIORITY)
        return carry

    lax.fori_loop(0, nb, body, 0)

    @pl.when(nb >= 2)
    def _():
        out_copy(b0 + nb - 2, lax.rem(nb, 2)).wait()

    @pl.when(nb >= 1)
    def _():
        out_copy(b0 + nb - 1, lax.rem(nb + 1, 2)).wait()

    @pl.when(e == N_EXPERTS - 1)
    def _():
        obuf[0] = jnp.zeros(obuf.shape[1:], obuf.dtype)
        _zero_fill_spare(plan_ref, obuf.at[0], y_hbm, col0, tn, osem.at[0], n_blocks)


def _moe_down(act, w_down, plan, n_blocks, n_chunks=2):
    n_rows, d_e = act.shape
    d = w_down.shape[-1]
    tn = d // n_chunks
    return pl.pallas_call(
        functools.partial(_moe_down_kernel, n_blocks=n_blocks),
        out_shape=jax.ShapeDtypeStruct((n_rows, d), F32),
        grid_spec=pltpu.PrefetchScalarGridSpec(
            num_scalar_prefetch=1,
            grid=(N_EXPERTS, n_chunks),
            in_specs=[pl.BlockSpec(memory_space=pl.ANY),
                      pl.BlockSpec((1, d_e, tn), lambda e, c, plan: (e, 0, c))],
            out_specs=pl.BlockSpec(memory_space=pl.ANY),
            scratch_shapes=[pltpu.VMEM((d_e, tn), BF16),
                            pltpu.VMEM((2, MOE_BLOCK, d_e), BF16), pltpu.VMEM((2, MOE_BLOCK, tn), F32),
                            pltpu.SemaphoreType.DMA((2,)), pltpu.SemaphoreType.DMA((2,))]),
        compiler_params=_cparams(("arbitrary", "arbitrary")),
        name="moe_down",
    )(plan, act, w_down)


def _combine_kernel(dest_ref, h_ref, route_ref, gn_ref, y_hbm, o_ref, rows_ref, sem):
    tm = h_ref.shape[0]
    n_rows = tm * TOP_K
    i = pl.program_id(0)
    n = pl.num_programs(0)

    def row_copy(step, r, slot):
        src = dest_ref[step * n_rows + r]
        return pltpu.make_async_copy(y_hbm.at[pl.ds(src, 1), :], rows_ref.at[slot, pl.ds(r, 1), :], sem.at[slot])

    def gather_start(step, slot):
        lax.fori_loop(0, n_rows, lambda r, cr: (row_copy(step, r, slot).start(priority=SIDE_DMA_PRIORITY), cr)[1], 0, unroll=DMA_UNROLL)

    def gather_wait(step, slot):
        lax.fori_loop(0, n_rows, lambda r, cr: (row_copy(step, r, slot).wait(), cr)[1], 0, unroll=DMA_UNROLL)

    slot = lax.rem(i, 2)

    @pl.when(i == 0)
    def _():
        gather_start(0, 0)

    @pl.when(i + 1 < n)
    def _():
        gather_start(i + 1, 1 - slot)

    gather_wait(i, slot)
    route = route_ref[...]
    rows = rows_ref[slot]
    h = h_ref[...] + route[:, 0:1] * rows[0:tm, :] + route[:, 1:2] * rows[tm:2 * tm, :]
    o_ref[...] = h * lax.rsqrt(jnp.mean(h * h, axis=-1, keepdims=True) + EPS) * gn_ref[...]


def _combine_final(h, route, dest, y_buf, g_final, tm=128):
    m, d = h.shape
    return pl.pallas_call(
        _combine_kernel,
        out_shape=jax.ShapeDtypeStruct((m, d), F32),
        grid_spec=pltpu.PrefetchScalarGridSpec(
            num_scalar_prefetch=1,
            grid=(m // tm,),
            in_specs=[pl.BlockSpec((tm, d), lambda i, dst: (i, 0)),
                      pl.BlockSpec((tm, LANES), lambda i, dst: (i, 0)),
                      pl.BlockSpec((1, d), lambda i, dst: (0, 0)),
                      pl.BlockSpec(memory_space=pl.ANY)],
            out_specs=pl.BlockSpec((tm, d), lambda i, dst: (i, 0)),
            scratch_shapes=[pltpu.VMEM((2, TOP_K * tm, d), F32), pltpu.SemaphoreType.DMA((2,))]),
        compiler_params=_cparams(("arbitrary",)),
        name="moe_combine_final",
    )(dest, h, route, g_final.reshape(1, d), y_buf)


def _gate_column_order():
    cols = []
    for hk in range(N_KV_HEADS):
        blk = [-1] * LANES
        for br in range(N_BRANCHES):
            for g in range(GQA_GROUP):
                blk[br * GQA_GROUP + g] = (hk * GQA_GROUP + g) * N_BRANCHES + br
        cols.extend(blk)
    return cols


def kernel(x, norm_mix, w_in, cmp_pe_k, cmp_w1_k, cmp_b1_k, cmp_w2_k, cmp_b2_k, cmp_pe_v, cmp_w1_v, cmp_b1_v, cmp_w2_v, cmp_b2_v, ssm_a_re, ssm_a_im, ssm_log_dt, ssm_b_re, ssm_b_im, ssm_c_re, ssm_c_im, ssm_d, ssm_w_glu, ssm_b_glu, norm_attn_out, norm_ssm_out, w_out, norm_ffn, router_w_coarse, router_b_coarse, router_w_fine, router_b_fine, w_gate, w_up, w_down, norm_final):
    bsz, seq, d_model = x.shape
    m = bsz * seq
    depth = w_in.shape[0]
    assert depth == 1, "the combine kernel fuses the final norm; stacked layers are not supported"
    layer = 0
    q_w = N_HEADS * HEAD_DIM
    kv_w = N_KV_HEADS * HEAD_DIM
    gate_w = N_HEADS * N_BRANCHES
    cut1, cut2 = q_w, q_w + 6 * kv_w
    cut3 = cut2 + gate_w
    h = x.astype(F32).reshape(m, d_model)
    gate_cols = jnp.array([c if c >= 0 else 0 for c in _gate_column_order()], jnp.int32)
    gate_keep = jnp.array([1.0 if c >= 0 else 0.0 for c in _gate_column_order()], F32)

    w_in_b = w_in[layer].astype(BF16)
    hn = _rmsnorm(h, norm_mix[layer], BF16)
    q = _matmul(hn, w_in_b[:, :cut1], name="proj_q")
    kv = _matmul(hn, w_in_b[:, cut1:cut2], name="proj_kv")
    w_gate_cols = (w_in[layer][:, cut2:cut3][:, gate_cols] * gate_keep[None, :]).astype(BF16)
    gates = _matmul(hn, w_gate_cols, tn=512, name="proj_gate")
    u = _matmul(hn, w_in_b[:, cut3:], name="proj_u")

    stack2 = lambda a, b: jnp.stack([a[layer], b[layer]])
    cmp_kv = _compress(kv, bsz, seq, stack2(cmp_pe_k, cmp_pe_v),
                       stack2(cmp_w1_k, cmp_w1_v).astype(BF16), stack2(cmp_b1_k, cmp_b1_v),
                       stack2(cmp_w2_k, cmp_w2_v).astype(BF16), stack2(cmp_b2_k, cmp_b2_v))
    ks, vs, kw, vw = _kv_prep(kv, seq)
    attn = _nsa_attention(q, gates, cmp_kv[0], cmp_kv[1], ks, vs, kw, vw, bsz, seq)

    ops = _s5_operators(ssm_a_re[layer], ssm_a_im[layer], ssm_log_dt[layer], ssm_b_re[layer],
                        ssm_b_im[layer], ssm_c_re[layer], ssm_c_im[layer])
    y = _s5_scan(u, bsz, seq, ops, ssm_d[layer].astype(F32))
    ssm = _glu_matmul(y, ssm_w_glu[layer].astype(BF16), ssm_b_glu[layer].astype(F32))

    mixed = _mixnorm(attn, ssm, norm_attn_out[layer], norm_ssm_out[layer])
    h = _matmul(mixed, w_out[layer].astype(BF16), residual=h, name="out_proj")

    n_r = N_EXPERT_GROUPS + N_EXPERTS
    w_router = jnp.pad(jnp.concatenate([router_w_coarse[layer], router_w_fine[layer]], axis=1),
                       ((0, 0), (0, LANES - n_r))).astype(F32)
    b_router = jnp.pad(jnp.concatenate([router_b_coarse[layer], router_b_fine[layer]]), (0, LANES - n_r))
    hn2, route = _ffn_norm_router(h, norm_ffn[layer], w_router, b_router.astype(F32))
    expert = route[:, 2:2 + TOP_K].astype(jnp.int32)
    tm_c = 128
    plan, buf_tok, dest, n_blocks = _dispatch_plan(expert, tm_c)
    act = _moe_up(hn2, w_gate[layer], w_up[layer], plan, buf_tok, n_blocks)
    y_buf = _moe_down(act, w_down[layer], plan, n_blocks)
    out = _combine_final(h, route, dest, y_buf, norm_final, tm=tm_c)
    return out.reshape(bsz, seq, d_model).astype(x.dtype)
```

```python
import functools
import math

import jax
import jax.numpy as jnp
from jax import lax
from jax.experimental import pallas as pl
from jax.experimental.pallas import tpu as pltpu

F32 = jnp.float32
BF16 = jnp.bfloat16

HEAD_DIM = 128
N_HEADS = 16
N_KV_HEADS = 4
GQA_GROUP = N_HEADS // N_KV_HEADS
ROT_DIM = HEAD_DIM // 4
ROPE_THETA = 500000.0
CMP_STRIDE = 16
CMP_BLOCK = 32
SLC_BLOCK = 64
N_SELECT = 16
WINDOW = 512
Q_TILE = 128
N_BRANCHES = 3
FORCE_BONUS = 1.0e4
NEG_INF = -1.0e30
MASK_BIG = 1.0e30
SSM_GROUP = 16
SSM_STATE = 64
SSM_CHUNK = 16
SSM_GROUPS_PER_BLOCK = 128 // SSM_GROUP
N_EXPERT_GROUPS = 8
EXPERTS_PER_GROUP = 8
N_EXPERTS = N_EXPERT_GROUPS * EXPERTS_PER_GROUP
TOP_K = 2
MOE_BLOCK = 128
EPS = 1.0e-6
LOG2_E = 1.4426950408889634

V7X_VMEM_LIMIT = 56 * 1024 * 1024
LANES = 128
SCAN_GROUP = 8
SCAN_TABLE_ROWS = 16
DMA_UNROLL = 8
SIDE_DMA_PRIORITY = 1
MOE_X_CAP = 12


def _cparams(sem, vmem=V7X_VMEM_LIMIT):
    return pltpu.CompilerParams(dimension_semantics=sem, vmem_limit_bytes=vmem)


def _gelu_tanh(x):
    return 0.5 * x * (1.0 + jnp.tanh(math.sqrt(2.0 / math.pi) * (x + 0.044715 * (x * x * x))))


def _sigmoid(x):
    return 1.0 / (1.0 + jnp.exp(-x))


def _rmsnorm_kernel(x_ref, g_ref, o_ref):
    x = x_ref[...]
    ms = jnp.mean(x * x, axis=-1, keepdims=True)
    o_ref[...] = (x * lax.rsqrt(ms + EPS) * g_ref[...]).astype(o_ref.dtype)


def _rmsnorm(x, g, out_dtype, tm=256):
    m, d = x.shape
    return pl.pallas_call(
        _rmsnorm_kernel,
        out_shape=jax.ShapeDtypeStruct((m, d), out_dtype),
        grid=(m // tm,),
        in_specs=[pl.BlockSpec((tm, d), lambda i: (i, 0)),
                  pl.BlockSpec((1, d), lambda i: (0, 0))],
        out_specs=pl.BlockSpec((tm, d), lambda i: (i, 0)),
        compiler_params=_cparams(("parallel",)),
        name="rmsnorm",
    )(x, g.reshape(1, d))


def _mm_kernel(a_ref, w_ref, o_ref):
    o_ref[...] = jnp.dot(a_ref[...], w_ref[...], preferred_element_type=F32)


def _mm_glu_kernel(y_ref, w_ref, b_ref, yg_ref, o_ref):
    z = jnp.dot(y_ref[...].astype(BF16), w_ref[...], preferred_element_type=F32) + b_ref[...]
    o_ref[...] = yg_ref[...] * _sigmoid(z)


def _mm_residual_kernel(a_ref, w_ref, r_ref, o_ref):
    o_ref[...] = r_ref[...] + jnp.dot(a_ref[...], w_ref[...], preferred_element_type=F32)


def _matmul(a, w, *, residual=None, tm=512, tn=1024, name="matmul"):
    m, k = a.shape
    n = w.shape[1]
    tn = min(tn, n)
    assert m % tm == 0 and n % tn == 0
    a_spec = pl.BlockSpec((tm, k), lambda j, i: (i, 0))
    w_spec = pl.BlockSpec((k, tn), lambda j, i: (0, j))
    o_spec = pl.BlockSpec((tm, tn), lambda j, i: (i, j))
    if residual is not None:
        kern, ins, specs = _mm_residual_kernel, (a, w, residual), [a_spec, w_spec, o_spec]
    else:
        kern, ins, specs = _mm_kernel, (a, w), [a_spec, w_spec]
    return pl.pallas_call(
        kern,
        out_shape=jax.ShapeDtypeStruct((m, n), F32),
        grid=(n // tn, m // tm),
        in_specs=specs,
        out_specs=o_spec,
        compiler_params=_cparams(("parallel", "parallel")),
        name=name,
    )(*ins)


def _glu_matmul(y, w, bias, tm=512, tn=1024):
    m, k = y.shape
    n = w.shape[1]
    return pl.pallas_call(
        _mm_glu_kernel,
        out_shape=jax.ShapeDtypeStruct((m, n), F32),
        grid=(n // tn, m // tm),
        in_specs=[pl.BlockSpec((tm, k), lambda j, i: (i, 0)),
                  pl.BlockSpec((k, tn), lambda j, i: (0, j)),
                  pl.BlockSpec((1, tn), lambda j, i: (0, j)),
                  pl.BlockSpec((tm, tn), lambda j, i: (i, j))],
        out_specs=pl.BlockSpec((tm, tn), lambda j, i: (i, j)),
        compiler_params=_cparams(("parallel", "parallel")),
        name="ssm_glu",
    )(y, w, bias.reshape(1, n), y)


def _rope_tables(seq, reps):
    half = ROT_DIM // 2
    inv_freq = ROPE_THETA ** (-jnp.arange(half, dtype=F32) / half)
    ang = jnp.arange(seq).astype(F32)[:, None] * inv_freq[None, :]
    cos, sin = jnp.cos(ang), jnp.sin(ang)
    ones = jnp.ones((seq, HEAD_DIM - ROT_DIM), F32)
    cos_t = jnp.concatenate([cos, cos, ones], axis=-1)
    sin_t = jnp.concatenate([-sin, sin, 0.0 * ones], axis=-1)
    return jnp.tile(cos_t, (1, reps)), jnp.tile(sin_t, (1, reps))


def _rope(x, cos_t, sin_t):
    half = ROT_DIM // 2
    n = x.shape[-1]
    lane = jnp.bitwise_and(lax.broadcasted_iota(jnp.int32, x.shape, x.ndim - 1), HEAD_DIM - 1)
    partner = jnp.where(lane < half, pltpu.roll(x, n - half, x.ndim - 1), pltpu.roll(x, half, x.ndim - 1))
    return x * cos_t + partner * sin_t


def _kv_prep_kernel(slc_ref, win_ref, cos_ref, sin_ref, ks_ref, vs_ref, kw_ref, vw_ref, *, tiles_per_seq):
    tm, w = vs_ref.shape
    cos_t, sin_t = cos_ref[...], sin_ref[...]
    slc = slc_ref[...]
    win = win_ref[...]
    k_slc = _rope(slc[:, :w], cos_t, sin_t).astype(BF16)
    pos = lax.rem(pl.program_id(0), tiles_per_seq) * tm + lax.broadcasted_iota(jnp.int32, (tm, HEAD_DIM), 0)
    lane = lax.broadcasted_iota(jnp.int32, (tm, HEAD_DIM), 1)
    own_blk = jnp.where(lane == lax.shift_right_logical(pos, int(math.log2(SLC_BLOCK))), MASK_BIG, 0.0).astype(BF16)
    for hk in range(N_KV_HEADS):
        ks_ref[:, 2 * hk * HEAD_DIM:(2 * hk + 1) * HEAD_DIM] = k_slc[:, hk * HEAD_DIM:(hk + 1) * HEAD_DIM]
        ks_ref[:, (2 * hk + 1) * HEAD_DIM:(2 * hk + 2) * HEAD_DIM] = own_blk
    vs_ref[...] = slc[:, w:].astype(BF16)
    kw_ref[...] = _rope(win[:, :w], cos_t, sin_t).astype(BF16)
    vw_ref[...] = win[:, w:].astype(BF16)


def _kv_prep(kv, seq, tm=256):
    assert seq // SLC_BLOCK <= HEAD_DIM
    m = kv.shape[0]
    w = N_KV_HEADS * HEAD_DIM
    cos_t, sin_t = _rope_tables(seq, N_KV_HEADS)
    tpb = seq // tm
    out = jax.ShapeDtypeStruct((m, w), BF16)
    o_spec = pl.BlockSpec((tm, w), lambda i: (i, 0))
    t_spec = pl.BlockSpec((tm, w), lambda i: (i % tpb, 0))
    return pl.pallas_call(
        functools.partial(_kv_prep_kernel, tiles_per_seq=tpb),
        out_shape=(jax.ShapeDtypeStruct((m, 2 * w), BF16), out, out, out),
        grid=(m // tm,),
        in_specs=[pl.BlockSpec((tm, 2 * w), lambda i: (i, 1)),
                  pl.BlockSpec((tm, 2 * w), lambda i: (i, 2)),
                  t_spec, t_spec],
        out_specs=(pl.BlockSpec((tm, 2 * w), lambda i: (i, 0)), o_spec, o_spec, o_spec),
        compiler_params=_cparams(("parallel",)),
        name="kv_prep",
    )(kv, kv, cos_t, sin_t)


def _compress_kernel(x_ref, pe_ref, w1_ref, b1_ref, w2_ref, b2_ref, o_ref):
    nc = x_ref.shape[0] // CMP_STRIDE
    hid = w1_ref.shape[-1]
    top = jnp.zeros((nc, hid), F32)
    bot = jnp.zeros((nc, hid), F32)
    for j in range(CMP_STRIDE):
        xj = x_ref[pl.ds(j, nc, stride=CMP_STRIDE), :]
        top += jnp.dot((xj + pe_ref[0, j:j + 1, :]).astype(BF16), w1_ref[0, j],
                       preferred_element_type=F32)
        bot += jnp.dot((xj + pe_ref[0, CMP_STRIDE + j:CMP_STRIDE + j + 1, :]).astype(BF16),
                       w1_ref[0, CMP_STRIDE + j], preferred_element_type=F32)
    h = _gelu_tanh(top + pltpu.roll(bot, nc - 1, 0) + b1_ref[0])
    o_ref[0, 0, 0] = (jnp.dot(h.astype(BF16), w2_ref[0], preferred_element_type=F32) + b2_ref[0]).astype(BF16)


def _compress(kv, bsz, seq, pe, w1, b1, w2, b2):
    nc = seq // CMP_STRIDE
    hid = w1.shape[-1]
    w1r = w1.reshape(2, CMP_BLOCK, HEAD_DIM, hid)
    return pl.pallas_call(
        _compress_kernel,
        out_shape=jax.ShapeDtypeStruct((2, bsz, N_KV_HEADS, nc, HEAD_DIM), BF16),
        grid=(2, bsz, N_KV_HEADS),
        in_specs=[pl.BlockSpec((seq, HEAD_DIM), lambda s, b, h: (b, s * N_KV_HEADS + h)),
                  pl.BlockSpec((1, CMP_BLOCK, HEAD_DIM), lambda s, b, h: (s, 0, 0)),
                  pl.BlockSpec((1, CMP_BLOCK, HEAD_DIM, hid), lambda s, b, h: (s, 0, 0, 0)),
                  pl.BlockSpec((1, 1, hid), lambda s, b, h: (s, 0, 0)),
                  pl.BlockSpec((1, hid, HEAD_DIM), lambda s, b, h: (s, 0, 0)),
                  pl.BlockSpec((1, 1, HEAD_DIM), lambda s, b, h: (s, 0, 0))],
        out_specs=pl.BlockSpec((1, 1, 1, nc, HEAD_DIM), lambda s, b, h: (s, b, h, 0, 0)),
        compiler_params=_cparams(("parallel", "parallel", "parallel")),
        name="nsa_compress",
    )(kv, pe, w1r, b1.reshape(2, 1, hid), w2, b2.reshape(2, 1, HEAD_DIM))


def _attn_kernel(q_ref, gate_ref, cos_ref, sin_ref, kc_ref, vc_ref, c2s_ref,
                 ks_ref, vs_ref, kw_ref, vw_ref, o_ref, *, seq):
    qt = pl.program_id(2)
    s0 = qt * Q_TILE
    n_slc = seq // SLC_BLOCK
    n_cmp = kc_ref.shape[-2]
    scale = HEAD_DIM ** -0.5 * LOG2_E
    rows = GQA_GROUP * Q_TILE
    nt = (((1,), (1,)), ((), ()))

    def tile4(x):
        return jnp.concatenate([x] * GQA_GROUP, axis=0)

    q4 = q_ref[...]
    qp = jnp.concatenate([q4[:, g * HEAD_DIM:(g + 1) * HEAD_DIM] for g in range(GQA_GROUP)], axis=0) * scale
    qr_b = _rope(qp, tile4(cos_ref[...]), tile4(sin_ref[...])).astype(BF16)
    qp_b = qp.astype(BF16)
    t_q = s0 + lax.broadcasted_iota(jnp.int32, (Q_TILE, 1), 0)

    n_idx = lax.broadcasted_iota(jnp.int32, (Q_TILE, n_cmp), 1)
    c_ok = (n_idx * CMP_STRIDE + (CMP_BLOCK - 1)) <= t_q
    s_c = lax.dot_general(qp_b, kc_ref[0, 0], nt, preferred_element_type=F32) + tile4(jnp.where(c_ok, 0.0, NEG_INF))
    e_c = jnp.exp2(s_c - jnp.max(s_c, axis=-1, keepdims=True)) * tile4(jnp.where(c_ok, 1.0, 0.0))
    l_c = jnp.sum(e_c, axis=-1, keepdims=True)
    p_c = e_c / jnp.where(l_c > 0.0, l_c, 1.0)
    o_c = jnp.dot(p_c.astype(BF16), vc_ref[0, 0], preferred_element_type=F32)
    p_sum = (p_c[0:Q_TILE] + p_c[Q_TILE:2 * Q_TILE]) + (p_c[2 * Q_TILE:3 * Q_TILE] + p_c[3 * Q_TILE:])
    p_hi = p_sum.astype(BF16)
    r1 = p_sum - p_hi.astype(F32)
    p_mid = r1.astype(BF16)
    p_lo = (r1 - p_mid.astype(F32)).astype(BF16)
    c2s = c2s_ref[...]
    imp_t = (lax.dot_general(c2s, p_hi, nt, preferred_element_type=F32)
             + lax.dot_general(c2s, p_mid, nt, preferred_element_type=F32)
             + lax.dot_general(c2s, p_lo, nt, preferred_element_type=F32))

    sid = lax.broadcasted_iota(jnp.int32, (n_slc, Q_TILE), 0)
    t_l = s0 + lax.broadcasted_iota(jnp.int32, (n_slc, Q_TILE), 1)
    cur = lax.shift_right_logical(t_l, int(math.log2(SLC_BLOCK)))
    causal_blk = sid <= cur
    forced = jnp.where(sid == 0, 1.0, jnp.where(sid == cur, 1.0, jnp.where(sid == cur - 1, 1.0, 0.0)))
    work = jnp.where(causal_blk, imp_t + FORCE_BONUS * forced, NEG_INF)
    sid_f = sid.astype(F32)
    sel_t = jnp.zeros((n_slc, Q_TILE), F32)
    for _ in range(min(N_SELECT, n_slc)):
        mx = jnp.max(work, axis=0, keepdims=True)
        first = jnp.min(jnp.where(work == mx, sid_f, float(n_slc)), axis=0, keepdims=True)
        pick = sid_f == first
        sel_t = jnp.where(pick, 1.0, sel_t)
        work = jnp.where(pick, -3.0e38, work)
    sel_t = jnp.where(causal_blk, sel_t, 0.0)
    unsel = jnp.concatenate([sel_t - 1.0, jnp.zeros((HEAD_DIM - n_slc, Q_TILE), F32)], axis=0).T
    q_aug = jnp.concatenate([qr_b, tile4(unsel.astype(BF16))], axis=1)

    kt = 512

    def slc_tile(j, carry, diag_bias):
        m, l, acc = carry
        k0 = pl.multiple_of(j * kt, kt)
        s = lax.dot_general(q_aug, ks_ref[pl.ds(k0, kt), :], nt, preferred_element_type=F32)
        if diag_bias is not None:
            s = s + diag_bias
        m_new = jnp.maximum(m, jnp.max(s, axis=-1, keepdims=True))
        alpha = jnp.exp2(m - m_new)
        p = jnp.exp2(s - m_new)
        l = alpha * l + jnp.sum(p, axis=-1, keepdims=True)
        acc = alpha * acc + jnp.dot(p.astype(BF16), vs_ref[pl.ds(k0, kt), :], preferred_element_type=F32)
        return m_new, l, acc

    slab = WINDOW + Q_TILE
    w0 = pl.multiple_of(jnp.maximum(s0 - WINDOW, 0), Q_TILE)
    dist = t_q - (w0 + lax.broadcasted_iota(jnp.int32, (Q_TILE, slab), 1))
    w_bias = jnp.where(dist >= 0, jnp.where(dist < WINDOW, 0.0, NEG_INF), NEG_INF)
    s_w = lax.dot_general(qr_b, kw_ref[pl.ds(w0, slab), :], nt, preferred_element_type=F32) + tile4(w_bias)
    e_w = jnp.exp2(s_w - jnp.max(s_w, axis=-1, keepdims=True))
    o_w = (jnp.dot(e_w.astype(BF16), vw_ref[pl.ds(w0, slab), :], preferred_element_type=F32)
           / jnp.sum(e_w, axis=-1, keepdims=True))

    j_diag = lax.shift_right_logical(s0, int(math.log2(kt)))
    kpos = j_diag * kt + lax.broadcasted_iota(jnp.int32, (Q_TILE, kt), 1)
    init = (jnp.full((rows, 1), NEG_INF, F32), jnp.zeros((rows, 1), F32), jnp.zeros((rows, HEAD_DIM), F32))
    carry = slc_tile(j_diag, init, tile4(jnp.where(kpos <= t_q, 0.0, NEG_INF)))
    carry = lax.fori_loop(0, lax.shift_right_logical(j_diag, 1),
                          lambda p, c: slc_tile(2 * p + 1, slc_tile(2 * p, c, None), None), carry)
    _, l_s, acc_s = lax.cond(jnp.bitwise_and(j_diag, 1) == 1,
                             lambda c: slc_tile(j_diag - 1, c, None), lambda c: c, carry)
    o_s = acc_s / l_s

    sg = _sigmoid(gate_ref[...])
    for g in range(GQA_GROUP):
        r = slice(g * Q_TILE, (g + 1) * Q_TILE)
        o_ref[:, g * HEAD_DIM:(g + 1) * HEAD_DIM] = (
            sg[:, g:g + 1] * o_c[r]
            + sg[:, GQA_GROUP + g:GQA_GROUP + g + 1] * o_s[r]
            + sg[:, 2 * GQA_GROUP + g:2 * GQA_GROUP + g + 1] * o_w[r])


def _cmp_to_slc_t(seq):
    n_cmp = seq // CMP_STRIDE - 1
    n_slc = seq // SLC_BLOCK
    cmp_start = jnp.arange(n_cmp) * CMP_STRIDE
    slc_start = jnp.arange(n_slc) * SLC_BLOCK
    overlap = jnp.clip(jnp.minimum(cmp_start[:, None] + CMP_BLOCK, slc_start[None, :] + SLC_BLOCK)
                       - jnp.maximum(cmp_start[:, None], slc_start[None, :]), 0, None)
    c2s = overlap.astype(F32) / CMP_BLOCK
    return jnp.pad(c2s, ((0, 1), (0, 0))).T.astype(BF16)


def _nsa_attention(q, gates, kc, vc, ks, vs, kw, vw, bsz, seq):
    m = q.shape[0]
    nq = seq // Q_TILE
    nc = seq // CMP_STRIDE
    cos_t, sin_t = _rope_tables(seq, 1)
    c2s_t = _cmp_to_slc_t(seq)
    gw = GQA_GROUP * HEAD_DIM
    kv_spec = pl.BlockSpec((seq, HEAD_DIM), lambda b, h, t: (b, h))
    cmp_spec = pl.BlockSpec((1, 1, nc, HEAD_DIM), lambda b, h, t: (b, h, 0, 0))
    tab_spec = pl.BlockSpec((Q_TILE, HEAD_DIM), lambda b, h, t: (t, 0))
    return pl.pallas_call(
        functools.partial(_attn_kernel, seq=seq),
        out_shape=jax.ShapeDtypeStruct((m, N_HEADS * HEAD_DIM), F32),
        grid=(bsz, N_KV_HEADS, nq),
        in_specs=[pl.BlockSpec((Q_TILE, gw), lambda b, h, t: (b * nq + t, h)),
                  pl.BlockSpec((Q_TILE, LANES), lambda b, h, t: (b * nq + t, h)),
                  tab_spec, tab_spec, cmp_spec, cmp_spec,
                  pl.BlockSpec(c2s_t.shape, lambda b, h, t: (0, 0)),
                  pl.BlockSpec((seq, 2 * HEAD_DIM), lambda b, h, t: (b, h)), kv_spec, kv_spec, kv_spec],
        out_specs=pl.BlockSpec((Q_TILE, gw), lambda b, h, t: (b * nq + t, h)),
        compiler_params=_cparams(("parallel", "parallel", "arbitrary")),
        name="nsa_attention",
    )(q, gates, cos_t, sin_t, kc, vc, c2s_t, ks, vs, kw, vw)


def _s5_operators(a_re, a_im, log_dt, b_re, b_im, c_re, c_im):
    hp = lax.Precision.HIGHEST
    n_g = a_re.shape[0]
    gpb = SSM_GROUPS_PER_BLOCK
    n_gb = n_g // gpb
    t_c, hh, pp = SSM_CHUNK, SSM_GROUP, SSM_STATE
    lam = lax.complex(a_re.astype(F32), a_im.astype(F32))
    dt = jnp.exp(log_dt.astype(F32))[:, None]
    a_bar = jnp.exp(lam * dt)
    b_bar = ((a_bar - 1.0) / lam)[:, :, None] * lax.complex(b_re.astype(F32), b_im.astype(F32))
    c_mat = lax.complex(c_re.astype(F32), c_im.astype(F32))
    steps = jnp.arange(t_c + 1, dtype=F32)[:, None, None]
    pw = jnp.exp((lam * dt)[None] * steps)

    def blockdiag(x, r, c):
        tile = jnp.tile(jnp.eye(c, dtype=F32), (1, gpb))
        rows = jnp.arange(gpb * r)[:, None] // r
        cols = jnp.arange(gpb * c)[None, :] // c
        tiled = jnp.einsum('...c,cd->...d', x, tile, precision=hp)
        return jnp.where(rows == cols, tiled, 0.0).astype(BF16)

    k_lag = jnp.einsum('ghp,kgp,gpi->kgih', c_mat, pw[:t_c], b_bar, precision=hp).real
    w_intra = blockdiag(k_lag.reshape(t_c, n_gb, gpb * hh, hh), hh, hh)
    w_intra = w_intra.transpose(1, 2, 0, 3).reshape(n_gb, LANES, t_c * LANES)
    p_in = jnp.swapaxes(pw[t_c - 1::-1][:, :, :, None] * b_bar[None], -1, -2)
    p_in = p_in.reshape(t_c, n_gb, gpb * hh, pp)
    s_in = jnp.concatenate([blockdiag(p_in.real, hh, pp), blockdiag(p_in.imag, hh, pp)], axis=-1)
    s_in = s_in.transpose(1, 0, 2, 3).reshape(n_gb, t_c * LANES, 2 * gpb * pp)
    m_out = jnp.swapaxes(c_mat[None] * pw[1:, :, None, :], -1, -2)
    m_out = m_out.reshape(t_c, n_gb, gpb * pp, hh)
    s_out = jnp.concatenate([blockdiag(m_out.real, pp, hh), blockdiag(-m_out.imag, pp, hh)], axis=2)
    s_out = s_out.transpose(1, 2, 0, 3).reshape(n_gb, 2 * gpb * pp, t_c * LANES)
    k_steps = jnp.arange(SCAN_TABLE_ROWS, dtype=F32)[:, None, None] * float(t_c)
    a_pow = jnp.exp((lam * dt)[None] * k_steps).reshape(SCAN_TABLE_ROWS, n_gb, gpb * pp).swapaxes(0, 1)
    return s_in, w_intra, s_out, a_pow.real.astype(F32), a_pow.imag.astype(F32)


def _s5_kernel(u_ref, sin_ref, wi_ref, sout_ref, ar_ref, ai_ref, d_ref, o_ref, xin_ref, xprev_ref, wt_ref):
    t_c = SSM_CHUNK
    nc = u_ref.shape[0] // t_c
    half = ar_ref.shape[-1]

    @pl.when(pl.program_id(1) == 0)
    def _():
        wt_ref[...] = jnp.zeros(wt_ref.shape, wt_ref.dtype)
        for tau in range(t_c):
            wt_ref[tau * LANES:(tau + 1) * LANES, tau * LANES:] = wi_ref[0, :, :(t_c - tau) * LANES]

    def step_rows(ref, t):
        return ref.at[pl.ds(t, nc, stride=t_c), :]

    u_flat = jnp.concatenate([step_rows(u_ref, tau)[...].astype(BF16) for tau in range(t_c)], axis=1)
    xin = jnp.dot(u_flat, sin_ref[0], preferred_element_type=F32)
    x_r, x_i = xin[:, :half], xin[:, half:]
    r_in_grp = jnp.bitwise_and(lax.broadcasted_iota(jnp.int32, (nc, 1), 0), SCAN_GROUP - 1)
    d = 1
    while d < SCAN_GROUP:
        keep = jnp.where(r_in_grp >= d, 1.0, 0.0)
        s_r, s_i = pltpu.roll(x_r, d, 0) * keep, pltpu.roll(x_i, d, 0) * keep
        ad_r, ad_i = ar_ref[0, d:d + 1, :], ai_ref[0, d:d + 1, :]
        x_r, x_i = x_r + ad_r * s_r - ad_i * s_i, x_i + ad_r * s_i + ad_i * s_r
        d *= 2
    keep = jnp.where(r_in_grp >= 1, 1.0, 0.0)
    xin_ref[:, :half] = x_r
    xin_ref[:, half:] = x_i
    xprev_ref[:, :half] = pltpu.roll(x_r, 1, 0) * keep
    xprev_ref[:, half:] = pltpu.roll(x_i, 1, 0) * keep
    ak_r, ak_i = ar_ref[0, 0:SCAN_GROUP, :], ai_ref[0, 0:SCAN_GROUP, :]
    ag_r, ag_i = ar_ref[0, SCAN_GROUP:SCAN_GROUP + 1, :], ai_ref[0, SCAN_GROUP:SCAN_GROUP + 1, :]

    def group(g, carry):
        c_r, c_i = carry
        rows = pl.ds(pl.multiple_of(g * SCAN_GROUP, SCAN_GROUP), SCAN_GROUP)
        xprev_ref[rows, :half] = xprev_ref[rows, :half] + ak_r * c_r - ak_i * c_i
        xprev_ref[rows, half:] = xprev_ref[rows, half:] + ak_r * c_i + ak_i * c_r
        last = xin_ref[pl.ds(g * SCAN_GROUP + SCAN_GROUP - 1, 1), :]
        return (last[:, :half] + ag_r * c_r - ag_i * c_i, last[:, half:] + ag_r * c_i + ag_i * c_r)

    zero = jnp.zeros((1, half), F32)
    lax.fori_loop(0, nc // SCAN_GROUP, group, (zero, zero))
    y = (jnp.dot(u_flat, wt_ref[...], preferred_element_type=F32)
         + jnp.dot(xprev_ref[...].astype(BF16), sout_ref[0], preferred_element_type=F32))
    for t in range(t_c):
        step_rows(o_ref, t)[...] = _gelu_tanh(y[:, t * LANES:(t + 1) * LANES] + d_ref[0] * step_rows(u_ref, t)[...])


def _s5_scan(u, bsz, seq, ops, d_skip):
    d_ssm = u.shape[1]
    t_c = SSM_CHUNK
    s_in, w_intra, s_out, a_r, a_i = ops
    n_gb = d_ssm // LANES
    nc = seq // t_c
    st = s_in.shape[-1]
    return pl.pallas_call(
        _s5_kernel,
        out_shape=jax.ShapeDtypeStruct(u.shape, F32),
        grid=(n_gb, bsz),
        in_specs=[pl.BlockSpec((seq, LANES), lambda g, b: (b, g)),
                  pl.BlockSpec((1,) + s_in.shape[1:], lambda g, b: (g, 0, 0)),
                  pl.BlockSpec((1,) + w_intra.shape[1:], lambda g, b: (g, 0, 0)),
                  pl.BlockSpec((1,) + s_out.shape[1:], lambda g, b: (g, 0, 0)),
                  pl.BlockSpec((1, SCAN_TABLE_ROWS, st // 2), lambda g, b: (g, 0, 0)),
                  pl.BlockSpec((1, SCAN_TABLE_ROWS, st // 2), lambda g, b: (g, 0, 0)),
                  pl.BlockSpec((1, 1, LANES), lambda g, b: (g, 0, 0))],
        out_specs=pl.BlockSpec((seq, LANES), lambda g, b: (b, g)),
        scratch_shapes=[pltpu.VMEM((nc, st), F32), pltpu.VMEM((nc, st), F32),
                        pltpu.VMEM((t_c * LANES, t_c * LANES), BF16)],
        compiler_params=_cparams(("arbitrary", "arbitrary")),
        name="s5_scan",
    )(u, s_in, w_intra, s_out, a_r, a_i, d_skip.reshape(n_gb, 1, LANES))


def _mixnorm_kernel(a_ref, s_ref, ga_ref, gs_ref, o_ref):
    da = a_ref.shape[-1]
    a = a_ref[...]
    s = s_ref[...]
    o_ref[:, :da] = (a * lax.rsqrt(jnp.mean(a * a, axis=-1, keepdims=True) + EPS) * ga_ref[...]).astype(BF16)
    o_ref[:, da:] = (s * lax.rsqrt(jnp.mean(s * s, axis=-1, keepdims=True) + EPS) * gs_ref[...]).astype(BF16)


def _mixnorm(attn, ssm, g_attn, g_ssm, tm=256):
    m, da = attn.shape
    ds = ssm.shape[1]
    return pl.pallas_call(
        _mixnorm_kernel,
        out_shape=jax.ShapeDtypeStruct((m, da + ds), BF16),
        grid=(m // tm,),
        in_specs=[pl.BlockSpec((tm, da), lambda i: (i, 0)), pl.BlockSpec((tm, ds), lambda i: (i, 0)),
                  pl.BlockSpec((1, da), lambda i: (0, 0)), pl.BlockSpec((1, ds), lambda i: (0, 0))],
        out_specs=pl.BlockSpec((tm, da + ds), lambda i: (i, 0)),
        compiler_params=_cparams(("parallel",)),
        name="mixnorm",
    )(attn, ssm, g_attn.reshape(1, da), g_ssm.reshape(1, ds))


def _split_bf16(x):
    hi = x.astype(BF16)
    return hi, (x - hi.astype(F32)).astype(BF16)


def _lane_max(x):
    return jnp.max(x, axis=-1, keepdims=True)


def _first_lane_of(cond_val, lane_f):
    return jnp.min(jnp.where(cond_val > 0.0, lane_f, float(LANES)), axis=-1, keepdims=True)


def _ffn_norm_router_kernel(h_ref, g_ref, whi_ref, wlo_ref, b_ref, hn_ref, route_ref):
    x = h_ref[...]
    hn = x * lax.rsqrt(jnp.mean(x * x, axis=-1, keepdims=True) + EPS) * g_ref[...]
    hn_ref[...] = _pack_bf16_pairs(hn)
    hi, lo = _split_bf16(hn)
    logits = (jnp.dot(hi, whi_ref[...], preferred_element_type=F32)
              + jnp.dot(lo, whi_ref[...], preferred_element_type=F32)
              + jnp.dot(hi, wlo_ref[...], preferred_element_type=F32)) + b_ref[...]
    lane = lax.broadcasted_iota(jnp.int32, logits.shape, 1)
    lane_f = lane.astype(F32)
    in_c = jnp.where(lane < N_EXPERT_GROUPS, 1.0, 0.0)
    c_l = jnp.where(in_c > 0.0, logits, NEG_INF)
    c_e = jnp.exp(c_l - _lane_max(c_l)) * in_c
    p_c = c_e / jnp.sum(c_e, axis=-1, keepdims=True)
    w_grp = _lane_max(p_c)
    grp = _first_lane_of(jnp.where(p_c == w_grp, in_c, 0.0), lane_f)
    lo_lane = float(N_EXPERT_GROUPS) + grp * float(EXPERTS_PER_GROUP)
    in_f = jnp.where(lane_f >= lo_lane, jnp.where(lane_f < lo_lane + float(EXPERTS_PER_GROUP), 1.0, 0.0), 0.0)
    f_l = jnp.where(in_f > 0.0, logits, NEG_INF)
    f_e = jnp.exp(f_l - _lane_max(f_l)) * in_f
    p_f = f_e / jnp.sum(f_e, axis=-1, keepdims=True)
    p_f = jnp.where(in_f > 0.0, p_f, -1.0)
    p1 = _lane_max(p_f)
    i1 = _first_lane_of(jnp.where(p_f == p1, in_f, 0.0), lane_f)
    rest = jnp.where(lane_f == i1, -1.0, p_f)
    p2 = _lane_max(rest)
    i2 = _first_lane_of(jnp.where(rest == p2, in_f, 0.0) * jnp.where(lane_f == i1, 0.0, 1.0), lane_f)
    denom = p1 + p2
    vals = (w_grp * p1 / denom, w_grp * p2 / denom,
            i1 - float(N_EXPERT_GROUPS), i2 - float(N_EXPERT_GROUPS))
    out = jnp.zeros(logits.shape, F32)
    for k, v in enumerate(vals):
        out = jnp.where(lane == k, v, out)
    route_ref[...] = out


def _ffn_norm_router(h, g, w_router, b_router, tm=256):
    m, d = h.shape
    n = w_router.shape[1]
    w_hi, w_lo = _split_bf16(w_router)
    return pl.pallas_call(
        _ffn_norm_router_kernel,
        out_shape=(jax.ShapeDtypeStruct((m, d // 2), jnp.uint32), jax.ShapeDtypeStruct((m, n), F32)),
        grid=(m // tm,),
        in_specs=[pl.BlockSpec((tm, d), lambda i: (i, 0)), pl.BlockSpec((1, d), lambda i: (0, 0)),
                  pl.BlockSpec((d, n), lambda i: (0, 0)), pl.BlockSpec((d, n), lambda i: (0, 0)),
                  pl.BlockSpec((1, n), lambda i: (0, 0))],
        out_specs=(pl.BlockSpec((tm, d // 2), lambda i: (i, 0)), pl.BlockSpec((tm, n), lambda i: (i, 0))),
        compiler_params=_cparams(("parallel",)),
        name="ffn_norm_router",
    )(h, g.reshape(1, d), w_hi, w_lo, b_router.reshape(1, n))


def _dispatch_plan(expert, tm):
    n_tok = expert.shape[0]
    n_assign = n_tok * TOP_K
    n_blocks = -(-(n_assign + N_EXPERTS * (MOE_BLOCK - 1)) // MOE_BLOCK)
    n_rows = n_blocks * MOE_BLOCK
    flat = expert.reshape(n_assign)
    ids = jnp.arange(n_assign, dtype=jnp.int32)
    _, order = lax.sort_key_val(flat, ids)
    _, inv = lax.sort_key_val(order, ids)
    e_ids = jnp.arange(N_EXPERTS, dtype=jnp.int32)
    counts = jnp.sum((flat[:, None] == e_ids[None, :]).astype(jnp.int32), axis=0)
    start = jnp.cumsum(counts) - counts
    padded = (counts + MOE_BLOCK - 1) // MOE_BLOCK * MOE_BLOCK
    pad_end = jnp.cumsum(padded)
    pad_start = pad_end - padded
    rows = jnp.arange(n_rows, dtype=jnp.int32)
    row_e = jnp.minimum(jnp.sum((pad_end[None, :] <= rows[:, None]).astype(jnp.int32), axis=1), N_EXPERTS - 1)
    slot = rows - pad_start[row_e]
    valid = (slot < counts[row_e]) & (rows < pad_end[-1])
    src = jnp.clip(start[row_e] + slot, 0, n_assign - 1)
    buf_tok = jnp.where(valid, order[src] // TOP_K, 0).astype(jnp.int32)
    dest = (pad_start[flat] + inv - start[flat]).astype(jnp.int32)
    dest = dest.reshape(n_tok // tm, tm, TOP_K).transpose(0, 2, 1).reshape(-1)
    plan = jnp.concatenate([pad_start // MOE_BLOCK, padded // MOE_BLOCK,
                            pad_end[-1:] // MOE_BLOCK]).astype(jnp.int32)
    return plan, buf_tok, dest, n_blocks


def _zero_fill_spare(plan_ref, zero_vmem, out_hbm, col0, width, sem, n_blocks):
    n_used = plan_ref[2 * N_EXPERTS]

    def cp(blk):
        return pltpu.make_async_copy(
            zero_vmem, out_hbm.at[pl.ds(pl.multiple_of(blk * MOE_BLOCK, MOE_BLOCK), MOE_BLOCK), pl.ds(col0, width)],
            sem)

    def start(blk, c):
        cp(blk).start(priority=SIDE_DMA_PRIORITY)
        return c

    def wait(blk, c):
        cp(blk).wait()
        return c

    lax.fori_loop(n_used, n_blocks, start, 0)
    lax.fori_loop(n_used, n_blocks, wait, 0)


def _pack_bf16_pairs(x):
    w = x.shape[1] // 2
    hi = lax.bitcast_convert_type(x[:, :w].astype(BF16).astype(F32), jnp.uint32)
    lo = lax.bitcast_convert_type(x[:, w:].astype(BF16).astype(F32), jnp.uint32)
    return hi | (lo >> 16)


def _unpack_bf16_pairs(p):
    first = lax.bitcast_convert_type(p & jnp.uint32(0xFFFF0000), F32)
    second = lax.bitcast_convert_type(p << 16, F32)
    return jnp.concatenate([first, second], axis=1).astype(BF16)


def _moe_up_kernel(plan_ref, tok_ref, x_hbm, wg_ref, wu_ref, act_hbm, wg_b, wu_b, xs, gbuf, obuf, gsem, osem,
                   *, n_blocks):
    e = pl.program_id(0)
    c = pl.program_id(1)
    n_c = pl.num_programs(1)
    tn = wg_b.shape[1]
    col0 = pl.multiple_of(c * tn, tn)
    b0 = plan_ref[e]
    nb = plan_ref[N_EXPERTS + e]
    n_res = jnp.minimum(nb, MOE_X_CAP)

    def row_copy(blk, r, slot):
        tok = tok_ref[blk * MOE_BLOCK + r]
        return pltpu.make_async_copy(x_hbm.at[pl.ds(tok, 1), :], gbuf.at[slot, pl.ds(r, 1), :], gsem.at[slot])

    def gather_start(blk, slot):
        lax.fori_loop(0, MOE_BLOCK, lambda r, cr: (row_copy(blk, r, slot).start(priority=SIDE_DMA_PRIORITY), cr)[1], 0, unroll=DMA_UNROLL)

    def gather_wait(blk, slot):
        lax.fori_loop(0, MOE_BLOCK, lambda r, cr: (row_copy(blk, r, slot).wait(), cr)[1], 0, unroll=DMA_UNROLL)

    def out_copy(blk, slot):
        rows = pl.ds(pl.multiple_of(blk * MOE_BLOCK, MOE_BLOCK), MOE_BLOCK)
        return pltpu.make_async_copy(obuf.at[slot], act_hbm.at[rows, pl.ds(col0, tn)], osem.at[slot])

    def emit(x, j):
        slot = lax.rem(j, 2)
        g = jnp.dot(x, wg_b[...], preferred_element_type=F32)
        u = jnp.dot(x, wu_b[...], preferred_element_type=F32)

        @pl.when(j >= 2)
        def _():
            out_copy(b0 + j - 2, slot).wait()

        obuf[slot] = (g * _sigmoid(g) * u).astype(BF16)
        out_copy(b0 + j, slot).start(priority=SIDE_DMA_PRIORITY)

    @pl.when(nb > 0)
    def _():
        wg_b[...] = wg_ref[0].astype(BF16)
        wu_b[...] = wu_ref[0].astype(BF16)

    @pl.when(c == 0)
    def _():
        @pl.when(jnp.logical_and(e == 0, nb > 0))
        def _():
            gather_start(b0, 0)

        def body(j, carry):
            slot = lax.rem(j, 2)
            gather_wait(b0 + j, slot)

            @pl.when(j + 1 < nb)
            def _():
                gather_start(b0 + j + 1, 1 - slot)

            x = _unpack_bf16_pairs(gbuf[slot])

            @pl.when(j < MOE_X_CAP)
            def _():
                xs[jnp.minimum(j, MOE_X_CAP - 1)] = x

            emit(x, j)
            return carry

        lax.fori_loop(0, nb, body, 0)

    @pl.when(c > 0)
    def _():
        def resident(j, carry):
            emit(xs[j], j)
            return carry

        def regathered(j, carry):
            gather_start(b0 + j, 0)
            gather_wait(b0 + j, 0)
            emit(_unpack_bf16_pairs(gbuf[0]), j)
            return carry

        lax.fori_loop(0, n_res, resident, 0)
        lax.fori_loop(n_res, nb, regathered, 0)

    @pl.when(nb >= 2)
    def _():
        out_copy(b0 + nb - 2, lax.rem(nb, 2)).wait()

    @pl.when(nb >= 1)
    def _():
        out_copy(b0 + nb - 1, lax.rem(nb + 1, 2)).wait()

    @pl.when(jnp.logical_and(c == n_c - 1, e + 1 < N_EXPERTS))
    def _():
        e_next = jnp.minimum(e + 1, N_EXPERTS - 1)

        @pl.when(plan_ref[N_EXPERTS + e_next] > 0)
        def _():
            gather_start(plan_ref[e_next], 0)

    @pl.when(e == N_EXPERTS - 1)
    def _():
        obuf[0] = jnp.zeros(obuf.shape[1:], obuf.dtype)
        _zero_fill_spare(plan_ref, obuf.at[0], act_hbm, col0, tn, osem.at[0], n_blocks)


def _moe_up(hn_packed, w_gate, w_up, plan, buf_tok, n_blocks, n_chunks=4):
    d = w_gate.shape[-2]
    d_e = w_gate.shape[-1]
    tn = d_e // n_chunks
    w_spec = pl.BlockSpec((1, d, tn), lambda e, c, plan, tok: (e, 0, c))
    return pl.pallas_call(
        functools.partial(_moe_up_kernel, n_blocks=n_blocks),
        out_shape=jax.ShapeDtypeStruct((n_blocks * MOE_BLOCK, d_e), BF16),
        grid_spec=pltpu.PrefetchScalarGridSpec(
            num_scalar_prefetch=2,
            grid=(N_EXPERTS, n_chunks),
            in_specs=[pl.BlockSpec(memory_space=pl.ANY), w_spec, w_spec],
            out_specs=pl.BlockSpec(memory_space=pl.ANY),
            scratch_shapes=[pltpu.VMEM((d, tn), BF16), pltpu.VMEM((d, tn), BF16),
                            pltpu.VMEM((MOE_X_CAP, MOE_BLOCK, d), BF16),
                            pltpu.VMEM((2, MOE_BLOCK, d // 2), jnp.uint32),
                            pltpu.VMEM((2, MOE_BLOCK, tn), BF16),
                            pltpu.SemaphoreType.DMA((2,)), pltpu.SemaphoreType.DMA((2,))]),
        compiler_params=_cparams(("arbitrary", "arbitrary")),
        name="moe_up",
    )(plan, buf_tok, hn_packed, w_gate, w_up)


def _moe_down_kernel(plan_ref, act_hbm, wd_ref, y_hbm, wd_b, abuf, obuf, asem, osem, *, n_blocks):
    e = pl.program_id(0)
    c = pl.program_id(1)
    tn = wd_b.shape[1]
    col0 = pl.multiple_of(c * tn, tn)
    b0 = plan_ref[e]
    nb = plan_ref[N_EXPERTS + e]

    def rows_of(blk):
        return pl.ds(pl.multiple_of(blk * MOE_BLOCK, MOE_BLOCK), MOE_BLOCK)

    def in_copy(blk, slot):
        return pltpu.make_async_copy(act_hbm.at[rows_of(blk), :], abuf.at[slot], asem.at[slot])

    def out_copy(blk, slot):
        return pltpu.make_async_copy(obuf.at[slot], y_hbm.at[rows_of(blk), pl.ds(col0, tn)], osem.at[slot])

    @pl.when(jnp.logical_and(jnp.logical_and(e == 0, c == 0), nb > 0))
    def _():
        in_copy(b0, 0).start(priority=SIDE_DMA_PRIORITY)

    @pl.when(nb > 0)
    def _():
        wd_b[...] = wd_ref[0].astype(BF16)

    def body(j, carry):
        slot = lax.rem(j, 2)
        in_copy(b0 + j, slot).wait()

        @pl.when(j + 1 < nb)
        def _():
            in_copy(b0 + j + 1, 1 - slot).start(priority=SIDE_DMA_PRIORITY)

        y = jnp.dot(abuf[slot], wd_b[...], preferred_element_type=F32)

        @pl.when(j >= 2)
        def _():
            out_copy(b0 + j - 2, slot).wait()

        obuf[slot] = y
        out_copy(b0 + j, slot).start(priority=SIDE_DMA_PRIORITY)
        return carry

    lax.fori_loop(0, nb, body, 0)

    @pl.when(nb >= 2)
    def _():
        out_copy(b0 + nb - 2, lax.rem(nb, 2)).wait()

    @pl.when(nb >= 1)
    def _():
        out_copy(b0 + nb - 1, lax.rem(nb + 1, 2)).wait()

    n_c = pl.num_programs(1)
    same_expert = c + 1 < n_c
    e_next = jnp.where(same_expert, e, jnp.minimum(e + 1, N_EXPERTS - 1))

    @pl.when(jnp.logical_and(jnp.logical_or(same_expert, e + 1 < N_EXPERTS), plan_ref[N_EXPERTS + e_next] > 0))
    def _():
        in_copy(plan_ref[e_next], 0).start(priority=SIDE_DMA_PRIORITY)

    @pl.when(e == N_EXPERTS - 1)
    def _():
        obuf[0] = jnp.zeros(obuf.shape[1:], obuf.dtype)
        _zero_fill_spare(plan_ref, obuf.at[0], y_hbm, col0, tn, osem.at[0], n_blocks)


def _moe_down(act, w_down, plan, n_blocks, n_chunks=2):
    n_rows, d_e = act.shape
    d = w_down.shape[-1]
    tn = d // n_chunks
    return pl.pallas_call(
        functools.partial(_moe_down_kernel, n_blocks=n_blocks),
        out_shape=jax.ShapeDtypeStruct((n_rows, d), F32),
        grid_spec=pltpu.PrefetchScalarGridSpec(
            num_scalar_prefetch=1,
            grid=(N_EXPERTS, n_chunks),
            in_specs=[pl.BlockSpec(memory_space=pl.ANY),
                      pl.BlockSpec((1, d_e, tn), lambda e, c, plan: (e, 0, c))],
            out_specs=pl.BlockSpec(memory_space=pl.ANY),
            scratch_shapes=[pltpu.VMEM((d_e, tn), BF16),
                            pltpu.VMEM((2, MOE_BLOCK, d_e), BF16), pltpu.VMEM((2, MOE_BLOCK, tn), F32),
                            pltpu.SemaphoreType.DMA((2,)), pltpu.SemaphoreType.DMA((2,))]),
        compiler_params=_cparams(("arbitrary", "arbitrary")),
        name="moe_down",
    )(plan, act, w_down)


def _combine_kernel(dest_ref, h_ref, route_ref, gn_ref, y_hbm, o_ref, rows_ref, sem):
    tm = h_ref.shape[0]
    n_rows = tm * TOP_K
    i = pl.program_id(0)
    n = pl.num_programs(0)

    def row_copy(step, r, slot):
        src = dest_ref[step * n_rows + r]
        return pltpu.make_async_copy(y_hbm.at[pl.ds(src, 1), :], rows_ref.at[slot, pl.ds(r, 1), :], sem.at[slot])

    def gather_start(step, slot):
        lax.fori_loop(0, n_rows, lambda r, cr: (row_copy(step, r, slot).start(priority=SIDE_DMA_PRIORITY), cr)[1], 0, unroll=DMA_UNROLL)

    def gather_wait(step, slot):
        lax.fori_loop(0, n_rows, lambda r, cr: (row_copy(step, r, slot).wait(), cr)[1], 0, unroll=DMA_UNROLL)

    slot = lax.rem(i, 2)

    @pl.when(i == 0)
    def _():
        gather_start(0, 0)

    @pl.when(i + 1 < n)
    def _():
        gather_start(i + 1, 1 - slot)

    gather_wait(i, slot)
    route = route_ref[...]
    rows = rows_ref[slot]
    h = h_ref[...] + route[:, 0:1] * rows[0:tm, :] + route[:, 1:2] * rows[tm:2 * tm, :]
    o_ref[...] = h * lax.rsqrt(jnp.mean(h * h, axis=-1, keepdims=True) + EPS) * gn_ref[...]


def _combine_final(h, route, dest, y_buf, g_final, tm=128):
    m, d = h.shape
    return pl.pallas_call(
        _combine_kernel,
        out_shape=jax.ShapeDtypeStruct((m, d), F32),
        grid_spec=pltpu.PrefetchScalarGridSpec(
            num_scalar_prefetch=1,
            grid=(m // tm,),
            in_specs=[pl.BlockSpec((tm, d), lambda i, dst: (i, 0)),
                      pl.BlockSpec((tm, LANES), lambda i, dst: (i, 0)),
                      pl.BlockSpec((1, d), lambda i, dst: (0, 0)),
                      pl.BlockSpec(memory_space=pl.ANY)],
            out_specs=pl.BlockSpec((tm, d), lambda i, dst: (i, 0)),
            scratch_shapes=[pltpu.VMEM((2, TOP_K * tm, d), F32), pltpu.SemaphoreType.DMA((2,))]),
        compiler_params=_cparams(("arbitrary",)),
        name="moe_combine_final",
    )(dest, h, route, g_final.reshape(1, d), y_buf)


def _gate_column_order():
    cols = []
    for hk in range(N_KV_HEADS):
        blk = [-1] * LANES
        for br in range(N_BRANCHES):
            for g in range(GQA_GROUP):
                blk[br * GQA_GROUP + g] = (hk * GQA_GROUP + g) * N_BRANCHES + br
        cols.extend(blk)
    return cols


def kernel(x, norm_mix, w_in, cmp_pe_k, cmp_w1_k, cmp_b1_k, cmp_w2_k, cmp_b2_k, cmp_pe_v, cmp_w1_v, cmp_b1_v, cmp_w2_v, cmp_b2_v, ssm_a_re, ssm_a_im, ssm_log_dt, ssm_b_re, ssm_b_im, ssm_c_re, ssm_c_im, ssm_d, ssm_w_glu, ssm_b_glu, norm_attn_out, norm_ssm_out, w_out, norm_ffn, router_w_coarse, router_b_coarse, router_w_fine, router_b_fine, w_gate, w_up, w_down, norm_final):
    bsz, seq, d_model = x.shape
    m = bsz * seq
    depth = w_in.shape[0]
    assert depth == 1, "the combine kernel fuses the final norm; stacked layers are not supported"
    layer = 0
    q_w = N_HEADS * HEAD_DIM
    kv_w = N_KV_HEADS * HEAD_DIM
    gate_w = N_HEADS * N_BRANCHES
    cut1, cut2 = q_w, q_w + 6 * kv_w
    cut3 = cut2 + gate_w
    h = x.astype(F32).reshape(m, d_model)
    gate_cols = jnp.array([c if c >= 0 else 0 for c in _gate_column_order()], jnp.int32)
    gate_keep = jnp.array([1.0 if c >= 0 else 0.0 for c in _gate_column_order()], F32)

    w_in_b = w_in[layer].astype(BF16)
    hn = _rmsnorm(h, norm_mix[layer], BF16)
    q = _matmul(hn, w_in_b[:, :cut1], name="proj_q")
    kv = _matmul(hn, w_in_b[:, cut1:cut2], name="proj_kv")
    w_gate_cols = (w_in[layer][:, cut2:cut3][:, gate_cols] * gate_keep[None, :]).astype(BF16)
    gates = _matmul(hn, w_gate_cols, tn=512, name="proj_gate")
    u = _matmul(hn, w_in_b[:, cut3:], name="proj_u")

    stack2 = lambda a, b: jnp.stack([a[layer], b[layer]])
    cmp_kv = _compress(kv, bsz, seq, stack2(cmp_pe_k, cmp_pe_v),
                       stack2(cmp_w1_k, cmp_w1_v).astype(BF16), stack2(cmp_b1_k, cmp_b1_v),
                       stack2(cmp_w2_k, cmp_w2_v).astype(BF16), stack2(cmp_b2_k, cmp_b2_v))
    ks, vs, kw, vw = _kv_prep(kv, seq)
    attn = _nsa_attention(q, gates, cmp_kv[0], cmp_kv[1], ks, vs, kw, vw, bsz, seq)

    ops = _s5_operators(ssm_a_re[layer], ssm_a_im[layer], ssm_log_dt[layer], ssm_b_re[layer],
                        ssm_b_im[layer], ssm_c_re[layer], ssm_c_im[layer])
    y = _s5_scan(u, bsz, seq, ops, ssm_d[layer].astype(F32))
    ssm = _glu_matmul(y, ssm_w_glu[layer].astype(BF16), ssm_b_glu[layer].astype(F32))

    mixed = _mixnorm(attn, ssm, norm_attn_out[layer], norm_ssm_out[layer])
    h = _matmul(mixed, w_out[layer].astype(BF16), residual=h, name="out_proj")

    n_r = N_EXPERT_GROUPS + N_EXPERTS
    w_router = jnp.pad(jnp.concatenate([router_w_coarse[layer], router_w_fine[layer]], axis=1),
                       ((0, 0), (0, LANES - n_r))).astype(F32)
    b_router = jnp.pad(jnp.concatenate([router_b_coarse[layer], router_b_fine[layer]]), (0, LANES - n_r))
    hn2, route = _ffn_norm_router(h, norm_ffn[layer], w_router, b_router.astype(F32))
    expert = route[:, 2:2 + TOP_K].astype(jnp.int32)
    tm_c = 128
    plan, buf_tok, dest, n_blocks = _dispatch_plan(expert, tm_c)
    act = _moe_up(hn2, w_gate[layer], w_up[layer], plan, buf_tok, n_blocks)
    y_buf = _moe_down(act, w_down[layer], plan, n_blocks)
    out = _combine_final(h, route, dest, y_buf, norm_final, tm=tm_c)
    return out.reshape(bsz, seq, d_model).astype(x.dtype)
```

```python
import functools
import math

import jax
import jax.numpy as jnp
from jax import lax
from jax.experimental import pallas as pl
from jax.experimental.pallas import tpu as pltpu

F32 = jnp.float32
BF16 = jnp.bfloat16

HEAD_DIM = 128
N_HEADS = 16
N_KV_HEADS = 4
GQA_GROUP = N_HEADS // N_KV_HEADS
ROT_DIM = HEAD_DIM // 4
ROPE_THETA = 500000.0
CMP_STRIDE = 16
CMP_BLOCK = 32
SLC_BLOCK = 64
N_SELECT = 16
WINDOW = 512
Q_TILE = 256
N_BRANCHES = 3
FORCE_BONUS = 1.0e4
NEG_INF = -1.0e30
MASK_BIG = 1.0e30
SSM_GROUP = 16
SSM_STATE = 64
SSM_CHUNK = 16
SSM_GROUPS_PER_BLOCK = 128 // SSM_GROUP
N_EXPERT_GROUPS = 8
EXPERTS_PER_GROUP = 8
N_EXPERTS = N_EXPERT_GROUPS * EXPERTS_PER_GROUP
TOP_K = 2
MOE_BLOCK = 128
EPS = 1.0e-6
LOG2_E = 1.4426950408889634

V7X_VMEM_LIMIT = 56 * 1024 * 1024
LANES = 128
SCAN_GROUP = 8
SCAN_TABLE_ROWS = 16
DMA_UNROLL = 8
SIDE_DMA_PRIORITY = 1
MOE_X_CAP = 12
W_DMA_SPLIT = 4


def _cparams(sem, vmem=V7X_VMEM_LIMIT):
    return pltpu.CompilerParams(dimension_semantics=sem, vmem_limit_bytes=vmem)


def _gelu_tanh(x):
    return 0.5 * x * (1.0 + jnp.tanh(math.sqrt(2.0 / math.pi) * (x + 0.044715 * (x * x * x))))


def _sigmoid(x):
    return 1.0 / (1.0 + jnp.exp(-x))


def _rmsnorm_kernel(x_ref, g_ref, o_ref):
    x = x_ref[...]
    ms = jnp.mean(x * x, axis=-1, keepdims=True)
    o_ref[...] = (x * lax.rsqrt(ms + EPS) * g_ref[...]).astype(o_ref.dtype)


def _rmsnorm(x, g, out_dtype, tm=256):
    m, d = x.shape
    return pl.pallas_call(
        _rmsnorm_kernel,
        out_shape=jax.ShapeDtypeStruct((m, d), out_dtype),
        grid=(m // tm,),
        in_specs=[pl.BlockSpec((tm, d), lambda i: (i, 0)),
                  pl.BlockSpec((1, d), lambda i: (0, 0))],
        out_specs=pl.BlockSpec((tm, d), lambda i: (i, 0)),
        compiler_params=_cparams(("parallel",)),
        name="rmsnorm",
    )(x, g.reshape(1, d))


def _mm_kernel(a_ref, w_ref, o_ref):
    o_ref[...] = jnp.dot(a_ref[...], w_ref[...], preferred_element_type=F32)


def _mm_glu_kernel(y_ref, w_ref, b_ref, yg_ref, o_ref):
    z = jnp.dot(y_ref[...].astype(BF16), w_ref[...], preferred_element_type=F32) + b_ref[...]
    o_ref[...] = yg_ref[...] * _sigmoid(z)


def _mm_residual_kernel(a_ref, w_ref, r_ref, o_ref):
    o_ref[...] = r_ref[...] + jnp.dot(a_ref[...], w_ref[...], preferred_element_type=F32)


def _matmul(a, w, *, residual=None, tm=512, tn=1024, name="matmul"):
    m, k = a.shape
    n = w.shape[1]
    tn = min(tn, n)
    assert m % tm == 0 and n % tn == 0
    a_spec = pl.BlockSpec((tm, k), lambda j, i: (i, 0))
    w_spec = pl.BlockSpec((k, tn), lambda j, i: (0, j))
    o_spec = pl.BlockSpec((tm, tn), lambda j, i: (i, j))
    if residual is not None:
        kern, ins, specs = _mm_residual_kernel, (a, w, residual), [a_spec, w_spec, o_spec]
    else:
        kern, ins, specs = _mm_kernel, (a, w), [a_spec, w_spec]
    return pl.pallas_call(
        kern,
        out_shape=jax.ShapeDtypeStruct((m, n), F32),
        grid=(n // tn, m // tm),
        in_specs=specs,
        out_specs=o_spec,
        compiler_params=_cparams(("parallel", "parallel")),
        name=name,
    )(*ins)


def _glu_matmul(y, w, bias, tm=512, tn=1024):
    m, k = y.shape
    n = w.shape[1]
    return pl.pallas_call(
        _mm_glu_kernel,
        out_shape=jax.ShapeDtypeStruct((m, n), F32),
        grid=(n // tn, m // tm),
        in_specs=[pl.BlockSpec((tm, k), lambda j, i: (i, 0)),
                  pl.BlockSpec((k, tn), lambda j, i: (0, j)),
                  pl.BlockSpec((1, tn), lambda j, i: (0, j)),
                  pl.BlockSpec((tm, tn), lambda j, i: (i, j))],
        out_specs=pl.BlockSpec((tm, tn), lambda j, i: (i, j)),
        compiler_params=_cparams(("parallel", "parallel")),
        name="ssm_glu",
    )(y, w, bias.reshape(1, n), y)


def _rope_tables(seq, reps):
    half = ROT_DIM // 2
    inv_freq = ROPE_THETA ** (-jnp.arange(half, dtype=F32) / half)
    ang = jnp.arange(seq).astype(F32)[:, None] * inv_freq[None, :]
    cos, sin = jnp.cos(ang), jnp.sin(ang)
    ones = jnp.ones((seq, HEAD_DIM - ROT_DIM), F32)
    cos_t = jnp.concatenate([cos, cos, ones], axis=-1)
    sin_t = jnp.concatenate([-sin, sin, 0.0 * ones], axis=-1)
    return jnp.tile(cos_t, (1, reps)), jnp.tile(sin_t, (1, reps))


def _rope(x, cos_t, sin_t):
    half = ROT_DIM // 2
    n = x.shape[-1]
    lane = jnp.bitwise_and(lax.broadcasted_iota(jnp.int32, x.shape, x.ndim - 1), HEAD_DIM - 1)
    partner = jnp.where(lane < half, pltpu.roll(x, n - half, x.ndim - 1), pltpu.roll(x, half, x.ndim - 1))
    return x * cos_t + partner * sin_t


def _kv_prep_kernel(slc_ref, win_ref, cos_ref, sin_ref, ks_ref, vs_ref, kw_ref, vw_ref, *, tiles_per_seq):
    tm, w = vs_ref.shape
    cos_t, sin_t = cos_ref[...], sin_ref[...]
    slc = slc_ref[...]
    win = win_ref[...]
    k_slc = _rope(slc[:, :w], cos_t, sin_t).astype(BF16)
    pos = lax.rem(pl.program_id(0), tiles_per_seq) * tm + lax.broadcasted_iota(jnp.int32, (tm, HEAD_DIM), 0)
    lane = lax.broadcasted_iota(jnp.int32, (tm, HEAD_DIM), 1)
    own_blk = jnp.where(lane == lax.shift_right_logical(pos, int(math.log2(SLC_BLOCK))), MASK_BIG, 0.0).astype(BF16)
    for hk in range(N_KV_HEADS):
        ks_ref[:, 2 * hk * HEAD_DIM:(2 * hk + 1) * HEAD_DIM] = k_slc[:, hk * HEAD_DIM:(hk + 1) * HEAD_DIM]
        ks_ref[:, (2 * hk + 1) * HEAD_DIM:(2 * hk + 2) * HEAD_DIM] = own_blk
    vs_ref[...] = slc[:, w:].astype(BF16)
    kw_ref[...] = _rope(win[:, :w], cos_t, sin_t).astype(BF16)
    vw_ref[...] = win[:, w:].astype(BF16)


def _kv_prep(kv, seq, tm=256):
    assert seq // SLC_BLOCK <= HEAD_DIM
    m = kv.shape[0]
    w = N_KV_HEADS * HEAD_DIM
    cos_t, sin_t = _rope_tables(seq, N_KV_HEADS)
    tpb = seq // tm
    out = jax.ShapeDtypeStruct((m, w), BF16)
    o_spec = pl.BlockSpec((tm, w), lambda i: (i, 0))
    t_spec = pl.BlockSpec((tm, w), lambda i: (i % tpb, 0))
    return pl.pallas_call(
        functools.partial(_kv_prep_kernel, tiles_per_seq=tpb),
        out_shape=(jax.ShapeDtypeStruct((m, 2 * w), BF16), out, out, out),
        grid=(m // tm,),
        in_specs=[pl.BlockSpec((tm, 2 * w), lambda i: (i, 1)),
                  pl.BlockSpec((tm, 2 * w), lambda i: (i, 2)),
                  t_spec, t_spec],
        out_specs=(pl.BlockSpec((tm, 2 * w), lambda i: (i, 0)), o_spec, o_spec, o_spec),
        compiler_params=_cparams(("parallel",)),
        name="kv_prep",
    )(kv, kv, cos_t, sin_t)


def _compress_kernel(x_ref, pe_ref, w1_ref, b1_ref, w2_ref, b2_ref, o_ref):
    nc = x_ref.shape[0] // CMP_STRIDE
    hid = w1_ref.shape[-1]
    top = jnp.zeros((nc, hid), F32)
    bot = jnp.zeros((nc, hid), F32)
    for j in range(CMP_STRIDE):
        xj = x_ref[pl.ds(j, nc, stride=CMP_STRIDE), :]
        top += jnp.dot((xj + pe_ref[0, j:j + 1, :]).astype(BF16), w1_ref[0, j],
                       preferred_element_type=F32)
        bot += jnp.dot((xj + pe_ref[0, CMP_STRIDE + j:CMP_STRIDE + j + 1, :]).astype(BF16),
                       w1_ref[0, CMP_STRIDE + j], preferred_element_type=F32)
    h = _gelu_tanh(top + pltpu.roll(bot, nc - 1, 0) + b1_ref[0])
    o_ref[0, 0, 0] = (jnp.dot(h.astype(BF16), w2_ref[0], preferred_element_type=F32) + b2_ref[0]).astype(BF16)


def _compress(kv, bsz, seq, pe, w1, b1, w2, b2):
    nc = seq // CMP_STRIDE
    hid = w1.shape[-1]
    w1r = w1.reshape(2, CMP_BLOCK, HEAD_DIM, hid)
    return pl.pallas_call(
        _compress_kernel,
        out_shape=jax.ShapeDtypeStruct((2, bsz, N_KV_HEADS, nc, HEAD_DIM), BF16),
        grid=(2, bsz, N_KV_HEADS),
        in_specs=[pl.BlockSpec((seq, HEAD_DIM), lambda s, b, h: (b, s * N_KV_HEADS + h)),
                  pl.BlockSpec((1, CMP_BLOCK, HEAD_DIM), lambda s, b, h: (s, 0, 0)),
                  pl.BlockSpec((1, CMP_BLOCK, HEAD_DIM, hid), lambda s, b, h: (s, 0, 0, 0)),
                  pl.BlockSpec((1, 1, hid), lambda s, b, h: (s, 0, 0)),
                  pl.BlockSpec((1, hid, HEAD_DIM), lambda s, b, h: (s, 0, 0)),
                  pl.BlockSpec((1, 1, HEAD_DIM), lambda s, b, h: (s, 0, 0))],
        out_specs=pl.BlockSpec((1, 1, 1, nc, HEAD_DIM), lambda s, b, h: (s, b, h, 0, 0)),
        compiler_params=_cparams(("parallel", "parallel", "parallel")),
        name="nsa_compress",
    )(kv, pe, w1r, b1.reshape(2, 1, hid), w2, b2.reshape(2, 1, HEAD_DIM))


def _attn_kernel(q_ref, gate_ref, cos_ref, sin_ref, kc_ref, vc_ref, c2s_ref,
                 ks_ref, vs_ref, kw_ref, vw_ref, o_ref, *, seq):
    qt = pl.program_id(2)
    s0 = qt * Q_TILE
    n_slc = seq // SLC_BLOCK
    n_cmp = kc_ref.shape[-2]
    scale = HEAD_DIM ** -0.5 * LOG2_E
    rows = GQA_GROUP * Q_TILE
    nt = (((1,), (1,)), ((), ()))

    def tile4(x):
        return jnp.concatenate([x] * GQA_GROUP, axis=0)

    q4 = q_ref[...]
    qp = jnp.concatenate([q4[:, g * HEAD_DIM:(g + 1) * HEAD_DIM] for g in range(GQA_GROUP)], axis=0) * scale
    qr_b = _rope(qp, tile4(cos_ref[...]), tile4(sin_ref[...])).astype(BF16)
    qp_b = qp.astype(BF16)
    t_q = s0 + lax.broadcasted_iota(jnp.int32, (Q_TILE, 1), 0)

    n_idx = lax.broadcasted_iota(jnp.int32, (Q_TILE, n_cmp), 1)
    c_ok = (n_idx * CMP_STRIDE + (CMP_BLOCK - 1)) <= t_q
    s_c = lax.dot_general(qp_b, kc_ref[0, 0], nt, preferred_element_type=F32) + tile4(jnp.where(c_ok, 0.0, NEG_INF))
    e_c = jnp.exp2(s_c - jnp.max(s_c, axis=-1, keepdims=True)) * tile4(jnp.where(c_ok, 1.0, 0.0))
    l_c = jnp.sum(e_c, axis=-1, keepdims=True)
    p_c = e_c / jnp.where(l_c > 0.0, l_c, 1.0)
    o_c = jnp.dot(p_c.astype(BF16), vc_ref[0, 0], preferred_element_type=F32)
    p_sum = (p_c[0:Q_TILE] + p_c[Q_TILE:2 * Q_TILE]) + (p_c[2 * Q_TILE:3 * Q_TILE] + p_c[3 * Q_TILE:])
    p_hi = p_sum.astype(BF16)
    r1 = p_sum - p_hi.astype(F32)
    p_mid = r1.astype(BF16)
    p_lo = (r1 - p_mid.astype(F32)).astype(BF16)
    c2s = c2s_ref[...]
    imp_t = (lax.dot_general(c2s, p_hi, nt, preferred_element_type=F32)
             + lax.dot_general(c2s, p_mid, nt, preferred_element_type=F32)
             + lax.dot_general(c2s, p_lo, nt, preferred_element_type=F32))

    sid = lax.broadcasted_iota(jnp.int32, (n_slc, Q_TILE), 0)
    t_l = s0 + lax.broadcasted_iota(jnp.int32, (n_slc, Q_TILE), 1)
    cur = lax.shift_right_logical(t_l, int(math.log2(SLC_BLOCK)))
    causal_blk = sid <= cur
    forced = jnp.where(sid == 0, 1.0, jnp.where(sid == cur, 1.0, jnp.where(sid == cur - 1, 1.0, 0.0)))
    work = jnp.where(causal_blk, imp_t + FORCE_BONUS * forced, NEG_INF)
    sid_f = sid.astype(F32)
    sel_t = jnp.zeros((n_slc, Q_TILE), F32)
    for _ in range(min(N_SELECT, n_slc)):
        mx = jnp.max(work, axis=0, keepdims=True)
        first = jnp.min(jnp.where(work == mx, sid_f, float(n_slc)), axis=0, keepdims=True)
        pick = sid_f == first
        sel_t = jnp.where(pick, 1.0, sel_t)
        work = jnp.where(pick, -3.0e38, work)
    sel_t = jnp.where(causal_blk, sel_t, 0.0)
    unsel = jnp.concatenate([sel_t - 1.0, jnp.zeros((HEAD_DIM - n_slc, Q_TILE), F32)], axis=0).T
    q_aug = jnp.concatenate([qr_b, tile4(unsel.astype(BF16))], axis=1)

    kt = 512

    def slc_tile(j, carry, diag_bias):
        m, l, acc = carry
        k0 = pl.multiple_of(j * kt, kt)
        s = lax.dot_general(q_aug, ks_ref[pl.ds(k0, kt), :], nt, preferred_element_type=F32)
        if diag_bias is not None:
            s = s + diag_bias
        m_new = jnp.maximum(m, jnp.max(s, axis=-1, keepdims=True))
        alpha = jnp.exp2(m - m_new)
        p = jnp.exp2(s - m_new)
        l = alpha * l + jnp.sum(p, axis=-1, keepdims=True)
        acc = alpha * acc + jnp.dot(p.astype(BF16), vs_ref[pl.ds(k0, kt), :], preferred_element_type=F32)
        return m_new, l, acc

    slab = WINDOW + Q_TILE
    w0 = pl.multiple_of(jnp.maximum(s0 - WINDOW, 0), Q_TILE)
    dist = t_q - (w0 + lax.broadcasted_iota(jnp.int32, (Q_TILE, slab), 1))
    w_bias = jnp.where(dist >= 0, jnp.where(dist < WINDOW, 0.0, NEG_INF), NEG_INF)
    s_w = lax.dot_general(qr_b, kw_ref[pl.ds(w0, slab), :], nt, preferred_element_type=F32) + tile4(w_bias)
    e_w = jnp.exp2(s_w - jnp.max(s_w, axis=-1, keepdims=True))
    o_w = (jnp.dot(e_w.astype(BF16), vw_ref[pl.ds(w0, slab), :], preferred_element_type=F32)
           / jnp.sum(e_w, axis=-1, keepdims=True))

    j_diag = lax.shift_right_logical(s0, int(math.log2(kt)))
    kpos = j_diag * kt + lax.broadcasted_iota(jnp.int32, (Q_TILE, kt), 1)
    init = (jnp.full((rows, 1), NEG_INF, F32), jnp.zeros((rows, 1), F32), jnp.zeros((rows, HEAD_DIM), F32))
    carry = slc_tile(j_diag, init, tile4(jnp.where(kpos <= t_q, 0.0, NEG_INF)))
    carry = lax.fori_loop(0, lax.shift_right_logical(j_diag, 1),
                          lambda p, c: slc_tile(2 * p + 1, slc_tile(2 * p, c, None), None), carry)
    _, l_s, acc_s = lax.cond(jnp.bitwise_and(j_diag, 1) == 1,
                             lambda c: slc_tile(j_diag - 1, c, None), lambda c: c, carry)
    o_s = acc_s / l_s

    sg = _sigmoid(gate_ref[...])
    for g in range(GQA_GROUP):
        r = slice(g * Q_TILE, (g + 1) * Q_TILE)
        o_ref[:, g * HEAD_DIM:(g + 1) * HEAD_DIM] = (
            sg[:, g:g + 1] * o_c[r]
            + sg[:, GQA_GROUP + g:GQA_GROUP + g + 1] * o_s[r]
            + sg[:, 2 * GQA_GROUP + g:2 * GQA_GROUP + g + 1] * o_w[r])


def _cmp_to_slc_t(seq):
    n_cmp = seq // CMP_STRIDE - 1
    n_slc = seq // SLC_BLOCK
    cmp_start = jnp.arange(n_cmp) * CMP_STRIDE
    slc_start = jnp.arange(n_slc) * SLC_BLOCK
    overlap = jnp.clip(jnp.minimum(cmp_start[:, None] + CMP_BLOCK, slc_start[None, :] + SLC_BLOCK)
                       - jnp.maximum(cmp_start[:, None], slc_start[None, :]), 0, None)
    c2s = overlap.astype(F32) / CMP_BLOCK
    return jnp.pad(c2s, ((0, 1), (0, 0))).T.astype(BF16)


def _nsa_attention(q, gates, kc, vc, ks, vs, kw, vw, bsz, seq):
    m = q.shape[0]
    nq = seq // Q_TILE
    nc = seq // CMP_STRIDE
    cos_t, sin_t = _rope_tables(seq, 1)
    c2s_t = _cmp_to_slc_t(seq)
    gw = GQA_GROUP * HEAD_DIM
    kv_spec = pl.BlockSpec((seq, HEAD_DIM), lambda b, h, t: (b, h))
    cmp_spec = pl.BlockSpec((1, 1, nc, HEAD_DIM), lambda b, h, t: (b, h, 0, 0))
    tab_spec = pl.BlockSpec((Q_TILE, HEAD_DIM), lambda b, h, t: (t, 0))
    return pl.pallas_call(
        functools.partial(_attn_kernel, seq=seq),
        out_shape=jax.ShapeDtypeStruct((m, N_HEADS * HEAD_DIM), F32),
        grid=(bsz, N_KV_HEADS, nq),
        in_specs=[pl.BlockSpec((Q_TILE, gw), lambda b, h, t: (b * nq + t, h)),
                  pl.BlockSpec((Q_TILE, LANES), lambda b, h, t: (b * nq + t, h)),
                  tab_spec, tab_spec, cmp_spec, cmp_spec,
                  pl.BlockSpec(c2s_t.shape, lambda b, h, t: (0, 0)),
                  pl.BlockSpec((seq, 2 * HEAD_DIM), lambda b, h, t: (b, h)), kv_spec, kv_spec, kv_spec],
        out_specs=pl.BlockSpec((Q_TILE, gw), lambda b, h, t: (b * nq + t, h)),
        compiler_params=_cparams(("parallel", "parallel", "arbitrary")),
        name="nsa_attention",
    )(q, gates, cos_t, sin_t, kc, vc, c2s_t, ks, vs, kw, vw)


def _s5_operators(a_re, a_im, log_dt, b_re, b_im, c_re, c_im):
    hp = lax.Precision.HIGHEST
    n_g = a_re.shape[0]
    gpb = SSM_GROUPS_PER_BLOCK
    n_gb = n_g // gpb
    t_c, hh, pp = SSM_CHUNK, SSM_GROUP, SSM_STATE
    lam = lax.complex(a_re.astype(F32), a_im.astype(F32))
    dt = jnp.exp(log_dt.astype(F32))[:, None]
    a_bar = jnp.exp(lam * dt)
    b_bar = ((a_bar - 1.0) / lam)[:, :, None] * lax.complex(b_re.astype(F32), b_im.astype(F32))
    c_mat = lax.complex(c_re.astype(F32), c_im.astype(F32))
    steps = jnp.arange(t_c + 1, dtype=F32)[:, None, None]
    pw = jnp.exp((lam * dt)[None] * steps)

    def blockdiag(x, r, c):
        tile = jnp.tile(jnp.eye(c, dtype=BF16), (1, gpb))
        rows = jnp.arange(gpb * r)[:, None] // r
        cols = jnp.arange(gpb * c)[None, :] // c
        tiled = jnp.einsum('...c,cd->...d', x.astype(BF16), tile, preferred_element_type=BF16)
        return jnp.where(rows == cols, tiled, jnp.zeros((), BF16))

    k_lag = jnp.einsum('ghp,kgp,gpi->kgih', c_mat, pw[:t_c], b_bar, precision=hp).real
    w_intra = blockdiag(k_lag.reshape(t_c, n_gb, gpb * hh, hh), hh, hh)
    w_intra = w_intra.transpose(1, 2, 0, 3).reshape(n_gb, LANES, t_c * LANES)
    p_in = jnp.swapaxes(pw[t_c - 1::-1][:, :, :, None] * b_bar[None], -1, -2)
    p_in = p_in.reshape(t_c, n_gb, gpb * hh, pp)
    s_in = jnp.concatenate([blockdiag(p_in.real, hh, pp), blockdiag(p_in.imag, hh, pp)], axis=-1)
    s_in = s_in.transpose(1, 0, 2, 3).reshape(n_gb, t_c * LANES, 2 * gpb * pp)
    m_out = jnp.swapaxes(c_mat[None] * pw[1:, :, None, :], -1, -2)
    m_out = m_out.reshape(t_c, n_gb, gpb * pp, hh)
    s_out = jnp.concatenate([blockdiag(m_out.real, pp, hh), blockdiag(-m_out.imag, pp, hh)], axis=2)
    s_out = s_out.transpose(1, 2, 0, 3).reshape(n_gb, 2 * gpb * pp, t_c * LANES)
    k_steps = jnp.arange(SCAN_TABLE_ROWS, dtype=F32)[:, None, None] * float(t_c)
    a_pow = jnp.exp((lam * dt)[None] * k_steps).reshape(SCAN_TABLE_ROWS, n_gb, gpb * pp).swapaxes(0, 1)
    return s_in, w_intra, s_out, a_pow.real.astype(F32), a_pow.imag.astype(F32)


def _s5_kernel(u_ref, sin_ref, wi_ref, sout_ref, ar_ref, ai_ref, d_ref, o_ref, xin_ref, xprev_ref, wt_ref):
    t_c = SSM_CHUNK
    nc = u_ref.shape[0] // t_c
    half = ar_ref.shape[-1]

    @pl.when(pl.program_id(1) == 0)
    def _():
        wt_ref[...] = jnp.zeros(wt_ref.shape, wt_ref.dtype)
        for tau in range(t_c):
            wt_ref[tau * LANES:(tau + 1) * LANES, tau * LANES:] = wi_ref[0, :, :(t_c - tau) * LANES]

    def step_rows(ref, t):
        return ref.at[pl.ds(t, nc, stride=t_c), :]

    u_flat = jnp.concatenate([step_rows(u_ref, tau)[...].astype(BF16) for tau in range(t_c)], axis=1)
    xin = jnp.dot(u_flat, sin_ref[0], preferred_element_type=F32)
    x_r, x_i = xin[:, :half], xin[:, half:]
    r_in_grp = jnp.bitwise_and(lax.broadcasted_iota(jnp.int32, (nc, 1), 0), SCAN_GROUP - 1)
    d = 1
    while d < SCAN_GROUP:
        keep = jnp.where(r_in_grp >= d, 1.0, 0.0)
        s_r, s_i = pltpu.roll(x_r, d, 0) * keep, pltpu.roll(x_i, d, 0) * keep
        ad_r, ad_i = ar_ref[0, d:d + 1, :], ai_ref[0, d:d + 1, :]
        x_r, x_i = x_r + ad_r * s_r - ad_i * s_i, x_i + ad_r * s_i + ad_i * s_r
        d *= 2
    keep = jnp.where(r_in_grp >= 1, 1.0, 0.0)
    xin_ref[:, :half] = x_r
    xin_ref[:, half:] = x_i
    xprev_ref[:, :half] = pltpu.roll(x_r, 1, 0) * keep
    xprev_ref[:, half:] = pltpu.roll(x_i, 1, 0) * keep
    ak_r, ak_i = ar_ref[0, 0:SCAN_GROUP, :], ai_ref[0, 0:SCAN_GROUP, :]
    ag_r, ag_i = ar_ref[0, SCAN_GROUP:SCAN_GROUP + 1, :], ai_ref[0, SCAN_GROUP:SCAN_GROUP + 1, :]

    def group(g, carry):
        c_r, c_i = carry
        rows = pl.ds(pl.multiple_of(g * SCAN_GROUP, SCAN_GROUP), SCAN_GROUP)
        xprev_ref[rows, :half] = xprev_ref[rows, :half] + ak_r * c_r - ak_i * c_i
        xprev_ref[rows, half:] = xprev_ref[rows, half:] + ak_r * c_i + ak_i * c_r
        last = xin_ref[pl.ds(g * SCAN_GROUP + SCAN_GROUP - 1, 1), :]
        return (last[:, :half] + ag_r * c_r - ag_i * c_i, last[:, half:] + ag_r * c_i + ag_i * c_r)

    zero = jnp.zeros((1, half), F32)
    lax.fori_loop(0, nc // SCAN_GROUP, group, (zero, zero))
    y = (jnp.dot(u_flat, wt_ref[...], preferred_element_type=F32)
         + jnp.dot(xprev_ref[...].astype(BF16), sout_ref[0], preferred_element_type=F32))
    for t in range(t_c):
        step_rows(o_ref, t)[...] = _gelu_tanh(y[:, t * LANES:(t + 1) * LANES] + d_ref[0] * step_rows(u_ref, t)[...])


def _s5_scan(u, bsz, seq, ops, d_skip):
    d_ssm = u.shape[1]
    t_c = SSM_CHUNK
    s_in, w_intra, s_out, a_r, a_i = ops
    n_gb = d_ssm // LANES
    nc = seq // t_c
    st = s_in.shape[-1]
    return pl.pallas_call(
        _s5_kernel,
        out_shape=jax.ShapeDtypeStruct(u.shape, F32),
        grid=(n_gb, bsz),
        in_specs=[pl.BlockSpec((seq, LANES), lambda g, b: (b, g)),
                  pl.BlockSpec((1,) + s_in.shape[1:], lambda g, b: (g, 0, 0)),
                  pl.BlockSpec((1,) + w_intra.shape[1:], lambda g, b: (g, 0, 0)),
                  pl.BlockSpec((1,) + s_out.shape[1:], lambda g, b: (g, 0, 0)),
                  pl.BlockSpec((1, SCAN_TABLE_ROWS, st // 2), lambda g, b: (g, 0, 0)),
                  pl.BlockSpec((1, SCAN_TABLE_ROWS, st // 2), lambda g, b: (g, 0, 0)),
                  pl.BlockSpec((1, 1, LANES), lambda g, b: (g, 0, 0))],
        out_specs=pl.BlockSpec((seq, LANES), lambda g, b: (b, g)),
        scratch_shapes=[pltpu.VMEM((nc, st), F32), pltpu.VMEM((nc, st), F32),
                        pltpu.VMEM((t_c * LANES, t_c * LANES), BF16)],
        compiler_params=_cparams(("arbitrary", "arbitrary")),
        name="s5_scan",
    )(u, s_in, w_intra, s_out, a_r, a_i, d_skip.reshape(n_gb, 1, LANES))


def _mixnorm_kernel(a_ref, s_ref, ga_ref, gs_ref, o_ref):
    da = a_ref.shape[-1]
    a = a_ref[...]
    s = s_ref[...]
    o_ref[:, :da] = (a * lax.rsqrt(jnp.mean(a * a, axis=-1, keepdims=True) + EPS) * ga_ref[...]).astype(BF16)
    o_ref[:, da:] = (s * lax.rsqrt(jnp.mean(s * s, axis=-1, keepdims=True) + EPS) * gs_ref[...]).astype(BF16)


def _mixnorm(attn, ssm, g_attn, g_ssm, tm=256):
    m, da = attn.shape
    ds = ssm.shape[1]
    return pl.pallas_call(
        _mixnorm_kernel,
        out_shape=jax.ShapeDtypeStruct((m, da + ds), BF16),
        grid=(m // tm,),
        in_specs=[pl.BlockSpec((tm, da), lambda i: (i, 0)), pl.BlockSpec((tm, ds), lambda i: (i, 0)),
                  pl.BlockSpec((1, da), lambda i: (0, 0)), pl.BlockSpec((1, ds), lambda i: (0, 0))],
        out_specs=pl.BlockSpec((tm, da + ds), lambda i: (i, 0)),
        compiler_params=_cparams(("parallel",)),
        name="mixnorm",
    )(attn, ssm, g_attn.reshape(1, da), g_ssm.reshape(1, ds))


def _split_bf16(x):
    hi = x.astype(BF16)
    return hi, (x - hi.astype(F32)).astype(BF16)


def _lane_max(x):
    return jnp.max(x, axis=-1, keepdims=True)


def _first_lane_of(cond_val, lane_f):
    return jnp.min(jnp.where(cond_val > 0.0, lane_f, float(LANES)), axis=-1, keepdims=True)


def _ffn_norm_router_kernel(h_ref, g_ref, whi_ref, wlo_ref, b_ref, hn_ref, route_ref):
    x = h_ref[...]
    hn = x * lax.rsqrt(jnp.mean(x * x, axis=-1, keepdims=True) + EPS) * g_ref[...]
    hn_ref[...] = _pack_bf16_pairs(hn)
    hi, lo = _split_bf16(hn)
    logits = (jnp.dot(hi, whi_ref[...], preferred_element_type=F32)
              + jnp.dot(lo, whi_ref[...], preferred_element_type=F32)
              + jnp.dot(hi, wlo_ref[...], preferred_element_type=F32)) + b_ref[...]
    lane = lax.broadcasted_iota(jnp.int32, logits.shape, 1)
    lane_f = lane.astype(F32)
    in_c = jnp.where(lane < N_EXPERT_GROUPS, 1.0, 0.0)
    c_l = jnp.where(in_c > 0.0, logits, NEG_INF)
    c_e = jnp.exp(c_l - _lane_max(c_l)) * in_c
    p_c = c_e / jnp.sum(c_e, axis=-1, keepdims=True)
    w_grp = _lane_max(p_c)
    grp = _first_lane_of(jnp.where(p_c == w_grp, in_c, 0.0), lane_f)
    lo_lane = float(N_EXPERT_GROUPS) + grp * float(EXPERTS_PER_GROUP)
    in_f = jnp.where(lane_f >= lo_lane, jnp.where(lane_f < lo_lane + float(EXPERTS_PER_GROUP), 1.0, 0.0), 0.0)
    f_l = jnp.where(in_f > 0.0, logits, NEG_INF)
    f_e = jnp.exp(f_l - _lane_max(f_l)) * in_f
    p_f = f_e / jnp.sum(f_e, axis=-1, keepdims=True)
    p_f = jnp.where(in_f > 0.0, p_f, -1.0)
    p1 = _lane_max(p_f)
    i1 = _first_lane_of(jnp.where(p_f == p1, in_f, 0.0), lane_f)
    rest = jnp.where(lane_f == i1, -1.0, p_f)
    p2 = _lane_max(rest)
    i2 = _first_lane_of(jnp.where(rest == p2, in_f, 0.0) * jnp.where(lane_f == i1, 0.0, 1.0), lane_f)
    denom = p1 + p2
    vals = (w_grp * p1 / denom, w_grp * p2 / denom,
            i1 - float(N_EXPERT_GROUPS), i2 - float(N_EXPERT_GROUPS))
    out = jnp.zeros(logits.shape, F32)
    for k, v in enumerate(vals):
        out = jnp.where(lane == k, v, out)
    route_ref[...] = out


def _ffn_norm_router(h, g, w_router, b_router, tm=256):
    m, d = h.shape
    n = w_router.shape[1]
    w_hi, w_lo = _split_bf16(w_router)
    return pl.pallas_call(
        _ffn_norm_router_kernel,
        out_shape=(jax.ShapeDtypeStruct((m, d // 2), jnp.uint32), jax.ShapeDtypeStruct((m, n), F32)),
        grid=(m // tm,),
        in_specs=[pl.BlockSpec((tm, d), lambda i: (i, 0)), pl.BlockSpec((1, d), lambda i: (0, 0)),
                  pl.BlockSpec((d, n), lambda i: (0, 0)), pl.BlockSpec((d, n), lambda i: (0, 0)),
                  pl.BlockSpec((1, n), lambda i: (0, 0))],
        out_specs=(pl.BlockSpec((tm, d // 2), lambda i: (i, 0)), pl.BlockSpec((tm, n), lambda i: (i, 0))),
        compiler_params=_cparams(("parallel",)),
        name="ffn_norm_router",
    )(h, g.reshape(1, d), w_hi, w_lo, b_router.reshape(1, n))


def _dispatch_plan(expert, tm):
    n_tok = expert.shape[0]
    n_assign = n_tok * TOP_K
    n_blocks = -(-(n_assign + N_EXPERTS * (MOE_BLOCK - 1)) // MOE_BLOCK)
    n_rows = n_blocks * MOE_BLOCK
    flat = expert.reshape(n_assign)
    ids = jnp.arange(n_assign, dtype=jnp.int32)
    _, order = lax.sort_key_val(flat, ids)
    _, inv = lax.sort_key_val(order, ids)
    e_ids = jnp.arange(N_EXPERTS, dtype=jnp.int32)
    counts = jnp.sum((flat[:, None] == e_ids[None, :]).astype(jnp.int32), axis=0)
    start = jnp.cumsum(counts) - counts
    padded = (counts + MOE_BLOCK - 1) // MOE_BLOCK * MOE_BLOCK
    pad_end = jnp.cumsum(padded)
    pad_start = pad_end - padded
    rows = jnp.arange(n_rows, dtype=jnp.int32)
    row_e = jnp.minimum(jnp.sum((pad_end[None, :] <= rows[:, None]).astype(jnp.int32), axis=1), N_EXPERTS - 1)
    slot = rows - pad_start[row_e]
    valid = (slot < counts[row_e]) & (rows < pad_end[-1])
    src = jnp.clip(start[row_e] + slot, 0, n_assign - 1)
    buf_tok = jnp.where(valid, order[src] // TOP_K, 0).astype(jnp.int32)
    dest = (pad_start[flat] + inv - start[flat]).astype(jnp.int32)
    dest = dest.reshape(n_tok // tm, tm, TOP_K).transpose(0, 2, 1).reshape(-1)
    plan = jnp.concatenate([pad_start // MOE_BLOCK, padded // MOE_BLOCK,
                            pad_end[-1:] // MOE_BLOCK]).astype(jnp.int32)
    return plan, buf_tok, dest, n_blocks


def _zero_fill_spare(plan_ref, zero_vmem, out_hbm, col0, width, sem, n_blocks):
    n_used = plan_ref[2 * N_EXPERTS]

    def cp(blk):
        return pltpu.make_async_copy(
            zero_vmem, out_hbm.at[pl.ds(pl.multiple_of(blk * MOE_BLOCK, MOE_BLOCK), MOE_BLOCK), pl.ds(col0, width)],
            sem)

    def start(blk, c):
        cp(blk).start(priority=SIDE_DMA_PRIORITY)
        return c

    def wait(blk, c):
        cp(blk).wait()
        return c

    lax.fori_loop(n_used, n_blocks, start, 0)
    lax.fori_loop(n_used, n_blocks, wait, 0)


def _pack_bf16_pairs(x):
    w = x.shape[1] // 2
    hi = lax.bitcast_convert_type(x[:, :w].astype(BF16).astype(F32), jnp.uint32)
    lo = lax.bitcast_convert_type(x[:, w:].astype(BF16).astype(F32), jnp.uint32)
    return hi | (lo >> 16)


def _unpack_bf16_pairs(p):
    first = lax.bitcast_convert_type(p & jnp.uint32(0xFFFF0000), F32)
    second = lax.bitcast_convert_type(p << 16, F32)
    return jnp.concatenate([first, second], axis=1).astype(BF16)


def _moe_up_kernel(plan_ref, tok_ref, x_hbm, *rest, n_blocks):
    w_refs, (act_hbm, wg_b, wu_b, xs, gbuf, obuf, gsem, osem) = rest[:2 * W_DMA_SPLIT], rest[2 * W_DMA_SPLIT:]
    wg_refs, wu_refs = w_refs[:W_DMA_SPLIT], w_refs[W_DMA_SPLIT:]
    e = pl.program_id(0)
    c = pl.program_id(1)
    n_c = pl.num_programs(1)
    tn = wg_b.shape[1]
    col0 = pl.multiple_of(c * tn, tn)
    b0 = plan_ref[e]
    nb = plan_ref[N_EXPERTS + e]
    n_res = jnp.minimum(nb, MOE_X_CAP)

    def row_copy(blk, r, slot):
        tok = tok_ref[blk * MOE_BLOCK + r]
        return pltpu.make_async_copy(x_hbm.at[pl.ds(tok, 1), :], gbuf.at[slot, pl.ds(r, 1), :], gsem.at[slot])

    def gather_start(blk, slot):
        lax.fori_loop(0, MOE_BLOCK, lambda r, cr: (row_copy(blk, r, slot).start(priority=SIDE_DMA_PRIORITY), cr)[1], 0, unroll=DMA_UNROLL)

    def gather_wait(blk, slot):
        lax.fori_loop(0, MOE_BLOCK, lambda r, cr: (row_copy(blk, r, slot).wait(), cr)[1], 0, unroll=DMA_UNROLL)

    def out_copy(blk, slot):
        rows = pl.ds(pl.multiple_of(blk * MOE_BLOCK, MOE_BLOCK), MOE_BLOCK)
        return pltpu.make_async_copy(obuf.at[slot], act_hbm.at[rows, pl.ds(col0, tn)], osem.at[slot])

    def emit(x, j):
        slot = lax.rem(j, 2)
        g = jnp.dot(x, wg_b[...], preferred_element_type=F32)
        u = jnp.dot(x, wu_b[...], preferred_element_type=F32)

        @pl.when(j >= 2)
        def _():
            out_copy(b0 + j - 2, slot).wait()

        obuf[slot] = (g * _sigmoid(g) * u).astype(BF16)
        out_copy(b0 + j, slot).start(priority=SIDE_DMA_PRIORITY)

    @pl.when(nb > 0)
    def _():
        rows = wg_b.shape[0] // W_DMA_SPLIT
        for k in range(W_DMA_SPLIT):
            wg_b[k * rows:(k + 1) * rows, :] = wg_refs[k][0].astype(BF16)
            wu_b[k * rows:(k + 1) * rows, :] = wu_refs[k][0].astype(BF16)

    @pl.when(c == 0)
    def _():
        @pl.when(jnp.logical_and(e == 0, nb > 0))
        def _():
            gather_start(b0, 0)

        def body(j, carry):
            slot = lax.rem(j, 2)
            gather_wait(b0 + j, slot)

            @pl.when(j + 1 < nb)
            def _():
                gather_start(b0 + j + 1, 1 - slot)

            x = _unpack_bf16_pairs(gbuf[slot])

            @pl.when(j < MOE_X_CAP)
            def _():
                xs[jnp.minimum(j, MOE_X_CAP - 1)] = x

            emit(x, j)
            return carry

        lax.fori_loop(0, nb, body, 0)

    @pl.when(c > 0)
    def _():
        def resident(j, carry):
            emit(xs[j], j)
            return carry

        def regathered(j, carry):
            gather_start(b0 + j, 0)
            gather_wait(b0 + j, 0)
            emit(_unpack_bf16_pairs(gbuf[0]), j)
            return carry

        lax.fori_loop(0, n_res, resident, 0)
        lax.fori_loop(n_res, nb, regathered, 0)

    @pl.when(nb >= 2)
    def _():
        out_copy(b0 + nb - 2, lax.rem(nb, 2)).wait()

    @pl.when(nb >= 1)
    def _():
        out_copy(b0 + nb - 1, lax.rem(nb + 1, 2)).wait()

    @pl.when(jnp.logical_and(c == n_c - 1, e + 1 < N_EXPERTS))
    def _():
        e_next = jnp.minimum(e + 1, N_EXPERTS - 1)

        @pl.when(plan_ref[N_EXPERTS + e_next] > 0)
        def _():
            gather_start(plan_ref[e_next], 0)

    @pl.when(e == N_EXPERTS - 1)
    def _():
        obuf[0] = jnp.zeros(obuf.shape[1:], obuf.dtype)
        _zero_fill_spare(plan_ref, obuf.at[0], act_hbm, col0, tn, osem.at[0], n_blocks)


def _moe_up(hn_packed, w_gate, w_up, plan, buf_tok, n_blocks, n_chunks=4):
    d = w_gate.shape[-2]
    d_e = w_gate.shape[-1]
    tn = d_e // n_chunks
    w_specs = [pl.BlockSpec((1, d // W_DMA_SPLIT, tn), functools.partial(lambda e, c, plan, tok, k: (e, k, c), k=k))
               for k in range(W_DMA_SPLIT)]
    return pl.pallas_call(
        functools.partial(_moe_up_kernel, n_blocks=n_blocks),
        out_shape=jax.ShapeDtypeStruct((n_blocks * MOE_BLOCK, d_e), BF16),
        grid_spec=pltpu.PrefetchScalarGridSpec(
            num_scalar_prefetch=2,
            grid=(N_EXPERTS, n_chunks),
            in_specs=[pl.BlockSpec(memory_space=pl.ANY)] + w_specs + w_specs,
            out_specs=pl.BlockSpec(memory_space=pl.ANY),
            scratch_shapes=[pltpu.VMEM((d, tn), BF16), pltpu.VMEM((d, tn), BF16),
                            pltpu.VMEM((MOE_X_CAP, MOE_BLOCK, d), BF16),
                            pltpu.VMEM((2, MOE_BLOCK, d // 2), jnp.uint32),
                            pltpu.VMEM((2, MOE_BLOCK, tn), BF16),
                            pltpu.SemaphoreType.DMA((2,)), pltpu.SemaphoreType.DMA((2,))]),
        compiler_params=_cparams(("arbitrary", "arbitrary")),
        name="moe_up",
    )(plan, buf_tok, hn_packed, *([w_gate] * W_DMA_SPLIT), *([w_up] * W_DMA_SPLIT))


def _moe_down_kernel(plan_ref, act_hbm, *rest, n_blocks):
    wd_refs, (y_hbm, wd_b, abuf, obuf, asem, osem) = rest[:W_DMA_SPLIT], rest[W_DMA_SPLIT:]
    e = pl.program_id(0)
    c = pl.program_id(1)
    tn = wd_b.shape[1]
    col0 = pl.multiple_of(c * tn, tn)
    b0 = plan_ref[e]
    nb = plan_ref[N_EXPERTS + e]

    def rows_of(blk):
        return pl.ds(pl.multiple_of(blk * MOE_BLOCK, MOE_BLOCK), MOE_BLOCK)

    def in_copy(blk, slot):
        return pltpu.make_async_copy(act_hbm.at[rows_of(blk), :], abuf.at[slot], asem.at[slot])

    def out_copy(blk, slot):
        return pltpu.make_async_copy(obuf.at[slot], y_hbm.at[rows_of(blk), pl.ds(col0, tn)], osem.at[slot])

    @pl.when(jnp.logical_and(jnp.logical_and(e == 0, c == 0), nb > 0))
    def _():
        in_copy(b0, 0).start(priority=SIDE_DMA_PRIORITY)

    @pl.when(nb > 0)
    def _():
        rows = wd_b.shape[0] // W_DMA_SPLIT
        for k in range(W_DMA_SPLIT):
            wd_b[k * rows:(k + 1) * rows, :] = wd_refs[k][0].astype(BF16)

    def body(j, carry):
        slot = lax.rem(j, 2)
        in_copy(b0 + j, slot).wait()

        @pl.when(j + 1 < nb)
        def _():
            in_copy(b0 + j + 1, 1 - slot).start(priority=SIDE_DMA_PRIORITY)

        y = jnp.dot(abuf[slot], wd_b[...], preferred_element_type=F32)

        @pl.when(j >= 2)
        def _():
            out_copy(b0 + j - 2, slot).wait()

        obuf[slot] = y
        out_copy(b0 + j, slot).start(priority=SIDE_DMA_PRIORITY)
        return carry

    lax.fori_loop(0, nb, body, 0)

    @pl.when(nb >= 2)
    def _():
        out_copy(b0 + nb - 2, lax.rem(nb, 2)).wait()

    @pl.when(nb >= 1)
    def _():
        out_copy(b0 + nb - 1, lax.rem(nb + 1, 2)).wait()

    n_c = pl.num_programs(1)
    same_expert = c + 1 < n_c
    e_next = jnp.where(same_expert, e, jnp.minimum(e + 1, N_EXPERTS - 1))

    @pl.when(jnp.logical_and(jnp.logical_or(same_expert, e + 1 < N_EXPERTS), plan_ref[N_EXPERTS + e_next] > 0))
    def _():
        in_copy(plan_ref[e_next], 0).start(priority=SIDE_DMA_PRIORITY)

    @pl.when(e == N_EXPERTS - 1)
    def _():
        obuf[0] = jnp.zeros(obuf.shape[1:], obuf.dtype)
        _zero_fill_spare(plan_ref, obuf.at[0], y_hbm, col0, tn, osem.at[0], n_blocks)


def _moe_down(act, w_down, plan, n_blocks, n_chunks=2):
    n_rows, d_e = act.shape
    d = w_down.shape[-1]
    tn = d // n_chunks
    return pl.pallas_call(
        functools.partial(_moe_down_kernel, n_blocks=n_blocks),
        out_shape=jax.ShapeDtypeStruct((n_rows, d), F32),
        grid_spec=pltpu.PrefetchScalarGridSpec(
            num_scalar_prefetch=1,
            grid=(N_EXPERTS, n_chunks),
            in_specs=[pl.BlockSpec(memory_space=pl.ANY)] + [
                pl.BlockSpec((1, d_e // W_DMA_SPLIT, tn), functools.partial(lambda e, c, plan, k: (e, k, c), k=k))
                for k in range(W_DMA_SPLIT)],
            out_specs=pl.BlockSpec(memory_space=pl.ANY),
            scratch_shapes=[pltpu.VMEM((d_e, tn), BF16),
                            pltpu.VMEM((2, MOE_BLOCK, d_e), BF16), pltpu.VMEM((2, MOE_BLOCK, tn), F32),
                            pltpu.SemaphoreType.DMA((2,)), pltpu.SemaphoreType.DMA((2,))]),
        compiler_params=_cparams(("arbitrary", "arbitrary")),
        name="moe_down",
    )(plan, act, *([w_down] * W_DMA_SPLIT))


def _combine_kernel(dest_ref, h_ref, route_ref, gn_ref, y_hbm, o_ref, rows_ref, sem):
    tm = h_ref.shape[0]
    n_rows = tm * TOP_K
    i = pl.program_id(0)
    n = pl.num_programs(0)

    def row_copy(step, r, slot):
        src = dest_ref[step * n_rows + r]
        return pltpu.make_async_copy(y_hbm.at[pl.ds(src, 1), :], rows_ref.at[slot, pl.ds(r, 1), :], sem.at[slot])

    def gather_start(step, slot):
        lax.fori_loop(0, n_rows, lambda r, cr: (row_copy(step, r, slot).start(priority=SIDE_DMA_PRIORITY), cr)[1], 0, unroll=DMA_UNROLL)

    def gather_wait(step, slot):
        lax.fori_loop(0, n_rows, lambda r, cr: (row_copy(step, r, slot).wait(), cr)[1], 0, unroll=DMA_UNROLL)

    slot = lax.rem(i, 2)

    @pl.when(i == 0)
    def _():
        gather_start(0, 0)

    @pl.when(i + 1 < n)
    def _():
        gather_start(i + 1, 1 - slot)

    gather_wait(i, slot)
    route = route_ref[...]
    rows = rows_ref[slot]
    h = h_ref[...] + route[:, 0:1] * rows[0:tm, :] + route[:, 1:2] * rows[tm:2 * tm, :]
    o_ref[...] = h * lax.rsqrt(jnp.mean(h * h, axis=-1, keepdims=True) + EPS) * gn_ref[...]


def _combine_final(h, route, dest, y_buf, g_final, tm=128):
    m, d = h.shape
    return pl.pallas_call(
        _combine_kernel,
        out_shape=jax.ShapeDtypeStruct((m, d), F32),
        grid_spec=pltpu.PrefetchScalarGridSpec(
            num_scalar_prefetch=1,
            grid=(m // tm,),
            in_specs=[pl.BlockSpec((tm, d), lambda i, dst: (i, 0)),
                      pl.BlockSpec((tm, LANES), lambda i, dst: (i, 0)),
                      pl.BlockSpec((1, d), lambda i, dst: (0, 0)),
                      pl.BlockSpec(memory_space=pl.ANY)],
            out_specs=pl.BlockSpec((tm, d), lambda i, dst: (i, 0)),
            scratch_shapes=[pltpu.VMEM((2, TOP_K * tm, d), F32), pltpu.SemaphoreType.DMA((2,))]),
        compiler_params=_cparams(("arbitrary",)),
        name="moe_combine_final",
    )(dest, h, route, g_final.reshape(1, d), y_buf)


def _gate_column_order():
    cols = []
    for hk in range(N_KV_HEADS):
        blk = [-1] * LANES
        for br in range(N_BRANCHES):
            for g in range(GQA_GROUP):
                blk[br * GQA_GROUP + g] = (hk * GQA_GROUP + g) * N_BRANCHES + br
        cols.extend(blk)
    return cols


def kernel(x, norm_mix, w_in, cmp_pe_k, cmp_w1_k, cmp_b1_k, cmp_w2_k, cmp_b2_k, cmp_pe_v, cmp_w1_v, cmp_b1_v, cmp_w2_v, cmp_b2_v, ssm_a_re, ssm_a_im, ssm_log_dt, ssm_b_re, ssm_b_im, ssm_c_re, ssm_c_im, ssm_d, ssm_w_glu, ssm_b_glu, norm_attn_out, norm_ssm_out, w_out, norm_ffn, router_w_coarse, router_b_coarse, router_w_fine, router_b_fine, w_gate, w_up, w_down, norm_final):
    bsz, seq, d_model = x.shape
    m = bsz * seq
    depth = w_in.shape[0]
    assert depth == 1, "the combine kernel fuses the final norm; stacked layers are not supported"
    layer = 0
    q_w = N_HEADS * HEAD_DIM
    kv_w = N_KV_HEADS * HEAD_DIM
    gate_w = N_HEADS * N_BRANCHES
    cut1, cut2 = q_w, q_w + 6 * kv_w
    cut3 = cut2 + gate_w
    h = x.astype(F32).reshape(m, d_model)
    gate_cols = jnp.array([c if c >= 0 else 0 for c in _gate_column_order()], jnp.int32)
    gate_keep = jnp.array([1.0 if c >= 0 else 0.0 for c in _gate_column_order()], F32)

    w_in_b = w_in[layer].astype(BF16)
    hn = _rmsnorm(h, norm_mix[layer], BF16)
    q = _matmul(hn, w_in_b[:, :cut1], name="proj_q")
    kv = _matmul(hn, w_in_b[:, cut1:cut2], name="proj_kv")
    w_gate_cols = (w_in[layer][:, cut2:cut3][:, gate_cols] * gate_keep[None, :]).astype(BF16)
    gates = _matmul(hn, w_gate_cols, tn=512, name="proj_gate")
    u = _matmul(hn, w_in_b[:, cut3:], name="proj_u")

    stack2 = lambda a, b: jnp.stack([a[layer], b[layer]])
    cmp_kv = _compress(kv, bsz, seq, stack2(cmp_pe_k, cmp_pe_v),
                       stack2(cmp_w1_k, cmp_w1_v).astype(BF16), stack2(cmp_b1_k, cmp_b1_v),
                       stack2(cmp_w2_k, cmp_w2_v).astype(BF16), stack2(cmp_b2_k, cmp_b2_v))
    ks, vs, kw, vw = _kv_prep(kv, seq)
    attn = _nsa_attention(q, gates, cmp_kv[0], cmp_kv[1], ks, vs, kw, vw, bsz, seq)

    ops = _s5_operators(ssm_a_re[layer], ssm_a_im[layer], ssm_log_dt[layer], ssm_b_re[layer],
                        ssm_b_im[layer], ssm_c_re[layer], ssm_c_im[layer])
    y = _s5_scan(u, bsz, seq, ops, ssm_d[layer].astype(F32))
    ssm = _glu_matmul(y, ssm_w_glu[layer].astype(BF16), ssm_b_glu[layer].astype(F32))

    mixed = _mixnorm(attn, ssm, norm_attn_out[layer], norm_ssm_out[layer])
    h = _matmul(mixed, w_out[layer].astype(BF16), residual=h, name="out_proj")

    n_r = N_EXPERT_GROUPS + N_EXPERTS
    w_router = jnp.pad(jnp.concatenate([router_w_coarse[layer], router_w_fine[layer]], axis=1),
                       ((0, 0), (0, LANES - n_r))).astype(F32)
    b_router = jnp.pad(jnp.concatenate([router_b_coarse[layer], router_b_fine[layer]]), (0, LANES - n_r))
    hn2, route = _ffn_norm_router(h, norm_ffn[layer], w_router, b_router.astype(F32))
    expert = route[:, 2:2 + TOP_K].astype(jnp.int32)
    tm_c = 128
    plan, buf_tok, dest, n_blocks = _dispatch_plan(expert, tm_c)
    act = _moe_up(hn2, w_gate[layer], w_up[layer], plan, buf_tok, n_blocks)
    y_buf = _moe_down(act, w_down[layer], plan, n_blocks)
    out = _combine_final(h, route, dest, y_buf, norm_final, tm=tm_c)
    return out.reshape(bsz, seq, d_model).astype(x.dtype)
```

```python
import functools
import math

import jax
import jax.numpy as jnp
from jax import lax
from jax.experimental import pallas as pl
from jax.experimental.pallas import tpu as pltpu

F32 = jnp.float32
BF16 = jnp.bfloat16

HEAD_DIM = 128
N_HEADS = 16
N_KV_HEADS = 4
GQA_GROUP = N_HEADS // N_KV_HEADS
ROT_DIM = HEAD_DIM // 4
ROPE_THETA = 500000.0
CMP_STRIDE = 16
CMP_BLOCK = 32
SLC_BLOCK = 64
N_SELECT = 16
WINDOW = 512
Q_TILE = 256
N_BRANCHES = 3
FORCE_BONUS = 1.0e4
NEG_INF = -1.0e30
MASK_BIG = 1.0e30
SSM_GROUP = 16
SSM_STATE = 64
SSM_CHUNK = 16
SSM_GROUPS_PER_BLOCK = 128 // SSM_GROUP
N_EXPERT_GROUPS = 8
EXPERTS_PER_GROUP = 8
N_EXPERTS = N_EXPERT_GROUPS * EXPERTS_PER_GROUP
TOP_K = 2
MOE_BLOCK = 128
EPS = 1.0e-6
LOG2_E = 1.4426950408889634

V7X_VMEM_LIMIT = 56 * 1024 * 1024
LANES = 128
SCAN_GROUP = 8
SCAN_TABLE_ROWS = 16
DMA_UNROLL = 8
SIDE_DMA_PRIORITY = 1
MOE_X_CAP = 12
W_DMA_SPLIT = 4


def _cparams(sem, vmem=V7X_VMEM_LIMIT):
    return pltpu.CompilerParams(dimension_semantics=sem, vmem_limit_bytes=vmem)


def _gelu_tanh(x):
    return 0.5 * x * (1.0 + jnp.tanh(math.sqrt(2.0 / math.pi) * (x + 0.044715 * (x * x * x))))


def _sigmoid(x):
    return 1.0 / (1.0 + jnp.exp(-x))


def _rmsnorm_kernel(x_ref, g_ref, o_ref):
    x = x_ref[...]
    ms = jnp.mean(x * x, axis=-1, keepdims=True)
    o_ref[...] = (x * lax.rsqrt(ms + EPS) * g_ref[...]).astype(o_ref.dtype)


def _rmsnorm(x, g, out_dtype, tm=256):
    m, d = x.shape
    return pl.pallas_call(
        _rmsnorm_kernel,
        out_shape=jax.ShapeDtypeStruct((m, d), out_dtype),
        grid=(m // tm,),
        in_specs=[pl.BlockSpec((tm, d), lambda i: (i, 0)),
                  pl.BlockSpec((1, d), lambda i: (0, 0))],
        out_specs=pl.BlockSpec((tm, d), lambda i: (i, 0)),
        compiler_params=_cparams(("parallel",)),
        name="rmsnorm",
    )(x, g.reshape(1, d))


def _mm_kernel(a_ref, w_ref, o_ref):
    o_ref[...] = jnp.dot(a_ref[...], w_ref[...], preferred_element_type=F32)


def _mm_glu_kernel(y_ref, w_ref, b_ref, yg_ref, o_ref):
    z = jnp.dot(y_ref[...].astype(BF16), w_ref[...], preferred_element_type=F32) + b_ref[...]
    o_ref[...] = yg_ref[...] * _sigmoid(z)


def _mm_residual_kernel(a_ref, w_ref, r_ref, o_ref):
    o_ref[...] = r_ref[...] + jnp.dot(a_ref[...], w_ref[...], preferred_element_type=F32)


def _matmul(a, w, *, residual=None, tm=512, tn=1024, name="matmul"):
    m, k = a.shape
    n = w.shape[1]
    tn = min(tn, n)
    assert m % tm == 0 and n % tn == 0
    a_spec = pl.BlockSpec((tm, k), lambda j, i: (i, 0))
    w_spec = pl.BlockSpec((k, tn), lambda j, i: (0, j))
    o_spec = pl.BlockSpec((tm, tn), lambda j, i: (i, j))
    if residual is not None:
        kern, ins, specs = _mm_residual_kernel, (a, w, residual), [a_spec, w_spec, o_spec]
    else:
        kern, ins, specs = _mm_kernel, (a, w), [a_spec, w_spec]
    return pl.pallas_call(
        kern,
        out_shape=jax.ShapeDtypeStruct((m, n), F32),
        grid=(n // tn, m // tm),
        in_specs=specs,
        out_specs=o_spec,
        compiler_params=_cparams(("parallel", "parallel")),
        name=name,
    )(*ins)


def _glu_matmul(y, w, bias, tm=512, tn=1024):
    m, k = y.shape
    n = w.shape[1]
    return pl.pallas_call(
        _mm_glu_kernel,
        out_shape=jax.ShapeDtypeStruct((m, n), F32),
        grid=(n // tn, m // tm),
        in_specs=[pl.BlockSpec((tm, k), lambda j, i: (i, 0)),
                  pl.BlockSpec((k, tn), lambda j, i: (0, j)),
                  pl.BlockSpec((1, tn), lambda j, i: (0, j)),
                  pl.BlockSpec((tm, tn), lambda j, i: (i, j))],
        out_specs=pl.BlockSpec((tm, tn), lambda j, i: (i, j)),
        compiler_params=_cparams(("parallel", "parallel")),
        name="ssm_glu",
    )(y, w, bias.reshape(1, n), y)


def _rope_tables(seq, reps):
    half = ROT_DIM // 2
    inv_freq = ROPE_THETA ** (-jnp.arange(half, dtype=F32) / half)
    ang = jnp.arange(seq).astype(F32)[:, None] * inv_freq[None, :]
    cos, sin = jnp.cos(ang), jnp.sin(ang)
    ones = jnp.ones((seq, HEAD_DIM - ROT_DIM), F32)
    cos_t = jnp.concatenate([cos, cos, ones], axis=-1)
    sin_t = jnp.concatenate([-sin, sin, 0.0 * ones], axis=-1)
    return jnp.tile(cos_t, (1, reps)), jnp.tile(sin_t, (1, reps))


def _rope(x, cos_t, sin_t):
    half = ROT_DIM // 2
    n = x.shape[-1]
    lane = jnp.bitwise_and(lax.broadcasted_iota(jnp.int32, x.shape, x.ndim - 1), HEAD_DIM - 1)
    partner = jnp.where(lane < half, pltpu.roll(x, n - half, x.ndim - 1), pltpu.roll(x, half, x.ndim - 1))
    return x * cos_t + partner * sin_t


def _kv_prep_kernel(slc_ref, win_ref, cos_ref, sin_ref, ks_ref, vs_ref, kw_ref, vw_ref, *, tiles_per_seq):
    tm, w = vs_ref.shape
    cos_t, sin_t = cos_ref[...], sin_ref[...]
    slc = slc_ref[...]
    win = win_ref[...]
    k_slc = _rope(slc[:, :w], cos_t, sin_t).astype(BF16)
    pos = lax.rem(pl.program_id(0), tiles_per_seq) * tm + lax.broadcasted_iota(jnp.int32, (tm, HEAD_DIM), 0)
    lane = lax.broadcasted_iota(jnp.int32, (tm, HEAD_DIM), 1)
    own_blk = jnp.where(lane == lax.shift_right_logical(pos, int(math.log2(SLC_BLOCK))), MASK_BIG, 0.0).astype(BF16)
    for hk in range(N_KV_HEADS):
        ks_ref[:, 2 * hk * HEAD_DIM:(2 * hk + 1) * HEAD_DIM] = k_slc[:, hk * HEAD_DIM:(hk + 1) * HEAD_DIM]
        ks_ref[:, (2 * hk + 1) * HEAD_DIM:(2 * hk + 2) * HEAD_DIM] = own_blk
    vs_ref[...] = slc[:, w:].astype(BF16)
    kw_ref[...] = _rope(win[:, :w], cos_t, sin_t).astype(BF16)
    vw_ref[...] = win[:, w:].astype(BF16)


def _kv_prep(kv, seq, tm=256):
    assert seq // SLC_BLOCK <= HEAD_DIM
    m = kv.shape[0]
    w = N_KV_HEADS * HEAD_DIM
    cos_t, sin_t = _rope_tables(seq, N_KV_HEADS)
    tpb = seq // tm
    out = jax.ShapeDtypeStruct((m, w), BF16)
    o_spec = pl.BlockSpec((tm, w), lambda i: (i, 0))
    t_spec = pl.BlockSpec((tm, w), lambda i: (i % tpb, 0))
    return pl.pallas_call(
        functools.partial(_kv_prep_kernel, tiles_per_seq=tpb),
        out_shape=(jax.ShapeDtypeStruct((m, 2 * w), BF16), out, out, out),
        grid=(m // tm,),
        in_specs=[pl.BlockSpec((tm, 2 * w), lambda i: (i, 1)),
                  pl.BlockSpec((tm, 2 * w), lambda i: (i, 2)),
                  t_spec, t_spec],
        out_specs=(pl.BlockSpec((tm, 2 * w), lambda i: (i, 0)), o_spec, o_spec, o_spec),
        compiler_params=_cparams(("parallel",)),
        name="kv_prep",
    )(kv, kv, cos_t, sin_t)


def _compress_kernel(x_ref, pe_ref, w1_ref, b1_ref, w2_ref, b2_ref, o_ref):
    nc = x_ref.shape[0] // CMP_STRIDE
    hid = w1_ref.shape[-1]
    top = jnp.zeros((nc, hid), F32)
    bot = jnp.zeros((nc, hid), F32)
    for j in range(CMP_STRIDE):
        xj = x_ref[pl.ds(j, nc, stride=CMP_STRIDE), :]
        top += jnp.dot((xj + pe_ref[0, j:j + 1, :]).astype(BF16), w1_ref[0, j],
                       preferred_element_type=F32)
        bot += jnp.dot((xj + pe_ref[0, CMP_STRIDE + j:CMP_STRIDE + j + 1, :]).astype(BF16),
                       w1_ref[0, CMP_STRIDE + j], preferred_element_type=F32)
    h = _gelu_tanh(top + pltpu.roll(bot, nc - 1, 0) + b1_ref[0])
    o_ref[0, 0, 0] = (jnp.dot(h.astype(BF16), w2_ref[0], preferred_element_type=F32) + b2_ref[0]).astype(BF16)


def _compress(kv, bsz, seq, pe, w1, b1, w2, b2):
    nc = seq // CMP_STRIDE
    hid = w1.shape[-1]
    w1r = w1.reshape(2, CMP_BLOCK, HEAD_DIM, hid)
    return pl.pallas_call(
        _compress_kernel,
        out_shape=jax.ShapeDtypeStruct((2, bsz, N_KV_HEADS, nc, HEAD_DIM), BF16),
        grid=(2, bsz, N_KV_HEADS),
        in_specs=[pl.BlockSpec((seq, HEAD_DIM), lambda s, b, h: (b, s * N_KV_HEADS + h)),
                  pl.BlockSpec((1, CMP_BLOCK, HEAD_DIM), lambda s, b, h: (s, 0, 0)),
                  pl.BlockSpec((1, CMP_BLOCK, HEAD_DIM, hid), lambda s, b, h: (s, 0, 0, 0)),
                  pl.BlockSpec((1, 1, hid), lambda s, b, h: (s, 0, 0)),
                  pl.BlockSpec((1, hid, HEAD_DIM), lambda s, b, h: (s, 0, 0)),
                  pl.BlockSpec((1, 1, HEAD_DIM), lambda s, b, h: (s, 0, 0))],
        out_specs=pl.BlockSpec((1, 1, 1, nc, HEAD_DIM), lambda s, b, h: (s, b, h, 0, 0)),
        compiler_params=_cparams(("parallel", "parallel", "parallel")),
        name="nsa_compress",
    )(kv, pe, w1r, b1.reshape(2, 1, hid), w2, b2.reshape(2, 1, HEAD_DIM))


def _attn_kernel(q_ref, gate_ref, cos_ref, sin_ref, kc_ref, vc_ref, c2s_ref,
                 ks_ref, vs_ref, kw_ref, vw_ref, o_ref, *, seq):
    qt = pl.program_id(2)
    s0 = qt * Q_TILE
    n_slc = seq // SLC_BLOCK
    n_cmp = kc_ref.shape[-2]
    scale = HEAD_DIM ** -0.5 * LOG2_E
    rows = GQA_GROUP * Q_TILE
    nt = (((1,), (1,)), ((), ()))

    def tile4(x):
        return jnp.concatenate([x] * GQA_GROUP, axis=0)

    q4 = q_ref[...]
    qp = jnp.concatenate([q4[:, g * HEAD_DIM:(g + 1) * HEAD_DIM] for g in range(GQA_GROUP)], axis=0) * scale
    qr_b = _rope(qp, tile4(cos_ref[...]), tile4(sin_ref[...])).astype(BF16)
    qp_b = qp.astype(BF16)
    t_q = s0 + lax.broadcasted_iota(jnp.int32, (Q_TILE, 1), 0)

    n_idx = lax.broadcasted_iota(jnp.int32, (Q_TILE, n_cmp), 1)
    c_ok = (n_idx * CMP_STRIDE + (CMP_BLOCK - 1)) <= t_q
    s_c = lax.dot_general(qp_b, kc_ref[0, 0], nt, preferred_element_type=F32) + tile4(jnp.where(c_ok, 0.0, NEG_INF))
    e_c = jnp.exp2(s_c - jnp.max(s_c, axis=-1, keepdims=True)) * tile4(jnp.where(c_ok, 1.0, 0.0))
    l_c = jnp.sum(e_c, axis=-1, keepdims=True)
    p_c = e_c / jnp.where(l_c > 0.0, l_c, 1.0)
    o_c = jnp.dot(p_c.astype(BF16), vc_ref[0, 0], preferred_element_type=F32)
    p_sum = (p_c[0:Q_TILE] + p_c[Q_TILE:2 * Q_TILE]) + (p_c[2 * Q_TILE:3 * Q_TILE] + p_c[3 * Q_TILE:])
    p_hi = p_sum.astype(BF16)
    r1 = p_sum - p_hi.astype(F32)
    p_mid = r1.astype(BF16)
    p_lo = (r1 - p_mid.astype(F32)).astype(BF16)
    c2s = c2s_ref[...]
    imp_t = (lax.dot_general(c2s, p_hi, nt, preferred_element_type=F32)
             + lax.dot_general(c2s, p_mid, nt, preferred_element_type=F32)
             + lax.dot_general(c2s, p_lo, nt, preferred_element_type=F32))

    sid = lax.broadcasted_iota(jnp.int32, (n_slc, Q_TILE), 0)
    t_l = s0 + lax.broadcasted_iota(jnp.int32, (n_slc, Q_TILE), 1)
    cur = lax.shift_right_logical(t_l, int(math.log2(SLC_BLOCK)))
    causal_blk = sid <= cur
    forced = jnp.where(sid == 0, 1.0, jnp.where(sid == cur, 1.0, jnp.where(sid == cur - 1, 1.0, 0.0)))
    work = jnp.where(causal_blk, imp_t + FORCE_BONUS * forced, NEG_INF)
    sid_f = sid.astype(F32)
    sel_t = jnp.zeros((n_slc, Q_TILE), F32)
    for _ in range(min(N_SELECT, n_slc)):
        mx = jnp.max(work, axis=0, keepdims=True)
        first = jnp.min(jnp.where(work == mx, sid_f, float(n_slc)), axis=0, keepdims=True)
        pick = sid_f == first
        sel_t = jnp.where(pick, 1.0, sel_t)
        work = jnp.where(pick, -3.0e38, work)
    sel_t = jnp.where(causal_blk, sel_t, 0.0)
    unsel = jnp.concatenate([sel_t - 1.0, jnp.zeros((HEAD_DIM - n_slc, Q_TILE), F32)], axis=0).T
    q_aug = jnp.concatenate([qr_b, tile4(unsel.astype(BF16))], axis=1)

    kt = 512

    def slc_tile(j, carry, diag_bias):
        m, l, acc = carry
        k0 = pl.multiple_of(j * kt, kt)
        s = lax.dot_general(q_aug, ks_ref[pl.ds(k0, kt), :], nt, preferred_element_type=F32)
        if diag_bias is not None:
            s = s + diag_bias
        m_new = jnp.maximum(m, jnp.max(s, axis=-1, keepdims=True))
        alpha = jnp.exp2(m - m_new)
        p = jnp.exp2(s - m_new)
        l = alpha * l + jnp.sum(p, axis=-1, keepdims=True)
        acc = alpha * acc + jnp.dot(p.astype(BF16), vs_ref[pl.ds(k0, kt), :], preferred_element_type=F32)
        return m_new, l, acc

    slab = WINDOW + Q_TILE
    w0 = pl.multiple_of(jnp.maximum(s0 - WINDOW, 0), Q_TILE)
    dist = t_q - (w0 + lax.broadcasted_iota(jnp.int32, (Q_TILE, slab), 1))
    w_bias = jnp.where(dist >= 0, jnp.where(dist < WINDOW, 0.0, NEG_INF), NEG_INF)
    s_w = lax.dot_general(qr_b, kw_ref[pl.ds(w0, slab), :], nt, preferred_element_type=F32) + tile4(w_bias)
    e_w = jnp.exp2(s_w - jnp.max(s_w, axis=-1, keepdims=True))
    o_w = (jnp.dot(e_w.astype(BF16), vw_ref[pl.ds(w0, slab), :], preferred_element_type=F32)
           / jnp.sum(e_w, axis=-1, keepdims=True))

    j_diag = lax.shift_right_logical(s0, int(math.log2(kt)))
    kpos = j_diag * kt + lax.broadcasted_iota(jnp.int32, (Q_TILE, kt), 1)
    init = (jnp.full((rows, 1), NEG_INF, F32), jnp.zeros((rows, 1), F32), jnp.zeros((rows, HEAD_DIM), F32))
    carry = slc_tile(j_diag, init, tile4(jnp.where(kpos <= t_q, 0.0, NEG_INF)))
    carry = lax.fori_loop(0, lax.shift_right_logical(j_diag, 1),
                          lambda p, c: slc_tile(2 * p + 1, slc_tile(2 * p, c, None), None), carry)
    _, l_s, acc_s = lax.cond(jnp.bitwise_and(j_diag, 1) == 1,
                             lambda c: slc_tile(j_diag - 1, c, None), lambda c: c, carry)
    o_s = acc_s / l_s

    sg = _sigmoid(gate_ref[...])
    for g in range(GQA_GROUP):
        r = slice(g * Q_TILE, (g + 1) * Q_TILE)
        o_ref[:, g * HEAD_DIM:(g + 1) * HEAD_DIM] = (
            sg[:, g:g + 1] * o_c[r]
            + sg[:, GQA_GROUP + g:GQA_GROUP + g + 1] * o_s[r]
            + sg[:, 2 * GQA_GROUP + g:2 * GQA_GROUP + g + 1] * o_w[r])


def _cmp_to_slc_t(seq):
    n_cmp = seq // CMP_STRIDE - 1
    n_slc = seq // SLC_BLOCK
    cmp_start = jnp.arange(n_cmp) * CMP_STRIDE
    slc_start = jnp.arange(n_slc) * SLC_BLOCK
    overlap = jnp.clip(jnp.minimum(cmp_start[:, None] + CMP_BLOCK, slc_start[None, :] + SLC_BLOCK)
                       - jnp.maximum(cmp_start[:, None], slc_start[None, :]), 0, None)
    c2s = overlap.astype(F32) / CMP_BLOCK
    return jnp.pad(c2s, ((0, 1), (0, 0))).T.astype(BF16)


def _nsa_attention(q, gates, kc, vc, ks, vs, kw, vw, bsz, seq):
    m = q.shape[0]
    nq = seq // Q_TILE
    nc = seq // CMP_STRIDE
    cos_t, sin_t = _rope_tables(seq, 1)
    c2s_t = _cmp_to_slc_t(seq)
    gw = GQA_GROUP * HEAD_DIM
    kv_spec = pl.BlockSpec((seq, HEAD_DIM), lambda b, h, t: (b, h))
    cmp_spec = pl.BlockSpec((1, 1, nc, HEAD_DIM), lambda b, h, t: (b, h, 0, 0))
    tab_spec = pl.BlockSpec((Q_TILE, HEAD_DIM), lambda b, h, t: (t, 0))
    return pl.pallas_call(
        functools.partial(_attn_kernel, seq=seq),
        out_shape=jax.ShapeDtypeStruct((m, N_HEADS * HEAD_DIM), F32),
        grid=(bsz, N_KV_HEADS, nq),
        in_specs=[pl.BlockSpec((Q_TILE, gw), lambda b, h, t: (b * nq + t, h)),
                  pl.BlockSpec((Q_TILE, LANES), lambda b, h, t: (b * nq + t, h)),
                  tab_spec, tab_spec, cmp_spec, cmp_spec,
                  pl.BlockSpec(c2s_t.shape, lambda b, h, t: (0, 0)),
                  pl.BlockSpec((seq, 2 * HEAD_DIM), lambda b, h, t: (b, h)), kv_spec, kv_spec, kv_spec],
        out_specs=pl.BlockSpec((Q_TILE, gw), lambda b, h, t: (b * nq + t, h)),
        compiler_params=_cparams(("parallel", "parallel", "arbitrary")),
        name="nsa_attention",
    )(q, gates, cos_t, sin_t, kc, vc, c2s_t, ks, vs, kw, vw)


def _s5_operators(a_re, a_im, log_dt, b_re, b_im, c_re, c_im):
    hp = lax.Precision.HIGHEST
    n_g = a_re.shape[0]
    gpb = SSM_GROUPS_PER_BLOCK
    n_gb = n_g // gpb
    t_c, hh, pp = SSM_CHUNK, SSM_GROUP, SSM_STATE
    lam = lax.complex(a_re.astype(F32), a_im.astype(F32))
    dt = jnp.exp(log_dt.astype(F32))[:, None]
    a_bar = jnp.exp(lam * dt)
    b_bar = ((a_bar - 1.0) / lam)[:, :, None] * lax.complex(b_re.astype(F32), b_im.astype(F32))
    c_mat = lax.complex(c_re.astype(F32), c_im.astype(F32))
    steps = jnp.arange(t_c + 1, dtype=F32)[:, None, None]
    pw = jnp.exp((lam * dt)[None] * steps)

    def blockdiag(x, r, c):
        tile = jnp.tile(jnp.eye(c, dtype=BF16), (1, gpb))
        rows = jnp.arange(gpb * r)[:, None] // r
        cols = jnp.arange(gpb * c)[None, :] // c
        tiled = jnp.einsum('...c,cd->...d', x.astype(BF16), tile, preferred_element_type=BF16)
        return jnp.where(rows == cols, tiled, jnp.zeros((), BF16))

    k_lag = jnp.einsum('ghp,kgp,gpi->kgih', c_mat, pw[:t_c], b_bar, precision=hp).real
    w_intra = blockdiag(k_lag.reshape(t_c, n_gb, gpb * hh, hh), hh, hh)
    w_intra = w_intra.transpose(1, 2, 0, 3).reshape(n_gb, LANES, t_c * LANES)
    p_in = jnp.swapaxes(pw[t_c - 1::-1][:, :, :, None] * b_bar[None], -1, -2)
    p_in = p_in.reshape(t_c, n_gb, gpb * hh, pp)
    s_in = jnp.concatenate([blockdiag(p_in.real, hh, pp), blockdiag(p_in.imag, hh, pp)], axis=-1)
    s_in = s_in.transpose(1, 0, 2, 3).reshape(n_gb, t_c * LANES, 2 * gpb * pp)
    m_out = jnp.swapaxes(c_mat[None] * pw[1:, :, None, :], -1, -2)
    m_out = m_out.reshape(t_c, n_gb, gpb * pp, hh)
    s_out = jnp.concatenate([blockdiag(m_out.real, pp, hh), blockdiag(-m_out.imag, pp, hh)], axis=2)
    s_out = s_out.transpose(1, 2, 0, 3).reshape(n_gb, 2 * gpb * pp, t_c * LANES)
    k_steps = jnp.arange(SCAN_TABLE_ROWS, dtype=F32)[:, None, None] * float(t_c)
    a_pow = jnp.exp((lam * dt)[None] * k_steps).reshape(SCAN_TABLE_ROWS, n_gb, gpb * pp).swapaxes(0, 1)
    return s_in, w_intra, s_out, a_pow.real.astype(F32), a_pow.imag.astype(F32)


def _s5_kernel(u_ref, sin_ref, wi_ref, sout_ref, ar_ref, ai_ref, d_ref, o_ref, xin_ref, xprev_ref, wt_ref):
    t_c = SSM_CHUNK
    nc = u_ref.shape[0] // t_c
    half = ar_ref.shape[-1]

    @pl.when(pl.program_id(1) == 0)
    def _():
        wt_ref[...] = jnp.zeros(wt_ref.shape, wt_ref.dtype)
        for tau in range(t_c):
            wt_ref[tau * LANES:(tau + 1) * LANES, tau * LANES:] = wi_ref[0, :, :(t_c - tau) * LANES]

    def step_rows(ref, t):
        return ref.at[pl.ds(t, nc, stride=t_c), :]

    u_flat = jnp.concatenate([step_rows(u_ref, tau)[...].astype(BF16) for tau in range(t_c)], axis=1)
    xin = jnp.dot(u_flat, sin_ref[0], preferred_element_type=F32)
    x_r, x_i = xin[:, :half], xin[:, half:]
    r_in_grp = jnp.bitwise_and(lax.broadcasted_iota(jnp.int32, (nc, 1), 0), SCAN_GROUP - 1)
    d = 1
    while d < SCAN_GROUP:
        keep = jnp.where(r_in_grp >= d, 1.0, 0.0)
        s_r, s_i = pltpu.roll(x_r, d, 0) * keep, pltpu.roll(x_i, d, 0) * keep
        ad_r, ad_i = ar_ref[0, d:d + 1, :], ai_ref[0, d:d + 1, :]
        x_r, x_i = x_r + ad_r * s_r - ad_i * s_i, x_i + ad_r * s_i + ad_i * s_r
        d *= 2
    keep = jnp.where(r_in_grp >= 1, 1.0, 0.0)
    xin_ref[:, :half] = x_r
    xin_ref[:, half:] = x_i
    xprev_ref[:, :half] = pltpu.roll(x_r, 1, 0) * keep
    xprev_ref[:, half:] = pltpu.roll(x_i, 1, 0) * keep
    ak_r, ak_i = ar_ref[0, 0:SCAN_GROUP, :], ai_ref[0, 0:SCAN_GROUP, :]
    ag_r, ag_i = ar_ref[0, SCAN_GROUP:SCAN_GROUP + 1, :], ai_ref[0, SCAN_GROUP:SCAN_GROUP + 1, :]

    def group(g, carry):
        c_r, c_i = carry
        rows = pl.ds(pl.multiple_of(g * SCAN_GROUP, SCAN_GROUP), SCAN_GROUP)
        xprev_ref[rows, :half] = xprev_ref[rows, :half] + ak_r * c_r - ak_i * c_i
        xprev_ref[rows, half:] = xprev_ref[rows, half:] + ak_r * c_i + ak_i * c_r
        last = xin_ref[pl.ds(g * SCAN_GROUP + SCAN_GROUP - 1, 1), :]
        return (last[:, :half] + ag_r * c_r - ag_i * c_i, last[:, half:] + ag_r * c_i + ag_i * c_r)

    zero = jnp.zeros((1, half), F32)
    lax.fori_loop(0, nc // SCAN_GROUP, group, (zero, zero))
    y = (jnp.dot(u_flat, wt_ref[...], preferred_element_type=F32)
         + jnp.dot(xprev_ref[...].astype(BF16), sout_ref[0], preferred_element_type=F32))
    for t in range(t_c):
        step_rows(o_ref, t)[...] = _gelu_tanh(y[:, t * LANES:(t + 1) * LANES] + d_ref[0] * step_rows(u_ref, t)[...])


def _s5_scan(u, bsz, seq, ops, d_skip):
    d_ssm = u.shape[1]
    t_c = SSM_CHUNK
    s_in, w_intra, s_out, a_r, a_i = ops
    n_gb = d_ssm // LANES
    nc = seq // t_c
    st = s_in.shape[-1]
    return pl.pallas_call(
        _s5_kernel,
        out_shape=jax.ShapeDtypeStruct(u.shape, F32),
        grid=(n_gb, bsz),
        in_specs=[pl.BlockSpec((seq, LANES), lambda g, b: (b, g)),
                  pl.BlockSpec((1,) + s_in.shape[1:], lambda g, b: (g, 0, 0)),
                  pl.BlockSpec((1,) + w_intra.shape[1:], lambda g, b: (g, 0, 0)),
                  pl.BlockSpec((1,) + s_out.shape[1:], lambda g, b: (g, 0, 0)),
                  pl.BlockSpec((1, SCAN_TABLE_ROWS, st // 2), lambda g, b: (g, 0, 0)),
                  pl.BlockSpec((1, SCAN_TABLE_ROWS, st // 2), lambda g, b: (g, 0, 0)),
                  pl.BlockSpec((1, 1, LANES), lambda g, b: (g, 0, 0))],
        out_specs=pl.BlockSpec((seq, LANES), lambda g, b: (b, g)),
        scratch_shapes=[pltpu.VMEM((nc, st), F32), pltpu.VMEM((nc, st), F32),
                        pltpu.VMEM((t_c * LANES, t_c * LANES), BF16)],
        compiler_params=_cparams(("arbitrary", "arbitrary")),
        name="s5_scan",
    )(u, s_in, w_intra, s_out, a_r, a_i, d_skip.reshape(n_gb, 1, LANES))


def _mixnorm_kernel(a_ref, s_ref, ga_ref, gs_ref, o_ref):
    da = a_ref.shape[-1]
    a = a_ref[...]
    s = s_ref[...]
    o_ref[:, :da] = (a * lax.rsqrt(jnp.mean(a * a, axis=-1, keepdims=True) + EPS) * ga_ref[...]).astype(BF16)
    o_ref[:, da:] = (s * lax.rsqrt(jnp.mean(s * s, axis=-1, keepdims=True) + EPS) * gs_ref[...]).astype(BF16)


def _mixnorm(attn, ssm, g_attn, g_ssm, tm=256):
    m, da = attn.shape
    ds = ssm.shape[1]
    return pl.pallas_call(
        _mixnorm_kernel,
        out_shape=jax.ShapeDtypeStruct((m, da + ds), BF16),
        grid=(m // tm,),
        in_specs=[pl.BlockSpec((tm, da), lambda i: (i, 0)), pl.BlockSpec((tm, ds), lambda i: (i, 0)),
                  pl.BlockSpec((1, da), lambda i: (0, 0)), pl.BlockSpec((1, ds), lambda i: (0, 0))],
        out_specs=pl.BlockSpec((tm, da + ds), lambda i: (i, 0)),
        compiler_params=_cparams(("parallel",)),
        name="mixnorm",
    )(attn, ssm, g_attn.reshape(1, da), g_ssm.reshape(1, ds))


def _split_bf16(x):
    hi = x.astype(BF16)
    return hi, (x - hi.astype(F32)).astype(BF16)


def _lane_max(x):
    return jnp.max(x, axis=-1, keepdims=True)


def _first_lane_of(cond_val, lane_f):
    return jnp.min(jnp.where(cond_val > 0.0, lane_f, float(LANES)), axis=-1, keepdims=True)


def _ffn_norm_router_kernel(h_ref, g_ref, whi_ref, wlo_ref, b_ref, hn_ref, route_ref):
    x = h_ref[...]
    hn = x * lax.rsqrt(jnp.mean(x * x, axis=-1, keepdims=True) + EPS) * g_ref[...]
    hn_ref[...] = _pack_bf16_pairs(hn)
    hi, lo = _split_bf16(hn)
    logits = (jnp.dot(hi, whi_ref[...], preferred_element_type=F32)
              + jnp.dot(lo, whi_ref[...], preferred_element_type=F32)
              + jnp.dot(hi, wlo_ref[...], preferred_element_type=F32)) + b_ref[...]
    lane = lax.broadcasted_iota(jnp.int32, logits.shape, 1)
    lane_f = lane.astype(F32)
    in_c = jnp.where(lane < N_EXPERT_GROUPS, 1.0, 0.0)
    c_l = jnp.where(in_c > 0.0, logits, NEG_INF)
    c_e = jnp.exp(c_l - _lane_max(c_l)) * in_c
    p_c = c_e / jnp.sum(c_e, axis=-1, keepdims=True)
    w_grp = _lane_max(p_c)
    grp = _first_lane_of(jnp.where(p_c == w_grp, in_c, 0.0), lane_f)
    lo_lane = float(N_EXPERT_GROUPS) + grp * float(EXPERTS_PER_GROUP)
    in_f = jnp.where(lane_f >= lo_lane, jnp.where(lane_f < lo_lane + float(EXPERTS_PER_GROUP), 1.0, 0.0), 0.0)
    f_l = jnp.where(in_f > 0.0, logits, NEG_INF)
    f_e = jnp.exp(f_l - _lane_max(f_l)) * in_f
    p_f = f_e / jnp.sum(f_e, axis=-1, keepdims=True)
    p_f = jnp.where(in_f > 0.0, p_f, -1.0)
    p1 = _lane_max(p_f)
    i1 = _first_lane_of(jnp.where(p_f == p1, in_f, 0.0), lane_f)
    rest = jnp.where(lane_f == i1, -1.0, p_f)
    p2 = _lane_max(rest)
    i2 = _first_lane_of(jnp.where(rest == p2, in_f, 0.0) * jnp.where(lane_f == i1, 0.0, 1.0), lane_f)
    denom = p1 + p2
    vals = (w_grp * p1 / denom, w_grp * p2 / denom,
            i1 - float(N_EXPERT_GROUPS), i2 - float(N_EXPERT_GROUPS))
    out = jnp.zeros(logits.shape, F32)
    for k, v in enumerate(vals):
        out = jnp.where(lane == k, v, out)
    route_ref[...] = out


def _ffn_norm_router(h, g, w_router, b_router, tm=256):
    m, d = h.shape
    n = w_router.shape[1]
    w_hi, w_lo = _split_bf16(w_router)
    return pl.pallas_call(
        _ffn_norm_router_kernel,
        out_shape=(jax.ShapeDtypeStruct((m, d // 2), jnp.uint32), jax.ShapeDtypeStruct((m, n), F32)),
        grid=(m // tm,),
        in_specs=[pl.BlockSpec((tm, d), lambda i: (i, 0)), pl.BlockSpec((1, d), lambda i: (0, 0)),
                  pl.BlockSpec((d, n), lambda i: (0, 0)), pl.BlockSpec((d, n), lambda i: (0, 0)),
                  pl.BlockSpec((1, n), lambda i: (0, 0))],
        out_specs=(pl.BlockSpec((tm, d // 2), lambda i: (i, 0)), pl.BlockSpec((tm, n), lambda i: (i, 0))),
        compiler_params=_cparams(("parallel",)),
        name="ffn_norm_router",
    )(h, g.reshape(1, d), w_hi, w_lo, b_router.reshape(1, n))


def _dispatch_plan(expert, tm):
    n_tok = expert.shape[0]
    n_assign = n_tok * TOP_K
    n_blocks = -(-(n_assign + N_EXPERTS * (MOE_BLOCK - 1)) // MOE_BLOCK)
    n_rows = n_blocks * MOE_BLOCK
    flat = expert.reshape(n_assign)
    ids = jnp.arange(n_assign, dtype=jnp.int32)
    _, order = lax.sort_key_val(flat, ids)
    _, inv = lax.sort_key_val(order, ids)
    e_ids = jnp.arange(N_EXPERTS, dtype=jnp.int32)
    counts = jnp.sum((flat[:, None] == e_ids[None, :]).astype(jnp.int32), axis=0)
    start = jnp.cumsum(counts) - counts
    padded = (counts + MOE_BLOCK - 1) // MOE_BLOCK * MOE_BLOCK
    pad_end = jnp.cumsum(padded)
    pad_start = pad_end - padded
    rows = jnp.arange(n_rows, dtype=jnp.int32)
    row_e = jnp.minimum(jnp.sum((pad_end[None, :] <= rows[:, None]).astype(jnp.int32), axis=1), N_EXPERTS - 1)
    slot = rows - pad_start[row_e]
    valid = (slot < counts[row_e]) & (rows < pad_end[-1])
    src = jnp.clip(start[row_e] + slot, 0, n_assign - 1)
    buf_tok = jnp.where(valid, order[src] // TOP_K, 0).astype(jnp.int32)
    dest = (pad_start[flat] + inv - start[flat]).astype(jnp.int32)
    dest = dest.reshape(n_tok // tm, tm, TOP_K).transpose(0, 2, 1).reshape(-1)
    plan = jnp.concatenate([pad_start // MOE_BLOCK, padded // MOE_BLOCK,
                            pad_end[-1:] // MOE_BLOCK]).astype(jnp.int32)
    return plan, buf_tok, dest, n_blocks


def _zero_fill_spare(plan_ref, zero_vmem, out_hbm, col0, width, sem, n_blocks):
    n_used = plan_ref[2 * N_EXPERTS]

    def cp(blk):
        return pltpu.make_async_copy(
            zero_vmem, out_hbm.at[pl.ds(pl.multiple_of(blk * MOE_BLOCK, MOE_BLOCK), MOE_BLOCK), pl.ds(col0, width)],
            sem)

    def start(blk, c):
        cp(blk).start(priority=SIDE_DMA_PRIORITY)
        return c

    def wait(blk, c):
        cp(blk).wait()
        return c

    lax.fori_loop(n_used, n_blocks, start, 0)
    lax.fori_loop(n_used, n_blocks, wait, 0)


def _pack_bf16_pairs(x):
    w = x.shape[1] // 2
    hi = lax.bitcast_convert_type(x[:, :w].astype(BF16).astype(F32), jnp.uint32)
    lo = lax.bitcast_convert_type(x[:, w:].astype(BF16).astype(F32), jnp.uint32)
    return hi | (lo >> 16)


def _unpack_bf16_pairs(p):
    first = lax.bitcast_convert_type(p & jnp.uint32(0xFFFF0000), F32)
    second = lax.bitcast_convert_type(p << 16, F32)
    return jnp.concatenate([first, second], axis=1).astype(BF16)


def _moe_up_kernel(plan_ref, tok_ref, x_hbm, *rest, n_blocks):
    w_refs = rest[:2 * W_DMA_SPLIT]
    act_hbm, wg_b, wu_b, xs, gbuf, obuf, afull, gsem, osem, fsem = rest[2 * W_DMA_SPLIT:]
    wg_refs, wu_refs = w_refs[:W_DMA_SPLIT], w_refs[W_DMA_SPLIT:]
    e = pl.program_id(0)
    c = pl.program_id(1)
    n_c = pl.num_programs(1)
    tn = wg_b.shape[1]
    col0 = pl.multiple_of(c * tn, tn)
    b0 = plan_ref[e]
    nb = plan_ref[N_EXPERTS + e]
    n_res = jnp.minimum(nb, MOE_X_CAP)

    def row_copy(blk, r, slot):
        tok = tok_ref[blk * MOE_BLOCK + r]
        return pltpu.make_async_copy(x_hbm.at[pl.ds(tok, 1), :], gbuf.at[slot, pl.ds(r, 1), :], gsem.at[slot])

    def gather_start(blk, slot):
        lax.fori_loop(0, MOE_BLOCK, lambda r, cr: (row_copy(blk, r, slot).start(priority=SIDE_DMA_PRIORITY), cr)[1], 0, unroll=DMA_UNROLL)

    def gather_wait(blk, slot):
        lax.fori_loop(0, MOE_BLOCK, lambda r, cr: (row_copy(blk, r, slot).wait(), cr)[1], 0, unroll=DMA_UNROLL)

    def rows_of(blk):
        return pl.ds(pl.multiple_of(blk * MOE_BLOCK, MOE_BLOCK), MOE_BLOCK)

    def out_copy(blk, slot):
        return pltpu.make_async_copy(obuf.at[slot], act_hbm.at[rows_of(blk), pl.ds(col0, tn)], osem.at[slot])

    def full_copy(j):
        return pltpu.make_async_copy(afull.at[j], act_hbm.at[rows_of(b0 + j), :], fsem)

    def act_of(x):
        g = jnp.dot(x, wg_b[...], preferred_element_type=F32)
        u = jnp.dot(x, wu_b[...], preferred_element_type=F32)
        return (g * _sigmoid(g) * u).astype(BF16)

    def emit_resident(x, j):
        act = act_of(x)
        for cc in range(act_hbm.shape[1] // tn):
            @pl.when(c == cc)
            def _():
                afull[j, :, cc * tn:(cc + 1) * tn] = act

    def emit_streamed(x, j):
        i = j - n_res
        slot = lax.rem(i, 2)
        act = act_of(x)

        @pl.when(i >= 2)
        def _():
            out_copy(b0 + j - 2, slot).wait()

        obuf[slot] = act
        out_copy(b0 + j, slot).start(priority=SIDE_DMA_PRIORITY)

    @pl.when(nb > 0)
    def _():
        rows = wg_b.shape[0] // W_DMA_SPLIT
        for k in range(W_DMA_SPLIT):
            wg_b[k * rows:(k + 1) * rows, :] = wg_refs[k][0].astype(BF16)
            wu_b[k * rows:(k + 1) * rows, :] = wu_refs[k][0].astype(BF16)

    @pl.when(c == 0)
    def _():
        @pl.when(jnp.logical_and(e == 0, nb > 0))
        def _():
            gather_start(b0, 0)

        def gathered(keep):
            def body(j, carry):
                slot = lax.rem(j, 2)
                gather_wait(b0 + j, slot)

                @pl.when(j + 1 < nb)
                def _():
                    gather_start(b0 + j + 1, 1 - slot)

                x = _unpack_bf16_pairs(gbuf[slot])
                if keep:
                    xs[j] = x
                    emit_resident(x, j)
                else:
                    emit_streamed(x, j)
                return carry
            return body

        lax.fori_loop(0, n_res, gathered(True), 0)
        lax.fori_loop(n_res, nb, gathered(False), 0)

    @pl.when(c > 0)
    def _():
        def resident(j, carry):
            emit_resident(xs[j], j)
            return carry

        def regathered(j, carry):
            gather_start(b0 + j, 0)
            gather_wait(b0 + j, 0)
            emit_streamed(_unpack_bf16_pairs(gbuf[0]), j)
            return carry

        lax.fori_loop(0, n_res, resident, 0)
        lax.fori_loop(n_res, nb, regathered, 0)

    n_str = nb - n_res

    @pl.when(n_str >= 2)
    def _():
        out_copy(b0 + nb - 2, lax.rem(n_str, 2)).wait()

    @pl.when(n_str >= 1)
    def _():
        out_copy(b0 + nb - 1, lax.rem(n_str + 1, 2)).wait()

    @pl.when(c == n_c - 1)
    def _():
        lax.fori_loop(0, n_res, lambda j, cr: (full_copy(j).start(priority=SIDE_DMA_PRIORITY), cr)[1], 0)
        lax.fori_loop(0, n_res, lambda j, cr: (full_copy(j).wait(), cr)[1], 0)

    @pl.when(jnp.logical_and(c == n_c - 1, e + 1 < N_EXPERTS))
    def _():
        e_next = jnp.minimum(e + 1, N_EXPERTS - 1)

        @pl.when(plan_ref[N_EXPERTS + e_next] > 0)
        def _():
            gather_start(plan_ref[e_next], 0)

    @pl.when(jnp.logical_and(e == N_EXPERTS - 1, c == n_c - 1))
    def _():
        afull[0] = jnp.zeros(afull.shape[1:], afull.dtype)
        _zero_fill_spare(plan_ref, afull.at[0], act_hbm, 0, act_hbm.shape[1], fsem, n_blocks)


def _moe_up(hn_packed, w_gate, w_up, plan, buf_tok, n_blocks, n_chunks=4):
    d = w_gate.shape[-2]
    d_e = w_gate.shape[-1]
    tn = d_e // n_chunks
    w_specs = [pl.BlockSpec((1, d // W_DMA_SPLIT, tn), functools.partial(lambda e, c, plan, tok, k: (e, k, c), k=k))
               for k in range(W_DMA_SPLIT)]
    return pl.pallas_call(
        functools.partial(_moe_up_kernel, n_blocks=n_blocks),
        out_shape=jax.ShapeDtypeStruct((n_blocks * MOE_BLOCK, d_e), BF16),
        grid_spec=pltpu.PrefetchScalarGridSpec(
            num_scalar_prefetch=2,
            grid=(N_EXPERTS, n_chunks),
            in_specs=[pl.BlockSpec(memory_space=pl.ANY)] + w_specs + w_specs,
            out_specs=pl.BlockSpec(memory_space=pl.ANY),
            scratch_shapes=[pltpu.VMEM((d, tn), BF16), pltpu.VMEM((d, tn), BF16),
                            pltpu.VMEM((MOE_X_CAP, MOE_BLOCK, d), BF16),
                            pltpu.VMEM((2, MOE_BLOCK, d // 2), jnp.uint32),
                            pltpu.VMEM((2, MOE_BLOCK, tn), BF16),
                            pltpu.VMEM((MOE_X_CAP, MOE_BLOCK, d_e), BF16),
                            pltpu.SemaphoreType.DMA((2,)), pltpu.SemaphoreType.DMA((2,)),
                            pltpu.SemaphoreType.DMA(())]),
        compiler_params=_cparams(("arbitrary", "arbitrary")),
        name="moe_up",
    )(plan, buf_tok, hn_packed, *([w_gate] * W_DMA_SPLIT), *([w_up] * W_DMA_SPLIT))


def _moe_down_kernel(plan_ref, act_hbm, *rest, n_blocks):
    wd_refs, (y_hbm, wd_b, abuf, obuf, asem, osem) = rest[:W_DMA_SPLIT], rest[W_DMA_SPLIT:]
    e = pl.program_id(0)
    c = pl.program_id(1)
    tn = wd_b.shape[1]
    col0 = pl.multiple_of(c * tn, tn)
    b0 = plan_ref[e]
    nb = plan_ref[N_EXPERTS + e]

    def rows_of(blk):
        return pl.ds(pl.multiple_of(blk * MOE_BLOCK, MOE_BLOCK), MOE_BLOCK)

    def in_copy(blk, slot):
        return pltpu.make_async_copy(act_hbm.at[rows_of(blk), :], abuf.at[slot], asem.at[slot])

    def out_copy(blk, slot):
        return pltpu.make_async_copy(obuf.at[slot], y_hbm.at[rows_of(blk), pl.ds(col0, tn)], osem.at[slot])

    @pl.when(jnp.logical_and(jnp.logical_and(e == 0, c == 0), nb > 0))
    def _():
        in_copy(b0, 0).start(priority=SIDE_DMA_PRIORITY)

    @pl.when(nb > 0)
    def _():
        rows = wd_b.shape[0] // W_DMA_SPLIT
        for k in range(W_DMA_SPLIT):
            wd_b[k * rows:(k + 1) * rows, :] = wd_refs[k][0].astype(BF16)

    def body(j, carry):
        slot = lax.rem(j, 2)
        in_copy(b0 + j, slot).wait()

        @pl.when(j + 1 < nb)
        def _():
            in_copy(b0 + j + 1, 1 - slot).start(priority=SIDE_DMA_PRIORITY)

        y = jnp.dot(abuf[slot], wd_b[...], preferred_element_type=F32)

        @pl.when(j >= 2)
        def _():
            out_copy(b0 + j - 2, slot).wait()

        obuf[slot] = y
        out_copy(b0 + j, slot).start(priority=SIDE_DMA_PRIORITY)
        return carry

    lax.fori_loop(0, nb, body, 0)

    @pl.when(nb >= 2)
    def _():
        out_copy(b0 + nb - 2, lax.rem(nb, 2)).wait()

    @pl.when(nb >= 1)
    def _():
        out_copy(b0 + nb - 1, lax.rem(nb + 1, 2)).wait()

    n_c = pl.num_programs(1)
    same_expert = c + 1 < n_c
    e_next = jnp.where(same_expert, e, jnp.minimum(e + 1, N_EXPERTS - 1))

    @pl.when(jnp.logical_and(jnp.logical_or(same_expert, e + 1 < N_EXPERTS), plan_ref[N_EXPERTS + e_next] > 0))
    def _():
        in_copy(plan_ref[e_next], 0).start(priority=SIDE_DMA_PRIORITY)

    @pl.when(e == N_EXPERTS - 1)
    def _():
        obuf[0] = jnp.zeros(obuf.shape[1:], obuf.dtype)
        _zero_fill_spare(plan_ref, obuf.at[0], y_hbm, col0, tn, osem.at[0], n_blocks)


def _moe_down(act, w_down, plan, n_blocks, n_chunks=1):
    n_rows, d_e = act.shape
    d = w_down.shape[-1]
    tn = d // n_chunks
    return pl.pallas_call(
        functools.partial(_moe_down_kernel, n_blocks=n_blocks),
        out_shape=jax.ShapeDtypeStruct((n_rows, d), F32),
        grid_spec=pltpu.PrefetchScalarGridSpec(
            num_scalar_prefetch=1,
            grid=(N_EXPERTS, n_chunks),
            in_specs=[pl.BlockSpec(memory_space=pl.ANY)] + [
                pl.BlockSpec((1, d_e // W_DMA_SPLIT, tn), functools.partial(lambda e, c, plan, k: (e, k, c), k=k))
                for k in range(W_DMA_SPLIT)],
            out_specs=pl.BlockSpec(memory_space=pl.ANY),
            scratch_shapes=[pltpu.VMEM((d_e, tn), BF16),
                            pltpu.VMEM((2, MOE_BLOCK, d_e), BF16), pltpu.VMEM((2, MOE_BLOCK, tn), F32),
                            pltpu.SemaphoreType.DMA((2,)), pltpu.SemaphoreType.DMA((2,))]),
        compiler_params=_cparams(("arbitrary", "arbitrary")),
        name="moe_down",
    )(plan, act, *([w_down] * W_DMA_SPLIT))


def _combine_kernel(dest_ref, h_ref, route_ref, gn_ref, y_hbm, o_ref, rows_ref, sem):
    tm = h_ref.shape[0]
    n_rows = tm * TOP_K
    i = pl.program_id(0)
    n = pl.num_programs(0)

    def row_copy(step, r, slot):
        src = dest_ref[step * n_rows + r]
        return pltpu.make_async_copy(y_hbm.at[pl.ds(src, 1), :], rows_ref.at[slot, pl.ds(r, 1), :], sem.at[slot])

    def gather_start(step, slot):
        lax.fori_loop(0, n_rows, lambda r, cr: (row_copy(step, r, slot).start(priority=SIDE_DMA_PRIORITY), cr)[1], 0, unroll=DMA_UNROLL)

    def gather_wait(step, slot):
        lax.fori_loop(0, n_rows, lambda r, cr: (row_copy(step, r, slot).wait(), cr)[1], 0, unroll=DMA_UNROLL)

    slot = lax.rem(i, 2)

    @pl.when(i == 0)
    def _():
        gather_start(0, 0)

    @pl.when(i + 1 < n)
    def _():
        gather_start(i + 1, 1 - slot)

    gather_wait(i, slot)
    route = route_ref[...]
    rows = rows_ref[slot]
    h = h_ref[...] + route[:, 0:1] * rows[0:tm, :] + route[:, 1:2] * rows[tm:2 * tm, :]
    o_ref[...] = h * lax.rsqrt(jnp.mean(h * h, axis=-1, keepdims=True) + EPS) * gn_ref[...]


def _combine_final(h, route, dest, y_buf, g_final, tm=128):
    m, d = h.shape
    return pl.pallas_call(
        _combine_kernel,
        out_shape=jax.ShapeDtypeStruct((m, d), F32),
        grid_spec=pltpu.PrefetchScalarGridSpec(
            num_scalar_prefetch=1,
            grid=(m // tm,),
            in_specs=[pl.BlockSpec((tm, d), lambda i, dst: (i, 0)),
                      pl.BlockSpec((tm, LANES), lambda i, dst: (i, 0)),
                      pl.BlockSpec((1, d), lambda i, dst: (0, 0)),
                      pl.BlockSpec(memory_space=pl.ANY)],
            out_specs=pl.BlockSpec((tm, d), lambda i, dst: (i, 0)),
            scratch_shapes=[pltpu.VMEM((2, TOP_K * tm, d), F32), pltpu.SemaphoreType.DMA((2,))]),
        compiler_params=_cparams(("arbitrary",)),
        name="moe_combine_final",
    )(dest, h, route, g_final.reshape(1, d), y_buf)


def _gate_column_order():
    cols = []
    for hk in range(N_KV_HEADS):
        blk = [-1] * LANES
        for br in range(N_BRANCHES):
            for g in range(GQA_GROUP):
                blk[br * GQA_GROUP + g] = (hk * GQA_GROUP + g) * N_BRANCHES + br
        cols.extend(blk)
    return cols


def kernel(x, norm_mix, w_in, cmp_pe_k, cmp_w1_k, cmp_b1_k, cmp_w2_k, cmp_b2_k, cmp_pe_v, cmp_w1_v, cmp_b1_v, cmp_w2_v, cmp_b2_v, ssm_a_re, ssm_a_im, ssm_log_dt, ssm_b_re, ssm_b_im, ssm_c_re, ssm_c_im, ssm_d, ssm_w_glu, ssm_b_glu, norm_attn_out, norm_ssm_out, w_out, norm_ffn, router_w_coarse, router_b_coarse, router_w_fine, router_b_fine, w_gate, w_up, w_down, norm_final):
    bsz, seq, d_model = x.shape
    m = bsz * seq
    depth = w_in.shape[0]
    assert depth == 1, "the combine kernel fuses the final norm; stacked layers are not supported"
    layer = 0
    q_w = N_HEADS * HEAD_DIM
    kv_w = N_KV_HEADS * HEAD_DIM
    gate_w = N_HEADS * N_BRANCHES
    cut1, cut2 = q_w, q_w + 6 * kv_w
    cut3 = cut2 + gate_w
    h = x.astype(F32).reshape(m, d_model)
    gate_cols = jnp.array([c if c >= 0 else 0 for c in _gate_column_order()], jnp.int32)
    gate_keep = jnp.array([1.0 if c >= 0 else 0.0 for c in _gate_column_order()], F32)

    w_in_b = w_in[layer].astype(BF16)
    hn = _rmsnorm(h, norm_mix[layer], BF16)
    q = _matmul(hn, w_in_b[:, :cut1], name="proj_q")
    kv = _matmul(hn, w_in_b[:, cut1:cut2], name="proj_kv")
    w_gate_cols = (w_in[layer][:, cut2:cut3][:, gate_cols] * gate_keep[None, :]).astype(BF16)
    gates = _matmul(hn, w_gate_cols, tn=512, name="proj_gate")
    u = _matmul(hn, w_in_b[:, cut3:], name="proj_u")

    stack2 = lambda a, b: jnp.stack([a[layer], b[layer]])
    cmp_kv = _compress(kv, bsz, seq, stack2(cmp_pe_k, cmp_pe_v),
                       stack2(cmp_w1_k, cmp_w1_v).astype(BF16), stack2(cmp_b1_k, cmp_b1_v),
                       stack2(cmp_w2_k, cmp_w2_v).astype(BF16), stack2(cmp_b2_k, cmp_b2_v))
    ks, vs, kw, vw = _kv_prep(kv, seq)
    attn = _nsa_attention(q, gates, cmp_kv[0], cmp_kv[1], ks, vs, kw, vw, bsz, seq)

    ops = _s5_operators(ssm_a_re[layer], ssm_a_im[layer], ssm_log_dt[layer], ssm_b_re[layer],
                        ssm_b_im[layer], ssm_c_re[layer], ssm_c_im[layer])
    y = _s5_scan(u, bsz, seq, ops, ssm_d[layer].astype(F32))
    ssm = _glu_matmul(y, ssm_w_glu[layer].astype(BF16), ssm_b_glu[layer].astype(F32))

    mixed = _mixnorm(attn, ssm, norm_attn_out[layer], norm_ssm_out[layer])
    h = _matmul(mixed, w_out[layer].astype(BF16), residual=h, name="out_proj")

    n_r = N_EXPERT_GROUPS + N_EXPERTS
    w_router = jnp.pad(jnp.concatenate([router_w_coarse[layer], router_w_fine[layer]], axis=1),
                       ((0, 0), (0, LANES - n_r))).astype(F32)
    b_router = jnp.pad(jnp.concatenate([router_b_coarse[layer], router_b_fine[layer]]), (0, LANES - n_r))
    hn2, route = _ffn_norm_router(h, norm_ffn[layer], w_router, b_router.astype(F32))
    expert = route[:, 2:2 + TOP_K].astype(jnp.int32)
    tm_c = 128
    plan, buf_tok, dest, n_blocks = _dispatch_plan(expert, tm_c)
    act = _moe_up(hn2, w_gate[layer], w_up[layer], plan, buf_tok, n_blocks)
    y_buf = _moe_down(act, w_down[layer], plan, n_blocks)
    out = _combine_final(h, route, dest, y_buf, norm_final, tm=tm_c)
    return out.reshape(bsz, seq, d_model).astype(x.dtype)
```

```python
import functools
import math

import jax
import jax.numpy as jnp
from jax import lax
from jax.experimental import pallas as pl
from jax.experimental.pallas import tpu as pltpu

F32 = jnp.float32
BF16 = jnp.bfloat16

HEAD_DIM = 128
N_HEADS = 16
N_KV_HEADS = 4
GQA_GROUP = N_HEADS // N_KV_HEADS
ROT_DIM = HEAD_DIM // 4
ROPE_THETA = 500000.0
CMP_STRIDE = 16
CMP_BLOCK = 32
SLC_BLOCK = 64
N_SELECT = 16
WINDOW = 512
Q_TILE = 256
N_BRANCHES = 3
FORCE_BONUS = 1.0e4
NEG_INF = -1.0e30
MASK_BIG = 1.0e30
SSM_GROUP = 16
SSM_STATE = 64
SSM_CHUNK = 16
SSM_GROUPS_PER_BLOCK = 128 // SSM_GROUP
N_EXPERT_GROUPS = 8
EXPERTS_PER_GROUP = 8
N_EXPERTS = N_EXPERT_GROUPS * EXPERTS_PER_GROUP
TOP_K = 2
MOE_BLOCK = 128
EPS = 1.0e-6
LOG2_E = 1.4426950408889634

V7X_VMEM_LIMIT = 56 * 1024 * 1024
LANES = 128
SCAN_GROUP = 8
SCAN_TABLE_ROWS = 16
DMA_UNROLL = 8
SIDE_DMA_PRIORITY = 1
MOE_X_CAP = 12
W_DMA_SPLIT = 4
GATHER_AHEAD = 3
GATHER_SLOTS = GATHER_AHEAD + 1


def _cparams(sem, vmem=V7X_VMEM_LIMIT):
    return pltpu.CompilerParams(dimension_semantics=sem, vmem_limit_bytes=vmem)


def _gelu_tanh(x):
    return 0.5 * x * (1.0 + jnp.tanh(math.sqrt(2.0 / math.pi) * (x + 0.044715 * (x * x * x))))


def _sigmoid(x):
    return 1.0 / (1.0 + jnp.exp(-x))


def _rmsnorm_kernel(x_ref, g_ref, o_ref):
    x = x_ref[...]
    ms = jnp.mean(x * x, axis=-1, keepdims=True)
    o_ref[...] = (x * lax.rsqrt(ms + EPS) * g_ref[...]).astype(o_ref.dtype)


def _rmsnorm(x, g, out_dtype, tm=256):
    m, d = x.shape
    return pl.pallas_call(
        _rmsnorm_kernel,
        out_shape=jax.ShapeDtypeStruct((m, d), out_dtype),
        grid=(m // tm,),
        in_specs=[pl.BlockSpec((tm, d), lambda i: (i, 0)),
                  pl.BlockSpec((1, d), lambda i: (0, 0))],
        out_specs=pl.BlockSpec((tm, d), lambda i: (i, 0)),
        compiler_params=_cparams(("parallel",)),
        name="rmsnorm",
    )(x, g.reshape(1, d))


def _mm_kernel(a_ref, w_ref, o_ref):
    o_ref[...] = jnp.dot(a_ref[...], w_ref[...], preferred_element_type=F32)


def _mm_glu_kernel(y_ref, w_ref, b_ref, yg_ref, o_ref):
    z = jnp.dot(y_ref[...].astype(BF16), w_ref[...], preferred_element_type=F32) + b_ref[...]
    o_ref[...] = yg_ref[...] * _sigmoid(z)


def _mm_residual_kernel(a_ref, w_ref, r_ref, o_ref):
    o_ref[...] = r_ref[...] + jnp.dot(a_ref[...], w_ref[...], preferred_element_type=F32)


def _matmul(a, w, *, residual=None, tm=512, tn=1024, name="matmul"):
    m, k = a.shape
    n = w.shape[1]
    tn = min(tn, n)
    assert m % tm == 0 and n % tn == 0
    a_spec = pl.BlockSpec((tm, k), lambda j, i: (i, 0))
    w_spec = pl.BlockSpec((k, tn), lambda j, i: (0, j))
    o_spec = pl.BlockSpec((tm, tn), lambda j, i: (i, j))
    if residual is not None:
        kern, ins, specs = _mm_residual_kernel, (a, w, residual), [a_spec, w_spec, o_spec]
    else:
        kern, ins, specs = _mm_kernel, (a, w), [a_spec, w_spec]
    return pl.pallas_call(
        kern,
        out_shape=jax.ShapeDtypeStruct((m, n), F32),
        grid=(n // tn, m // tm),
        in_specs=specs,
        out_specs=o_spec,
        compiler_params=_cparams(("parallel", "parallel")),
        name=name,
    )(*ins)


def _glu_matmul(y, w, bias, tm=512, tn=1024):
    m, k = y.shape
    n = w.shape[1]
    return pl.pallas_call(
        _mm_glu_kernel,
        out_shape=jax.ShapeDtypeStruct((m, n), F32),
        grid=(n // tn, m // tm),
        in_specs=[pl.BlockSpec((tm, k), lambda j, i: (i, 0)),
                  pl.BlockSpec((k, tn), lambda j, i: (0, j)),
                  pl.BlockSpec((1, tn), lambda j, i: (0, j)),
                  pl.BlockSpec((tm, tn), lambda j, i: (i, j))],
        out_specs=pl.BlockSpec((tm, tn), lambda j, i: (i, j)),
        compiler_params=_cparams(("parallel", "parallel")),
        name="ssm_glu",
    )(y, w, bias.reshape(1, n), y)


def _rope_tables(seq, reps):
    half = ROT_DIM // 2
    inv_freq = ROPE_THETA ** (-jnp.arange(half, dtype=F32) / half)
    ang = jnp.arange(seq).astype(F32)[:, None] * inv_freq[None, :]
    cos, sin = jnp.cos(ang), jnp.sin(ang)
    ones = jnp.ones((seq, HEAD_DIM - ROT_DIM), F32)
    cos_t = jnp.concatenate([cos, cos, ones], axis=-1)
    sin_t = jnp.concatenate([-sin, sin, 0.0 * ones], axis=-1)
    return jnp.tile(cos_t, (1, reps)), jnp.tile(sin_t, (1, reps))


def _rope(x, cos_t, sin_t):
    half = ROT_DIM // 2
    n = x.shape[-1]
    lane = jnp.bitwise_and(lax.broadcasted_iota(jnp.int32, x.shape, x.ndim - 1), HEAD_DIM - 1)
    partner = jnp.where(lane < half, pltpu.roll(x, n - half, x.ndim - 1), pltpu.roll(x, half, x.ndim - 1))
    return x * cos_t + partner * sin_t


def _kv_prep_kernel(slc_ref, win_ref, cos_ref, sin_ref, ks_ref, vs_ref, kw_ref, vw_ref, *, tiles_per_seq):
    tm, w = vs_ref.shape
    cos_t, sin_t = cos_ref[...], sin_ref[...]
    slc = slc_ref[...]
    win = win_ref[...]
    k_slc = _rope(slc[:, :w], cos_t, sin_t).astype(BF16)
    pos = lax.rem(pl.program_id(0), tiles_per_seq) * tm + lax.broadcasted_iota(jnp.int32, (tm, HEAD_DIM), 0)
    lane = lax.broadcasted_iota(jnp.int32, (tm, HEAD_DIM), 1)
    own_blk = jnp.where(lane == lax.shift_right_logical(pos, int(math.log2(SLC_BLOCK))), MASK_BIG, 0.0).astype(BF16)
    for hk in range(N_KV_HEADS):
        ks_ref[:, 2 * hk * HEAD_DIM:(2 * hk + 1) * HEAD_DIM] = k_slc[:, hk * HEAD_DIM:(hk + 1) * HEAD_DIM]
        ks_ref[:, (2 * hk + 1) * HEAD_DIM:(2 * hk + 2) * HEAD_DIM] = own_blk
    vs_ref[...] = slc[:, w:].astype(BF16)
    kw_ref[...] = _rope(win[:, :w], cos_t, sin_t).astype(BF16)
    vw_ref[...] = win[:, w:].astype(BF16)


def _kv_prep(kv, seq, tm=256):
    assert seq // SLC_BLOCK <= HEAD_DIM
    m = kv.shape[0]
    w = N_KV_HEADS * HEAD_DIM
    cos_t, sin_t = _rope_tables(seq, N_KV_HEADS)
    tpb = seq // tm
    out = jax.ShapeDtypeStruct((m, w), BF16)
    o_spec = pl.BlockSpec((tm, w), lambda i: (i, 0))
    t_spec = pl.BlockSpec((tm, w), lambda i: (i % tpb, 0))
    return pl.pallas_call(
        functools.partial(_kv_prep_kernel, tiles_per_seq=tpb),
        out_shape=(jax.ShapeDtypeStruct((m, 2 * w), BF16), out, out, out),
        grid=(m // tm,),
        in_specs=[pl.BlockSpec((tm, 2 * w), lambda i: (i, 1)),
                  pl.BlockSpec((tm, 2 * w), lambda i: (i, 2)),
                  t_spec, t_spec],
        out_specs=(pl.BlockSpec((tm, 2 * w), lambda i: (i, 0)), o_spec, o_spec, o_spec),
        compiler_params=_cparams(("parallel",)),
        name="kv_prep",
    )(kv, kv, cos_t, sin_t)


def _compress_kernel(x_ref, pe_ref, w1_ref, b1_ref, w2_ref, b2_ref, o_ref):
    nc = x_ref.shape[0] // CMP_STRIDE
    hid = w1_ref.shape[-1]
    top = jnp.zeros((nc, hid), F32)
    bot = jnp.zeros((nc, hid), F32)
    for j in range(CMP_STRIDE):
        xj = x_ref[pl.ds(j, nc, stride=CMP_STRIDE), :]
        top += jnp.dot((xj + pe_ref[0, j:j + 1, :]).astype(BF16), w1_ref[0, j],
                       preferred_element_type=F32)
        bot += jnp.dot((xj + pe_ref[0, CMP_STRIDE + j:CMP_STRIDE + j + 1, :]).astype(BF16),
                       w1_ref[0, CMP_STRIDE + j], preferred_element_type=F32)
    h = _gelu_tanh(top + pltpu.roll(bot, nc - 1, 0) + b1_ref[0])
    o_ref[0, 0, 0] = (jnp.dot(h.astype(BF16), w2_ref[0], preferred_element_type=F32) + b2_ref[0]).astype(BF16)


def _compress(kv, bsz, seq, pe, w1, b1, w2, b2):
    nc = seq // CMP_STRIDE
    hid = w1.shape[-1]
    w1r = w1.reshape(2, CMP_BLOCK, HEAD_DIM, hid)
    return pl.pallas_call(
        _compress_kernel,
        out_shape=jax.ShapeDtypeStruct((2, bsz, N_KV_HEADS, nc, HEAD_DIM), BF16),
        grid=(2, bsz, N_KV_HEADS),
        in_specs=[pl.BlockSpec((seq, HEAD_DIM), lambda s, b, h: (b, s * N_KV_HEADS + h)),
                  pl.BlockSpec((1, CMP_BLOCK, HEAD_DIM), lambda s, b, h: (s, 0, 0)),
                  pl.BlockSpec((1, CMP_BLOCK, HEAD_DIM, hid), lambda s, b, h: (s, 0, 0, 0)),
                  pl.BlockSpec((1, 1, hid), lambda s, b, h: (s, 0, 0)),
                  pl.BlockSpec((1, hid, HEAD_DIM), lambda s, b, h: (s, 0, 0)),
                  pl.BlockSpec((1, 1, HEAD_DIM), lambda s, b, h: (s, 0, 0))],
        out_specs=pl.BlockSpec((1, 1, 1, nc, HEAD_DIM), lambda s, b, h: (s, b, h, 0, 0)),
        compiler_params=_cparams(("parallel", "parallel", "parallel")),
        name="nsa_compress",
    )(kv, pe, w1r, b1.reshape(2, 1, hid), w2, b2.reshape(2, 1, HEAD_DIM))


def _attn_kernel(q_ref, gate_ref, cos_ref, sin_ref, kc_ref, vc_ref, c2s_ref,
                 ks_ref, vs_ref, kw_ref, vw_ref, o_ref, *, seq):
    qt = pl.program_id(2)
    s0 = qt * Q_TILE
    n_slc = seq // SLC_BLOCK
    n_cmp = kc_ref.shape[-2]
    scale = HEAD_DIM ** -0.5 * LOG2_E
    rows = GQA_GROUP * Q_TILE
    nt = (((1,), (1,)), ((), ()))

    def tile4(x):
        return jnp.concatenate([x] * GQA_GROUP, axis=0)

    q4 = q_ref[...]
    qp = jnp.concatenate([q4[:, g * HEAD_DIM:(g + 1) * HEAD_DIM] for g in range(GQA_GROUP)], axis=0) * scale
    qr_b = _rope(qp, tile4(cos_ref[...]), tile4(sin_ref[...])).astype(BF16)
    qp_b = qp.astype(BF16)
    t_q = s0 + lax.broadcasted_iota(jnp.int32, (Q_TILE, 1), 0)

    n_idx = lax.broadcasted_iota(jnp.int32, (Q_TILE, n_cmp), 1)
    c_ok = (n_idx * CMP_STRIDE + (CMP_BLOCK - 1)) <= t_q
    s_c = lax.dot_general(qp_b, kc_ref[0, 0], nt, preferred_element_type=F32) + tile4(jnp.where(c_ok, 0.0, NEG_INF))
    e_c = jnp.exp2(s_c - jnp.max(s_c, axis=-1, keepdims=True)) * tile4(jnp.where(c_ok, 1.0, 0.0))
    l_c = jnp.sum(e_c, axis=-1, keepdims=True)
    p_c = e_c / jnp.where(l_c > 0.0, l_c, 1.0)
    o_c = jnp.dot(p_c.astype(BF16), vc_ref[0, 0], preferred_element_type=F32)
    p_sum = (p_c[0:Q_TILE] + p_c[Q_TILE:2 * Q_TILE]) + (p_c[2 * Q_TILE:3 * Q_TILE] + p_c[3 * Q_TILE:])
    p_hi = p_sum.astype(BF16)
    r1 = p_sum - p_hi.astype(F32)
    p_mid = r1.astype(BF16)
    p_lo = (r1 - p_mid.astype(F32)).astype(BF16)
    c2s = c2s_ref[...]
    imp_t = (lax.dot_general(c2s, p_hi, nt, preferred_element_type=F32)
             + lax.dot_general(c2s, p_mid, nt, preferred_element_type=F32)
             + lax.dot_general(c2s, p_lo, nt, preferred_element_type=F32))

    sid = lax.broadcasted_iota(jnp.int32, (n_slc, Q_TILE), 0)
    t_l = s0 + lax.broadcasted_iota(jnp.int32, (n_slc, Q_TILE), 1)
    cur = lax.shift_right_logical(t_l, int(math.log2(SLC_BLOCK)))
    causal_blk = sid <= cur
    forced = jnp.where(sid == 0, 1.0, jnp.where(sid == cur, 1.0, jnp.where(sid == cur - 1, 1.0, 0.0)))
    work = jnp.where(causal_blk, imp_t + FORCE_BONUS * forced, NEG_INF)
    sid_f = sid.astype(F32)
    sel_t = jnp.zeros((n_slc, Q_TILE), F32)
    for _ in range(min(N_SELECT, n_slc)):
        mx = jnp.max(work, axis=0, keepdims=True)
        first = jnp.min(jnp.where(work == mx, sid_f, float(n_slc)), axis=0, keepdims=True)
        pick = sid_f == first
        sel_t = jnp.where(pick, 1.0, sel_t)
        work = jnp.where(pick, -3.0e38, work)
    sel_t = jnp.where(causal_blk, sel_t, 0.0)
    unsel = jnp.concatenate([sel_t - 1.0, jnp.zeros((HEAD_DIM - n_slc, Q_TILE), F32)], axis=0).T
    q_aug = jnp.concatenate([qr_b, tile4(unsel.astype(BF16))], axis=1)

    kt = 512

    def slc_tile(j, carry, diag_bias):
        m, l, acc = carry
        k0 = pl.multiple_of(j * kt, kt)
        s = lax.dot_general(q_aug, ks_ref[pl.ds(k0, kt), :], nt, preferred_element_type=F32)
        if diag_bias is not None:
            s = s + diag_bias
        m_new = jnp.maximum(m, jnp.max(s, axis=-1, keepdims=True))
        alpha = jnp.exp2(m - m_new)
        p = jnp.exp2(s - m_new)
        l = alpha * l + jnp.sum(p, axis=-1, keepdims=True)
        acc = alpha * acc + jnp.dot(p.astype(BF16), vs_ref[pl.ds(k0, kt), :], preferred_element_type=F32)
        return m_new, l, acc

    slab = WINDOW + Q_TILE
    w0 = pl.multiple_of(jnp.maximum(s0 - WINDOW, 0), Q_TILE)
    dist = t_q - (w0 + lax.broadcasted_iota(jnp.int32, (Q_TILE, slab), 1))
    w_bias = jnp.where(dist >= 0, jnp.where(dist < WINDOW, 0.0, NEG_INF), NEG_INF)
    s_w = lax.dot_general(qr_b, kw_ref[pl.ds(w0, slab), :], nt, preferred_element_type=F32) + tile4(w_bias)
    e_w = jnp.exp2(s_w - jnp.max(s_w, axis=-1, keepdims=True))
    o_w = (jnp.dot(e_w.astype(BF16), vw_ref[pl.ds(w0, slab), :], preferred_element_type=F32)
           / jnp.sum(e_w, axis=-1, keepdims=True))

    j_diag = lax.shift_right_logical(s0, int(math.log2(kt)))
    kpos = j_diag * kt + lax.broadcasted_iota(jnp.int32, (Q_TILE, kt), 1)
    init = (jnp.full((rows, 1), NEG_INF, F32), jnp.zeros((rows, 1), F32), jnp.zeros((rows, HEAD_DIM), F32))
    carry = slc_tile(j_diag, init, tile4(jnp.where(kpos <= t_q, 0.0, NEG_INF)))
    carry = lax.fori_loop(0, lax.shift_right_logical(j_diag, 1),
                          lambda p, c: slc_tile(2 * p + 1, slc_tile(2 * p, c, None), None), carry)
    _, l_s, acc_s = lax.cond(jnp.bitwise_and(j_diag, 1) == 1,
                             lambda c: slc_tile(j_diag - 1, c, None), lambda c: c, carry)
    o_s = acc_s / l_s

    sg = _sigmoid(gate_ref[...])
    for g in range(GQA_GROUP):
        r = slice(g * Q_TILE, (g + 1) * Q_TILE)
        o_ref[:, g * HEAD_DIM:(g + 1) * HEAD_DIM] = (
            sg[:, g:g + 1] * o_c[r]
            + sg[:, GQA_GROUP + g:GQA_GROUP + g + 1] * o_s[r]
            + sg[:, 2 * GQA_GROUP + g:2 * GQA_GROUP + g + 1] * o_w[r])


def _cmp_to_slc_t(seq):
    n_cmp = seq // CMP_STRIDE - 1
    n_slc = seq // SLC_BLOCK
    cmp_start = jnp.arange(n_cmp) * CMP_STRIDE
    slc_start = jnp.arange(n_slc) * SLC_BLOCK
    overlap = jnp.clip(jnp.minimum(cmp_start[:, None] + CMP_BLOCK, slc_start[None, :] + SLC_BLOCK)
                       - jnp.maximum(cmp_start[:, None], slc_start[None, :]), 0, None)
    c2s = overlap.astype(F32) / CMP_BLOCK
    return jnp.pad(c2s, ((0, 1), (0, 0))).T.astype(BF16)


def _nsa_attention(q, gates, kc, vc, ks, vs, kw, vw, bsz, seq):
    m = q.shape[0]
    nq = seq // Q_TILE
    nc = seq // CMP_STRIDE
    cos_t, sin_t = _rope_tables(seq, 1)
    c2s_t = _cmp_to_slc_t(seq)
    gw = GQA_GROUP * HEAD_DIM
    kv_spec = pl.BlockSpec((seq, HEAD_DIM), lambda b, h, t: (b, h))
    cmp_spec = pl.BlockSpec((1, 1, nc, HEAD_DIM), lambda b, h, t: (b, h, 0, 0))
    tab_spec = pl.BlockSpec((Q_TILE, HEAD_DIM), lambda b, h, t: (t, 0))
    return pl.pallas_call(
        functools.partial(_attn_kernel, seq=seq),
        out_shape=jax.ShapeDtypeStruct((m, N_HEADS * HEAD_DIM), F32),
        grid=(bsz, N_KV_HEADS, nq),
        in_specs=[pl.BlockSpec((Q_TILE, gw), lambda b, h, t: (b * nq + t, h)),
                  pl.BlockSpec((Q_TILE, LANES), lambda b, h, t: (b * nq + t, h)),
                  tab_spec, tab_spec, cmp_spec, cmp_spec,
                  pl.BlockSpec(c2s_t.shape, lambda b, h, t: (0, 0)),
                  pl.BlockSpec((seq, 2 * HEAD_DIM), lambda b, h, t: (b, h)), kv_spec, kv_spec, kv_spec],
        out_specs=pl.BlockSpec((Q_TILE, gw), lambda b, h, t: (b * nq + t, h)),
        compiler_params=_cparams(("parallel", "parallel", "arbitrary")),
        name="nsa_attention",
    )(q, gates, cos_t, sin_t, kc, vc, c2s_t, ks, vs, kw, vw)


def _s5_operators(a_re, a_im, log_dt, b_re, b_im, c_re, c_im):
    hp = lax.Precision.HIGHEST
    n_g = a_re.shape[0]
    gpb = SSM_GROUPS_PER_BLOCK
    n_gb = n_g // gpb
    t_c, hh, pp = SSM_CHUNK, SSM_GROUP, SSM_STATE
    lam = lax.complex(a_re.astype(F32), a_im.astype(F32))
    dt = jnp.exp(log_dt.astype(F32))[:, None]
    a_bar = jnp.exp(lam * dt)
    b_bar = ((a_bar - 1.0) / lam)[:, :, None] * lax.complex(b_re.astype(F32), b_im.astype(F32))
    c_mat = lax.complex(c_re.astype(F32), c_im.astype(F32))
    steps = jnp.arange(t_c + 1, dtype=F32)[:, None, None]
    pw = jnp.exp((lam * dt)[None] * steps)

    def blockdiag(x, r, c):
        tile = jnp.tile(jnp.eye(c, dtype=BF16), (1, gpb))
        rows = jnp.arange(gpb * r)[:, None] // r
        cols = jnp.arange(gpb * c)[None, :] // c
        tiled = jnp.einsum('...c,cd->...d', x.astype(BF16), tile, preferred_element_type=BF16)
        return jnp.where(rows == cols, tiled, jnp.zeros((), BF16))

    k_lag = jnp.einsum('ghp,kgp,gpi->kgih', c_mat, pw[:t_c], b_bar, precision=hp).real
    w_intra = blockdiag(k_lag.reshape(t_c, n_gb, gpb * hh, hh), hh, hh)
    w_intra = w_intra.transpose(1, 2, 0, 3).reshape(n_gb, LANES, t_c * LANES)
    p_in = jnp.swapaxes(pw[t_c - 1::-1][:, :, :, None] * b_bar[None], -1, -2)
    p_in = p_in.reshape(t_c, n_gb, gpb * hh, pp)
    s_in = jnp.concatenate([blockdiag(p_in.real, hh, pp), blockdiag(p_in.imag, hh, pp)], axis=-1)
    s_in = s_in.transpose(1, 0, 2, 3).reshape(n_gb, t_c * LANES, 2 * gpb * pp)
    m_out = jnp.swapaxes(c_mat[None] * pw[1:, :, None, :], -1, -2)
    m_out = m_out.reshape(t_c, n_gb, gpb * pp, hh)
    s_out = jnp.concatenate([blockdiag(m_out.real, pp, hh), blockdiag(-m_out.imag, pp, hh)], axis=2)
    s_out = s_out.transpose(1, 2, 0, 3).reshape(n_gb, 2 * gpb * pp, t_c * LANES)
    k_steps = jnp.arange(SCAN_TABLE_ROWS, dtype=F32)[:, None, None] * float(t_c)
    a_pow = jnp.exp((lam * dt)[None] * k_steps).reshape(SCAN_TABLE_ROWS, n_gb, gpb * pp).swapaxes(0, 1)
    return s_in, w_intra, s_out, a_pow.real.astype(F32), a_pow.imag.astype(F32)


def _s5_kernel(u_ref, sin_ref, wi_ref, sout_ref, ar_ref, ai_ref, d_ref, o_ref, xin_ref, xprev_ref, wt_ref):
    t_c = SSM_CHUNK
    nc = u_ref.shape[0] // t_c
    half = ar_ref.shape[-1]

    @pl.when(pl.program_id(1) == 0)
    def _():
        wt_ref[...] = jnp.zeros(wt_ref.shape, wt_ref.dtype)
        for tau in range(t_c):
            wt_ref[tau * LANES:(tau + 1) * LANES, tau * LANES:] = wi_ref[0, :, :(t_c - tau) * LANES]

    def step_rows(ref, t):
        return ref.at[pl.ds(t, nc, stride=t_c), :]

    u_flat = jnp.concatenate([step_rows(u_ref, tau)[...].astype(BF16) for tau in range(t_c)], axis=1)
    xin = jnp.dot(u_flat, sin_ref[0], preferred_element_type=F32)
    x_r, x_i = xin[:, :half], xin[:, half:]
    r_in_grp = jnp.bitwise_and(lax.broadcasted_iota(jnp.int32, (nc, 1), 0), SCAN_GROUP - 1)
    d = 1
    while d < SCAN_GROUP:
        keep = jnp.where(r_in_grp >= d, 1.0, 0.0)
        s_r, s_i = pltpu.roll(x_r, d, 0) * keep, pltpu.roll(x_i, d, 0) * keep
        ad_r, ad_i = ar_ref[0, d:d + 1, :], ai_ref[0, d:d + 1, :]
        x_r, x_i = x_r + ad_r * s_r - ad_i * s_i, x_i + ad_r * s_i + ad_i * s_r
        d *= 2
    keep = jnp.where(r_in_grp >= 1, 1.0, 0.0)
    xin_ref[:, :half] = x_r
    xin_ref[:, half:] = x_i
    xprev_ref[:, :half] = pltpu.roll(x_r, 1, 0) * keep
    xprev_ref[:, half:] = pltpu.roll(x_i, 1, 0) * keep
    ak_r, ak_i = ar_ref[0, 0:SCAN_GROUP, :], ai_ref[0, 0:SCAN_GROUP, :]
    ag_r, ag_i = ar_ref[0, SCAN_GROUP:SCAN_GROUP + 1, :], ai_ref[0, SCAN_GROUP:SCAN_GROUP + 1, :]

    def group(g, carry):
        c_r, c_i = carry
        rows = pl.ds(pl.multiple_of(g * SCAN_GROUP, SCAN_GROUP), SCAN_GROUP)
        xprev_ref[rows, :half] = xprev_ref[rows, :half] + ak_r * c_r - ak_i * c_i
        xprev_ref[rows, half:] = xprev_ref[rows, half:] + ak_r * c_i + ak_i * c_r
        last = xin_ref[pl.ds(g * SCAN_GROUP + SCAN_GROUP - 1, 1), :]
        return (last[:, :half] + ag_r * c_r - ag_i * c_i, last[:, half:] + ag_r * c_i + ag_i * c_r)

    zero = jnp.zeros((1, half), F32)
    lax.fori_loop(0, nc // SCAN_GROUP, group, (zero, zero))
    y = (jnp.dot(u_flat, wt_ref[...], preferred_element_type=F32)
         + jnp.dot(xprev_ref[...].astype(BF16), sout_ref[0], preferred_element_type=F32))
    for t in range(t_c):
        step_rows(o_ref, t)[...] = _gelu_tanh(y[:, t * LANES:(t + 1) * LANES] + d_ref[0] * step_rows(u_ref, t)[...])


def _s5_scan(u, bsz, seq, ops, d_skip):
    d_ssm = u.shape[1]
    t_c = SSM_CHUNK
    s_in, w_intra, s_out, a_r, a_i = ops
    n_gb = d_ssm // LANES
    nc = seq // t_c
    st = s_in.shape[-1]
    return pl.pallas_call(
        _s5_kernel,
        out_shape=jax.ShapeDtypeStruct(u.shape, F32),
        grid=(n_gb, bsz),
        in_specs=[pl.BlockSpec((seq, LANES), lambda g, b: (b, g)),
                  pl.BlockSpec((1,) + s_in.shape[1:], lambda g, b: (g, 0, 0)),
                  pl.BlockSpec((1,) + w_intra.shape[1:], lambda g, b: (g, 0, 0)),
                  pl.BlockSpec((1,) + s_out.shape[1:], lambda g, b: (g, 0, 0)),
                  pl.BlockSpec((1, SCAN_TABLE_ROWS, st // 2), lambda g, b: (g, 0, 0)),
                  pl.BlockSpec((1, SCAN_TABLE_ROWS, st // 2), lambda g, b: (g, 0, 0)),
                  pl.BlockSpec((1, 1, LANES), lambda g, b: (g, 0, 0))],
        out_specs=pl.BlockSpec((seq, LANES), lambda g, b: (b, g)),
        scratch_shapes=[pltpu.VMEM((nc, st), F32), pltpu.VMEM((nc, st), F32),
                        pltpu.VMEM((t_c * LANES, t_c * LANES), BF16)],
        compiler_params=_cparams(("arbitrary", "arbitrary")),
        name="s5_scan",
    )(u, s_in, w_intra, s_out, a_r, a_i, d_skip.reshape(n_gb, 1, LANES))


def _mixnorm_kernel(a_ref, s_ref, ga_ref, gs_ref, o_ref):
    da = a_ref.shape[-1]
    a = a_ref[...]
    s = s_ref[...]
    o_ref[:, :da] = (a * lax.rsqrt(jnp.mean(a * a, axis=-1, keepdims=True) + EPS) * ga_ref[...]).astype(BF16)
    o_ref[:, da:] = (s * lax.rsqrt(jnp.mean(s * s, axis=-1, keepdims=True) + EPS) * gs_ref[...]).astype(BF16)


def _mixnorm(attn, ssm, g_attn, g_ssm, tm=256):
    m, da = attn.shape
    ds = ssm.shape[1]
    return pl.pallas_call(
        _mixnorm_kernel,
        out_shape=jax.ShapeDtypeStruct((m, da + ds), BF16),
        grid=(m // tm,),
        in_specs=[pl.BlockSpec((tm, da), lambda i: (i, 0)), pl.BlockSpec((tm, ds), lambda i: (i, 0)),
                  pl.BlockSpec((1, da), lambda i: (0, 0)), pl.BlockSpec((1, ds), lambda i: (0, 0))],
        out_specs=pl.BlockSpec((tm, da + ds), lambda i: (i, 0)),
        compiler_params=_cparams(("parallel",)),
        name="mixnorm",
    )(attn, ssm, g_attn.reshape(1, da), g_ssm.reshape(1, ds))


def _split_bf16(x):
    hi = x.astype(BF16)
    return hi, (x - hi.astype(F32)).astype(BF16)


def _lane_max(x):
    return jnp.max(x, axis=-1, keepdims=True)


def _first_lane_of(cond_val, lane_f):
    return jnp.min(jnp.where(cond_val > 0.0, lane_f, float(LANES)), axis=-1, keepdims=True)


def _ffn_norm_router_kernel(h_ref, g_ref, whi_ref, wlo_ref, b_ref, hn_ref, route_ref):
    x = h_ref[...]
    hn = x * lax.rsqrt(jnp.mean(x * x, axis=-1, keepdims=True) + EPS) * g_ref[...]
    hn_ref[...] = _pack_bf16_pairs(hn)
    hi, lo = _split_bf16(hn)
    logits = (jnp.dot(hi, whi_ref[...], preferred_element_type=F32)
              + jnp.dot(lo, whi_ref[...], preferred_element_type=F32)
              + jnp.dot(hi, wlo_ref[...], preferred_element_type=F32)) + b_ref[...]
    lane = lax.broadcasted_iota(jnp.int32, logits.shape, 1)
    lane_f = lane.astype(F32)
    in_c = jnp.where(lane < N_EXPERT_GROUPS, 1.0, 0.0)
    c_l = jnp.where(in_c > 0.0, logits, NEG_INF)
    c_e = jnp.exp(c_l - _lane_max(c_l)) * in_c
    p_c = c_e / jnp.sum(c_e, axis=-1, keepdims=True)
    w_grp = _lane_max(p_c)
    grp = _first_lane_of(jnp.where(p_c == w_grp, in_c, 0.0), lane_f)
    lo_lane = float(N_EXPERT_GROUPS) + grp * float(EXPERTS_PER_GROUP)
    in_f = jnp.where(lane_f >= lo_lane, jnp.where(lane_f < lo_lane + float(EXPERTS_PER_GROUP), 1.0, 0.0), 0.0)
    f_l = jnp.where(in_f > 0.0, logits, NEG_INF)
    f_e = jnp.exp(f_l - _lane_max(f_l)) * in_f
    p_f = f_e / jnp.sum(f_e, axis=-1, keepdims=True)
    p_f = jnp.where(in_f > 0.0, p_f, -1.0)
    p1 = _lane_max(p_f)
    i1 = _first_lane_of(jnp.where(p_f == p1, in_f, 0.0), lane_f)
    rest = jnp.where(lane_f == i1, -1.0, p_f)
    p2 = _lane_max(rest)
    i2 = _first_lane_of(jnp.where(rest == p2, in_f, 0.0) * jnp.where(lane_f == i1, 0.0, 1.0), lane_f)
    denom = p1 + p2
    vals = (w_grp * p1 / denom, w_grp * p2 / denom,
            i1 - float(N_EXPERT_GROUPS), i2 - float(N_EXPERT_GROUPS))
    out = jnp.zeros(logits.shape, F32)
    for k, v in enumerate(vals):
        out = jnp.where(lane == k, v, out)
    route_ref[...] = out


def _ffn_norm_router(h, g, w_router, b_router, tm=256):
    m, d = h.shape
    n = w_router.shape[1]
    w_hi, w_lo = _split_bf16(w_router)
    return pl.pallas_call(
        _ffn_norm_router_kernel,
        out_shape=(jax.ShapeDtypeStruct((m, d // 2), jnp.uint32), jax.ShapeDtypeStruct((m, n), F32)),
        grid=(m // tm,),
        in_specs=[pl.BlockSpec((tm, d), lambda i: (i, 0)), pl.BlockSpec((1, d), lambda i: (0, 0)),
                  pl.BlockSpec((d, n), lambda i: (0, 0)), pl.BlockSpec((d, n), lambda i: (0, 0)),
                  pl.BlockSpec((1, n), lambda i: (0, 0))],
        out_specs=(pl.BlockSpec((tm, d // 2), lambda i: (i, 0)), pl.BlockSpec((tm, n), lambda i: (i, 0))),
        compiler_params=_cparams(("parallel",)),
        name="ffn_norm_router",
    )(h, g.reshape(1, d), w_hi, w_lo, b_router.reshape(1, n))


def _dispatch_plan(expert, tm):
    n_tok = expert.shape[0]
    n_assign = n_tok * TOP_K
    n_blocks = -(-(n_assign + N_EXPERTS * (MOE_BLOCK - 1)) // MOE_BLOCK)
    n_rows = n_blocks * MOE_BLOCK
    flat = expert.reshape(n_assign)
    ids = jnp.arange(n_assign, dtype=jnp.int32)
    _, order = lax.sort_key_val(flat, ids)
    _, inv = lax.sort_key_val(order, ids)
    e_ids = jnp.arange(N_EXPERTS, dtype=jnp.int32)
    counts = jnp.sum((flat[:, None] == e_ids[None, :]).astype(jnp.int32), axis=0)
    start = jnp.cumsum(counts) - counts
    padded = (counts + MOE_BLOCK - 1) // MOE_BLOCK * MOE_BLOCK
    pad_end = jnp.cumsum(padded)
    pad_start = pad_end - padded
    rows = jnp.arange(n_rows, dtype=jnp.int32)
    blk_row0 = jnp.arange(n_blocks, dtype=jnp.int32) * MOE_BLOCK
    blk_e = jnp.minimum(jnp.sum((pad_end[None, :] <= blk_row0[:, None]).astype(jnp.int32), axis=1), N_EXPERTS - 1)
    row_e = jnp.repeat(blk_e, MOE_BLOCK)
    slot = rows - pad_start[row_e]
    valid = (slot < counts[row_e]) & (rows < pad_end[-1])
    src = jnp.clip(start[row_e] + slot, 0, n_assign - 1)
    buf_tok = jnp.where(valid, order[src] // TOP_K, 0).astype(jnp.int32)
    dest = (pad_start[flat] + inv - start[flat]).astype(jnp.int32)
    dest = dest.reshape(n_tok // tm, tm, TOP_K).transpose(0, 2, 1).reshape(-1)
    plan = jnp.concatenate([pad_start // MOE_BLOCK, padded // MOE_BLOCK,
                            pad_end[-1:] // MOE_BLOCK]).astype(jnp.int32)
    return plan, buf_tok, dest, n_blocks


def _zero_fill_spare(plan_ref, zero_vmem, out_hbm, col0, width, sem, n_blocks):
    n_used = plan_ref[2 * N_EXPERTS]

    def cp(blk):
        return pltpu.make_async_copy(
            zero_vmem, out_hbm.at[pl.ds(pl.multiple_of(blk * MOE_BLOCK, MOE_BLOCK), MOE_BLOCK), pl.ds(col0, width)],
            sem)

    def start(blk, c):
        cp(blk).start(priority=SIDE_DMA_PRIORITY)
        return c

    def wait(blk, c):
        cp(blk).wait()
        return c

    lax.fori_loop(n_used, n_blocks, start, 0)
    lax.fori_loop(n_used, n_blocks, wait, 0)


def _pack_bf16_pairs(x):
    w = x.shape[1] // 2
    hi = lax.bitcast_convert_type(x[:, :w].astype(BF16).astype(F32), jnp.uint32)
    lo = lax.bitcast_convert_type(x[:, w:].astype(BF16).astype(F32), jnp.uint32)
    return hi | (lo >> 16)


def _unpack_bf16_pairs(p):
    first = lax.bitcast_convert_type(p & jnp.uint32(0xFFFF0000), F32)
    second = lax.bitcast_convert_type(p << 16, F32)
    return jnp.concatenate([first, second], axis=1).astype(BF16)


def _moe_up_kernel(plan_ref, tok_ref, x_hbm, *rest, n_blocks):
    w_refs = rest[:2 * W_DMA_SPLIT]
    act_hbm, wg_b, wu_b, xs, gbuf, obuf, afull, gsem, osem, fsem = rest[2 * W_DMA_SPLIT:]
    wg_refs, wu_refs = w_refs[:W_DMA_SPLIT], w_refs[W_DMA_SPLIT:]
    e = pl.program_id(0)
    c = pl.program_id(1)
    n_c = pl.num_programs(1)
    tn = wg_b.shape[1]
    col0 = pl.multiple_of(c * tn, tn)
    b0 = plan_ref[e]
    nb = plan_ref[N_EXPERTS + e]
    n_res = jnp.minimum(nb, MOE_X_CAP)

    def row_copy(blk, r, slot):
        tok = tok_ref[blk * MOE_BLOCK + r]
        return pltpu.make_async_copy(x_hbm.at[pl.ds(tok, 1), :], gbuf.at[slot, pl.ds(r, 1), :], gsem.at[slot])

    def gather_start(blk, slot):
        lax.fori_loop(0, MOE_BLOCK, lambda r, cr: (row_copy(blk, r, slot).start(priority=SIDE_DMA_PRIORITY), cr)[1], 0, unroll=DMA_UNROLL)

    def gather_wait(blk, slot):
        lax.fori_loop(0, MOE_BLOCK, lambda r, cr: (row_copy(blk, r, slot).wait(), cr)[1], 0, unroll=DMA_UNROLL)

    def gather_ahead(first_blk, n_blk):
        for k in range(GATHER_AHEAD):
            @pl.when(n_blk > k)
            def _():
                gather_start(first_blk + k, k)

    def rows_of(blk):
        return pl.ds(pl.multiple_of(blk * MOE_BLOCK, MOE_BLOCK), MOE_BLOCK)

    def out_copy(blk, slot):
        return pltpu.make_async_copy(obuf.at[slot], act_hbm.at[rows_of(blk), pl.ds(col0, tn)], osem.at[slot])

    def full_copy(j):
        return pltpu.make_async_copy(afull.at[j], act_hbm.at[rows_of(b0 + j), :], fsem)

    def act_of(x):
        g = jnp.dot(x, wg_b[...], preferred_element_type=F32)
        u = jnp.dot(x, wu_b[...], preferred_element_type=F32)
        return (g * _sigmoid(g) * u).astype(BF16)

    def emit_resident(x, j):
        act = act_of(x)
        for cc in range(act_hbm.shape[1] // tn):
            @pl.when(c == cc)
            def _():
                afull[j, :, cc * tn:(cc + 1) * tn] = act

    def emit_streamed(x, j):
        i = j - n_res
        slot = lax.rem(i, 2)
        act = act_of(x)

        @pl.when(i >= 2)
        def _():
            out_copy(b0 + j - 2, slot).wait()

        obuf[slot] = act
        out_copy(b0 + j, slot).start(priority=SIDE_DMA_PRIORITY)

    @pl.when(nb > 0)
    def _():
        rows = wg_b.shape[0] // W_DMA_SPLIT
        for k in range(W_DMA_SPLIT):
            wg_b[k * rows:(k + 1) * rows, :] = wg_refs[k][0].astype(BF16)
            wu_b[k * rows:(k + 1) * rows, :] = wu_refs[k][0].astype(BF16)

    @pl.when(c == 0)
    def _():
        @pl.when(e == 0)
        def _():
            gather_ahead(b0, nb)

        def gathered(keep):
            def body(j, carry):
                slot = lax.rem(j, GATHER_SLOTS)
                gather_wait(b0 + j, slot)

                @pl.when(j + GATHER_AHEAD < nb)
                def _():
                    gather_start(b0 + j + GATHER_AHEAD, lax.rem(j + GATHER_AHEAD, GATHER_SLOTS))

                x = _unpack_bf16_pairs(gbuf[slot])
                if keep:
                    xs[j] = x
                    emit_resident(x, j)
                else:
                    emit_streamed(x, j)
                return carry
            return body

        lax.fori_loop(0, n_res, gathered(True), 0)
        lax.fori_loop(n_res, nb, gathered(False), 0)

    @pl.when(c > 0)
    def _():
        def resident(j, carry):
            emit_resident(xs[j], j)
            return carry

        def regathered(j, carry):
            gather_start(b0 + j, 0)
            gather_wait(b0 + j, 0)
            emit_streamed(_unpack_bf16_pairs(gbuf[0]), j)
            return carry

        lax.fori_loop(0, n_res, resident, 0)
        lax.fori_loop(n_res, nb, regathered, 0)

    n_str = nb - n_res

    @pl.when(n_str >= 2)
    def _():
        out_copy(b0 + nb - 2, lax.rem(n_str, 2)).wait()

    @pl.when(n_str >= 1)
    def _():
        out_copy(b0 + nb - 1, lax.rem(n_str + 1, 2)).wait()

    @pl.when(c == n_c - 1)
    def _():
        lax.fori_loop(0, n_res, lambda j, cr: (full_copy(j).start(priority=SIDE_DMA_PRIORITY), cr)[1], 0)
        lax.fori_loop(0, n_res, lambda j, cr: (full_copy(j).wait(), cr)[1], 0)

    @pl.when(jnp.logical_and(c == n_c - 1, e + 1 < N_EXPERTS))
    def _():
        e_next = jnp.minimum(e + 1, N_EXPERTS - 1)
        gather_ahead(plan_ref[e_next], plan_ref[N_EXPERTS + e_next])

    @pl.when(jnp.logical_and(e == N_EXPERTS - 1, c == n_c - 1))
    def _():
        afull[0] = jnp.zeros(afull.shape[1:], afull.dtype)
        _zero_fill_spare(plan_ref, afull.at[0], act_hbm, 0, act_hbm.shape[1], fsem, n_blocks)


def _moe_up(hn_packed, w_gate, w_up, plan, buf_tok, n_blocks, n_chunks=4):
    d = w_gate.shape[-2]
    d_e = w_gate.shape[-1]
    tn = d_e // n_chunks
    w_specs = [pl.BlockSpec((1, d // W_DMA_SPLIT, tn), functools.partial(lambda e, c, plan, tok, k: (e, k, c), k=k))
               for k in range(W_DMA_SPLIT)]
    return pl.pallas_call(
        functools.partial(_moe_up_kernel, n_blocks=n_blocks),
        out_shape=jax.ShapeDtypeStruct((n_blocks * MOE_BLOCK, d_e), BF16),
        grid_spec=pltpu.PrefetchScalarGridSpec(
            num_scalar_prefetch=2,
            grid=(N_EXPERTS, n_chunks),
            in_specs=[pl.BlockSpec(memory_space=pl.ANY)] + w_specs + w_specs,
            out_specs=pl.BlockSpec(memory_space=pl.ANY),
            scratch_shapes=[pltpu.VMEM((d, tn), BF16), pltpu.VMEM((d, tn), BF16),
                            pltpu.VMEM((MOE_X_CAP, MOE_BLOCK, d), BF16),
                            pltpu.VMEM((GATHER_SLOTS, MOE_BLOCK, d // 2), jnp.uint32),
                            pltpu.VMEM((2, MOE_BLOCK, tn), BF16),
                            pltpu.VMEM((MOE_X_CAP, MOE_BLOCK, d_e), BF16),
                            pltpu.SemaphoreType.DMA((GATHER_SLOTS,)), pltpu.SemaphoreType.DMA((2,)),
                            pltpu.SemaphoreType.DMA(())]),
        compiler_params=_cparams(("arbitrary", "arbitrary")),
        name="moe_up",
    )(plan, buf_tok, hn_packed, *([w_gate] * W_DMA_SPLIT), *([w_up] * W_DMA_SPLIT))


def _moe_down_kernel(plan_ref, act_hbm, *rest, n_blocks):
    wd_refs, (y_hbm, wd_b, abuf, obuf, asem, osem) = rest[:W_DMA_SPLIT], rest[W_DMA_SPLIT:]
    e = pl.program_id(0)
    c = pl.program_id(1)
    tn = wd_b.shape[1]
    col0 = pl.multiple_of(c * tn, tn)
    b0 = plan_ref[e]
    nb = plan_ref[N_EXPERTS + e]

    def rows_of(blk):
        return pl.ds(pl.multiple_of(blk * MOE_BLOCK, MOE_BLOCK), MOE_BLOCK)

    def in_copy(blk, slot):
        return pltpu.make_async_copy(act_hbm.at[rows_of(blk), :], abuf.at[slot], asem.at[slot])

    def out_copy(blk, slot):
        return pltpu.make_async_copy(obuf.at[slot], y_hbm.at[rows_of(blk), :], osem.at[slot])

    @pl.when(jnp.logical_and(jnp.logical_and(e == 0, c == 0), nb > 0))
    def _():
        in_copy(b0, 0).start(priority=SIDE_DMA_PRIORITY)

    @pl.when(nb > 0)
    def _():
        rows = wd_b.shape[0] // W_DMA_SPLIT
        for k in range(W_DMA_SPLIT):
            wd_b[k * rows:(k + 1) * rows, :] = wd_refs[k][0].astype(BF16)

    def body(j, carry):
        slot = lax.rem(j, 2)
        in_copy(b0 + j, slot).wait()

        @pl.when(j + 1 < nb)
        def _():
            in_copy(b0 + j + 1, 1 - slot).start(priority=SIDE_DMA_PRIORITY)

        y = jnp.dot(abuf[slot], wd_b[...], preferred_element_type=F32)

        @pl.when(j >= 2)
        def _():
            out_copy(b0 + j - 2, slot).wait()

        obuf[slot] = _pack_bf16_pairs(y)
        out_copy(b0 + j, slot).start(priority=SIDE_DMA_PRIORITY)
        return carry

    lax.fori_loop(0, nb, body, 0)

    @pl.when(nb >= 2)
    def _():
        out_copy(b0 + nb - 2, lax.rem(nb, 2)).wait()

    @pl.when(nb >= 1)
    def _():
        out_copy(b0 + nb - 1, lax.rem(nb + 1, 2)).wait()

    n_c = pl.num_programs(1)
    same_expert = c + 1 < n_c
    e_next = jnp.where(same_expert, e, jnp.minimum(e + 1, N_EXPERTS - 1))

    @pl.when(jnp.logical_and(jnp.logical_or(same_expert, e + 1 < N_EXPERTS), plan_ref[N_EXPERTS + e_next] > 0))
    def _():
        in_copy(plan_ref[e_next], 0).start(priority=SIDE_DMA_PRIORITY)

    @pl.when(e == N_EXPERTS - 1)
    def _():
        obuf[0] = jnp.zeros(obuf.shape[1:], obuf.dtype)
        _zero_fill_spare(plan_ref, obuf.at[0], y_hbm, 0, y_hbm.shape[1], osem.at[0], n_blocks)


def _moe_down(act, w_down, plan, n_blocks):
    n_rows, d_e = act.shape
    d = w_down.shape[-1]
    n_chunks, tn = 1, d
    return pl.pallas_call(
        functools.partial(_moe_down_kernel, n_blocks=n_blocks),
        out_shape=jax.ShapeDtypeStruct((n_rows, d // 2), jnp.uint32),
        grid_spec=pltpu.PrefetchScalarGridSpec(
            num_scalar_prefetch=1,
            grid=(N_EXPERTS, n_chunks),
            in_specs=[pl.BlockSpec(memory_space=pl.ANY)] + [
                pl.BlockSpec((1, d_e // W_DMA_SPLIT, tn), functools.partial(lambda e, c, plan, k: (e, k, c), k=k))
                for k in range(W_DMA_SPLIT)],
            out_specs=pl.BlockSpec(memory_space=pl.ANY),
            scratch_shapes=[pltpu.VMEM((d_e, tn), BF16),
                            pltpu.VMEM((2, MOE_BLOCK, d_e), BF16), pltpu.VMEM((2, MOE_BLOCK, d // 2), jnp.uint32),
                            pltpu.SemaphoreType.DMA((2,)), pltpu.SemaphoreType.DMA((2,))]),
        compiler_params=_cparams(("arbitrary", "arbitrary")),
        name="moe_down",
    )(plan, act, *([w_down] * W_DMA_SPLIT))


def _combine_kernel(dest_ref, h_ref, route_ref, gn_ref, y_hbm, o_ref, rows_ref, sem):
    tm = h_ref.shape[0]
    n_rows = tm * TOP_K
    i = pl.program_id(0)
    n = pl.num_programs(0)

    def row_copy(step, r, slot):
        src = dest_ref[step * n_rows + r]
        return pltpu.make_async_copy(y_hbm.at[pl.ds(src, 1), :], rows_ref.at[slot, pl.ds(r, 1), :], sem.at[slot])

    def gather_start(step, slot):
        lax.fori_loop(0, n_rows, lambda r, cr: (row_copy(step, r, slot).start(priority=SIDE_DMA_PRIORITY), cr)[1], 0, unroll=DMA_UNROLL)

    def gather_wait(step, slot):
        lax.fori_loop(0, n_rows, lambda r, cr: (row_copy(step, r, slot).wait(), cr)[1], 0, unroll=DMA_UNROLL)

    slot = lax.rem(i, 2)

    @pl.when(i == 0)
    def _():
        gather_start(0, 0)

    @pl.when(i + 1 < n)
    def _():
        gather_start(i + 1, 1 - slot)

    gather_wait(i, slot)
    route = route_ref[...]
    rows = rows_ref[slot]
    g0, g1 = route[:, 0:1], route[:, 1:2]
    r0, r1 = rows[0:tm, :], rows[tm:2 * tm, :]
    half = rows.shape[1]

    def first(p):
        return lax.bitcast_convert_type(p & jnp.uint32(0xFFFF0000), F32)

    def second(p):
        return lax.bitcast_convert_type(p << 16, F32)

    h = jnp.concatenate([h_ref[:, :half] + g0 * first(r0) + g1 * first(r1),
                         h_ref[:, half:] + g0 * second(r0) + g1 * second(r1)], axis=1)
    o_ref[...] = h * lax.rsqrt(jnp.mean(h * h, axis=-1, keepdims=True) + EPS) * gn_ref[...]


def _combine_final(h, route, dest, y_buf, g_final, tm=128):
    m, d = h.shape
    return pl.pallas_call(
        _combine_kernel,
        out_shape=jax.ShapeDtypeStruct((m, d), F32),
        grid_spec=pltpu.PrefetchScalarGridSpec(
            num_scalar_prefetch=1,
            grid=(m // tm,),
            in_specs=[pl.BlockSpec((tm, d), lambda i, dst: (i, 0)),
                      pl.BlockSpec((tm, LANES), lambda i, dst: (i, 0)),
                      pl.BlockSpec((1, d), lambda i, dst: (0, 0)),
                      pl.BlockSpec(memory_space=pl.ANY)],
            out_specs=pl.BlockSpec((tm, d), lambda i, dst: (i, 0)),
            scratch_shapes=[pltpu.VMEM((2, TOP_K * tm, d // 2), jnp.uint32), pltpu.SemaphoreType.DMA((2,))]),
        compiler_params=_cparams(("arbitrary",)),
        name="moe_combine_final",
    )(dest, h, route, g_final.reshape(1, d), y_buf)


def _gate_column_order():
    cols = []
    for hk in range(N_KV_HEADS):
        blk = [-1] * LANES
        for br in range(N_BRANCHES):
            for g in range(GQA_GROUP):
                blk[br * GQA_GROUP + g] = (hk * GQA_GROUP + g) * N_BRANCHES + br
        cols.extend(blk)
    return cols


def kernel(x, norm_mix, w_in, cmp_pe_k, cmp_w1_k, cmp_b1_k, cmp_w2_k, cmp_b2_k, cmp_pe_v, cmp_w1_v, cmp_b1_v, cmp_w2_v, cmp_b2_v, ssm_a_re, ssm_a_im, ssm_log_dt, ssm_b_re, ssm_b_im, ssm_c_re, ssm_c_im, ssm_d, ssm_w_glu, ssm_b_glu, norm_attn_out, norm_ssm_out, w_out, norm_ffn, router_w_coarse, router_b_coarse, router_w_fine, router_b_fine, w_gate, w_up, w_down, norm_final):
    bsz, seq, d_model = x.shape
    m = bsz * seq
    depth = w_in.shape[0]
    assert depth == 1, "the combine kernel fuses the final norm; stacked layers are not supported"
    layer = 0
    q_w = N_HEADS * HEAD_DIM
    kv_w = N_KV_HEADS * HEAD_DIM
    gate_w = N_HEADS * N_BRANCHES
    cut1, cut2 = q_w, q_w + 6 * kv_w
    cut3 = cut2 + gate_w
    h = x.astype(F32).reshape(m, d_model)
    gate_cols = jnp.array([c if c >= 0 else 0 for c in _gate_column_order()], jnp.int32)
    gate_keep = jnp.array([1.0 if c >= 0 else 0.0 for c in _gate_column_order()], F32)

    w_in_b = w_in[layer].astype(BF16)
    hn = _rmsnorm(h, norm_mix[layer], BF16)
    q = _matmul(hn, w_in_b[:, :cut1], name="proj_q")
    kv = _matmul(hn, w_in_b[:, cut1:cut2], name="proj_kv")
    w_gate_cols = (w_in[layer][:, cut2:cut3][:, gate_cols] * gate_keep[None, :]).astype(BF16)
    gates = _matmul(hn, w_gate_cols, tn=512, name="proj_gate")
    u = _matmul(hn, w_in_b[:, cut3:], name="proj_u")

    stack2 = lambda a, b: jnp.stack([a[layer], b[layer]])
    cmp_kv = _compress(kv, bsz, seq, stack2(cmp_pe_k, cmp_pe_v),
                       stack2(cmp_w1_k, cmp_w1_v).astype(BF16), stack2(cmp_b1_k, cmp_b1_v),
                       stack2(cmp_w2_k, cmp_w2_v).astype(BF16), stack2(cmp_b2_k, cmp_b2_v))
    ks, vs, kw, vw = _kv_prep(kv, seq)
    attn = _nsa_attention(q, gates, cmp_kv[0], cmp_kv[1], ks, vs, kw, vw, bsz, seq)

    ops = _s5_operators(ssm_a_re[layer], ssm_a_im[layer], ssm_log_dt[layer], ssm_b_re[layer],
                        ssm_b_im[layer], ssm_c_re[layer], ssm_c_im[layer])
    y = _s5_scan(u, bsz, seq, ops, ssm_d[layer].astype(F32))
    ssm = _glu_matmul(y, ssm_w_glu[layer].astype(BF16), ssm_b_glu[layer].astype(F32))

    mixed = _mixnorm(attn, ssm, norm_attn_out[layer], norm_ssm_out[layer])
    h = _matmul(mixed, w_out[layer].astype(BF16), residual=h, name="out_proj")

    n_r = N_EXPERT_GROUPS + N_EXPERTS
    w_router = jnp.pad(jnp.concatenate([router_w_coarse[layer], router_w_fine[layer]], axis=1),
                       ((0, 0), (0, LANES - n_r))).astype(F32)
    b_router = jnp.pad(jnp.concatenate([router_b_coarse[layer], router_b_fine[layer]]), (0, LANES - n_r))
    hn2, route = _ffn_norm_router(h, norm_ffn[layer], w_router, b_router.astype(F32))
    expert = route[:, 2:2 + TOP_K].astype(jnp.int32)
    tm_c = 128
    plan, buf_tok, dest, n_blocks = _dispatch_plan(expert, tm_c)
    act = _moe_up(hn2, w_gate[layer], w_up[layer], plan, buf_tok, n_blocks)
    y_buf = _moe_down(act, w_down[layer], plan, n_blocks)
    out = _combine_final(h, route, dest, y_buf, norm_final, tm=tm_c)
    return out.reshape(bsz, seq, d_model).astype(x.dtype)
```

```python
import functools
import math

import jax
import jax.numpy as jnp
from jax import lax
from jax.experimental import pallas as pl
from jax.experimental.pallas import tpu as pltpu

F32 = jnp.float32
BF16 = jnp.bfloat16

HEAD_DIM = 128
N_HEADS = 16
N_KV_HEADS = 4
GQA_GROUP = N_HEADS // N_KV_HEADS
ROT_DIM = HEAD_DIM // 4
ROPE_THETA = 500000.0
CMP_STRIDE = 16
CMP_BLOCK = 32
SLC_BLOCK = 64
N_SELECT = 16
WINDOW = 512
Q_TILE = 256
N_BRANCHES = 3
FORCE_BONUS = 1.0e4
NEG_INF = -1.0e30
MASK_BIG = 1.0e30
SSM_GROUP = 16
SSM_STATE = 64
SSM_CHUNK = 16
SSM_GROUPS_PER_BLOCK = 128 // SSM_GROUP
N_EXPERT_GROUPS = 8
EXPERTS_PER_GROUP = 8
N_EXPERTS = N_EXPERT_GROUPS * EXPERTS_PER_GROUP
TOP_K = 2
MOE_BLOCK = 128
EPS = 1.0e-6
LOG2_E = 1.4426950408889634

V7X_VMEM_LIMIT = 56 * 1024 * 1024
LANES = 128
SCAN_GROUP = 8
SCAN_TABLE_ROWS = 16
DMA_UNROLL = 8
SIDE_DMA_PRIORITY = 1
MOE_X_CAP = 12
W_DMA_SPLIT = 4
GATHER_SLOTS = MOE_X_CAP + 1
DOWN_ACT_CAP = 8
DOWN_OUT_SLOTS = 4


def _cparams(sem, vmem=V7X_VMEM_LIMIT):
    return pltpu.CompilerParams(dimension_semantics=sem, vmem_limit_bytes=vmem)


def _gelu_tanh(x):
    return 0.5 * x * (1.0 + jnp.tanh(math.sqrt(2.0 / math.pi) * (x + 0.044715 * (x * x * x))))


def _sigmoid(x):
    return 1.0 / (1.0 + jnp.exp(-x))


def _rmsnorm_kernel(x_ref, g_ref, o_ref):
    x = x_ref[...]
    ms = jnp.mean(x * x, axis=-1, keepdims=True)
    o_ref[...] = (x * lax.rsqrt(ms + EPS) * g_ref[...]).astype(o_ref.dtype)


def _rmsnorm(x, g, out_dtype, tm=256):
    m, d = x.shape
    return pl.pallas_call(
        _rmsnorm_kernel,
        out_shape=jax.ShapeDtypeStruct((m, d), out_dtype),
        grid=(m // tm,),
        in_specs=[pl.BlockSpec((tm, d), lambda i: (i, 0)),
                  pl.BlockSpec((1, d), lambda i: (0, 0))],
        out_specs=pl.BlockSpec((tm, d), lambda i: (i, 0)),
        compiler_params=_cparams(("parallel",)),
        name="rmsnorm",
    )(x, g.reshape(1, d))


def _mm_kernel(a_ref, w_ref, o_ref):
    o_ref[...] = jnp.dot(a_ref[...], w_ref[...], preferred_element_type=F32)


def _mm_glu_kernel(y_ref, w_ref, b_ref, yg_ref, o_ref):
    z = jnp.dot(y_ref[...].astype(BF16), w_ref[...], preferred_element_type=F32) + b_ref[...]
    o_ref[...] = yg_ref[...] * _sigmoid(z)


def _mm_residual_kernel(a_ref, w_ref, r_ref, o_ref):
    o_ref[...] = r_ref[...] + jnp.dot(a_ref[...], w_ref[...], preferred_element_type=F32)


def _matmul(a, w, *, residual=None, tm=512, tn=1024, name="matmul"):
    m, k = a.shape
    n = w.shape[1]
    tn = min(tn, n)
    assert m % tm == 0 and n % tn == 0
    a_spec = pl.BlockSpec((tm, k), lambda j, i: (i, 0))
    w_spec = pl.BlockSpec((k, tn), lambda j, i: (0, j))
    o_spec = pl.BlockSpec((tm, tn), lambda j, i: (i, j))
    if residual is not None:
        kern, ins, specs = _mm_residual_kernel, (a, w, residual), [a_spec, w_spec, o_spec]
    else:
        kern, ins, specs = _mm_kernel, (a, w), [a_spec, w_spec]
    return pl.pallas_call(
        kern,
        out_shape=jax.ShapeDtypeStruct((m, n), F32),
        grid=(n // tn, m // tm),
        in_specs=specs,
        out_specs=o_spec,
        compiler_params=_cparams(("parallel", "parallel")),
        name=name,
    )(*ins)


def _glu_matmul(y, w, bias, tm=512, tn=1024):
    m, k = y.shape
    n = w.shape[1]
    return pl.pallas_call(
        _mm_glu_kernel,
        out_shape=jax.ShapeDtypeStruct((m, n), F32),
        grid=(n // tn, m // tm),
        in_specs=[pl.BlockSpec((tm, k), lambda j, i: (i, 0)),
                  pl.BlockSpec((k, tn), lambda j, i: (0, j)),
                  pl.BlockSpec((1, tn), lambda j, i: (0, j)),
                  pl.BlockSpec((tm, tn), lambda j, i: (i, j))],
        out_specs=pl.BlockSpec((tm, tn), lambda j, i: (i, j)),
        compiler_params=_cparams(("parallel", "parallel")),
        name="ssm_glu",
    )(y, w, bias.reshape(1, n), y)


def _rope_tables(seq, reps):
    half = ROT_DIM // 2
    inv_freq = ROPE_THETA ** (-jnp.arange(half, dtype=F32) / half)
    ang = jnp.arange(seq).astype(F32)[:, None] * inv_freq[None, :]
    cos, sin = jnp.cos(ang), jnp.sin(ang)
    ones = jnp.ones((seq, HEAD_DIM - ROT_DIM), F32)
    cos_t = jnp.concatenate([cos, cos, ones], axis=-1)
    sin_t = jnp.concatenate([-sin, sin, 0.0 * ones], axis=-1)
    return jnp.tile(cos_t, (1, reps)), jnp.tile(sin_t, (1, reps))


def _rope(x, cos_t, sin_t):
    half = ROT_DIM // 2
    n = x.shape[-1]
    lane = jnp.bitwise_and(lax.broadcasted_iota(jnp.int32, x.shape, x.ndim - 1), HEAD_DIM - 1)
    partner = jnp.where(lane < half, pltpu.roll(x, n - half, x.ndim - 1), pltpu.roll(x, half, x.ndim - 1))
    return x * cos_t + partner * sin_t


def _kv_prep_kernel(slc_ref, win_ref, cos_ref, sin_ref, ks_ref, vs_ref, kw_ref, vw_ref, *, tiles_per_seq):
    tm, w = vs_ref.shape
    cos_t, sin_t = cos_ref[...], sin_ref[...]
    slc = slc_ref[...]
    win = win_ref[...]
    k_slc = _rope(slc[:, :w], cos_t, sin_t).astype(BF16)
    pos = lax.rem(pl.program_id(0), tiles_per_seq) * tm + lax.broadcasted_iota(jnp.int32, (tm, HEAD_DIM), 0)
    lane = lax.broadcasted_iota(jnp.int32, (tm, HEAD_DIM), 1)
    own_blk = jnp.where(lane == lax.shift_right_logical(pos, int(math.log2(SLC_BLOCK))), MASK_BIG, 0.0).astype(BF16)
    for hk in range(N_KV_HEADS):
        ks_ref[:, 2 * hk * HEAD_DIM:(2 * hk + 1) * HEAD_DIM] = k_slc[:, hk * HEAD_DIM:(hk + 1) * HEAD_DIM]
        ks_ref[:, (2 * hk + 1) * HEAD_DIM:(2 * hk + 2) * HEAD_DIM] = own_blk
    vs_ref[...] = slc[:, w:].astype(BF16)
    kw_ref[...] = _rope(win[:, :w], cos_t, sin_t).astype(BF16)
    vw_ref[...] = win[:, w:].astype(BF16)


def _kv_prep(kv, seq, tm=256):
    assert seq // SLC_BLOCK <= HEAD_DIM
    m = kv.shape[0]
    w = N_KV_HEADS * HEAD_DIM
    cos_t, sin_t = _rope_tables(seq, N_KV_HEADS)
    tpb = seq // tm
    out = jax.ShapeDtypeStruct((m, w), BF16)
    o_spec = pl.BlockSpec((tm, w), lambda i: (i, 0))
    t_spec = pl.BlockSpec((tm, w), lambda i: (i % tpb, 0))
    return pl.pallas_call(
        functools.partial(_kv_prep_kernel, tiles_per_seq=tpb),
        out_shape=(jax.ShapeDtypeStruct((m, 2 * w), BF16), out, out, out),
        grid=(m // tm,),
        in_specs=[pl.BlockSpec((tm, 2 * w), lambda i: (i, 1)),
                  pl.BlockSpec((tm, 2 * w), lambda i: (i, 2)),
                  t_spec, t_spec],
        out_specs=(pl.BlockSpec((tm, 2 * w), lambda i: (i, 0)), o_spec, o_spec, o_spec),
        compiler_params=_cparams(("parallel",)),
        name="kv_prep",
    )(kv, kv, cos_t, sin_t)


def _compress_kernel(x_ref, pe_ref, w1_ref, b1_ref, w2_ref, b2_ref, o_ref):
    nc = x_ref.shape[0] // CMP_STRIDE
    hid = w1_ref.shape[-1]
    top = jnp.zeros((nc, hid), F32)
    bot = jnp.zeros((nc, hid), F32)
    for j in range(CMP_STRIDE):
        xj = x_ref[pl.ds(j, nc, stride=CMP_STRIDE), :]
        top += jnp.dot((xj + pe_ref[0, j:j + 1, :]).astype(BF16), w1_ref[0, j],
                       preferred_element_type=F32)
        bot += jnp.dot((xj + pe_ref[0, CMP_STRIDE + j:CMP_STRIDE + j + 1, :]).astype(BF16),
                       w1_ref[0, CMP_STRIDE + j], preferred_element_type=F32)
    h = _gelu_tanh(top + pltpu.roll(bot, nc - 1, 0) + b1_ref[0])
    o_ref[0, 0, 0] = (jnp.dot(h.astype(BF16), w2_ref[0], preferred_element_type=F32) + b2_ref[0]).astype(BF16)


def _compress(kv, bsz, seq, pe, w1, b1, w2, b2):
    nc = seq // CMP_STRIDE
    hid = w1.shape[-1]
    w1r = w1.reshape(2, CMP_BLOCK, HEAD_DIM, hid)
    return pl.pallas_call(
        _compress_kernel,
        out_shape=jax.ShapeDtypeStruct((2, bsz, N_KV_HEADS, nc, HEAD_DIM), BF16),
        grid=(2, bsz, N_KV_HEADS),
        in_specs=[pl.BlockSpec((seq, HEAD_DIM), lambda s, b, h: (b, s * N_KV_HEADS + h)),
                  pl.BlockSpec((1, CMP_BLOCK, HEAD_DIM), lambda s, b, h: (s, 0, 0)),
                  pl.BlockSpec((1, CMP_BLOCK, HEAD_DIM, hid), lambda s, b, h: (s, 0, 0, 0)),
                  pl.BlockSpec((1, 1, hid), lambda s, b, h: (s, 0, 0)),
                  pl.BlockSpec((1, hid, HEAD_DIM), lambda s, b, h: (s, 0, 0)),
                  pl.BlockSpec((1, 1, HEAD_DIM), lambda s, b, h: (s, 0, 0))],
        out_specs=pl.BlockSpec((1, 1, 1, nc, HEAD_DIM), lambda s, b, h: (s, b, h, 0, 0)),
        compiler_params=_cparams(("parallel", "parallel", "parallel")),
        name="nsa_compress",
    )(kv, pe, w1r, b1.reshape(2, 1, hid), w2, b2.reshape(2, 1, HEAD_DIM))


def _attn_kernel(q_ref, gate_ref, cos_ref, sin_ref, kc_ref, vc_ref, c2s_ref,
                 ks_ref, vs_ref, kw_ref, vw_ref, o_ref, *, seq):
    qt = pl.program_id(2)
    s0 = qt * Q_TILE
    n_slc = seq // SLC_BLOCK
    n_cmp = kc_ref.shape[-2]
    scale = HEAD_DIM ** -0.5 * LOG2_E
    rows = GQA_GROUP * Q_TILE
    nt = (((1,), (1,)), ((), ()))

    def tile4(x):
        return jnp.concatenate([x] * GQA_GROUP, axis=0)

    q4 = q_ref[...]
    qp = jnp.concatenate([q4[:, g * HEAD_DIM:(g + 1) * HEAD_DIM] for g in range(GQA_GROUP)], axis=0) * scale
    qr_b = _rope(qp, tile4(cos_ref[...]), tile4(sin_ref[...])).astype(BF16)
    qp_b = qp.astype(BF16)
    t_q = s0 + lax.broadcasted_iota(jnp.int32, (Q_TILE, 1), 0)

    n_idx = lax.broadcasted_iota(jnp.int32, (Q_TILE, n_cmp), 1)
    c_ok = (n_idx * CMP_STRIDE + (CMP_BLOCK - 1)) <= t_q
    s_c = lax.dot_general(qp_b, kc_ref[0, 0], nt, preferred_element_type=F32) + tile4(jnp.where(c_ok, 0.0, NEG_INF))
    e_c = jnp.exp2(s_c - jnp.max(s_c, axis=-1, keepdims=True)) * tile4(jnp.where(c_ok, 1.0, 0.0))
    l_c = jnp.sum(e_c, axis=-1, keepdims=True)
    p_c = e_c / jnp.where(l_c > 0.0, l_c, 1.0)
    o_c = jnp.dot(p_c.astype(BF16), vc_ref[0, 0], preferred_element_type=F32)
    p_sum = (p_c[0:Q_TILE] + p_c[Q_TILE:2 * Q_TILE]) + (p_c[2 * Q_TILE:3 * Q_TILE] + p_c[3 * Q_TILE:])
    p_hi = p_sum.astype(BF16)
    r1 = p_sum - p_hi.astype(F32)
    p_mid = r1.astype(BF16)
    p_lo = (r1 - p_mid.astype(F32)).astype(BF16)
    c2s = c2s_ref[...]
    imp_t = (lax.dot_general(c2s, p_hi, nt, preferred_element_type=F32)
             + lax.dot_general(c2s, p_mid, nt, preferred_element_type=F32)
             + lax.dot_general(c2s, p_lo, nt, preferred_element_type=F32))

    sid = lax.broadcasted_iota(jnp.int32, (n_slc, Q_TILE), 0)
    t_l = s0 + lax.broadcasted_iota(jnp.int32, (n_slc, Q_TILE), 1)
    cur = lax.shift_right_logical(t_l, int(math.log2(SLC_BLOCK)))
    causal_blk = sid <= cur
    forced = jnp.where(sid == 0, 1.0, jnp.where(sid == cur, 1.0, jnp.where(sid == cur - 1, 1.0, 0.0)))
    work = jnp.where(causal_blk, imp_t + FORCE_BONUS * forced, NEG_INF)
    sid_f = sid.astype(F32)
    sel_t = jnp.zeros((n_slc, Q_TILE), F32)
    for _ in range(min(N_SELECT, n_slc)):
        mx = jnp.max(work, axis=0, keepdims=True)
        first = jnp.min(jnp.where(work == mx, sid_f, float(n_slc)), axis=0, keepdims=True)
        pick = sid_f == first
        sel_t = jnp.where(pick, 1.0, sel_t)
        work = jnp.where(pick, -3.0e38, work)
    sel_t = jnp.where(causal_blk, sel_t, 0.0)
    unsel = jnp.concatenate([sel_t - 1.0, jnp.zeros((HEAD_DIM - n_slc, Q_TILE), F32)], axis=0).T
    q_aug = jnp.concatenate([qr_b, tile4(unsel.astype(BF16))], axis=1)

    kt = 512

    def slc_tile(j, carry, diag_bias):
        m, l, acc = carry
        k0 = pl.multiple_of(j * kt, kt)
        s = lax.dot_general(q_aug, ks_ref[pl.ds(k0, kt), :], nt, preferred_element_type=F32)
        if diag_bias is not None:
            s = s + diag_bias
        m_new = jnp.maximum(m, jnp.max(s, axis=-1, keepdims=True))
        alpha = jnp.exp2(m - m_new)
        p = jnp.exp2(s - m_new)
        l = alpha * l + jnp.sum(p, axis=-1, keepdims=True)
        acc = alpha * acc + jnp.dot(p.astype(BF16), vs_ref[pl.ds(k0, kt), :], preferred_element_type=F32)
        return m_new, l, acc

    slab = WINDOW + Q_TILE
    w0 = pl.multiple_of(jnp.maximum(s0 - WINDOW, 0), Q_TILE)
    dist = t_q - (w0 + lax.broadcasted_iota(jnp.int32, (Q_TILE, slab), 1))
    w_bias = jnp.where(dist >= 0, jnp.where(dist < WINDOW, 0.0, NEG_INF), NEG_INF)
    s_w = lax.dot_general(qr_b, kw_ref[pl.ds(w0, slab), :], nt, preferred_element_type=F32) + tile4(w_bias)
    e_w = jnp.exp2(s_w - jnp.max(s_w, axis=-1, keepdims=True))
    o_w = (jnp.dot(e_w.astype(BF16), vw_ref[pl.ds(w0, slab), :], preferred_element_type=F32)
           / jnp.sum(e_w, axis=-1, keepdims=True))

    j_diag = lax.shift_right_logical(s0, int(math.log2(kt)))
    kpos = j_diag * kt + lax.broadcasted_iota(jnp.int32, (Q_TILE, kt), 1)
    init = (jnp.full((rows, 1), NEG_INF, F32), jnp.zeros((rows, 1), F32), jnp.zeros((rows, HEAD_DIM), F32))
    carry = slc_tile(j_diag, init, tile4(jnp.where(kpos <= t_q, 0.0, NEG_INF)))
    carry = lax.fori_loop(0, lax.shift_right_logical(j_diag, 1),
                          lambda p, c: slc_tile(2 * p + 1, slc_tile(2 * p, c, None), None), carry)
    _, l_s, acc_s = lax.cond(jnp.bitwise_and(j_diag, 1) == 1,
                             lambda c: slc_tile(j_diag - 1, c, None), lambda c: c, carry)
    o_s = acc_s / l_s

    sg = _sigmoid(gate_ref[...])
    for g in range(GQA_GROUP):
        r = slice(g * Q_TILE, (g + 1) * Q_TILE)
        o_ref[:, g * HEAD_DIM:(g + 1) * HEAD_DIM] = (
            sg[:, g:g + 1] * o_c[r]
            + sg[:, GQA_GROUP + g:GQA_GROUP + g + 1] * o_s[r]
            + sg[:, 2 * GQA_GROUP + g:2 * GQA_GROUP + g + 1] * o_w[r])


def _cmp_to_slc_t(seq):
    n_cmp = seq // CMP_STRIDE - 1
    n_slc = seq // SLC_BLOCK
    cmp_start = jnp.arange(n_cmp) * CMP_STRIDE
    slc_start = jnp.arange(n_slc) * SLC_BLOCK
    overlap = jnp.clip(jnp.minimum(cmp_start[:, None] + CMP_BLOCK, slc_start[None, :] + SLC_BLOCK)
                       - jnp.maximum(cmp_start[:, None], slc_start[None, :]), 0, None)
    c2s = overlap.astype(F32) / CMP_BLOCK
    return jnp.pad(c2s, ((0, 1), (0, 0))).T.astype(BF16)


def _nsa_attention(q, gates, kc, vc, ks, vs, kw, vw, bsz, seq):
    m = q.shape[0]
    nq = seq // Q_TILE
    nc = seq // CMP_STRIDE
    cos_t, sin_t = _rope_tables(seq, 1)
    c2s_t = _cmp_to_slc_t(seq)
    gw = GQA_GROUP * HEAD_DIM
    kv_spec = pl.BlockSpec((seq, HEAD_DIM), lambda b, h, t: (b, h))
    cmp_spec = pl.BlockSpec((1, 1, nc, HEAD_DIM), lambda b, h, t: (b, h, 0, 0))
    tab_spec = pl.BlockSpec((Q_TILE, HEAD_DIM), lambda b, h, t: (t, 0))
    return pl.pallas_call(
        functools.partial(_attn_kernel, seq=seq),
        out_shape=jax.ShapeDtypeStruct((m, N_HEADS * HEAD_DIM), F32),
        grid=(bsz, N_KV_HEADS, nq),
        in_specs=[pl.BlockSpec((Q_TILE, gw), lambda b, h, t: (b * nq + t, h)),
                  pl.BlockSpec((Q_TILE, LANES), lambda b, h, t: (b * nq + t, h)),
                  tab_spec, tab_spec, cmp_spec, cmp_spec,
                  pl.BlockSpec(c2s_t.shape, lambda b, h, t: (0, 0)),
                  pl.BlockSpec((seq, 2 * HEAD_DIM), lambda b, h, t: (b, h)), kv_spec, kv_spec, kv_spec],
        out_specs=pl.BlockSpec((Q_TILE, gw), lambda b, h, t: (b * nq + t, h)),
        compiler_params=_cparams(("parallel", "parallel", "arbitrary")),
        name="nsa_attention",
    )(q, gates, cos_t, sin_t, kc, vc, c2s_t, ks, vs, kw, vw)


def _s5_operators(a_re, a_im, log_dt, b_re, b_im, c_re, c_im):
    hp = lax.Precision.HIGHEST
    n_g = a_re.shape[0]
    gpb = SSM_GROUPS_PER_BLOCK
    n_gb = n_g // gpb
    t_c, hh, pp = SSM_CHUNK, SSM_GROUP, SSM_STATE
    lam = lax.complex(a_re.astype(F32), a_im.astype(F32))
    dt = jnp.exp(log_dt.astype(F32))[:, None]
    a_bar = jnp.exp(lam * dt)
    b_bar = ((a_bar - 1.0) / lam)[:, :, None] * lax.complex(b_re.astype(F32), b_im.astype(F32))
    c_mat = lax.complex(c_re.astype(F32), c_im.astype(F32))
    steps = jnp.arange(t_c + 1, dtype=F32)[:, None, None]
    pw = jnp.exp((lam * dt)[None] * steps)

    def blockdiag(x, r, c):
        tile = jnp.tile(jnp.eye(c, dtype=BF16), (1, gpb))
        rows = jnp.arange(gpb * r)[:, None] // r
        cols = jnp.arange(gpb * c)[None, :] // c
        tiled = jnp.einsum('...c,cd->...d', x.astype(BF16), tile, preferred_element_type=BF16)
        return jnp.where(rows == cols, tiled, jnp.zeros((), BF16))

    k_lag = jnp.einsum('ghp,kgp,gpi->kgih', c_mat, pw[:t_c], b_bar, precision=hp).real
    w_intra = blockdiag(k_lag.reshape(t_c, n_gb, gpb * hh, hh), hh, hh)
    w_intra = w_intra.transpose(1, 2, 0, 3).reshape(n_gb, LANES, t_c * LANES)
    p_in = jnp.swapaxes(pw[t_c - 1::-1][:, :, :, None] * b_bar[None], -1, -2)
    p_in = p_in.reshape(t_c, n_gb, gpb * hh, pp)
    s_in = jnp.concatenate([blockdiag(p_in.real, hh, pp), blockdiag(p_in.imag, hh, pp)], axis=-1)
    s_in = s_in.transpose(1, 0, 2, 3).reshape(n_gb, t_c * LANES, 2 * gpb * pp)
    m_out = jnp.swapaxes(c_mat[None] * pw[1:, :, None, :], -1, -2)
    m_out = m_out.reshape(t_c, n_gb, gpb * pp, hh)
    s_out = jnp.concatenate([blockdiag(m_out.real, pp, hh), blockdiag(-m_out.imag, pp, hh)], axis=2)
    s_out = s_out.transpose(1, 2, 0, 3).reshape(n_gb, 2 * gpb * pp, t_c * LANES)
    k_steps = jnp.arange(SCAN_TABLE_ROWS, dtype=F32)[:, None, None] * float(t_c)
    a_pow = jnp.exp((lam * dt)[None] * k_steps).reshape(SCAN_TABLE_ROWS, n_gb, gpb * pp).swapaxes(0, 1)
    return s_in, w_intra, s_out, a_pow.real.astype(F32), a_pow.imag.astype(F32)


def _s5_kernel(u_ref, sin_ref, wi_ref, sout_ref, ar_ref, ai_ref, d_ref, o_ref, xin_ref, xprev_ref, wt_ref):
    t_c = SSM_CHUNK
    nc = u_ref.shape[0] // t_c
    half = ar_ref.shape[-1]

    @pl.when(pl.program_id(1) == 0)
    def _():
        wt_ref[...] = jnp.zeros(wt_ref.shape, wt_ref.dtype)
        for tau in range(t_c):
            wt_ref[tau * LANES:(tau + 1) * LANES, tau * LANES:] = wi_ref[0, :, :(t_c - tau) * LANES]

    def step_rows(ref, t):
        return ref.at[pl.ds(t, nc, stride=t_c), :]

    u_flat = jnp.concatenate([step_rows(u_ref, tau)[...].astype(BF16) for tau in range(t_c)], axis=1)
    xin = jnp.dot(u_flat, sin_ref[0], preferred_element_type=F32)
    x_r, x_i = xin[:, :half], xin[:, half:]
    r_in_grp = jnp.bitwise_and(lax.broadcasted_iota(jnp.int32, (nc, 1), 0), SCAN_GROUP - 1)
    d = 1
    while d < SCAN_GROUP:
        keep = jnp.where(r_in_grp >= d, 1.0, 0.0)
        s_r, s_i = pltpu.roll(x_r, d, 0) * keep, pltpu.roll(x_i, d, 0) * keep
        ad_r, ad_i = ar_ref[0, d:d + 1, :], ai_ref[0, d:d + 1, :]
        x_r, x_i = x_r + ad_r * s_r - ad_i * s_i, x_i + ad_r * s_i + ad_i * s_r
        d *= 2
    keep = jnp.where(r_in_grp >= 1, 1.0, 0.0)
    xin_ref[:, :half] = x_r
    xin_ref[:, half:] = x_i
    xprev_ref[:, :half] = pltpu.roll(x_r, 1, 0) * keep
    xprev_ref[:, half:] = pltpu.roll(x_i, 1, 0) * keep
    ak_r, ak_i = ar_ref[0, 0:SCAN_GROUP, :], ai_ref[0, 0:SCAN_GROUP, :]
    ag_r, ag_i = ar_ref[0, SCAN_GROUP:SCAN_GROUP + 1, :], ai_ref[0, SCAN_GROUP:SCAN_GROUP + 1, :]

    def group(g, carry):
        c_r, c_i = carry
        rows = pl.ds(pl.multiple_of(g * SCAN_GROUP, SCAN_GROUP), SCAN_GROUP)
        xprev_ref[rows, :half] = xprev_ref[rows, :half] + ak_r * c_r - ak_i * c_i
        xprev_ref[rows, half:] = xprev_ref[rows, half:] + ak_r * c_i + ak_i * c_r
        last = xin_ref[pl.ds(g * SCAN_GROUP + SCAN_GROUP - 1, 1), :]
        return (last[:, :half] + ag_r * c_r - ag_i * c_i, last[:, half:] + ag_r * c_i + ag_i * c_r)

    zero = jnp.zeros((1, half), F32)
    lax.fori_loop(0, nc // SCAN_GROUP, group, (zero, zero))
    y = (jnp.dot(u_flat, wt_ref[...], preferred_element_type=F32)
         + jnp.dot(xprev_ref[...].astype(BF16), sout_ref[0], preferred_element_type=F32))
    for t in range(t_c):
        step_rows(o_ref, t)[...] = _gelu_tanh(y[:, t * LANES:(t + 1) * LANES] + d_ref[0] * step_rows(u_ref, t)[...])


def _s5_scan(u, bsz, seq, ops, d_skip):
    d_ssm = u.shape[1]
    t_c = SSM_CHUNK
    s_in, w_intra, s_out, a_r, a_i = ops
    n_gb = d_ssm // LANES
    nc = seq // t_c
    st = s_in.shape[-1]
    return pl.pallas_call(
        _s5_kernel,
        out_shape=jax.ShapeDtypeStruct(u.shape, F32),
        grid=(n_gb, bsz),
        in_specs=[pl.BlockSpec((seq, LANES), lambda g, b: (b, g)),
                  pl.BlockSpec((1,) + s_in.shape[1:], lambda g, b: (g, 0, 0)),
                  pl.BlockSpec((1,) + w_intra.shape[1:], lambda g, b: (g, 0, 0)),
                  pl.BlockSpec((1,) + s_out.shape[1:], lambda g, b: (g, 0, 0)),
                  pl.BlockSpec((1, SCAN_TABLE_ROWS, st // 2), lambda g, b: (g, 0, 0)),
                  pl.BlockSpec((1, SCAN_TABLE_ROWS, st // 2), lambda g, b: (g, 0, 0)),
                  pl.BlockSpec((1, 1, LANES), lambda g, b: (g, 0, 0))],
        out_specs=pl.BlockSpec((seq, LANES), lambda g, b: (b, g)),
        scratch_shapes=[pltpu.VMEM((nc, st), F32), pltpu.VMEM((nc, st), F32),
                        pltpu.VMEM((t_c * LANES, t_c * LANES), BF16)],
        compiler_params=_cparams(("arbitrary", "arbitrary")),
        name="s5_scan",
    )(u, s_in, w_intra, s_out, a_r, a_i, d_skip.reshape(n_gb, 1, LANES))


def _mixnorm_kernel(a_ref, s_ref, ga_ref, gs_ref, o_ref):
    da = a_ref.shape[-1]
    a = a_ref[...]
    s = s_ref[...]
    o_ref[:, :da] = (a * lax.rsqrt(jnp.mean(a * a, axis=-1, keepdims=True) + EPS) * ga_ref[...]).astype(BF16)
    o_ref[:, da:] = (s * lax.rsqrt(jnp.mean(s * s, axis=-1, keepdims=True) + EPS) * gs_ref[...]).astype(BF16)


def _mixnorm(attn, ssm, g_attn, g_ssm, tm=256):
    m, da = attn.shape
    ds = ssm.shape[1]
    return pl.pallas_call(
        _mixnorm_kernel,
        out_shape=jax.ShapeDtypeStruct((m, da + ds), BF16),
        grid=(m // tm,),
        in_specs=[pl.BlockSpec((tm, da), lambda i: (i, 0)), pl.BlockSpec((tm, ds), lambda i: (i, 0)),
                  pl.BlockSpec((1, da), lambda i: (0, 0)), pl.BlockSpec((1, ds), lambda i: (0, 0))],
        out_specs=pl.BlockSpec((tm, da + ds), lambda i: (i, 0)),
        compiler_params=_cparams(("parallel",)),
        name="mixnorm",
    )(attn, ssm, g_attn.reshape(1, da), g_ssm.reshape(1, ds))


def _split_bf16(x):
    hi = x.astype(BF16)
    return hi, (x - hi.astype(F32)).astype(BF16)


def _lane_max(x):
    return jnp.max(x, axis=-1, keepdims=True)


def _first_lane_of(cond_val, lane_f):
    return jnp.min(jnp.where(cond_val > 0.0, lane_f, float(LANES)), axis=-1, keepdims=True)


def _ffn_norm_router_kernel(h_ref, g_ref, whi_ref, wlo_ref, b_ref, hn_ref, route_ref):
    x = h_ref[...]
    hn = x * lax.rsqrt(jnp.mean(x * x, axis=-1, keepdims=True) + EPS) * g_ref[...]
    hn_ref[...] = _pack_bf16_pairs(hn)
    hi, lo = _split_bf16(hn)
    logits = (jnp.dot(hi, whi_ref[...], preferred_element_type=F32)
              + jnp.dot(lo, whi_ref[...], preferred_element_type=F32)
              + jnp.dot(hi, wlo_ref[...], preferred_element_type=F32)) + b_ref[...]
    lane = lax.broadcasted_iota(jnp.int32, logits.shape, 1)
    lane_f = lane.astype(F32)
    in_c = jnp.where(lane < N_EXPERT_GROUPS, 1.0, 0.0)
    c_l = jnp.where(in_c > 0.0, logits, NEG_INF)
    c_e = jnp.exp(c_l - _lane_max(c_l)) * in_c
    p_c = c_e / jnp.sum(c_e, axis=-1, keepdims=True)
    w_grp = _lane_max(p_c)
    grp = _first_lane_of(jnp.where(p_c == w_grp, in_c, 0.0), lane_f)
    lo_lane = float(N_EXPERT_GROUPS) + grp * float(EXPERTS_PER_GROUP)
    in_f = jnp.where(lane_f >= lo_lane, jnp.where(lane_f < lo_lane + float(EXPERTS_PER_GROUP), 1.0, 0.0), 0.0)
    f_l = jnp.where(in_f > 0.0, logits, NEG_INF)
    f_e = jnp.exp(f_l - _lane_max(f_l)) * in_f
    p_f = f_e / jnp.sum(f_e, axis=-1, keepdims=True)
    p_f = jnp.where(in_f > 0.0, p_f, -1.0)
    p1 = _lane_max(p_f)
    i1 = _first_lane_of(jnp.where(p_f == p1, in_f, 0.0), lane_f)
    rest = jnp.where(lane_f == i1, -1.0, p_f)
    p2 = _lane_max(rest)
    i2 = _first_lane_of(jnp.where(rest == p2, in_f, 0.0) * jnp.where(lane_f == i1, 0.0, 1.0), lane_f)
    denom = p1 + p2
    vals = (w_grp * p1 / denom, w_grp * p2 / denom,
            i1 - float(N_EXPERT_GROUPS), i2 - float(N_EXPERT_GROUPS))
    out = jnp.zeros(logits.shape, F32)
    for k, v in enumerate(vals):
        out = jnp.where(lane == k, v, out)
    route_ref[...] = out


def _ffn_norm_router(h, g, w_router, b_router, tm=256):
    m, d = h.shape
    n = w_router.shape[1]
    w_hi, w_lo = _split_bf16(w_router)
    return pl.pallas_call(
        _ffn_norm_router_kernel,
        out_shape=(jax.ShapeDtypeStruct((m, d // 2), jnp.uint32), jax.ShapeDtypeStruct((m, n), F32)),
        grid=(m // tm,),
        in_specs=[pl.BlockSpec((tm, d), lambda i: (i, 0)), pl.BlockSpec((1, d), lambda i: (0, 0)),
                  pl.BlockSpec((d, n), lambda i: (0, 0)), pl.BlockSpec((d, n), lambda i: (0, 0)),
                  pl.BlockSpec((1, n), lambda i: (0, 0))],
        out_specs=(pl.BlockSpec((tm, d // 2), lambda i: (i, 0)), pl.BlockSpec((tm, n), lambda i: (i, 0))),
        compiler_params=_cparams(("parallel",)),
        name="ffn_norm_router",
    )(h, g.reshape(1, d), w_hi, w_lo, b_router.reshape(1, n))


def _dispatch_plan(expert, tm):
    n_tok = expert.shape[0]
    n_assign = n_tok * TOP_K
    n_blocks = -(-(n_assign + N_EXPERTS * (MOE_BLOCK - 1)) // MOE_BLOCK)
    n_rows = n_blocks * MOE_BLOCK
    flat = expert.reshape(n_assign)
    ids = jnp.arange(n_assign, dtype=jnp.int32)
    _, order = lax.sort_key_val(flat, ids)
    _, inv = lax.sort_key_val(order, ids)
    e_ids = jnp.arange(N_EXPERTS, dtype=jnp.int32)
    counts = jnp.sum((flat[:, None] == e_ids[None, :]).astype(jnp.int32), axis=0)
    start = jnp.cumsum(counts) - counts
    padded = (counts + MOE_BLOCK - 1) // MOE_BLOCK * MOE_BLOCK
    pad_end = jnp.cumsum(padded)
    pad_start = pad_end - padded
    rows = jnp.arange(n_rows, dtype=jnp.int32)
    blk_row0 = jnp.arange(n_blocks, dtype=jnp.int32) * MOE_BLOCK
    blk_e = jnp.minimum(jnp.sum((pad_end[None, :] <= blk_row0[:, None]).astype(jnp.int32), axis=1), N_EXPERTS - 1)
    row_e = jnp.repeat(blk_e, MOE_BLOCK)
    slot = rows - pad_start[row_e]
    valid = (slot < counts[row_e]) & (rows < pad_end[-1])
    src = jnp.clip(start[row_e] + slot, 0, n_assign - 1)
    buf_tok = jnp.where(valid, order[src] // TOP_K, 0).astype(jnp.int32)
    dest = (pad_start[flat] + inv - start[flat]).astype(jnp.int32)
    dest = dest.reshape(n_tok // tm, tm, TOP_K).transpose(0, 2, 1).reshape(-1)
    plan = jnp.concatenate([pad_start // MOE_BLOCK, padded // MOE_BLOCK,
                            pad_end[-1:] // MOE_BLOCK]).astype(jnp.int32)
    return plan, buf_tok, dest, n_blocks


def _zero_fill_spare(plan_ref, zero_vmem, out_hbm, col0, width, sem, n_blocks):
    n_used = plan_ref[2 * N_EXPERTS]

    def cp(blk):
        return pltpu.make_async_copy(
            zero_vmem, out_hbm.at[pl.ds(pl.multiple_of(blk * MOE_BLOCK, MOE_BLOCK), MOE_BLOCK), pl.ds(col0, width)],
            sem)

    def start(blk, c):
        cp(blk).start(priority=SIDE_DMA_PRIORITY)
        return c

    def wait(blk, c):
        cp(blk).wait()
        return c

    lax.fori_loop(n_used, n_blocks, start, 0)
    lax.fori_loop(n_used, n_blocks, wait, 0)


def _pack_bf16_pairs(x):
    w = x.shape[1] // 2
    hi = lax.bitcast_convert_type(x[:, :w].astype(BF16).astype(F32), jnp.uint32)
    lo = lax.bitcast_convert_type(x[:, w:].astype(BF16).astype(F32), jnp.uint32)
    return hi | (lo >> 16)


def _unpack_bf16_pairs(p):
    first = lax.bitcast_convert_type(p & jnp.uint32(0xFFFF0000), F32)
    second = lax.bitcast_convert_type(p << 16, F32)
    return jnp.concatenate([first, second], axis=1).astype(BF16)


def _moe_up_kernel(plan_ref, tok_ref, x_hbm, *rest, n_blocks):
    w_refs = rest[:2 * W_DMA_SPLIT]
    act_hbm, wg_b, wu_b, xs, gbuf, obuf, afull, gsem, osem, fsem = rest[2 * W_DMA_SPLIT:]
    wg_refs, wu_refs = w_refs[:W_DMA_SPLIT], w_refs[W_DMA_SPLIT:]
    e = pl.program_id(0)
    c = pl.program_id(1)
    n_c = pl.num_programs(1)
    tn = wg_b.shape[1]
    col0 = pl.multiple_of(c * tn, tn)
    b0 = plan_ref[e]
    nb = plan_ref[N_EXPERTS + e]
    n_res = jnp.minimum(nb, MOE_X_CAP)

    def row_copy(blk, r, slot):
        tok = tok_ref[blk * MOE_BLOCK + r]
        return pltpu.make_async_copy(x_hbm.at[pl.ds(tok, 1), :], gbuf.at[slot, pl.ds(r, 1), :], gsem.at[slot])

    def gather_start(blk, slot):
        lax.fori_loop(0, MOE_BLOCK, lambda r, cr: (row_copy(blk, r, slot).start(priority=SIDE_DMA_PRIORITY), cr)[1], 0, unroll=DMA_UNROLL)

    def gather_wait(blk, slot):
        lax.fori_loop(0, MOE_BLOCK, lambda r, cr: (row_copy(blk, r, slot).wait(), cr)[1], 0, unroll=DMA_UNROLL)

    def gather_expert(first_blk, n_blk):
        lax.fori_loop(0, jnp.minimum(n_blk, MOE_X_CAP), lambda k, cr: (gather_start(first_blk + k, k), cr)[1], 0)

    def gather_now(blk):
        gather_start(blk, MOE_X_CAP)
        gather_wait(blk, MOE_X_CAP)
        return _unpack_bf16_pairs(gbuf[MOE_X_CAP])

    @pl.when(jnp.logical_and(c == 1, e + 1 < N_EXPERTS))
    def _():
        e_next = jnp.minimum(e + 1, N_EXPERTS - 1)
        gather_expert(plan_ref[e_next], plan_ref[N_EXPERTS + e_next])

    def rows_of(blk):
        return pl.ds(pl.multiple_of(blk * MOE_BLOCK, MOE_BLOCK), MOE_BLOCK)

    def out_copy(blk, slot):
        return pltpu.make_async_copy(obuf.at[slot], act_hbm.at[rows_of(blk), pl.ds(col0, tn)], osem.at[slot])

    def full_copy(j):
        return pltpu.make_async_copy(afull.at[j], act_hbm.at[rows_of(b0 + j), :], fsem)

    def act_of(x):
        g = jnp.dot(x, wg_b[...], preferred_element_type=F32)
        u = jnp.dot(x, wu_b[...], preferred_element_type=F32)
        return (g * _sigmoid(g) * u).astype(BF16)

    def emit_resident(x, j):
        act = act_of(x)
        for cc in range(act_hbm.shape[1] // tn):
            @pl.when(c == cc)
            def _():
                afull[j, :, cc * tn:(cc + 1) * tn] = act

    def emit_streamed(x, j):
        i = j - n_res
        slot = lax.rem(i, 2)
        act = act_of(x)

        @pl.when(i >= 2)
        def _():
            out_copy(b0 + j - 2, slot).wait()

        obuf[slot] = act
        out_copy(b0 + j, slot).start(priority=SIDE_DMA_PRIORITY)

    @pl.when(nb > 0)
    def _():
        rows = wg_b.shape[0] // W_DMA_SPLIT
        for k in range(W_DMA_SPLIT):
            wg_b[k * rows:(k + 1) * rows, :] = wg_refs[k][0].astype(BF16)
            wu_b[k * rows:(k + 1) * rows, :] = wu_refs[k][0].astype(BF16)

    @pl.when(c == 0)
    def _():
        @pl.when(e == 0)
        def _():
            gather_expert(b0, nb)

        def landed(j, carry):
            gather_wait(b0 + j, j)
            x = _unpack_bf16_pairs(gbuf[j])
            xs[j] = x
            emit_resident(x, j)
            return carry

        def overflow(j, carry):
            emit_streamed(gather_now(b0 + j), j)
            return carry

        lax.fori_loop(0, n_res, landed, 0)
        lax.fori_loop(n_res, nb, overflow, 0)

    @pl.when(c > 0)
    def _():
        def resident(j, carry):
            emit_resident(xs[j], j)
            return carry

        def overflow(j, carry):
            emit_streamed(gather_now(b0 + j), j)
            return carry

        lax.fori_loop(0, n_res, resident, 0)
        lax.fori_loop(n_res, nb, overflow, 0)

    n_str = nb - n_res

    @pl.when(n_str >= 2)
    def _():
        out_copy(b0 + nb - 2, lax.rem(n_str, 2)).wait()

    @pl.when(n_str >= 1)
    def _():
        out_copy(b0 + nb - 1, lax.rem(n_str + 1, 2)).wait()

    @pl.when(c == n_c - 1)
    def _():
        lax.fori_loop(0, n_res, lambda j, cr: (full_copy(j).start(priority=SIDE_DMA_PRIORITY), cr)[1], 0)
        lax.fori_loop(0, n_res, lambda j, cr: (full_copy(j).wait(), cr)[1], 0)

    @pl.when(jnp.logical_and(e == N_EXPERTS - 1, c == n_c - 1))
    def _():
        afull[0] = jnp.zeros(afull.shape[1:], afull.dtype)
        _zero_fill_spare(plan_ref, afull.at[0], act_hbm, 0, act_hbm.shape[1], fsem, n_blocks)


def _moe_up(hn_packed, w_gate, w_up, plan, buf_tok, n_blocks, n_chunks=4):
    d = w_gate.shape[-2]
    d_e = w_gate.shape[-1]
    tn = d_e // n_chunks
    w_specs = [pl.BlockSpec((1, d // W_DMA_SPLIT, tn), functools.partial(lambda e, c, plan, tok, k: (e, k, c), k=k))
               for k in range(W_DMA_SPLIT)]
    return pl.pallas_call(
        functools.partial(_moe_up_kernel, n_blocks=n_blocks),
        out_shape=jax.ShapeDtypeStruct((n_blocks * MOE_BLOCK, d_e), BF16),
        grid_spec=pltpu.PrefetchScalarGridSpec(
            num_scalar_prefetch=2,
            grid=(N_EXPERTS, n_chunks),
            in_specs=[pl.BlockSpec(memory_space=pl.ANY)] + w_specs + w_specs,
            out_specs=pl.BlockSpec(memory_space=pl.ANY),
            scratch_shapes=[pltpu.VMEM((d, tn), BF16), pltpu.VMEM((d, tn), BF16),
                            pltpu.VMEM((MOE_X_CAP, MOE_BLOCK, d), BF16),
                            pltpu.VMEM((GATHER_SLOTS, MOE_BLOCK, d // 2), jnp.uint32),
                            pltpu.VMEM((2, MOE_BLOCK, tn), BF16),
                            pltpu.VMEM((MOE_X_CAP, MOE_BLOCK, d_e), BF16),
                            pltpu.SemaphoreType.DMA((GATHER_SLOTS,)), pltpu.SemaphoreType.DMA((2,)),
                            pltpu.SemaphoreType.DMA(())]),
        compiler_params=_cparams(("arbitrary", "arbitrary")),
        name="moe_up",
    )(plan, buf_tok, hn_packed, *([w_gate] * W_DMA_SPLIT), *([w_up] * W_DMA_SPLIT))


def _moe_down_kernel(plan_ref, act_hbm, *rest, n_blocks):
    wd_refs, (y_hbm, wd_b, abuf, obuf, asem, osem) = rest[:W_DMA_SPLIT], rest[W_DMA_SPLIT:]
    e = pl.program_id(0)
    b0 = plan_ref[e]
    nb = plan_ref[N_EXPERTS + e]
    n_res = jnp.minimum(nb, DOWN_ACT_CAP)
    par = lax.rem(e, 2)

    def rows_of(blk):
        return pl.ds(pl.multiple_of(blk * MOE_BLOCK, MOE_BLOCK), MOE_BLOCK)

    def in_copy(blk, half, slot):
        return pltpu.make_async_copy(act_hbm.at[rows_of(blk), :], abuf.at[half, slot], asem.at[half, slot])

    def out_copy(blk, slot):
        return pltpu.make_async_copy(obuf.at[slot], y_hbm.at[rows_of(blk), :], osem.at[slot])

    def request_expert(first_blk, n_blk, half):
        lax.fori_loop(0, jnp.minimum(n_blk, DOWN_ACT_CAP),
                      lambda k, cr: (in_copy(first_blk + k, half, k).start(priority=SIDE_DMA_PRIORITY), cr)[1], 0)

    @pl.when(e == 0)
    def _():
        request_expert(b0, nb, 0)

    @pl.when(e + 1 < N_EXPERTS)
    def _():
        e_next = jnp.minimum(e + 1, N_EXPERTS - 1)
        request_expert(plan_ref[e_next], plan_ref[N_EXPERTS + e_next], 1 - par)

    @pl.when(nb > 0)
    def _():
        rows = wd_b.shape[0] // W_DMA_SPLIT
        for k in range(W_DMA_SPLIT):
            wd_b[k * rows:(k + 1) * rows, :] = wd_refs[k][0].astype(BF16)

    def emit(a, j):
        slot = lax.rem(j, DOWN_OUT_SLOTS)
        y = jnp.dot(a, wd_b[...], preferred_element_type=F32)

        @pl.when(j >= DOWN_OUT_SLOTS)
        def _():
            out_copy(b0 + j - DOWN_OUT_SLOTS, slot).wait()

        obuf[slot] = _pack_bf16_pairs(y)
        out_copy(b0 + j, slot).start(priority=SIDE_DMA_PRIORITY)

    def landed(j, carry):
        in_copy(b0 + j, par, j).wait()
        emit(abuf[par, j], j)
        return carry

    def overflow(j, carry):
        in_copy(b0 + j, par, 0).start(priority=SIDE_DMA_PRIORITY)
        in_copy(b0 + j, par, 0).wait()
        emit(abuf[par, 0], j)
        return carry

    lax.fori_loop(0, n_res, landed, 0)
    lax.fori_loop(n_res, nb, overflow, 0)
    lax.fori_loop(jnp.maximum(nb - DOWN_OUT_SLOTS, 0), nb,
                  lambda j, cr: (out_copy(b0 + j, lax.rem(j, DOWN_OUT_SLOTS)).wait(), cr)[1], 0)

    @pl.when(e == N_EXPERTS - 1)
    def _():
        obuf[0] = jnp.zeros(obuf.shape[1:], obuf.dtype)
        _zero_fill_spare(plan_ref, obuf.at[0], y_hbm, 0, y_hbm.shape[1], osem.at[0], n_blocks)


def _moe_down(act, w_down, plan, n_blocks):
    n_rows, d_e = act.shape
    d = w_down.shape[-1]
    return pl.pallas_call(
        functools.partial(_moe_down_kernel, n_blocks=n_blocks),
        out_shape=jax.ShapeDtypeStruct((n_rows, d // 2), jnp.uint32),
        grid_spec=pltpu.PrefetchScalarGridSpec(
            num_scalar_prefetch=1,
            grid=(N_EXPERTS,),
            in_specs=[pl.BlockSpec(memory_space=pl.ANY)] + [
                pl.BlockSpec((1, d_e // W_DMA_SPLIT, d), functools.partial(lambda e, plan, k: (e, k, 0), k=k))
                for k in range(W_DMA_SPLIT)],
            out_specs=pl.BlockSpec(memory_space=pl.ANY),
            scratch_shapes=[pltpu.VMEM((d_e, d), BF16),
                            pltpu.VMEM((2, DOWN_ACT_CAP, MOE_BLOCK, d_e), BF16),
                            pltpu.VMEM((DOWN_OUT_SLOTS, MOE_BLOCK, d // 2), jnp.uint32),
                            pltpu.SemaphoreType.DMA((2, DOWN_ACT_CAP)), pltpu.SemaphoreType.DMA((DOWN_OUT_SLOTS,))]),
        compiler_params=_cparams(("arbitrary",)),
        name="moe_down",
    )(plan, act, *([w_down] * W_DMA_SPLIT))


def _combine_kernel(dest_ref, h_ref, route_ref, gn_ref, y_hbm, o_ref, rows_ref, sem):
    tm = h_ref.shape[0]
    n_rows = tm * TOP_K
    i = pl.program_id(0)
    n = pl.num_programs(0)

    def row_copy(step, r, slot):
        src = dest_ref[step * n_rows + r]
        return pltpu.make_async_copy(y_hbm.at[pl.ds(src, 1), :], rows_ref.at[slot, pl.ds(r, 1), :], sem.at[slot])

    def gather_start(step, slot):
        lax.fori_loop(0, n_rows, lambda r, cr: (row_copy(step, r, slot).start(priority=SIDE_DMA_PRIORITY), cr)[1], 0, unroll=DMA_UNROLL)

    def gather_wait(step, slot):
        lax.fori_loop(0, n_rows, lambda r, cr: (row_copy(step, r, slot).wait(), cr)[1], 0, unroll=DMA_UNROLL)

    slot = lax.rem(i, 2)

    @pl.when(i == 0)
    def _():
        gather_start(0, 0)

    @pl.when(i + 1 < n)
    def _():
        gather_start(i + 1, 1 - slot)

    gather_wait(i, slot)
    route = route_ref[...]
    rows = rows_ref[slot]
    g0, g1 = route[:, 0:1], route[:, 1:2]
    r0, r1 = rows[0:tm, :], rows[tm:2 * tm, :]
    half = rows.shape[1]

    def first(p):
        return lax.bitcast_convert_type(p & jnp.uint32(0xFFFF0000), F32)

    def second(p):
        return lax.bitcast_convert_type(p << 16, F32)

    h = jnp.concatenate([h_ref[:, :half] + g0 * first(r0) + g1 * first(r1),
                         h_ref[:, half:] + g0 * second(r0) + g1 * second(r1)], axis=1)
    o_ref[...] = h * lax.rsqrt(jnp.mean(h * h, axis=-1, keepdims=True) + EPS) * gn_ref[...]


def _combine_final(h, route, dest, y_buf, g_final, tm=128):
    m, d = h.shape
    return pl.pallas_call(
        _combine_kernel,
        out_shape=jax.ShapeDtypeStruct((m, d), F32),
        grid_spec=pltpu.PrefetchScalarGridSpec(
            num_scalar_prefetch=1,
            grid=(m // tm,),
            in_specs=[pl.BlockSpec((tm, d), lambda i, dst: (i, 0)),
                      pl.BlockSpec((tm, LANES), lambda i, dst: (i, 0)),
                      pl.BlockSpec((1, d), lambda i, dst: (0, 0)),
                      pl.BlockSpec(memory_space=pl.ANY)],
            out_specs=pl.BlockSpec((tm, d), lambda i, dst: (i, 0)),
            scratch_shapes=[pltpu.VMEM((2, TOP_K * tm, d // 2), jnp.uint32), pltpu.SemaphoreType.DMA((2,))]),
        compiler_params=_cparams(("arbitrary",)),
        name="moe_combine_final",
    )(dest, h, route, g_final.reshape(1, d), y_buf)


def _gate_column_order():
    cols = []
    for hk in range(N_KV_HEADS):
        blk = [-1] * LANES
        for br in range(N_BRANCHES):
            for g in range(GQA_GROUP):
                blk[br * GQA_GROUP + g] = (hk * GQA_GROUP + g) * N_BRANCHES + br
        cols.extend(blk)
    return cols


def kernel(x, norm_mix, w_in, cmp_pe_k, cmp_w1_k, cmp_b1_k, cmp_w2_k, cmp_b2_k, cmp_pe_v, cmp_w1_v, cmp_b1_v, cmp_w2_v, cmp_b2_v, ssm_a_re, ssm_a_im, ssm_log_dt, ssm_b_re, ssm_b_im, ssm_c_re, ssm_c_im, ssm_d, ssm_w_glu, ssm_b_glu, norm_attn_out, norm_ssm_out, w_out, norm_ffn, router_w_coarse, router_b_coarse, router_w_fine, router_b_fine, w_gate, w_up, w_down, norm_final):
    bsz, seq, d_model = x.shape
    m = bsz * seq
    depth = w_in.shape[0]
    assert depth == 1, "the combine kernel fuses the final norm; stacked layers are not supported"
    layer = 0
    q_w = N_HEADS * HEAD_DIM
    kv_w = N_KV_HEADS * HEAD_DIM
    gate_w = N_HEADS * N_BRANCHES
    cut1, cut2 = q_w, q_w + 6 * kv_w
    cut3 = cut2 + gate_w
    h = x.astype(F32).reshape(m, d_model)
    gate_cols = jnp.array([c if c >= 0 else 0 for c in _gate_column_order()], jnp.int32)
    gate_keep = jnp.array([1.0 if c >= 0 else 0.0 for c in _gate_column_order()], F32)

    w_in_b = w_in[layer].astype(BF16)
    hn = _rmsnorm(h, norm_mix[layer], BF16)
    q = _matmul(hn, w_in_b[:, :cut1], name="proj_q")
    kv = _matmul(hn, w_in_b[:, cut1:cut2], name="proj_kv")
    w_gate_cols = (w_in[layer][:, cut2:cut3][:, gate_cols] * gate_keep[None, :]).astype(BF16)
    gates = _matmul(hn, w_gate_cols, tn=512, name="proj_gate")
    u = _matmul(hn, w_in_b[:, cut3:], name="proj_u")

    stack2 = lambda a, b: jnp.stack([a[layer], b[layer]])
    cmp_kv = _compress(kv, bsz, seq, stack2(cmp_pe_k, cmp_pe_v),
                       stack2(cmp_w1_k, cmp_w1_v).astype(BF16), stack2(cmp_b1_k, cmp_b1_v),
                       stack2(cmp_w2_k, cmp_w2_v).astype(BF16), stack2(cmp_b2_k, cmp_b2_v))
    ks, vs, kw, vw = _kv_prep(kv, seq)
    attn = _nsa_attention(q, gates, cmp_kv[0], cmp_kv[1], ks, vs, kw, vw, bsz, seq)

    ops = _s5_operators(ssm_a_re[layer], ssm_a_im[layer], ssm_log_dt[layer], ssm_b_re[layer],
                        ssm_b_im[layer], ssm_c_re[layer], ssm_c_im[layer])
    y = _s5_scan(u, bsz, seq, ops, ssm_d[layer].astype(F32))
    ssm = _glu_matmul(y, ssm_w_glu[layer].astype(BF16), ssm_b_glu[layer].astype(F32))

    mixed = _mixnorm(attn, ssm, norm_attn_out[layer], norm_ssm_out[layer])
    h = _matmul(mixed, w_out[layer].astype(BF16), residual=h, name="out_proj")

    n_r = N_EXPERT_GROUPS + N_EXPERTS
    w_router = jnp.pad(jnp.concatenate([router_w_coarse[layer], router_w_fine[layer]], axis=1),
                       ((0, 0), (0, LANES - n_r))).astype(F32)
    b_router = jnp.pad(jnp.concatenate([router_b_coarse[layer], router_b_fine[layer]]), (0, LANES - n_r))
    hn2, route = _ffn_norm_router(h, norm_ffn[layer], w_router, b_router.astype(F32))
    expert = route[:, 2:2 + TOP_K].astype(jnp.int32)
    tm_c = 128
    plan, buf_tok, dest, n_blocks = _dispatch_plan(expert, tm_c)
    act = _moe_up(hn2, w_gate[layer], w_up[layer], plan, buf_tok, n_blocks)
    y_buf = _moe_down(act, w_down[layer], plan, n_blocks)
    out = _combine_final(h, route, dest, y_buf, norm_final, tm=tm_c)
    return out.reshape(bsz, seq, d_model).astype(x.dtype)
```

```python
import functools
import math

import jax
import jax.numpy as jnp
from jax import lax
from jax.experimental import pallas as pl
from jax.experimental.pallas import tpu as pltpu

F32 = jnp.float32
BF16 = jnp.bfloat16

HEAD_DIM = 128
N_HEADS = 16
N_KV_HEADS = 4
GQA_GROUP = N_HEADS // N_KV_HEADS
ROT_DIM = HEAD_DIM // 4
ROPE_THETA = 500000.0
CMP_STRIDE = 16
CMP_BLOCK = 32
SLC_BLOCK = 64
N_SELECT = 16
WINDOW = 512
Q_TILE = 256
N_BRANCHES = 3
FORCE_BONUS = 1.0e4
NEG_INF = -1.0e30
MASK_BIG = 1.0e30
SSM_GROUP = 16
SSM_STATE = 64
SSM_CHUNK = 16
SSM_GROUPS_PER_BLOCK = 128 // SSM_GROUP
N_EXPERT_GROUPS = 8
EXPERTS_PER_GROUP = 8
N_EXPERTS = N_EXPERT_GROUPS * EXPERTS_PER_GROUP
TOP_K = 2
MOE_BLOCK = 128
EPS = 1.0e-6
LOG2_E = 1.4426950408889634

V7X_VMEM_LIMIT = 56 * 1024 * 1024
LANES = 128
SCAN_GROUP = 8
SCAN_TABLE_ROWS = 16
DMA_UNROLL = 8
SIDE_DMA_PRIORITY = 1
MOE_X_CAP = 12
W_DMA_SPLIT = 4
GATHER_SLOTS = MOE_X_CAP + 1
DOWN_ACT_CAP = 8
DOWN_OUT_SLOTS = 4


def _cparams(sem, vmem=V7X_VMEM_LIMIT):
    return pltpu.CompilerParams(dimension_semantics=sem, vmem_limit_bytes=vmem)


def _gelu_tanh(x):
    return 0.5 * x * (1.0 + jnp.tanh(math.sqrt(2.0 / math.pi) * (x + 0.044715 * (x * x * x))))


def _sigmoid(x):
    return 1.0 / (1.0 + jnp.exp(-x))


def _rmsnorm_kernel(x_ref, g_ref, o_ref):
    x = x_ref[...]
    ms = jnp.mean(x * x, axis=-1, keepdims=True)
    o_ref[...] = (x * lax.rsqrt(ms + EPS) * g_ref[...]).astype(o_ref.dtype)


def _rmsnorm(x, g, out_dtype, tm=256):
    m, d = x.shape
    return pl.pallas_call(
        _rmsnorm_kernel,
        out_shape=jax.ShapeDtypeStruct((m, d), out_dtype),
        grid=(m // tm,),
        in_specs=[pl.BlockSpec((tm, d), lambda i: (i, 0)),
                  pl.BlockSpec((1, d), lambda i: (0, 0))],
        out_specs=pl.BlockSpec((tm, d), lambda i: (i, 0)),
        compiler_params=_cparams(("parallel",)),
        name="rmsnorm",
    )(x, g.reshape(1, d))


def _mm_kernel(a_ref, w_ref, o_ref):
    o_ref[...] = jnp.dot(a_ref[...], w_ref[...], preferred_element_type=F32)


def _mm_glu_kernel(y_ref, w_ref, b_ref, yg_ref, o_ref):
    z = jnp.dot(y_ref[...].astype(BF16), w_ref[...], preferred_element_type=F32) + b_ref[...]
    o_ref[...] = yg_ref[...] * _sigmoid(z)


def _mm_residual_kernel(a_ref, w_ref, r_ref, o_ref):
    o_ref[...] = r_ref[...] + jnp.dot(a_ref[...], w_ref[...], preferred_element_type=F32)


def _matmul(a, w, *, residual=None, tm=512, tn=1024, name="matmul"):
    m, k = a.shape
    n = w.shape[1]
    tn = min(tn, n)
    assert m % tm == 0 and n % tn == 0
    a_spec = pl.BlockSpec((tm, k), lambda j, i: (i, 0))
    w_spec = pl.BlockSpec((k, tn), lambda j, i: (0, j))
    o_spec = pl.BlockSpec((tm, tn), lambda j, i: (i, j))
    if residual is not None:
        kern, ins, specs = _mm_residual_kernel, (a, w, residual), [a_spec, w_spec, o_spec]
    else:
        kern, ins, specs = _mm_kernel, (a, w), [a_spec, w_spec]
    return pl.pallas_call(
        kern,
        out_shape=jax.ShapeDtypeStruct((m, n), F32),
        grid=(n // tn, m // tm),
        in_specs=specs,
        out_specs=o_spec,
        compiler_params=_cparams(("parallel", "parallel")),
        name=name,
    )(*ins)


def _glu_matmul(y, w, bias, tm=512, tn=1024):
    m, k = y.shape
    n = w.shape[1]
    return pl.pallas_call(
        _mm_glu_kernel,
        out_shape=jax.ShapeDtypeStruct((m, n), F32),
        grid=(n // tn, m // tm),
        in_specs=[pl.BlockSpec((tm, k), lambda j, i: (i, 0)),
                  pl.BlockSpec((k, tn), lambda j, i: (0, j)),
                  pl.BlockSpec((1, tn), lambda j, i: (0, j)),
                  pl.BlockSpec((tm, tn), lambda j, i: (i, j))],
        out_specs=pl.BlockSpec((tm, tn), lambda j, i: (i, j)),
        compiler_params=_cparams(("parallel", "parallel")),
        name="ssm_glu",
    )(y, w, bias.reshape(1, n), y)


def _rope_tables(seq, reps):
    half = ROT_DIM // 2
    inv_freq = ROPE_THETA ** (-jnp.arange(half, dtype=F32) / half)
    ang = jnp.arange(seq).astype(F32)[:, None] * inv_freq[None, :]
    cos, sin = jnp.cos(ang), jnp.sin(ang)
    ones = jnp.ones((seq, HEAD_DIM - ROT_DIM), F32)
    cos_t = jnp.concatenate([cos, cos, ones], axis=-1)
    sin_t = jnp.concatenate([-sin, sin, 0.0 * ones], axis=-1)
    return jnp.tile(cos_t, (1, reps)), jnp.tile(sin_t, (1, reps))


def _rope(x, cos_t, sin_t):
    half = ROT_DIM // 2
    n = x.shape[-1]
    lane = jnp.bitwise_and(lax.broadcasted_iota(jnp.int32, x.shape, x.ndim - 1), HEAD_DIM - 1)
    partner = jnp.where(lane < half, pltpu.roll(x, n - half, x.ndim - 1), pltpu.roll(x, half, x.ndim - 1))
    return x * cos_t + partner * sin_t


def _kv_prep_kernel(slc_ref, win_ref, cos_ref, sin_ref, ks_ref, vs_ref, kw_ref, vw_ref, *, tiles_per_seq):
    tm, w = vs_ref.shape
    cos_t, sin_t = cos_ref[...], sin_ref[...]
    slc = slc_ref[...]
    win = win_ref[...]
    k_slc = _rope(slc[:, :w], cos_t, sin_t).astype(BF16)
    pos = lax.rem(pl.program_id(0), tiles_per_seq) * tm + lax.broadcasted_iota(jnp.int32, (tm, HEAD_DIM), 0)
    lane = lax.broadcasted_iota(jnp.int32, (tm, HEAD_DIM), 1)
    own_blk = jnp.where(lane == lax.shift_right_logical(pos, int(math.log2(SLC_BLOCK))), MASK_BIG, 0.0).astype(BF16)
    for hk in range(N_KV_HEADS):
        ks_ref[:, 2 * hk * HEAD_DIM:(2 * hk + 1) * HEAD_DIM] = k_slc[:, hk * HEAD_DIM:(hk + 1) * HEAD_DIM]
        ks_ref[:, (2 * hk + 1) * HEAD_DIM:(2 * hk + 2) * HEAD_DIM] = own_blk
    vs_ref[...] = slc[:, w:].astype(BF16)
    kw_ref[...] = _rope(win[:, :w], cos_t, sin_t).astype(BF16)
    vw_ref[...] = win[:, w:].astype(BF16)


def _kv_prep(kv, seq, tm=256):
    assert seq // SLC_BLOCK <= HEAD_DIM
    m = kv.shape[0]
    w = N_KV_HEADS * HEAD_DIM
    cos_t, sin_t = _rope_tables(seq, N_KV_HEADS)
    tpb = seq // tm
    out = jax.ShapeDtypeStruct((m, w), BF16)
    o_spec = pl.BlockSpec((tm, w), lambda i: (i, 0))
    t_spec = pl.BlockSpec((tm, w), lambda i: (i % tpb, 0))
    return pl.pallas_call(
        functools.partial(_kv_prep_kernel, tiles_per_seq=tpb),
        out_shape=(jax.ShapeDtypeStruct((m, 2 * w), BF16), out, out, out),
        grid=(m // tm,),
        in_specs=[pl.BlockSpec((tm, 2 * w), lambda i: (i, 1)),
                  pl.BlockSpec((tm, 2 * w), lambda i: (i, 2)),
                  t_spec, t_spec],
        out_specs=(pl.BlockSpec((tm, 2 * w), lambda i: (i, 0)), o_spec, o_spec, o_spec),
        compiler_params=_cparams(("parallel",)),
        name="kv_prep",
    )(kv, kv, cos_t, sin_t)


def _compress_kernel(x_ref, pe_ref, w1_ref, b1_ref, w2_ref, b2_ref, o_ref):
    nc = x_ref.shape[0] // CMP_STRIDE
    hid = w1_ref.shape[-1]
    top = jnp.zeros((nc, hid), F32)
    bot = jnp.zeros((nc, hid), F32)
    for j in range(CMP_STRIDE):
        xj = x_ref[pl.ds(j, nc, stride=CMP_STRIDE), :]
        top += jnp.dot((xj + pe_ref[0, j:j + 1, :]).astype(BF16), w1_ref[0, j],
                       preferred_element_type=F32)
        bot += jnp.dot((xj + pe_ref[0, CMP_STRIDE + j:CMP_STRIDE + j + 1, :]).astype(BF16),
                       w1_ref[0, CMP_STRIDE + j], preferred_element_type=F32)
    h = _gelu_tanh(top + pltpu.roll(bot, nc - 1, 0) + b1_ref[0])
    o_ref[0, 0, 0] = (jnp.dot(h.astype(BF16), w2_ref[0], preferred_element_type=F32) + b2_ref[0]).astype(BF16)


def _compress(kv, bsz, seq, pe, w1, b1, w2, b2):
    nc = seq // CMP_STRIDE
    hid = w1.shape[-1]
    w1r = w1.reshape(2, CMP_BLOCK, HEAD_DIM, hid)
    return pl.pallas_call(
        _compress_kernel,
        out_shape=jax.ShapeDtypeStruct((2, bsz, N_KV_HEADS, nc, HEAD_DIM), BF16),
        grid=(2, bsz, N_KV_HEADS),
        in_specs=[pl.BlockSpec((seq, HEAD_DIM), lambda s, b, h: (b, s * N_KV_HEADS + h)),
                  pl.BlockSpec((1, CMP_BLOCK, HEAD_DIM), lambda s, b, h: (s, 0, 0)),
                  pl.BlockSpec((1, CMP_BLOCK, HEAD_DIM, hid), lambda s, b, h: (s, 0, 0, 0)),
                  pl.BlockSpec((1, 1, hid), lambda s, b, h: (s, 0, 0)),
                  pl.BlockSpec((1, hid, HEAD_DIM), lambda s, b, h: (s, 0, 0)),
                  pl.BlockSpec((1, 1, HEAD_DIM), lambda s, b, h: (s, 0, 0))],
        out_specs=pl.BlockSpec((1, 1, 1, nc, HEAD_DIM), lambda s, b, h: (s, b, h, 0, 0)),
        compiler_params=_cparams(("parallel", "parallel", "parallel")),
        name="nsa_compress",
    )(kv, pe, w1r, b1.reshape(2, 1, hid), w2, b2.reshape(2, 1, HEAD_DIM))


def _attn_kernel(q_ref, gate_ref, cos_ref, sin_ref, kc_ref, vc_ref, c2s_ref,
                 ks_ref, vs_ref, kw_ref, vw_ref, o_ref, *, seq):
    qt = pl.program_id(2)
    s0 = qt * Q_TILE
    n_slc = seq // SLC_BLOCK
    n_cmp = kc_ref.shape[-2]
    scale = HEAD_DIM ** -0.5 * LOG2_E
    rows = GQA_GROUP * Q_TILE
    nt = (((1,), (1,)), ((), ()))

    def tile4(x):
        return jnp.concatenate([x] * GQA_GROUP, axis=0)

    q4 = q_ref[...]
    qp = jnp.concatenate([q4[:, g * HEAD_DIM:(g + 1) * HEAD_DIM] for g in range(GQA_GROUP)], axis=0) * scale
    qr_b = _rope(qp, tile4(cos_ref[...]), tile4(sin_ref[...])).astype(BF16)
    qp_b = qp.astype(BF16)
    t_q = s0 + lax.broadcasted_iota(jnp.int32, (Q_TILE, 1), 0)

    n_idx = lax.broadcasted_iota(jnp.int32, (Q_TILE, n_cmp), 1)
    c_ok = (n_idx * CMP_STRIDE + (CMP_BLOCK - 1)) <= t_q
    s_c = lax.dot_general(qp_b, kc_ref[0, 0], nt, preferred_element_type=F32) + tile4(jnp.where(c_ok, 0.0, NEG_INF))
    e_c = jnp.exp2(s_c - jnp.max(s_c, axis=-1, keepdims=True)) * tile4(jnp.where(c_ok, 1.0, 0.0))
    l_c = jnp.sum(e_c, axis=-1, keepdims=True)
    p_c = e_c / jnp.where(l_c > 0.0, l_c, 1.0)
    o_c = jnp.dot(p_c.astype(BF16), vc_ref[0, 0], preferred_element_type=F32)
    p_sum = (p_c[0:Q_TILE] + p_c[Q_TILE:2 * Q_TILE]) + (p_c[2 * Q_TILE:3 * Q_TILE] + p_c[3 * Q_TILE:])
    p_hi = p_sum.astype(BF16)
    r1 = p_sum - p_hi.astype(F32)
    p_mid = r1.astype(BF16)
    p_lo = (r1 - p_mid.astype(F32)).astype(BF16)
    c2s = c2s_ref[...]
    imp_t = (lax.dot_general(c2s, p_hi, nt, preferred_element_type=F32)
             + lax.dot_general(c2s, p_mid, nt, preferred_element_type=F32)
             + lax.dot_general(c2s, p_lo, nt, preferred_element_type=F32))

    sid = lax.broadcasted_iota(jnp.int32, (n_slc, Q_TILE), 0)
    t_l = s0 + lax.broadcasted_iota(jnp.int32, (n_slc, Q_TILE), 1)
    cur = lax.shift_right_logical(t_l, int(math.log2(SLC_BLOCK)))
    causal_blk = sid <= cur
    forced = jnp.where(sid == 0, 1.0, jnp.where(sid == cur, 1.0, jnp.where(sid == cur - 1, 1.0, 0.0)))
    work = jnp.where(causal_blk, imp_t + FORCE_BONUS * forced, NEG_INF)
    sid_f = sid.astype(F32)
    sel_t = jnp.zeros((n_slc, Q_TILE), F32)
    for _ in range(min(N_SELECT, n_slc)):
        mx = jnp.max(work, axis=0, keepdims=True)
        first = jnp.min(jnp.where(work == mx, sid_f, float(n_slc)), axis=0, keepdims=True)
        pick = sid_f == first
        sel_t = jnp.where(pick, 1.0, sel_t)
        work = jnp.where(pick, -3.0e38, work)
    sel_t = jnp.where(causal_blk, sel_t, 0.0)
    unsel = jnp.concatenate([sel_t - 1.0, jnp.zeros((HEAD_DIM - n_slc, Q_TILE), F32)], axis=0).T
    q_aug = jnp.concatenate([qr_b, tile4(unsel.astype(BF16))], axis=1)

    kt = 512

    def slc_tile(j, carry, diag_bias):
        m, l, acc = carry
        k0 = pl.multiple_of(j * kt, kt)
        s = lax.dot_general(q_aug, ks_ref[pl.ds(k0, kt), :], nt, preferred_element_type=F32)
        if diag_bias is not None:
            s = s + diag_bias
        m_new = jnp.maximum(m, jnp.max(s, axis=-1, keepdims=True))
        alpha = jnp.exp2(m - m_new)
        p = jnp.exp2(s - m_new)
        l = alpha * l + jnp.sum(p, axis=-1, keepdims=True)
        acc = alpha * acc + jnp.dot(p.astype(BF16), vs_ref[pl.ds(k0, kt), :], preferred_element_type=F32)
        return m_new, l, acc

    slab = WINDOW + Q_TILE
    w0 = pl.multiple_of(jnp.maximum(s0 - WINDOW, 0), Q_TILE)
    dist = t_q - (w0 + lax.broadcasted_iota(jnp.int32, (Q_TILE, slab), 1))
    w_bias = jnp.where(dist >= 0, jnp.where(dist < WINDOW, 0.0, NEG_INF), NEG_INF)
    s_w = lax.dot_general(qr_b, kw_ref[pl.ds(w0, slab), :], nt, preferred_element_type=F32) + tile4(w_bias)
    e_w = jnp.exp2(s_w - jnp.max(s_w, axis=-1, keepdims=True))
    o_w = (jnp.dot(e_w.astype(BF16), vw_ref[pl.ds(w0, slab), :], preferred_element_type=F32)
           / jnp.sum(e_w, axis=-1, keepdims=True))

    j_diag = lax.shift_right_logical(s0, int(math.log2(kt)))
    kpos = j_diag * kt + lax.broadcasted_iota(jnp.int32, (Q_TILE, kt), 1)
    init = (jnp.full((rows, 1), NEG_INF, F32), jnp.zeros((rows, 1), F32), jnp.zeros((rows, HEAD_DIM), F32))
    carry = slc_tile(j_diag, init, tile4(jnp.where(kpos <= t_q, 0.0, NEG_INF)))
    carry = lax.fori_loop(0, lax.shift_right_logical(j_diag, 1),
                          lambda p, c: slc_tile(2 * p + 1, slc_tile(2 * p, c, None), None), carry)
    _, l_s, acc_s = lax.cond(jnp.bitwise_and(j_diag, 1) == 1,
                             lambda c: slc_tile(j_diag - 1, c, None), lambda c: c, carry)
    o_s = acc_s / l_s

    sg = _sigmoid(gate_ref[...])
    for g in range(GQA_GROUP):
        r = slice(g * Q_TILE, (g + 1) * Q_TILE)
        o_ref[:, g * HEAD_DIM:(g + 1) * HEAD_DIM] = (
            sg[:, g:g + 1] * o_c[r]
            + sg[:, GQA_GROUP + g:GQA_GROUP + g + 1] * o_s[r]
            + sg[:, 2 * GQA_GROUP + g:2 * GQA_GROUP + g + 1] * o_w[r])


def _cmp_to_slc_t(seq):
    n_cmp = seq // CMP_STRIDE - 1
    n_slc = seq // SLC_BLOCK
    cmp_start = jnp.arange(n_cmp) * CMP_STRIDE
    slc_start = jnp.arange(n_slc) * SLC_BLOCK
    overlap = jnp.clip(jnp.minimum(cmp_start[:, None] + CMP_BLOCK, slc_start[None, :] + SLC_BLOCK)
                       - jnp.maximum(cmp_start[:, None], slc_start[None, :]), 0, None)
    c2s = overlap.astype(F32) / CMP_BLOCK
    return jnp.pad(c2s, ((0, 1), (0, 0))).T.astype(BF16)


def _nsa_attention(q, gates, kc, vc, ks, vs, kw, vw, bsz, seq):
    m = q.shape[0]
    nq = seq // Q_TILE
    nc = seq // CMP_STRIDE
    cos_t, sin_t = _rope_tables(seq, 1)
    c2s_t = _cmp_to_slc_t(seq)
    gw = GQA_GROUP * HEAD_DIM
    kv_spec = pl.BlockSpec((seq, HEAD_DIM), lambda b, h, t: (b, h))
    cmp_spec = pl.BlockSpec((1, 1, nc, HEAD_DIM), lambda b, h, t: (b, h, 0, 0))
    tab_spec = pl.BlockSpec((Q_TILE, HEAD_DIM), lambda b, h, t: (t, 0))
    return pl.pallas_call(
        functools.partial(_attn_kernel, seq=seq),
        out_shape=jax.ShapeDtypeStruct((m, N_HEADS * HEAD_DIM), F32),
        grid=(bsz, N_KV_HEADS, nq),
        in_specs=[pl.BlockSpec((Q_TILE, gw), lambda b, h, t: (b * nq + t, h)),
                  pl.BlockSpec((Q_TILE, LANES), lambda b, h, t: (b * nq + t, h)),
                  tab_spec, tab_spec, cmp_spec, cmp_spec,
                  pl.BlockSpec(c2s_t.shape, lambda b, h, t: (0, 0)),
                  pl.BlockSpec((seq, 2 * HEAD_DIM), lambda b, h, t: (b, h)), kv_spec, kv_spec, kv_spec],
        out_specs=pl.BlockSpec((Q_TILE, gw), lambda b, h, t: (b * nq + t, h)),
        compiler_params=_cparams(("parallel", "parallel", "arbitrary")),
        name="nsa_attention",
    )(q, gates, cos_t, sin_t, kc, vc, c2s_t, ks, vs, kw, vw)


def _s5_operators(a_re, a_im, log_dt, b_re, b_im, c_re, c_im):
    hp = lax.Precision.HIGHEST
    n_g = a_re.shape[0]
    gpb = SSM_GROUPS_PER_BLOCK
    n_gb = n_g // gpb
    t_c, hh, pp = SSM_CHUNK, SSM_GROUP, SSM_STATE
    lam = lax.complex(a_re.astype(F32), a_im.astype(F32))
    dt = jnp.exp(log_dt.astype(F32))[:, None]
    a_bar = jnp.exp(lam * dt)
    b_bar = ((a_bar - 1.0) / lam)[:, :, None] * lax.complex(b_re.astype(F32), b_im.astype(F32))
    c_mat = lax.complex(c_re.astype(F32), c_im.astype(F32))
    steps = jnp.arange(t_c + 1, dtype=F32)[:, None, None]
    pw = jnp.exp((lam * dt)[None] * steps)

    def blockdiag(x, r, c):
        tile = jnp.tile(jnp.eye(c, dtype=BF16), (1, gpb))
        rows = jnp.arange(gpb * r)[:, None] // r
        cols = jnp.arange(gpb * c)[None, :] // c
        tiled = jnp.einsum('...c,cd->...d', x.astype(BF16), tile, preferred_element_type=BF16)
        return jnp.where(rows == cols, tiled, jnp.zeros((), BF16))

    k_lag = jnp.einsum('ghp,kgp,gpi->kgih', c_mat, pw[:t_c], b_bar, precision=hp).real
    w_intra = blockdiag(k_lag.reshape(t_c, n_gb, gpb * hh, hh), hh, hh)
    w_intra = w_intra.transpose(1, 2, 0, 3).reshape(n_gb, LANES, t_c * LANES)
    p_in = jnp.swapaxes(pw[t_c - 1::-1][:, :, :, None] * b_bar[None], -1, -2)
    p_in = p_in.reshape(t_c, n_gb, gpb * hh, pp)
    s_in = jnp.concatenate([blockdiag(p_in.real, hh, pp), blockdiag(p_in.imag, hh, pp)], axis=-1)
    s_in = s_in.transpose(1, 0, 2, 3).reshape(n_gb, t_c * LANES, 2 * gpb * pp)
    m_out = jnp.swapaxes(c_mat[None] * pw[1:, :, None, :], -1, -2)
    m_out = m_out.reshape(t_c, n_gb, gpb * pp, hh)
    s_out = jnp.concatenate([blockdiag(m_out.real, pp, hh), blockdiag(-m_out.imag, pp, hh)], axis=2)
    s_out = s_out.transpose(1, 2, 0, 3).reshape(n_gb, 2 * gpb * pp, t_c * LANES)
    k_steps = jnp.arange(SCAN_TABLE_ROWS, dtype=F32)[:, None, None] * float(t_c)
    a_pow = jnp.exp((lam * dt)[None] * k_steps).reshape(SCAN_TABLE_ROWS, n_gb, gpb * pp).swapaxes(0, 1)
    return s_in, w_intra, s_out, a_pow.real.astype(F32), a_pow.imag.astype(F32)


def _s5_kernel(u_ref, sin_ref, wi_ref, sout_ref, ar_ref, ai_ref, d_ref, o_ref, xin_ref, xprev_ref, wt_ref):
    t_c = SSM_CHUNK
    nc = u_ref.shape[0] // t_c
    half = ar_ref.shape[-1]

    @pl.when(pl.program_id(1) == 0)
    def _():
        wt_ref[...] = jnp.zeros(wt_ref.shape, wt_ref.dtype)
        for tau in range(t_c):
            wt_ref[tau * LANES:(tau + 1) * LANES, tau * LANES:] = wi_ref[0, :, :(t_c - tau) * LANES]

    def step_rows(ref, t):
        return ref.at[pl.ds(t, nc, stride=t_c), :]

    u_flat = jnp.concatenate([step_rows(u_ref, tau)[...].astype(BF16) for tau in range(t_c)], axis=1)
    xin = jnp.dot(u_flat, sin_ref[0], preferred_element_type=F32)
    x_r, x_i = xin[:, :half], xin[:, half:]
    r_in_grp = jnp.bitwise_and(lax.broadcasted_iota(jnp.int32, (nc, 1), 0), SCAN_GROUP - 1)
    d = 1
    while d < SCAN_GROUP:
        keep = jnp.where(r_in_grp >= d, 1.0, 0.0)
        s_r, s_i = pltpu.roll(x_r, d, 0) * keep, pltpu.roll(x_i, d, 0) * keep
        ad_r, ad_i = ar_ref[0, d:d + 1, :], ai_ref[0, d:d + 1, :]
        x_r, x_i = x_r + ad_r * s_r - ad_i * s_i, x_i + ad_r * s_i + ad_i * s_r
        d *= 2
    keep = jnp.where(r_in_grp >= 1, 1.0, 0.0)
    xin_ref[:, :half] = x_r
    xin_ref[:, half:] = x_i
    xprev_ref[:, :half] = pltpu.roll(x_r, 1, 0) * keep
    xprev_ref[:, half:] = pltpu.roll(x_i, 1, 0) * keep
    ak_r, ak_i = ar_ref[0, 0:SCAN_GROUP, :], ai_ref[0, 0:SCAN_GROUP, :]
    ag_r, ag_i = ar_ref[0, SCAN_GROUP:SCAN_GROUP + 1, :], ai_ref[0, SCAN_GROUP:SCAN_GROUP + 1, :]

    def group(g, carry):
        c_r, c_i = carry
        rows = pl.ds(pl.multiple_of(g * SCAN_GROUP, SCAN_GROUP), SCAN_GROUP)
        xprev_ref[rows, :half] = xprev_ref[rows, :half] + ak_r * c_r - ak_i * c_i
        xprev_ref[rows, half:] = xprev_ref[rows, half:] + ak_r * c_i + ak_i * c_r
        last = xin_ref[pl.ds(g * SCAN_GROUP + SCAN_GROUP - 1, 1), :]
        return (last[:, :half] + ag_r * c_r - ag_i * c_i, last[:, half:] + ag_r * c_i + ag_i * c_r)

    zero = jnp.zeros((1, half), F32)
    lax.fori_loop(0, nc // SCAN_GROUP, group, (zero, zero))
    h_w = (t_c // 2) * LANES
    x_prev = xprev_ref[...].astype(BF16)
    y_lo = (jnp.dot(u_flat[:, :h_w], wt_ref[:h_w, :h_w], preferred_element_type=F32)
            + jnp.dot(x_prev, sout_ref[0, :, :h_w], preferred_element_type=F32))
    y_hi = (jnp.dot(u_flat, wt_ref[:, h_w:], preferred_element_type=F32)
            + jnp.dot(x_prev, sout_ref[0, :, h_w:], preferred_element_type=F32))
    for t in range(t_c):
        y_t = (y_lo if t < t_c // 2 else y_hi)[:, (t % (t_c // 2)) * LANES:(t % (t_c // 2) + 1) * LANES]
        step_rows(o_ref, t)[...] = _gelu_tanh(y_t + d_ref[0] * step_rows(u_ref, t)[...])


def _s5_scan(u, bsz, seq, ops, d_skip):
    d_ssm = u.shape[1]
    t_c = SSM_CHUNK
    s_in, w_intra, s_out, a_r, a_i = ops
    n_gb = d_ssm // LANES
    nc = seq // t_c
    st = s_in.shape[-1]
    return pl.pallas_call(
        _s5_kernel,
        out_shape=jax.ShapeDtypeStruct(u.shape, F32),
        grid=(n_gb, bsz),
        in_specs=[pl.BlockSpec((seq, LANES), lambda g, b: (b, g)),
                  pl.BlockSpec((1,) + s_in.shape[1:], lambda g, b: (g, 0, 0)),
                  pl.BlockSpec((1,) + w_intra.shape[1:], lambda g, b: (g, 0, 0)),
                  pl.BlockSpec((1,) + s_out.shape[1:], lambda g, b: (g, 0, 0)),
                  pl.BlockSpec((1, SCAN_TABLE_ROWS, st // 2), lambda g, b: (g, 0, 0)),
                  pl.BlockSpec((1, SCAN_TABLE_ROWS, st // 2), lambda g, b: (g, 0, 0)),
                  pl.BlockSpec((1, 1, LANES), lambda g, b: (g, 0, 0))],
        out_specs=pl.BlockSpec((seq, LANES), lambda g, b: (b, g)),
        scratch_shapes=[pltpu.VMEM((nc, st), F32), pltpu.VMEM((nc, st), F32),
                        pltpu.VMEM((t_c * LANES, t_c * LANES), BF16)],
        compiler_params=_cparams(("arbitrary", "arbitrary")),
        name="s5_scan",
    )(u, s_in, w_intra, s_out, a_r, a_i, d_skip.reshape(n_gb, 1, LANES))


def _mixnorm_kernel(a_ref, s_ref, ga_ref, gs_ref, o_ref):
    da = a_ref.shape[-1]
    a = a_ref[...]
    s = s_ref[...]
    o_ref[:, :da] = (a * lax.rsqrt(jnp.mean(a * a, axis=-1, keepdims=True) + EPS) * ga_ref[...]).astype(BF16)
    o_ref[:, da:] = (s * lax.rsqrt(jnp.mean(s * s, axis=-1, keepdims=True) + EPS) * gs_ref[...]).astype(BF16)


def _mixnorm(attn, ssm, g_attn, g_ssm, tm=256):
    m, da = attn.shape
    ds = ssm.shape[1]
    return pl.pallas_call(
        _mixnorm_kernel,
        out_shape=jax.ShapeDtypeStruct((m, da + ds), BF16),
        grid=(m // tm,),
        in_specs=[pl.BlockSpec((tm, da), lambda i: (i, 0)), pl.BlockSpec((tm, ds), lambda i: (i, 0)),
                  pl.BlockSpec((1, da), lambda i: (0, 0)), pl.BlockSpec((1, ds), lambda i: (0, 0))],
        out_specs=pl.BlockSpec((tm, da + ds), lambda i: (i, 0)),
        compiler_params=_cparams(("parallel",)),
        name="mixnorm",
    )(attn, ssm, g_attn.reshape(1, da), g_ssm.reshape(1, ds))


def _split_bf16(x):
    hi = x.astype(BF16)
    return hi, (x - hi.astype(F32)).astype(BF16)


def _lane_max(x):
    return jnp.max(x, axis=-1, keepdims=True)


def _first_lane_of(cond_val, lane_f):
    return jnp.min(jnp.where(cond_val > 0.0, lane_f, float(LANES)), axis=-1, keepdims=True)


def _ffn_norm_router_kernel(h_ref, g_ref, whi_ref, wlo_ref, b_ref, hn_ref, route_ref):
    x = h_ref[...]
    hn = x * lax.rsqrt(jnp.mean(x * x, axis=-1, keepdims=True) + EPS) * g_ref[...]
    hn_ref[...] = _pack_bf16_pairs(hn)
    hi, lo = _split_bf16(hn)
    logits = (jnp.dot(hi, whi_ref[...], preferred_element_type=F32)
              + jnp.dot(lo, whi_ref[...], preferred_element_type=F32)
              + jnp.dot(hi, wlo_ref[...], preferred_element_type=F32)) + b_ref[...]
    lane = lax.broadcasted_iota(jnp.int32, logits.shape, 1)
    lane_f = lane.astype(F32)
    in_c = jnp.where(lane < N_EXPERT_GROUPS, 1.0, 0.0)
    c_l = jnp.where(in_c > 0.0, logits, NEG_INF)
    c_e = jnp.exp(c_l - _lane_max(c_l)) * in_c
    p_c = c_e / jnp.sum(c_e, axis=-1, keepdims=True)
    w_grp = _lane_max(p_c)
    grp = _first_lane_of(jnp.where(p_c == w_grp, in_c, 0.0), lane_f)
    lo_lane = float(N_EXPERT_GROUPS) + grp * float(EXPERTS_PER_GROUP)
    in_f = jnp.where(lane_f >= lo_lane, jnp.where(lane_f < lo_lane + float(EXPERTS_PER_GROUP), 1.0, 0.0), 0.0)
    f_l = jnp.where(in_f > 0.0, logits, NEG_INF)
    f_e = jnp.exp(f_l - _lane_max(f_l)) * in_f
    p_f = f_e / jnp.sum(f_e, axis=-1, keepdims=True)
    p_f = jnp.where(in_f > 0.0, p_f, -1.0)
    p1 = _lane_max(p_f)
    i1 = _first_lane_of(jnp.where(p_f == p1, in_f, 0.0), lane_f)
    rest = jnp.where(lane_f == i1, -1.0, p_f)
    p2 = _lane_max(rest)
    i2 = _first_lane_of(jnp.where(rest == p2, in_f, 0.0) * jnp.where(lane_f == i1, 0.0, 1.0), lane_f)
    denom = p1 + p2
    vals = (w_grp * p1 / denom, w_grp * p2 / denom,
            i1 - float(N_EXPERT_GROUPS), i2 - float(N_EXPERT_GROUPS))
    out = jnp.zeros(logits.shape, F32)
    for k, v in enumerate(vals):
        out = jnp.where(lane == k, v, out)
    route_ref[...] = out


def _ffn_norm_router(h, g, w_router, b_router, tm=256):
    m, d = h.shape
    n = w_router.shape[1]
    w_hi, w_lo = _split_bf16(w_router)
    return pl.pallas_call(
        _ffn_norm_router_kernel,
        out_shape=(jax.ShapeDtypeStruct((m, d // 2), jnp.uint32), jax.ShapeDtypeStruct((m, n), F32)),
        grid=(m // tm,),
        in_specs=[pl.BlockSpec((tm, d), lambda i: (i, 0)), pl.BlockSpec((1, d), lambda i: (0, 0)),
                  pl.BlockSpec((d, n), lambda i: (0, 0)), pl.BlockSpec((d, n), lambda i: (0, 0)),
                  pl.BlockSpec((1, n), lambda i: (0, 0))],
        out_specs=(pl.BlockSpec((tm, d // 2), lambda i: (i, 0)), pl.BlockSpec((tm, n), lambda i: (i, 0))),
        compiler_params=_cparams(("parallel",)),
        name="ffn_norm_router",
    )(h, g.reshape(1, d), w_hi, w_lo, b_router.reshape(1, n))


def _dispatch_plan(expert, tm):
    n_tok = expert.shape[0]
    n_assign = n_tok * TOP_K
    n_blocks = -(-(n_assign + N_EXPERTS * (MOE_BLOCK - 1)) // MOE_BLOCK)
    n_rows = n_blocks * MOE_BLOCK
    flat = expert.reshape(n_assign)
    ids = jnp.arange(n_assign, dtype=jnp.int32)
    _, order = lax.sort_key_val(flat, ids)
    _, inv = lax.sort_key_val(order, ids)
    e_ids = jnp.arange(N_EXPERTS, dtype=jnp.int32)
    counts = jnp.sum((flat[:, None] == e_ids[None, :]).astype(jnp.int32), axis=0)
    start = jnp.cumsum(counts) - counts
    padded = (counts + MOE_BLOCK - 1) // MOE_BLOCK * MOE_BLOCK
    pad_end = jnp.cumsum(padded)
    pad_start = pad_end - padded
    rows = jnp.arange(n_rows, dtype=jnp.int32)
    blk_row0 = jnp.arange(n_blocks, dtype=jnp.int32) * MOE_BLOCK
    blk_e = jnp.minimum(jnp.sum((pad_end[None, :] <= blk_row0[:, None]).astype(jnp.int32), axis=1), N_EXPERTS - 1)
    row_e = jnp.repeat(blk_e, MOE_BLOCK)
    slot = rows - pad_start[row_e]
    valid = (slot < counts[row_e]) & (rows < pad_end[-1])
    src = jnp.clip(start[row_e] + slot, 0, n_assign - 1)
    buf_tok = jnp.where(valid, order[src] // TOP_K, 0).astype(jnp.int32)
    dest = (pad_start[flat] + inv - start[flat]).astype(jnp.int32)
    dest = dest.reshape(n_tok // tm, tm, TOP_K).transpose(0, 2, 1).reshape(-1)
    plan = jnp.concatenate([pad_start // MOE_BLOCK, padded // MOE_BLOCK,
                            pad_end[-1:] // MOE_BLOCK]).astype(jnp.int32)
    return plan, buf_tok, dest, n_blocks


def _zero_fill_spare(plan_ref, zero_vmem, out_hbm, col0, width, sem, n_blocks):
    n_used = plan_ref[2 * N_EXPERTS]

    def cp(blk):
        return pltpu.make_async_copy(
            zero_vmem, out_hbm.at[pl.ds(pl.multiple_of(blk * MOE_BLOCK, MOE_BLOCK), MOE_BLOCK), pl.ds(col0, width)],
            sem)

    def start(blk, c):
        cp(blk).start(priority=SIDE_DMA_PRIORITY)
        return c

    def wait(blk, c):
        cp(blk).wait()
        return c

    lax.fori_loop(n_used, n_blocks, start, 0)
    lax.fori_loop(n_used, n_blocks, wait, 0)


def _pack_bf16_pairs(x):
    w = x.shape[1] // 2
    hi = lax.bitcast_convert_type(x[:, :w].astype(BF16).astype(F32), jnp.uint32)
    lo = lax.bitcast_convert_type(x[:, w:].astype(BF16).astype(F32), jnp.uint32)
    return hi | (lo >> 16)


def _unpack_bf16_pairs(p):
    first = lax.bitcast_convert_type(p & jnp.uint32(0xFFFF0000), F32)
    second = lax.bitcast_convert_type(p << 16, F32)
    return jnp.concatenate([first, second], axis=1).astype(BF16)


def _moe_up_kernel(plan_ref, tok_ref, x_hbm, *rest, n_blocks):
    w_refs = rest[:2 * W_DMA_SPLIT]
    act_hbm, wg_b, wu_b, xs, gbuf, obuf, afull, gsem, osem, fsem = rest[2 * W_DMA_SPLIT:]
    wg_refs, wu_refs = w_refs[:W_DMA_SPLIT], w_refs[W_DMA_SPLIT:]
    e = pl.program_id(0)
    c = pl.program_id(1)
    n_c = pl.num_programs(1)
    tn = wg_b.shape[1]
    col0 = pl.multiple_of(c * tn, tn)
    b0 = plan_ref[e]
    nb = plan_ref[N_EXPERTS + e]
    n_res = jnp.minimum(nb, MOE_X_CAP)

    def row_copy(blk, r, slot):
        tok = tok_ref[blk * MOE_BLOCK + r]
        return pltpu.make_async_copy(x_hbm.at[pl.ds(tok, 1), :], gbuf.at[slot, pl.ds(r, 1), :], gsem.at[slot])

    def gather_start(blk, slot):
        lax.fori_loop(0, MOE_BLOCK, lambda r, cr: (row_copy(blk, r, slot).start(priority=SIDE_DMA_PRIORITY), cr)[1], 0, unroll=DMA_UNROLL)

    def gather_wait(blk, slot):
        lax.fori_loop(0, MOE_BLOCK, lambda r, cr: (row_copy(blk, r, slot).wait(), cr)[1], 0, unroll=DMA_UNROLL)

    def gather_expert(first_blk, n_blk):
        lax.fori_loop(0, jnp.minimum(n_blk, MOE_X_CAP), lambda k, cr: (gather_start(first_blk + k, k), cr)[1], 0)

    def gather_now(blk):
        gather_start(blk, MOE_X_CAP)
        gather_wait(blk, MOE_X_CAP)
        return _unpack_bf16_pairs(gbuf[MOE_X_CAP])

    @pl.when(jnp.logical_and(c == 1, e + 1 < N_EXPERTS))
    def _():
        e_next = jnp.minimum(e + 1, N_EXPERTS - 1)
        gather_expert(plan_ref[e_next], plan_ref[N_EXPERTS + e_next])

    def rows_of(blk):
        return pl.ds(pl.multiple_of(blk * MOE_BLOCK, MOE_BLOCK), MOE_BLOCK)

    def out_copy(blk, slot):
        return pltpu.make_async_copy(obuf.at[slot], act_hbm.at[rows_of(blk), pl.ds(col0, tn)], osem.at[slot])

    def full_copy(j):
        return pltpu.make_async_copy(afull.at[j], act_hbm.at[rows_of(b0 + j), :], fsem)

    def act_of(x):
        g = jnp.dot(x, wg_b[...], preferred_element_type=F32)
        u = jnp.dot(x, wu_b[...], preferred_element_type=F32)
        return (g * _sigmoid(g) * u).astype(BF16)

    def emit_resident(x, j):
        act = act_of(x)
        for cc in range(act_hbm.shape[1] // tn):
            @pl.when(c == cc)
            def _():
                afull[j, :, cc * tn:(cc + 1) * tn] = act

    def emit_streamed(x, j):
        i = j - n_res
        slot = lax.rem(i, 2)
        act = act_of(x)

        @pl.when(i >= 2)
        def _():
            out_copy(b0 + j - 2, slot).wait()

        obuf[slot] = act
        out_copy(b0 + j, slot).start(priority=SIDE_DMA_PRIORITY)

    @pl.when(nb > 0)
    def _():
        rows = wg_b.shape[0] // W_DMA_SPLIT
        for k in range(W_DMA_SPLIT):
            wg_b[k * rows:(k + 1) * rows, :] = wg_refs[k][0].astype(BF16)
            wu_b[k * rows:(k + 1) * rows, :] = wu_refs[k][0].astype(BF16)

    @pl.when(c == 0)
    def _():
        @pl.when(e == 0)
        def _():
            gather_expert(b0, nb)

        def landed(j, carry):
            gather_wait(b0 + j, j)
            x = _unpack_bf16_pairs(gbuf[j])
            xs[j] = x
            emit_resident(x, j)
            return carry

        def overflow(j, carry):
            emit_streamed(gather_now(b0 + j), j)
            return carry

        lax.fori_loop(0, n_res, landed, 0)
        lax.fori_loop(n_res, nb, overflow, 0)

    @pl.when(c > 0)
    def _():
        def resident(j, carry):
            emit_resident(xs[j], j)
            return carry

        def overflow(j, carry):
            emit_streamed(gather_now(b0 + j), j)
            return carry

        lax.fori_loop(0, n_res, resident, 0)
        lax.fori_loop(n_res, nb, overflow, 0)

    n_str = nb - n_res

    @pl.when(n_str >= 2)
    def _():
        out_copy(b0 + nb - 2, lax.rem(n_str, 2)).wait()

    @pl.when(n_str >= 1)
    def _():
        out_copy(b0 + nb - 1, lax.rem(n_str + 1, 2)).wait()

    @pl.when(c == n_c - 1)
    def _():
        lax.fori_loop(0, n_res, lambda j, cr: (full_copy(j).start(priority=SIDE_DMA_PRIORITY), cr)[1], 0)
        lax.fori_loop(0, n_res, lambda j, cr: (full_copy(j).wait(), cr)[1], 0)

    @pl.when(jnp.logical_and(e == N_EXPERTS - 1, c == n_c - 1))
    def _():
        afull[0] = jnp.zeros(afull.shape[1:], afull.dtype)
        _zero_fill_spare(plan_ref, afull.at[0], act_hbm, 0, act_hbm.shape[1], fsem, n_blocks)


def _moe_up(hn_packed, w_gate, w_up, plan, buf_tok, n_blocks, n_chunks=4):
    d = w_gate.shape[-2]
    d_e = w_gate.shape[-1]
    tn = d_e // n_chunks
    w_specs = [pl.BlockSpec((1, d // W_DMA_SPLIT, tn), functools.partial(lambda e, c, plan, tok, k: (e, k, c), k=k))
               for k in range(W_DMA_SPLIT)]
    return pl.pallas_call(
        functools.partial(_moe_up_kernel, n_blocks=n_blocks),
        out_shape=jax.ShapeDtypeStruct((n_blocks * MOE_BLOCK, d_e), BF16),
        grid_spec=pltpu.PrefetchScalarGridSpec(
            num_scalar_prefetch=2,
            grid=(N_EXPERTS, n_chunks),
            in_specs=[pl.BlockSpec(memory_space=pl.ANY)] + w_specs + w_specs,
            out_specs=pl.BlockSpec(memory_space=pl.ANY),
            scratch_shapes=[pltpu.VMEM((d, tn), BF16), pltpu.VMEM((d, tn), BF16),
                            pltpu.VMEM((MOE_X_CAP, MOE_BLOCK, d), BF16),
                            pltpu.VMEM((GATHER_SLOTS, MOE_BLOCK, d // 2), jnp.uint32),
                            pltpu.VMEM((2, MOE_BLOCK, tn), BF16),
                            pltpu.VMEM((MOE_X_CAP, MOE_BLOCK, d_e), BF16),
                            pltpu.SemaphoreType.DMA((GATHER_SLOTS,)), pltpu.SemaphoreType.DMA((2,)),
                            pltpu.SemaphoreType.DMA(())]),
        compiler_params=_cparams(("arbitrary", "arbitrary")),
        name="moe_up",
    )(plan, buf_tok, hn_packed, *([w_gate] * W_DMA_SPLIT), *([w_up] * W_DMA_SPLIT))


def _moe_down_kernel(plan_ref, act_hbm, *rest, n_blocks):
    wd_refs, (y_hbm, wd_b, abuf, obuf, asem, osem) = rest[:W_DMA_SPLIT], rest[W_DMA_SPLIT:]
    e = pl.program_id(0)
    b0 = plan_ref[e]
    nb = plan_ref[N_EXPERTS + e]
    n_res = jnp.minimum(nb, DOWN_ACT_CAP)
    par = lax.rem(e, 2)

    def rows_of(blk):
        return pl.ds(pl.multiple_of(blk * MOE_BLOCK, MOE_BLOCK), MOE_BLOCK)

    def in_copy(blk, half, slot):
        return pltpu.make_async_copy(act_hbm.at[rows_of(blk), :], abuf.at[half, slot], asem.at[half, slot])

    def out_copy(blk, slot):
        return pltpu.make_async_copy(obuf.at[slot], y_hbm.at[rows_of(blk), :], osem.at[slot])

    def request_expert(first_blk, n_blk, half):
        lax.fori_loop(0, jnp.minimum(n_blk, DOWN_ACT_CAP),
                      lambda k, cr: (in_copy(first_blk + k, half, k).start(priority=SIDE_DMA_PRIORITY), cr)[1], 0)

    @pl.when(e == 0)
    def _():
        request_expert(b0, nb, 0)

    @pl.when(e + 1 < N_EXPERTS)
    def _():
        e_next = jnp.minimum(e + 1, N_EXPERTS - 1)
        request_expert(plan_ref[e_next], plan_ref[N_EXPERTS + e_next], 1 - par)

    @pl.when(nb > 0)
    def _():
        rows = wd_b.shape[0] // W_DMA_SPLIT
        for k in range(W_DMA_SPLIT):
            wd_b[k * rows:(k + 1) * rows, :] = wd_refs[k][0].astype(BF16)

    def emit(a, j):
        slot = lax.rem(j, DOWN_OUT_SLOTS)
        y = jnp.dot(a, wd_b[...], preferred_element_type=F32)

        @pl.when(j >= DOWN_OUT_SLOTS)
        def _():
            out_copy(b0 + j - DOWN_OUT_SLOTS, slot).wait()

        obuf[slot] = _pack_bf16_pairs(y)
        out_copy(b0 + j, slot).start(priority=SIDE_DMA_PRIORITY)

    def landed(j, carry):
        in_copy(b0 + j, par, j).wait()
        emit(abuf[par, j], j)
        return carry

    def overflow(j, carry):
        in_copy(b0 + j, par, 0).start(priority=SIDE_DMA_PRIORITY)
        in_copy(b0 + j, par, 0).wait()
        emit(abuf[par, 0], j)
        return carry

    lax.fori_loop(0, n_res, landed, 0)
    lax.fori_loop(n_res, nb, overflow, 0)
    lax.fori_loop(jnp.maximum(nb - DOWN_OUT_SLOTS, 0), nb,
                  lambda j, cr: (out_copy(b0 + j, lax.rem(j, DOWN_OUT_SLOTS)).wait(), cr)[1], 0)

    @pl.when(e == N_EXPERTS - 1)
    def _():
        obuf[0] = jnp.zeros(obuf.shape[1:], obuf.dtype)
        _zero_fill_spare(plan_ref, obuf.at[0], y_hbm, 0, y_hbm.shape[1], osem.at[0], n_blocks)


def _moe_down(act, w_down, plan, n_blocks):
    n_rows, d_e = act.shape
    d = w_down.shape[-1]
    return pl.pallas_call(
        functools.partial(_moe_down_kernel, n_blocks=n_blocks),
        out_shape=jax.ShapeDtypeStruct((n_rows, d // 2), jnp.uint32),
        grid_spec=pltpu.PrefetchScalarGridSpec(
            num_scalar_prefetch=1,
            grid=(N_EXPERTS,),
            in_specs=[pl.BlockSpec(memory_space=pl.ANY)] + [
                pl.BlockSpec((1, d_e // W_DMA_SPLIT, d), functools.partial(lambda e, plan, k: (e, k, 0), k=k))
                for k in range(W_DMA_SPLIT)],
            out_specs=pl.BlockSpec(memory_space=pl.ANY),
            scratch_shapes=[pltpu.VMEM((d_e, d), BF16),
                            pltpu.VMEM((2, DOWN_ACT_CAP, MOE_BLOCK, d_e), BF16),
                            pltpu.VMEM((DOWN_OUT_SLOTS, MOE_BLOCK, d // 2), jnp.uint32),
                            pltpu.SemaphoreType.DMA((2, DOWN_ACT_CAP)), pltpu.SemaphoreType.DMA((DOWN_OUT_SLOTS,))]),
        compiler_params=_cparams(("arbitrary",)),
        name="moe_down",
    )(plan, act, *([w_down] * W_DMA_SPLIT))


def _combine_kernel(dest_ref, h_ref, route_ref, gn_ref, y_hbm, o_ref, rows_ref, sem):
    tm = h_ref.shape[0]
    n_rows = tm * TOP_K
    i = pl.program_id(0)
    n = pl.num_programs(0)

    def row_copy(step, r, slot):
        src = dest_ref[step * n_rows + r]
        return pltpu.make_async_copy(y_hbm.at[pl.ds(src, 1), :], rows_ref.at[slot, pl.ds(r, 1), :], sem.at[slot])

    def gather_start(step, slot):
        lax.fori_loop(0, n_rows, lambda r, cr: (row_copy(step, r, slot).start(priority=SIDE_DMA_PRIORITY), cr)[1], 0, unroll=DMA_UNROLL)

    def gather_wait(step, slot):
        lax.fori_loop(0, n_rows, lambda r, cr: (row_copy(step, r, slot).wait(), cr)[1], 0, unroll=DMA_UNROLL)

    slot = lax.rem(i, 2)

    @pl.when(i == 0)
    def _():
        gather_start(0, 0)

    nxt = jnp.minimum(i + 1, n - 1)
    for r in range(n_rows):
        row_copy(nxt, r, 1 - slot).start(priority=SIDE_DMA_PRIORITY)

    gather_wait(i, slot)
    route = route_ref[...]
    rows = rows_ref[slot]
    g0, g1 = route[:, 0:1], route[:, 1:2]
    r0, r1 = rows[0:tm, :], rows[tm:2 * tm, :]
    half = rows.shape[1]

    def first(p):
        return lax.bitcast_convert_type(p & jnp.uint32(0xFFFF0000), F32)

    def second(p):
        return lax.bitcast_convert_type(p << 16, F32)

    h = jnp.concatenate([h_ref[:, :half] + g0 * first(r0) + g1 * first(r1),
                         h_ref[:, half:] + g0 * second(r0) + g1 * second(r1)], axis=1)
    o_ref[...] = h * lax.rsqrt(jnp.mean(h * h, axis=-1, keepdims=True) + EPS) * gn_ref[...]

    @pl.when(i == n - 1)
    def _():
        gather_wait(nxt, 1 - slot)


def _combine_final(h, route, dest, y_buf, g_final, tm=128):
    m, d = h.shape
    return pl.pallas_call(
        _combine_kernel,
        out_shape=jax.ShapeDtypeStruct((m, d), F32),
        grid_spec=pltpu.PrefetchScalarGridSpec(
            num_scalar_prefetch=1,
            grid=(m // tm,),
            in_specs=[pl.BlockSpec((tm, d), lambda i, dst: (i, 0)),
                      pl.BlockSpec((tm, LANES), lambda i, dst: (i, 0)),
                      pl.BlockSpec((1, d), lambda i, dst: (0, 0)),
                      pl.BlockSpec(memory_space=pl.ANY)],
            out_specs=pl.BlockSpec((tm, d), lambda i, dst: (i, 0)),
            scratch_shapes=[pltpu.VMEM((2, TOP_K * tm, d // 2), jnp.uint32), pltpu.SemaphoreType.DMA((2,))]),
        compiler_params=_cparams(("arbitrary",)),
        name="moe_combine_final",
    )(dest, h, route, g_final.reshape(1, d), y_buf)


def _gate_column_order():
    cols = []
    for hk in range(N_KV_HEADS):
        blk = [-1] * LANES
        for br in range(N_BRANCHES):
            for g in range(GQA_GROUP):
                blk[br * GQA_GROUP + g] = (hk * GQA_GROUP + g) * N_BRANCHES + br
        cols.extend(blk)
    return cols


def kernel(x, norm_mix, w_in, cmp_pe_k, cmp_w1_k, cmp_b1_k, cmp_w2_k, cmp_b2_k, cmp_pe_v, cmp_w1_v, cmp_b1_v, cmp_w2_v, cmp_b2_v, ssm_a_re, ssm_a_im, ssm_log_dt, ssm_b_re, ssm_b_im, ssm_c_re, ssm_c_im, ssm_d, ssm_w_glu, ssm_b_glu, norm_attn_out, norm_ssm_out, w_out, norm_ffn, router_w_coarse, router_b_coarse, router_w_fine, router_b_fine, w_gate, w_up, w_down, norm_final):
    bsz, seq, d_model = x.shape
    m = bsz * seq
    depth = w_in.shape[0]
    assert depth == 1, "the combine kernel fuses the final norm; stacked layers are not supported"
    layer = 0
    q_w = N_HEADS * HEAD_DIM
    kv_w = N_KV_HEADS * HEAD_DIM
    gate_w = N_HEADS * N_BRANCHES
    cut1, cut2 = q_w, q_w + 6 * kv_w
    cut3 = cut2 + gate_w
    h = x.astype(F32).reshape(m, d_model)
    gate_cols = jnp.array([c if c >= 0 else 0 for c in _gate_column_order()], jnp.int32)
    gate_keep = jnp.array([1.0 if c >= 0 else 0.0 for c in _gate_column_order()], F32)

    w_in_b = w_in[layer].astype(BF16)
    hn = _rmsnorm(h, norm_mix[layer], BF16)
    q = _matmul(hn, w_in_b[:, :cut1], name="proj_q")
    kv = _matmul(hn, w_in_b[:, cut1:cut2], name="proj_kv")
    w_gate_cols = (w_in[layer][:, cut2:cut3][:, gate_cols] * gate_keep[None, :]).astype(BF16)
    gates = _matmul(hn, w_gate_cols, tn=512, name="proj_gate")
    u = _matmul(hn, w_in_b[:, cut3:], name="proj_u")

    stack2 = lambda a, b: jnp.stack([a[layer], b[layer]])
    cmp_kv = _compress(kv, bsz, seq, stack2(cmp_pe_k, cmp_pe_v),
                       stack2(cmp_w1_k, cmp_w1_v).astype(BF16), stack2(cmp_b1_k, cmp_b1_v),
                       stack2(cmp_w2_k, cmp_w2_v).astype(BF16), stack2(cmp_b2_k, cmp_b2_v))
    ks, vs, kw, vw = _kv_prep(kv, seq)
    attn = _nsa_attention(q, gates, cmp_kv[0], cmp_kv[1], ks, vs, kw, vw, bsz, seq)

    ops = _s5_operators(ssm_a_re[layer], ssm_a_im[layer], ssm_log_dt[layer], ssm_b_re[layer],
                        ssm_b_im[layer], ssm_c_re[layer], ssm_c_im[layer])
    y = _s5_scan(u, bsz, seq, ops, ssm_d[layer].astype(F32))
    ssm = _glu_matmul(y, ssm_w_glu[layer].astype(BF16), ssm_b_glu[layer].astype(F32))

    mixed = _mixnorm(attn, ssm, norm_attn_out[layer], norm_ssm_out[layer])
    h = _matmul(mixed, w_out[layer].astype(BF16), residual=h, name="out_proj")

    n_r = N_EXPERT_GROUPS + N_EXPERTS
    w_router = jnp.pad(jnp.concatenate([router_w_coarse[layer], router_w_fine[layer]], axis=1),
                       ((0, 0), (0, LANES - n_r))).astype(F32)
    b_router = jnp.pad(jnp.concatenate([router_b_coarse[layer], router_b_fine[layer]]), (0, LANES - n_r))
    hn2, route = _ffn_norm_router(h, norm_ffn[layer], w_router, b_router.astype(F32))
    expert = route[:, 2:2 + TOP_K].astype(jnp.int32)
    tm_c = 128
    plan, buf_tok, dest, n_blocks = _dispatch_plan(expert, tm_c)
    act = _moe_up(hn2, w_gate[layer], w_up[layer], plan, buf_tok, n_blocks)
    y_buf = _moe_down(act, w_down[layer], plan, n_blocks)
    out = _combine_final(h, route, dest, y_buf, norm_final, tm=tm_c)
    return out.reshape(bsz, seq, d_model).astype(x.dtype)
```

```python
import functools
import math

import jax
import jax.numpy as jnp
from jax import lax
from jax.experimental import pallas as pl
from jax.experimental.pallas import tpu as pltpu

F32 = jnp.float32
BF16 = jnp.bfloat16

HEAD_DIM = 128
N_HEADS = 16
N_KV_HEADS = 4
GQA_GROUP = N_HEADS // N_KV_HEADS
ROT_DIM = HEAD_DIM // 4
ROPE_THETA = 500000.0
CMP_STRIDE = 16
CMP_BLOCK = 32
SLC_BLOCK = 64
N_SELECT = 16
WINDOW = 512
Q_TILE = 256
N_BRANCHES = 3
FORCE_BONUS = 1.0e4
NEG_INF = -1.0e30
MASK_BIG = 1.0e30
SSM_GROUP = 16
SSM_STATE = 64
SSM_CHUNK = 16
SSM_GROUPS_PER_BLOCK = 128 // SSM_GROUP
N_EXPERT_GROUPS = 8
EXPERTS_PER_GROUP = 8
N_EXPERTS = N_EXPERT_GROUPS * EXPERTS_PER_GROUP
TOP_K = 2
MOE_BLOCK = 128
EPS = 1.0e-6
LOG2_E = 1.4426950408889634

V7X_VMEM_LIMIT = 56 * 1024 * 1024
LANES = 128
SCAN_GROUP = 8
SCAN_TABLE_ROWS = 16
DMA_UNROLL = 8
SIDE_DMA_PRIORITY = 1
MOE_X_CAP = 12
W_DMA_SPLIT = 4
GATHER_SLOTS = MOE_X_CAP + 1
DOWN_ACT_CAP = 8
DOWN_OUT_SLOTS = 4


def _cparams(sem, vmem=V7X_VMEM_LIMIT):
    return pltpu.CompilerParams(dimension_semantics=sem, vmem_limit_bytes=vmem)


def _gelu_tanh(x):
    return 0.5 * x * (1.0 + jnp.tanh(math.sqrt(2.0 / math.pi) * (x + 0.044715 * (x * x * x))))


def _sigmoid(x):
    return 1.0 / (1.0 + jnp.exp(-x))


def _rmsnorm_kernel(x_ref, g_ref, o_ref):
    x = x_ref[...]
    ms = jnp.mean(x * x, axis=-1, keepdims=True)
    o_ref[...] = (x * lax.rsqrt(ms + EPS) * g_ref[...]).astype(o_ref.dtype)


def _rmsnorm(x, g, out_dtype, tm=256):
    m, d = x.shape
    return pl.pallas_call(
        _rmsnorm_kernel,
        out_shape=jax.ShapeDtypeStruct((m, d), out_dtype),
        grid=(m // tm,),
        in_specs=[pl.BlockSpec((tm, d), lambda i: (i, 0)),
                  pl.BlockSpec((1, d), lambda i: (0, 0))],
        out_specs=pl.BlockSpec((tm, d), lambda i: (i, 0)),
        compiler_params=_cparams(("parallel",)),
        name="rmsnorm",
    )(x, g.reshape(1, d))


def _mm_kernel(a_ref, w_ref, o_ref):
    o_ref[...] = jnp.dot(a_ref[...], w_ref[...], preferred_element_type=F32)


def _mm_glu_kernel(y_ref, w_ref, b_ref, yg_ref, o_ref):
    z = jnp.dot(y_ref[...].astype(BF16), w_ref[...], preferred_element_type=F32) + b_ref[...]
    o_ref[...] = yg_ref[...] * _sigmoid(z)


def _mm_residual_kernel(a_ref, w_ref, r_ref, o_ref):
    o_ref[...] = r_ref[...] + jnp.dot(a_ref[...], w_ref[...], preferred_element_type=F32)


def _matmul(a, w, *, residual=None, tm=512, tn=1024, name="matmul"):
    m, k = a.shape
    n = w.shape[1]
    tn = min(tn, n)
    assert m % tm == 0 and n % tn == 0
    a_spec = pl.BlockSpec((tm, k), lambda j, i: (i, 0))
    w_spec = pl.BlockSpec((k, tn), lambda j, i: (0, j))
    o_spec = pl.BlockSpec((tm, tn), lambda j, i: (i, j))
    if residual is not None:
        kern, ins, specs = _mm_residual_kernel, (a, w, residual), [a_spec, w_spec, o_spec]
    else:
        kern, ins, specs = _mm_kernel, (a, w), [a_spec, w_spec]
    return pl.pallas_call(
        kern,
        out_shape=jax.ShapeDtypeStruct((m, n), F32),
        grid=(n // tn, m // tm),
        in_specs=specs,
        out_specs=o_spec,
        compiler_params=_cparams(("parallel", "parallel")),
        name=name,
    )(*ins)


def _glu_matmul(y, w, bias, tm=512, tn=1024):
    m, k = y.shape
    n = w.shape[1]
    return pl.pallas_call(
        _mm_glu_kernel,
        out_shape=jax.ShapeDtypeStruct((m, n), F32),
        grid=(n // tn, m // tm),
        in_specs=[pl.BlockSpec((tm, k), lambda j, i: (i, 0)),
                  pl.BlockSpec((k, tn), lambda j, i: (0, j)),
                  pl.BlockSpec((1, tn), lambda j, i: (0, j)),
                  pl.BlockSpec((tm, tn), lambda j, i: (i, j))],
        out_specs=pl.BlockSpec((tm, tn), lambda j, i: (i, j)),
        compiler_params=_cparams(("parallel", "parallel")),
        name="ssm_glu",
    )(y, w, bias.reshape(1, n), y)


def _rope_tables(seq, reps):
    half = ROT_DIM // 2
    inv_freq = ROPE_THETA ** (-jnp.arange(half, dtype=F32) / half)
    ang = jnp.arange(seq).astype(F32)[:, None] * inv_freq[None, :]
    cos, sin = jnp.cos(ang), jnp.sin(ang)
    ones = jnp.ones((seq, HEAD_DIM - ROT_DIM), F32)
    cos_t = jnp.concatenate([cos, cos, ones], axis=-1)
    sin_t = jnp.concatenate([-sin, sin, 0.0 * ones], axis=-1)
    return jnp.tile(cos_t, (1, reps)), jnp.tile(sin_t, (1, reps))


def _rope(x, cos_t, sin_t):
    half = ROT_DIM // 2
    n = x.shape[-1]
    lane = jnp.bitwise_and(lax.broadcasted_iota(jnp.int32, x.shape, x.ndim - 1), HEAD_DIM - 1)
    partner = jnp.where(lane < half, pltpu.roll(x, n - half, x.ndim - 1), pltpu.roll(x, half, x.ndim - 1))
    return x * cos_t + partner * sin_t


def _kv_prep_kernel(slc_ref, win_ref, cos_ref, sin_ref, ks_ref, vs_ref, kw_ref, vw_ref, *, tiles_per_seq):
    tm, w = vs_ref.shape
    cos_t, sin_t = cos_ref[...], sin_ref[...]
    slc = slc_ref[...]
    win = win_ref[...]
    k_slc = _rope(slc[:, :w], cos_t, sin_t).astype(BF16)
    pos = lax.rem(pl.program_id(0), tiles_per_seq) * tm + lax.broadcasted_iota(jnp.int32, (tm, HEAD_DIM), 0)
    lane = lax.broadcasted_iota(jnp.int32, (tm, HEAD_DIM), 1)
    own_blk = jnp.where(lane == lax.shift_right_logical(pos, int(math.log2(SLC_BLOCK))), MASK_BIG, 0.0).astype(BF16)
    for hk in range(N_KV_HEADS):
        ks_ref[:, 2 * hk * HEAD_DIM:(2 * hk + 1) * HEAD_DIM] = k_slc[:, hk * HEAD_DIM:(hk + 1) * HEAD_DIM]
        ks_ref[:, (2 * hk + 1) * HEAD_DIM:(2 * hk + 2) * HEAD_DIM] = own_blk
    vs_ref[...] = slc[:, w:].astype(BF16)
    kw_ref[...] = _rope(win[:, :w], cos_t, sin_t).astype(BF16)
    vw_ref[...] = win[:, w:].astype(BF16)


def _kv_prep(kv, seq, tm=256):
    assert seq // SLC_BLOCK <= HEAD_DIM
    m = kv.shape[0]
    w = N_KV_HEADS * HEAD_DIM
    cos_t, sin_t = _rope_tables(seq, N_KV_HEADS)
    tpb = seq // tm
    out = jax.ShapeDtypeStruct((m, w), BF16)
    o_spec = pl.BlockSpec((tm, w), lambda i: (i, 0))
    t_spec = pl.BlockSpec((tm, w), lambda i: (i % tpb, 0))
    return pl.pallas_call(
        functools.partial(_kv_prep_kernel, tiles_per_seq=tpb),
        out_shape=(jax.ShapeDtypeStruct((m, 2 * w), BF16), out, out, out),
        grid=(m // tm,),
        in_specs=[pl.BlockSpec((tm, 2 * w), lambda i: (i, 1)),
                  pl.BlockSpec((tm, 2 * w), lambda i: (i, 2)),
                  t_spec, t_spec],
        out_specs=(pl.BlockSpec((tm, 2 * w), lambda i: (i, 0)), o_spec, o_spec, o_spec),
        compiler_params=_cparams(("parallel",)),
        name="kv_prep",
    )(kv, kv, cos_t, sin_t)


def _compress_kernel(x_ref, pe_ref, w1_ref, b1_ref, w2_ref, b2_ref, o_ref):
    nc = x_ref.shape[0] // CMP_STRIDE
    hid = w1_ref.shape[-1]
    top = jnp.zeros((nc, hid), F32)
    bot = jnp.zeros((nc, hid), F32)
    for j in range(CMP_STRIDE):
        xj = x_ref[pl.ds(j, nc, stride=CMP_STRIDE), :]
        top += jnp.dot((xj + pe_ref[0, j:j + 1, :]).astype(BF16), w1_ref[0, j],
                       preferred_element_type=F32)
        bot += jnp.dot((xj + pe_ref[0, CMP_STRIDE + j:CMP_STRIDE + j + 1, :]).astype(BF16),
                       w1_ref[0, CMP_STRIDE + j], preferred_element_type=F32)
    h = _gelu_tanh(top + pltpu.roll(bot, nc - 1, 0) + b1_ref[0])
    o_ref[0, 0, 0] = (jnp.dot(h.astype(BF16), w2_ref[0], preferred_element_type=F32) + b2_ref[0]).astype(BF16)


def _compress(kv, bsz, seq, pe, w1, b1, w2, b2):
    nc = seq // CMP_STRIDE
    hid = w1.shape[-1]
    w1r = w1.reshape(2, CMP_BLOCK, HEAD_DIM, hid)
    return pl.pallas_call(
        _compress_kernel,
        out_shape=jax.ShapeDtypeStruct((2, bsz, N_KV_HEADS, nc, HEAD_DIM), BF16),
        grid=(2, bsz, N_KV_HEADS),
        in_specs=[pl.BlockSpec((seq, HEAD_DIM), lambda s, b, h: (b, s * N_KV_HEADS + h)),
                  pl.BlockSpec((1, CMP_BLOCK, HEAD_DIM), lambda s, b, h: (s, 0, 0)),
                  pl.BlockSpec((1, CMP_BLOCK, HEAD_DIM, hid), lambda s, b, h: (s, 0, 0, 0)),
                  pl.BlockSpec((1, 1, hid), lambda s, b, h: (s, 0, 0)),
                  pl.BlockSpec((1, hid, HEAD_DIM), lambda s, b, h: (s, 0, 0)),
                  pl.BlockSpec((1, 1, HEAD_DIM), lambda s, b, h: (s, 0, 0))],
        out_specs=pl.BlockSpec((1, 1, 1, nc, HEAD_DIM), lambda s, b, h: (s, b, h, 0, 0)),
        compiler_params=_cparams(("parallel", "parallel", "parallel")),
        name="nsa_compress",
    )(kv, pe, w1r, b1.reshape(2, 1, hid), w2, b2.reshape(2, 1, HEAD_DIM))


def _attn_kernel(q_ref, gate_ref, cos_ref, sin_ref, kc_ref, vc_ref, c2s_ref,
                 ks_ref, vs_ref, kw_ref, vw_ref, o_ref, *, seq):
    qt = pl.program_id(2)
    s0 = qt * Q_TILE
    n_slc = seq // SLC_BLOCK
    n_cmp = kc_ref.shape[-2]
    scale = HEAD_DIM ** -0.5 * LOG2_E
    rows = GQA_GROUP * Q_TILE
    nt = (((1,), (1,)), ((), ()))

    def tile4(x):
        return jnp.concatenate([x] * GQA_GROUP, axis=0)

    q4 = q_ref[...]
    qp = jnp.concatenate([q4[:, g * HEAD_DIM:(g + 1) * HEAD_DIM] for g in range(GQA_GROUP)], axis=0) * scale
    qr_b = _rope(qp, tile4(cos_ref[...]), tile4(sin_ref[...])).astype(BF16)
    qp_b = qp.astype(BF16)
    t_q = s0 + lax.broadcasted_iota(jnp.int32, (Q_TILE, 1), 0)

    n_idx = lax.broadcasted_iota(jnp.int32, (Q_TILE, n_cmp), 1)
    c_ok = (n_idx * CMP_STRIDE + (CMP_BLOCK - 1)) <= t_q
    s_c = lax.dot_general(qp_b, kc_ref[0, 0], nt, preferred_element_type=F32) + tile4(jnp.where(c_ok, 0.0, NEG_INF))
    e_c = jnp.exp2(s_c - jnp.max(s_c, axis=-1, keepdims=True)) * tile4(jnp.where(c_ok, 1.0, 0.0))
    l_c = jnp.sum(e_c, axis=-1, keepdims=True)
    p_c = e_c / jnp.where(l_c > 0.0, l_c, 1.0)
    o_c = jnp.dot(p_c.astype(BF16), vc_ref[0, 0], preferred_element_type=F32)
    p_sum = (p_c[0:Q_TILE] + p_c[Q_TILE:2 * Q_TILE]) + (p_c[2 * Q_TILE:3 * Q_TILE] + p_c[3 * Q_TILE:])
    p_hi = p_sum.astype(BF16)
    r1 = p_sum - p_hi.astype(F32)
    p_mid = r1.astype(BF16)
    p_lo = (r1 - p_mid.astype(F32)).astype(BF16)
    c2s = c2s_ref[...]
    imp_t = (lax.dot_general(c2s, p_hi, nt, preferred_element_type=F32)
             + lax.dot_general(c2s, p_mid, nt, preferred_element_type=F32)
             + lax.dot_general(c2s, p_lo, nt, preferred_element_type=F32))

    sid = lax.broadcasted_iota(jnp.int32, (n_slc, Q_TILE), 0)
    t_l = s0 + lax.broadcasted_iota(jnp.int32, (n_slc, Q_TILE), 1)
    cur = lax.shift_right_logical(t_l, int(math.log2(SLC_BLOCK)))
    causal_blk = sid <= cur
    forced = jnp.where(sid == 0, 1.0, jnp.where(sid == cur, 1.0, jnp.where(sid == cur - 1, 1.0, 0.0)))
    work = jnp.where(causal_blk, imp_t + FORCE_BONUS * forced, NEG_INF)
    sid_f = sid.astype(F32)
    sel_t = jnp.zeros((n_slc, Q_TILE), F32)
    for _ in range(min(N_SELECT, n_slc)):
        mx = jnp.max(work, axis=0, keepdims=True)
        first = jnp.min(jnp.where(work == mx, sid_f, float(n_slc)), axis=0, keepdims=True)
        pick = sid_f == first
        sel_t = jnp.where(pick, 1.0, sel_t)
        work = jnp.where(pick, -3.0e38, work)
    sel_t = jnp.where(causal_blk, sel_t, 0.0)
    unsel = jnp.concatenate([sel_t - 1.0, jnp.zeros((HEAD_DIM - n_slc, Q_TILE), F32)], axis=0).T
    q_aug = jnp.concatenate([qr_b, tile4(unsel.astype(BF16))], axis=1)

    kt = 512

    def slc_tile(j, carry, diag_bias):
        m, l, acc = carry
        k0 = pl.multiple_of(j * kt, kt)
        s = lax.dot_general(q_aug, ks_ref[pl.ds(k0, kt), :], nt, preferred_element_type=F32)
        if diag_bias is not None:
            s = s + diag_bias
        m_new = jnp.maximum(m, jnp.max(s, axis=-1, keepdims=True))
        alpha = jnp.exp2(m - m_new)
        p = jnp.exp2(s - m_new)
        l = alpha * l + jnp.sum(p, axis=-1, keepdims=True)
        acc = alpha * acc + jnp.dot(p.astype(BF16), vs_ref[pl.ds(k0, kt), :], preferred_element_type=F32)
        return m_new, l, acc

    slab = WINDOW + Q_TILE
    w0 = pl.multiple_of(jnp.maximum(s0 - WINDOW, 0), Q_TILE)
    dist = t_q - (w0 + lax.broadcasted_iota(jnp.int32, (Q_TILE, slab), 1))
    w_bias = jnp.where(dist >= 0, jnp.where(dist < WINDOW, 0.0, NEG_INF), NEG_INF)
    s_w = lax.dot_general(qr_b, kw_ref[pl.ds(w0, slab), :], nt, preferred_element_type=F32) + tile4(w_bias)
    e_w = jnp.exp2(s_w - jnp.max(s_w, axis=-1, keepdims=True))
    o_w = (jnp.dot(e_w.astype(BF16), vw_ref[pl.ds(w0, slab), :], preferred_element_type=F32)
           / jnp.sum(e_w, axis=-1, keepdims=True))

    j_diag = lax.shift_right_logical(s0, int(math.log2(kt)))
    kpos = j_diag * kt + lax.broadcasted_iota(jnp.int32, (Q_TILE, kt), 1)
    init = (jnp.full((rows, 1), NEG_INF, F32), jnp.zeros((rows, 1), F32), jnp.zeros((rows, HEAD_DIM), F32))
    carry = slc_tile(j_diag, init, tile4(jnp.where(kpos <= t_q, 0.0, NEG_INF)))
    carry = lax.fori_loop(0, lax.shift_right_logical(j_diag, 1),
                          lambda p, c: slc_tile(2 * p + 1, slc_tile(2 * p, c, None), None), carry)
    _, l_s, acc_s = lax.cond(jnp.bitwise_and(j_diag, 1) == 1,
                             lambda c: slc_tile(j_diag - 1, c, None), lambda c: c, carry)
    o_s = acc_s / l_s

    sg = _sigmoid(gate_ref[...])
    for g in range(GQA_GROUP):
        r = slice(g * Q_TILE, (g + 1) * Q_TILE)
        o_ref[:, g * HEAD_DIM:(g + 1) * HEAD_DIM] = (
            sg[:, g:g + 1] * o_c[r]
            + sg[:, GQA_GROUP + g:GQA_GROUP + g + 1] * o_s[r]
            + sg[:, 2 * GQA_GROUP + g:2 * GQA_GROUP + g + 1] * o_w[r])


def _cmp_to_slc_t(seq):
    n_cmp = seq // CMP_STRIDE - 1
    n_slc = seq // SLC_BLOCK
    cmp_start = jnp.arange(n_cmp) * CMP_STRIDE
    slc_start = jnp.arange(n_slc) * SLC_BLOCK
    overlap = jnp.clip(jnp.minimum(cmp_start[:, None] + CMP_BLOCK, slc_start[None, :] + SLC_BLOCK)
                       - jnp.maximum(cmp_start[:, None], slc_start[None, :]), 0, None)
    c2s = overlap.astype(F32) / CMP_BLOCK
    return jnp.pad(c2s, ((0, 1), (0, 0))).T.astype(BF16)


def _nsa_attention(q, gates, kc, vc, ks, vs, kw, vw, bsz, seq):
    m = q.shape[0]
    nq = seq // Q_TILE
    nc = seq // CMP_STRIDE
    cos_t, sin_t = _rope_tables(seq, 1)
    c2s_t = _cmp_to_slc_t(seq)
    gw = GQA_GROUP * HEAD_DIM
    kv_spec = pl.BlockSpec((seq, HEAD_DIM), lambda b, h, t: (b, h))
    cmp_spec = pl.BlockSpec((1, 1, nc, HEAD_DIM), lambda b, h, t: (b, h, 0, 0))
    tab_spec = pl.BlockSpec((Q_TILE, HEAD_DIM), lambda b, h, t: (t, 0))
    return pl.pallas_call(
        functools.partial(_attn_kernel, seq=seq),
        out_shape=jax.ShapeDtypeStruct((m, N_HEADS * HEAD_DIM), F32),
        grid=(bsz, N_KV_HEADS, nq),
        in_specs=[pl.BlockSpec((Q_TILE, gw), lambda b, h, t: (b * nq + t, h)),
                  pl.BlockSpec((Q_TILE, LANES), lambda b, h, t: (b * nq + t, h)),
                  tab_spec, tab_spec, cmp_spec, cmp_spec,
                  pl.BlockSpec(c2s_t.shape, lambda b, h, t: (0, 0)),
                  pl.BlockSpec((seq, 2 * HEAD_DIM), lambda b, h, t: (b, h)), kv_spec, kv_spec, kv_spec],
        out_specs=pl.BlockSpec((Q_TILE, gw), lambda b, h, t: (b * nq + t, h)),
        compiler_params=_cparams(("parallel", "parallel", "arbitrary")),
        name="nsa_attention",
    )(q, gates, cos_t, sin_t, kc, vc, c2s_t, ks, vs, kw, vw)


def _s5_operators(a_re, a_im, log_dt, b_re, b_im, c_re, c_im):
    hp = lax.Precision.HIGHEST
    n_g = a_re.shape[0]
    gpb = SSM_GROUPS_PER_BLOCK
    n_gb = n_g // gpb
    t_c, hh, pp = SSM_CHUNK, SSM_GROUP, SSM_STATE
    lam = lax.complex(a_re.astype(F32), a_im.astype(F32))
    dt = jnp.exp(log_dt.astype(F32))[:, None]
    a_bar = jnp.exp(lam * dt)
    b_bar = ((a_bar - 1.0) / lam)[:, :, None] * lax.complex(b_re.astype(F32), b_im.astype(F32))
    c_mat = lax.complex(c_re.astype(F32), c_im.astype(F32))
    steps = jnp.arange(t_c + 1, dtype=F32)[:, None, None]
    pw = jnp.exp((lam * dt)[None] * steps)

    def blockdiag(x, r, c):
        tile = jnp.tile(jnp.eye(c, dtype=BF16), (1, gpb))
        rows = jnp.arange(gpb * r)[:, None] // r
        cols = jnp.arange(gpb * c)[None, :] // c
        tiled = jnp.einsum('...c,cd->...d', x.astype(BF16), tile, preferred_element_type=BF16)
        return jnp.where(rows == cols, tiled, jnp.zeros((), BF16))

    k_lag = jnp.einsum('ghp,kgp,gpi->kgih', c_mat, pw[:t_c], b_bar, precision=hp).real
    w_intra = blockdiag(k_lag.reshape(t_c, n_gb, gpb * hh, hh), hh, hh)
    w_intra = w_intra.transpose(1, 2, 0, 3).reshape(n_gb, LANES, t_c * LANES)
    p_in = jnp.swapaxes(pw[t_c - 1::-1][:, :, :, None] * b_bar[None], -1, -2)
    p_in = p_in.reshape(t_c, n_gb, gpb * hh, pp)
    s_in = jnp.concatenate([blockdiag(p_in.real, hh, pp), blockdiag(p_in.imag, hh, pp)], axis=-1)
    s_in = s_in.transpose(1, 0, 2, 3).reshape(n_gb, t_c * LANES, 2 * gpb * pp)
    m_out = jnp.swapaxes(c_mat[None] * pw[1:, :, None, :], -1, -2)
    m_out = m_out.reshape(t_c, n_gb, gpb * pp, hh)
    s_out = jnp.concatenate([blockdiag(m_out.real, pp, hh), blockdiag(-m_out.imag, pp, hh)], axis=2)
    s_out = s_out.transpose(1, 2, 0, 3).reshape(n_gb, 2 * gpb * pp, t_c * LANES)
    k_steps = jnp.arange(SCAN_TABLE_ROWS, dtype=F32)[:, None, None] * float(t_c)
    a_pow = jnp.exp((lam * dt)[None] * k_steps).reshape(SCAN_TABLE_ROWS, n_gb, gpb * pp).swapaxes(0, 1)
    return s_in, w_intra, s_out, a_pow.real.astype(F32), a_pow.imag.astype(F32)


def _s5_kernel(u_ref, sin_ref, wi_ref, sout_ref, ar_ref, ai_ref, d_ref, o_ref, xin_ref, xprev_ref, wt_ref):
    t_c = SSM_CHUNK
    nc = u_ref.shape[0] // t_c
    half = ar_ref.shape[-1]

    @pl.when(pl.program_id(1) == 0)
    def _():
        wt_ref[...] = jnp.zeros(wt_ref.shape, wt_ref.dtype)
        for tau in range(t_c):
            wt_ref[tau * LANES:(tau + 1) * LANES, tau * LANES:] = wi_ref[0, :, :(t_c - tau) * LANES]

    def step_rows(ref, t):
        return ref.at[pl.ds(t, nc, stride=t_c), :]

    u_flat = jnp.concatenate([step_rows(u_ref, tau)[...].astype(BF16) for tau in range(t_c)], axis=1)
    xin = jnp.dot(u_flat, sin_ref[0], preferred_element_type=F32)
    x_r, x_i = xin[:, :half], xin[:, half:]
    r_in_grp = jnp.bitwise_and(lax.broadcasted_iota(jnp.int32, (nc, 1), 0), SCAN_GROUP - 1)
    d = 1
    while d < SCAN_GROUP:
        keep = jnp.where(r_in_grp >= d, 1.0, 0.0)
        s_r, s_i = pltpu.roll(x_r, d, 0) * keep, pltpu.roll(x_i, d, 0) * keep
        ad_r, ad_i = ar_ref[0, d:d + 1, :], ai_ref[0, d:d + 1, :]
        x_r, x_i = x_r + ad_r * s_r - ad_i * s_i, x_i + ad_r * s_i + ad_i * s_r
        d *= 2
    keep = jnp.where(r_in_grp >= 1, 1.0, 0.0)
    xin_ref[:, :half] = x_r
    xin_ref[:, half:] = x_i
    xprev_ref[:, :half] = pltpu.roll(x_r, 1, 0) * keep
    xprev_ref[:, half:] = pltpu.roll(x_i, 1, 0) * keep
    ak_r, ak_i = ar_ref[0, 0:SCAN_GROUP, :], ai_ref[0, 0:SCAN_GROUP, :]
    ag_r, ag_i = ar_ref[0, SCAN_GROUP:SCAN_GROUP + 1, :], ai_ref[0, SCAN_GROUP:SCAN_GROUP + 1, :]

    def group(g, carry):
        c_r, c_i = carry
        rows = pl.ds(pl.multiple_of(g * SCAN_GROUP, SCAN_GROUP), SCAN_GROUP)
        xprev_ref[rows, :half] = xprev_ref[rows, :half] + ak_r * c_r - ak_i * c_i
        xprev_ref[rows, half:] = xprev_ref[rows, half:] + ak_r * c_i + ak_i * c_r
        last = xin_ref[pl.ds(g * SCAN_GROUP + SCAN_GROUP - 1, 1), :]
        return (last[:, :half] + ag_r * c_r - ag_i * c_i, last[:, half:] + ag_r * c_i + ag_i * c_r)

    zero = jnp.zeros((1, half), F32)
    lax.fori_loop(0, nc // SCAN_GROUP, group, (zero, zero))
    h_w = (t_c // 2) * LANES
    x_prev = xprev_ref[...].astype(BF16)
    y_lo = (jnp.dot(u_flat[:, :h_w], wt_ref[:h_w, :h_w], preferred_element_type=F32)
            + jnp.dot(x_prev, sout_ref[0, :, :h_w], preferred_element_type=F32))
    y_hi = (jnp.dot(u_flat, wt_ref[:, h_w:], preferred_element_type=F32)
            + jnp.dot(x_prev, sout_ref[0, :, h_w:], preferred_element_type=F32))
    for t in range(t_c):
        y_t = (y_lo if t < t_c // 2 else y_hi)[:, (t % (t_c // 2)) * LANES:(t % (t_c // 2) + 1) * LANES]
        step_rows(o_ref, t)[...] = _gelu_tanh(y_t + d_ref[0] * step_rows(u_ref, t)[...])


def _s5_scan(u, bsz, seq, ops, d_skip):
    d_ssm = u.shape[1]
    t_c = SSM_CHUNK
    s_in, w_intra, s_out, a_r, a_i = ops
    n_gb = d_ssm // LANES
    nc = seq // t_c
    st = s_in.shape[-1]
    return pl.pallas_call(
        _s5_kernel,
        out_shape=jax.ShapeDtypeStruct(u.shape, F32),
        grid=(n_gb, bsz),
        in_specs=[pl.BlockSpec((seq, LANES), lambda g, b: (b, g)),
                  pl.BlockSpec((1,) + s_in.shape[1:], lambda g, b: (g, 0, 0)),
                  pl.BlockSpec((1,) + w_intra.shape[1:], lambda g, b: (g, 0, 0)),
                  pl.BlockSpec((1,) + s_out.shape[1:], lambda g, b: (g, 0, 0)),
                  pl.BlockSpec((1, SCAN_TABLE_ROWS, st // 2), lambda g, b: (g, 0, 0)),
                  pl.BlockSpec((1, SCAN_TABLE_ROWS, st // 2), lambda g, b: (g, 0, 0)),
                  pl.BlockSpec((1, 1, LANES), lambda g, b: (g, 0, 0))],
        out_specs=pl.BlockSpec((seq, LANES), lambda g, b: (b, g)),
        scratch_shapes=[pltpu.VMEM((nc, st), F32), pltpu.VMEM((nc, st), F32),
                        pltpu.VMEM((t_c * LANES, t_c * LANES), BF16)],
        compiler_params=_cparams(("arbitrary", "arbitrary")),
        name="s5_scan",
    )(u, s_in, w_intra, s_out, a_r, a_i, d_skip.reshape(n_gb, 1, LANES))


def _mixnorm_kernel(a_ref, s_ref, ga_ref, gs_ref, o_ref):
    da = a_ref.shape[-1]
    a = a_ref[...]
    s = s_ref[...]
    o_ref[:, :da] = (a * lax.rsqrt(jnp.mean(a * a, axis=-1, keepdims=True) + EPS) * ga_ref[...]).astype(BF16)
    o_ref[:, da:] = (s * lax.rsqrt(jnp.mean(s * s, axis=-1, keepdims=True) + EPS) * gs_ref[...]).astype(BF16)


def _mixnorm(attn, ssm, g_attn, g_ssm, tm=256):
    m, da = attn.shape
    ds = ssm.shape[1]
    return pl.pallas_call(
        _mixnorm_kernel,
        out_shape=jax.ShapeDtypeStruct((m, da + ds), BF16),
        grid=(m // tm,),
        in_specs=[pl.BlockSpec((tm, da), lambda i: (i, 0)), pl.BlockSpec((tm, ds), lambda i: (i, 0)),
                  pl.BlockSpec((1, da), lambda i: (0, 0)), pl.BlockSpec((1, ds), lambda i: (0, 0))],
        out_specs=pl.BlockSpec((tm, da + ds), lambda i: (i, 0)),
        compiler_params=_cparams(("parallel",)),
        name="mixnorm",
    )(attn, ssm, g_attn.reshape(1, da), g_ssm.reshape(1, ds))


def _split_bf16(x):
    hi = x.astype(BF16)
    return hi, (x - hi.astype(F32)).astype(BF16)


def _lane_max(x):
    return jnp.max(x, axis=-1, keepdims=True)


def _first_lane_of(cond_val, lane_f):
    return jnp.min(jnp.where(cond_val > 0.0, lane_f, float(LANES)), axis=-1, keepdims=True)


def _ffn_norm_router_kernel(h_ref, g_ref, whi_ref, wlo_ref, b_ref, hn_ref, route_ref):
    x = h_ref[...]
    hn = x * lax.rsqrt(jnp.mean(x * x, axis=-1, keepdims=True) + EPS) * g_ref[...]
    hn_ref[...] = _pack_bf16_pairs(hn)
    hi, lo = _split_bf16(hn)
    logits = (jnp.dot(hi, whi_ref[...], preferred_element_type=F32)
              + jnp.dot(lo, whi_ref[...], preferred_element_type=F32)
              + jnp.dot(hi, wlo_ref[...], preferred_element_type=F32)) + b_ref[...]
    lane = lax.broadcasted_iota(jnp.int32, logits.shape, 1)
    lane_f = lane.astype(F32)
    in_c = jnp.where(lane < N_EXPERT_GROUPS, 1.0, 0.0)
    c_l = jnp.where(in_c > 0.0, logits, NEG_INF)
    c_e = jnp.exp(c_l - _lane_max(c_l)) * in_c
    p_c = c_e / jnp.sum(c_e, axis=-1, keepdims=True)
    w_grp = _lane_max(p_c)
    grp = _first_lane_of(jnp.where(p_c == w_grp, in_c, 0.0), lane_f)
    lo_lane = float(N_EXPERT_GROUPS) + grp * float(EXPERTS_PER_GROUP)
    in_f = jnp.where(lane_f >= lo_lane, jnp.where(lane_f < lo_lane + float(EXPERTS_PER_GROUP), 1.0, 0.0), 0.0)
    f_l = jnp.where(in_f > 0.0, logits, NEG_INF)
    f_e = jnp.exp(f_l - _lane_max(f_l)) * in_f
    p_f = f_e / jnp.sum(f_e, axis=-1, keepdims=True)
    p_f = jnp.where(in_f > 0.0, p_f, -1.0)
    p1 = _lane_max(p_f)
    i1 = _first_lane_of(jnp.where(p_f == p1, in_f, 0.0), lane_f)
    rest = jnp.where(lane_f == i1, -1.0, p_f)
    p2 = _lane_max(rest)
    i2 = _first_lane_of(jnp.where(rest == p2, in_f, 0.0) * jnp.where(lane_f == i1, 0.0, 1.0), lane_f)
    denom = p1 + p2
    vals = (w_grp * p1 / denom, w_grp * p2 / denom,
            i1 - float(N_EXPERT_GROUPS), i2 - float(N_EXPERT_GROUPS))
    out = jnp.zeros(logits.shape, F32)
    for k, v in enumerate(vals):
        out = jnp.where(lane == k, v, out)
    route_ref[...] = out


def _ffn_norm_router(h, g, w_router, b_router, tm=256):
    m, d = h.shape
    n = w_router.shape[1]
    w_hi, w_lo = _split_bf16(w_router)
    return pl.pallas_call(
        _ffn_norm_router_kernel,
        out_shape=(jax.ShapeDtypeStruct((m, d // 2), jnp.uint32), jax.ShapeDtypeStruct((m, n), F32)),
        grid=(m // tm,),
        in_specs=[pl.BlockSpec((tm, d), lambda i: (i, 0)), pl.BlockSpec((1, d), lambda i: (0, 0)),
                  pl.BlockSpec((d, n), lambda i: (0, 0)), pl.BlockSpec((d, n), lambda i: (0, 0)),
                  pl.BlockSpec((1, n), lambda i: (0, 0))],
        out_specs=(pl.BlockSpec((tm, d // 2), lambda i: (i, 0)), pl.BlockSpec((tm, n), lambda i: (i, 0))),
        compiler_params=_cparams(("parallel",)),
        name="ffn_norm_router",
    )(h, g.reshape(1, d), w_hi, w_lo, b_router.reshape(1, n))


def _dispatch_plan(expert, tm):
    n_tok = expert.shape[0]
    n_assign = n_tok * TOP_K
    n_blocks = -(-(n_assign + N_EXPERTS * (MOE_BLOCK - 1)) // MOE_BLOCK)
    n_rows = n_blocks * MOE_BLOCK
    flat = expert.reshape(n_assign)
    ids = jnp.arange(n_assign, dtype=jnp.int32)
    _, order = lax.sort_key_val(flat, ids)
    _, inv = lax.sort_key_val(order, ids)
    e_ids = jnp.arange(N_EXPERTS, dtype=jnp.int32)
    counts = jnp.sum((flat[:, None] == e_ids[None, :]).astype(jnp.int32), axis=0)
    start = jnp.cumsum(counts) - counts
    padded = (counts + MOE_BLOCK - 1) // MOE_BLOCK * MOE_BLOCK
    pad_end = jnp.cumsum(padded)
    pad_start = pad_end - padded
    rows = jnp.arange(n_rows, dtype=jnp.int32)
    blk_row0 = jnp.arange(n_blocks, dtype=jnp.int32) * MOE_BLOCK
    blk_e = jnp.minimum(jnp.sum((pad_end[None, :] <= blk_row0[:, None]).astype(jnp.int32), axis=1), N_EXPERTS - 1)
    shift = pad_start - start
    row_shift = jnp.repeat(shift[blk_e], MOE_BLOCK)
    row_end = jnp.repeat((pad_start + counts)[blk_e], MOE_BLOCK)
    valid = (rows < row_end) & (rows < pad_end[-1])
    src = jnp.clip(rows - row_shift, 0, n_assign - 1)
    buf_tok = jnp.where(valid, order[src] // TOP_K, 0).astype(jnp.int32)
    dest = (inv + shift[flat]).astype(jnp.int32)
    dest = dest.reshape(n_tok // tm, tm, TOP_K).transpose(0, 2, 1).reshape(-1)
    plan = jnp.concatenate([pad_start // MOE_BLOCK, padded // MOE_BLOCK,
                            pad_end[-1:] // MOE_BLOCK]).astype(jnp.int32)
    return plan, buf_tok, dest, n_blocks


def _zero_fill_spare(plan_ref, zero_vmem, out_hbm, col0, width, sem, n_blocks):
    n_used = plan_ref[2 * N_EXPERTS]

    def cp(blk):
        return pltpu.make_async_copy(
            zero_vmem, out_hbm.at[pl.ds(pl.multiple_of(blk * MOE_BLOCK, MOE_BLOCK), MOE_BLOCK), pl.ds(col0, width)],
            sem)

    def start(blk, c):
        cp(blk).start(priority=SIDE_DMA_PRIORITY)
        return c

    def wait(blk, c):
        cp(blk).wait()
        return c

    lax.fori_loop(n_used, n_blocks, start, 0)
    lax.fori_loop(n_used, n_blocks, wait, 0)


def _pack_bf16_pairs(x):
    w = x.shape[1] // 2
    hi = lax.bitcast_convert_type(x[:, :w].astype(BF16).astype(F32), jnp.uint32)
    lo = lax.bitcast_convert_type(x[:, w:].astype(BF16).astype(F32), jnp.uint32)
    return hi | (lo >> 16)


def _unpack_bf16_pairs(p):
    first = lax.bitcast_convert_type(p & jnp.uint32(0xFFFF0000), F32)
    second = lax.bitcast_convert_type(p << 16, F32)
    return jnp.concatenate([first, second], axis=1).astype(BF16)


def _moe_up_kernel(plan_ref, tok_ref, x_hbm, *rest, n_blocks):
    w_refs = rest[:2 * W_DMA_SPLIT]
    act_hbm, wg_b, wu_b, xs, gbuf, obuf, afull, gsem, osem, fsem = rest[2 * W_DMA_SPLIT:]
    wg_refs, wu_refs = w_refs[:W_DMA_SPLIT], w_refs[W_DMA_SPLIT:]
    e = pl.program_id(0)
    c = pl.program_id(1)
    n_c = pl.num_programs(1)
    tn = wg_b.shape[1]
    col0 = pl.multiple_of(c * tn, tn)
    b0 = plan_ref[e]
    nb = plan_ref[N_EXPERTS + e]
    n_res = jnp.minimum(nb, MOE_X_CAP)

    def row_copy(blk, r, slot):
        tok = tok_ref[blk * MOE_BLOCK + r]
        return pltpu.make_async_copy(x_hbm.at[pl.ds(tok, 1), :], gbuf.at[slot, pl.ds(r, 1), :], gsem.at[slot])

    def gather_start(blk, slot):
        lax.fori_loop(0, MOE_BLOCK, lambda r, cr: (row_copy(blk, r, slot).start(priority=SIDE_DMA_PRIORITY), cr)[1], 0, unroll=DMA_UNROLL)

    def gather_wait(blk, slot):
        lax.fori_loop(0, MOE_BLOCK, lambda r, cr: (row_copy(blk, r, slot).wait(), cr)[1], 0, unroll=DMA_UNROLL)

    def gather_expert(first_blk, n_blk):
        lax.fori_loop(0, jnp.minimum(n_blk, MOE_X_CAP), lambda k, cr: (gather_start(first_blk + k, k), cr)[1], 0)

    def gather_now(blk):
        gather_start(blk, MOE_X_CAP)
        gather_wait(blk, MOE_X_CAP)
        return _unpack_bf16_pairs(gbuf[MOE_X_CAP])

    @pl.when(jnp.logical_and(c == 1, e + 1 < N_EXPERTS))
    def _():
        e_next = jnp.minimum(e + 1, N_EXPERTS - 1)
        gather_expert(plan_ref[e_next], plan_ref[N_EXPERTS + e_next])

    def rows_of(blk):
        return pl.ds(pl.multiple_of(blk * MOE_BLOCK, MOE_BLOCK), MOE_BLOCK)

    def out_copy(blk, slot):
        return pltpu.make_async_copy(obuf.at[slot], act_hbm.at[rows_of(blk), pl.ds(col0, tn)], osem.at[slot])

    def full_copy(j):
        return pltpu.make_async_copy(afull.at[j], act_hbm.at[rows_of(b0 + j), :], fsem)

    def act_of(x):
        g = jnp.dot(x, wg_b[...], preferred_element_type=F32)
        u = jnp.dot(x, wu_b[...], preferred_element_type=F32)
        return (g * _sigmoid(g) * u).astype(BF16)

    def emit_resident(x, j):
        act = act_of(x)
        for cc in range(act_hbm.shape[1] // tn):
            @pl.when(c == cc)
            def _():
                afull[j, :, cc * tn:(cc + 1) * tn] = act

    def emit_streamed(x, j):
        i = j - n_res
        slot = lax.rem(i, 2)
        act = act_of(x)

        @pl.when(i >= 2)
        def _():
            out_copy(b0 + j - 2, slot).wait()

        obuf[slot] = act
        out_copy(b0 + j, slot).start(priority=SIDE_DMA_PRIORITY)

    @pl.when(nb > 0)
    def _():
        rows = wg_b.shape[0] // W_DMA_SPLIT
        for k in range(W_DMA_SPLIT):
            wg_b[k * rows:(k + 1) * rows, :] = wg_refs[k][0].astype(BF16)
            wu_b[k * rows:(k + 1) * rows, :] = wu_refs[k][0].astype(BF16)

    @pl.when(c == 0)
    def _():
        @pl.when(e == 0)
        def _():
            gather_expert(b0, nb)

        def landed(j, carry):
            gather_wait(b0 + j, j)
            x = _unpack_bf16_pairs(gbuf[j])
            xs[j] = x
            emit_resident(x, j)
            return carry

        def overflow(j, carry):
            emit_streamed(gather_now(b0 + j), j)
            return carry

        lax.fori_loop(0, n_res, landed, 0)
        lax.fori_loop(n_res, nb, overflow, 0)

    @pl.when(c > 0)
    def _():
        def resident(j, carry):
            emit_resident(xs[j], j)
            return carry

        def overflow(j, carry):
            emit_streamed(gather_now(b0 + j), j)
            return carry

        lax.fori_loop(0, n_res, resident, 0)
        lax.fori_loop(n_res, nb, overflow, 0)

    n_str = nb - n_res

    @pl.when(n_str >= 2)
    def _():
        out_copy(b0 + nb - 2, lax.rem(n_str, 2)).wait()

    @pl.when(n_str >= 1)
    def _():
        out_copy(b0 + nb - 1, lax.rem(n_str + 1, 2)).wait()

    @pl.when(c == n_c - 1)
    def _():
        lax.fori_loop(0, n_res, lambda j, cr: (full_copy(j).start(priority=SIDE_DMA_PRIORITY), cr)[1], 0)
        lax.fori_loop(0, n_res, lambda j, cr: (full_copy(j).wait(), cr)[1], 0)

    @pl.when(jnp.logical_and(e == N_EXPERTS - 1, c == n_c - 1))
    def _():
        afull[0] = jnp.zeros(afull.shape[1:], afull.dtype)
        _zero_fill_spare(plan_ref, afull.at[0], act_hbm, 0, act_hbm.shape[1], fsem, n_blocks)


def _moe_up(hn_packed, w_gate, w_up, plan, buf_tok, n_blocks, n_chunks=4):
    d = w_gate.shape[-2]
    d_e = w_gate.shape[-1]
    tn = d_e // n_chunks
    w_specs = [pl.BlockSpec((1, d // W_DMA_SPLIT, tn), functools.partial(lambda e, c, plan, tok, k: (e, k, c), k=k))
               for k in range(W_DMA_SPLIT)]
    return pl.pallas_call(
        functools.partial(_moe_up_kernel, n_blocks=n_blocks),
        out_shape=jax.ShapeDtypeStruct((n_blocks * MOE_BLOCK, d_e), BF16),
        grid_spec=pltpu.PrefetchScalarGridSpec(
            num_scalar_prefetch=2,
            grid=(N_EXPERTS, n_chunks),
            in_specs=[pl.BlockSpec(memory_space=pl.ANY)] + w_specs + w_specs,
            out_specs=pl.BlockSpec(memory_space=pl.ANY),
            scratch_shapes=[pltpu.VMEM((d, tn), BF16), pltpu.VMEM((d, tn), BF16),
                            pltpu.VMEM((MOE_X_CAP, MOE_BLOCK, d), BF16),
                            pltpu.VMEM((GATHER_SLOTS, MOE_BLOCK, d // 2), jnp.uint32),
                            pltpu.VMEM((2, MOE_BLOCK, tn), BF16),
                            pltpu.VMEM((MOE_X_CAP, MOE_BLOCK, d_e), BF16),
                            pltpu.SemaphoreType.DMA((GATHER_SLOTS,)), pltpu.SemaphoreType.DMA((2,)),
                            pltpu.SemaphoreType.DMA(())]),
        compiler_params=_cparams(("arbitrary", "arbitrary")),
        name="moe_up",
    )(plan, buf_tok, hn_packed, *([w_gate] * W_DMA_SPLIT), *([w_up] * W_DMA_SPLIT))


def _moe_down_kernel(plan_ref, act_hbm, *rest, n_blocks):
    wd_refs, (y_hbm, wd_b, abuf, obuf, asem, osem) = rest[:W_DMA_SPLIT], rest[W_DMA_SPLIT:]
    e = pl.program_id(0)
    b0 = plan_ref[e]
    nb = plan_ref[N_EXPERTS + e]
    n_res = jnp.minimum(nb, DOWN_ACT_CAP)
    par = lax.rem(e, 2)

    def rows_of(blk):
        return pl.ds(pl.multiple_of(blk * MOE_BLOCK, MOE_BLOCK), MOE_BLOCK)

    def in_copy(blk, half, slot):
        return pltpu.make_async_copy(act_hbm.at[rows_of(blk), :], abuf.at[half, slot], asem.at[half, slot])

    def out_copy(blk, slot):
        return pltpu.make_async_copy(obuf.at[slot], y_hbm.at[rows_of(blk), :], osem.at[slot])

    def request_expert(first_blk, n_blk, half):
        lax.fori_loop(0, jnp.minimum(n_blk, DOWN_ACT_CAP),
                      lambda k, cr: (in_copy(first_blk + k, half, k).start(priority=SIDE_DMA_PRIORITY), cr)[1], 0)

    @pl.when(e == 0)
    def _():
        request_expert(b0, nb, 0)

    @pl.when(e + 1 < N_EXPERTS)
    def _():
        e_next = jnp.minimum(e + 1, N_EXPERTS - 1)
        request_expert(plan_ref[e_next], plan_ref[N_EXPERTS + e_next], 1 - par)

    @pl.when(nb > 0)
    def _():
        rows = wd_b.shape[0] // W_DMA_SPLIT
        for k in range(W_DMA_SPLIT):
            wd_b[k * rows:(k + 1) * rows, :] = wd_refs[k][0].astype(BF16)

    def emit(a, j):
        slot = lax.rem(j, DOWN_OUT_SLOTS)
        y = jnp.dot(a, wd_b[...], preferred_element_type=F32)

        @pl.when(j >= DOWN_OUT_SLOTS)
        def _():
            out_copy(b0 + j - DOWN_OUT_SLOTS, slot).wait()

        obuf[slot] = _pack_bf16_pairs(y)
        out_copy(b0 + j, slot).start(priority=SIDE_DMA_PRIORITY)

    def landed(j, carry):
        in_copy(b0 + j, par, j).wait()
        emit(abuf[par, j], j)
        return carry

    def overflow(j, carry):
        in_copy(b0 + j, par, 0).start(priority=SIDE_DMA_PRIORITY)
        in_copy(b0 + j, par, 0).wait()
        emit(abuf[par, 0], j)
        return carry

    lax.fori_loop(0, n_res, landed, 0)
    lax.fori_loop(n_res, nb, overflow, 0)
    lax.fori_loop(jnp.maximum(nb - DOWN_OUT_SLOTS, 0), nb,
                  lambda j, cr: (out_copy(b0 + j, lax.rem(j, DOWN_OUT_SLOTS)).wait(), cr)[1], 0)

    @pl.when(e == N_EXPERTS - 1)
    def _():
        obuf[0] = jnp.zeros(obuf.shape[1:], obuf.dtype)
        _zero_fill_spare(plan_ref, obuf.at[0], y_hbm, 0, y_hbm.shape[1], osem.at[0], n_blocks)


def _moe_down(act, w_down, plan, n_blocks):
    n_rows, d_e = act.shape
    d = w_down.shape[-1]
    return pl.pallas_call(
        functools.partial(_moe_down_kernel, n_blocks=n_blocks),
        out_shape=jax.ShapeDtypeStruct((n_rows, d // 2), jnp.uint32),
        grid_spec=pltpu.PrefetchScalarGridSpec(
            num_scalar_prefetch=1,
            grid=(N_EXPERTS,),
            in_specs=[pl.BlockSpec(memory_space=pl.ANY)] + [
                pl.BlockSpec((1, d_e // W_DMA_SPLIT, d), functools.partial(lambda e, plan, k: (e, k, 0), k=k))
                for k in range(W_DMA_SPLIT)],
            out_specs=pl.BlockSpec(memory_space=pl.ANY),
            scratch_shapes=[pltpu.VMEM((d_e, d), BF16),
                            pltpu.VMEM((2, DOWN_ACT_CAP, MOE_BLOCK, d_e), BF16),
                            pltpu.VMEM((DOWN_OUT_SLOTS, MOE_BLOCK, d // 2), jnp.uint32),
                            pltpu.SemaphoreType.DMA((2, DOWN_ACT_CAP)), pltpu.SemaphoreType.DMA((DOWN_OUT_SLOTS,))]),
        compiler_params=_cparams(("arbitrary",)),
        name="moe_down",
    )(plan, act, *([w_down] * W_DMA_SPLIT))


def _combine_kernel(dest_ref, h_ref, route_ref, gn_ref, y_hbm, o_ref, rows_ref, sem):
    tm = h_ref.shape[0]
    n_rows = tm * TOP_K
    i = pl.program_id(0)
    n = pl.num_programs(0)

    def row_copy(step, r, slot):
        src = dest_ref[step * n_rows + r]
        return pltpu.make_async_copy(y_hbm.at[pl.ds(src, 1), :], rows_ref.at[slot, pl.ds(r, 1), :], sem.at[slot])

    def gather_start(step, slot):
        lax.fori_loop(0, n_rows, lambda r, cr: (row_copy(step, r, slot).start(priority=SIDE_DMA_PRIORITY), cr)[1], 0, unroll=DMA_UNROLL)

    def gather_wait(step, slot):
        lax.fori_loop(0, n_rows, lambda r, cr: (row_copy(step, r, slot).wait(), cr)[1], 0, unroll=DMA_UNROLL)

    slot = lax.rem(i, 2)

    @pl.when(i == 0)
    def _():
        gather_start(0, 0)

    nxt = jnp.minimum(i + 1, n - 1)
    for r in range(n_rows):
        row_copy(nxt, r, 1 - slot).start(priority=SIDE_DMA_PRIORITY)

    gather_wait(i, slot)
    route = route_ref[...]
    rows = rows_ref[slot]
    g0, g1 = route[:, 0:1], route[:, 1:2]
    r0, r1 = rows[0:tm, :], rows[tm:2 * tm, :]
    half = rows.shape[1]

    def first(p):
        return lax.bitcast_convert_type(p & jnp.uint32(0xFFFF0000), F32)

    def second(p):
        return lax.bitcast_convert_type(p << 16, F32)

    h = jnp.concatenate([h_ref[:, :half] + g0 * first(r0) + g1 * first(r1),
                         h_ref[:, half:] + g0 * second(r0) + g1 * second(r1)], axis=1)
    o_ref[...] = h * lax.rsqrt(jnp.mean(h * h, axis=-1, keepdims=True) + EPS) * gn_ref[...]

    @pl.when(i == n - 1)
    def _():
        gather_wait(nxt, 1 - slot)


def _combine_final(h, route, dest, y_buf, g_final, tm=128):
    m, d = h.shape
    return pl.pallas_call(
        _combine_kernel,
        out_shape=jax.ShapeDtypeStruct((m, d), F32),
        grid_spec=pltpu.PrefetchScalarGridSpec(
            num_scalar_prefetch=1,
            grid=(m // tm,),
            in_specs=[pl.BlockSpec((tm, d), lambda i, dst: (i, 0)),
                      pl.BlockSpec((tm, LANES), lambda i, dst: (i, 0)),
                      pl.BlockSpec((1, d), lambda i, dst: (0, 0)),
                      pl.BlockSpec(memory_space=pl.ANY)],
            out_specs=pl.BlockSpec((tm, d), lambda i, dst: (i, 0)),
            scratch_shapes=[pltpu.VMEM((2, TOP_K * tm, d // 2), jnp.uint32), pltpu.SemaphoreType.DMA((2,))]),
        compiler_params=_cparams(("arbitrary",)),
        name="moe_combine_final",
    )(dest, h, route, g_final.reshape(1, d), y_buf)


def _gate_column_order():
    cols = []
    for hk in range(N_KV_HEADS):
        blk = [-1] * LANES
        for br in range(N_BRANCHES):
            for g in range(GQA_GROUP):
                blk[br * GQA_GROUP + g] = (hk * GQA_GROUP + g) * N_BRANCHES + br
        cols.extend(blk)
    return cols


def kernel(x, norm_mix, w_in, cmp_pe_k, cmp_w1_k, cmp_b1_k, cmp_w2_k, cmp_b2_k, cmp_pe_v, cmp_w1_v, cmp_b1_v, cmp_w2_v, cmp_b2_v, ssm_a_re, ssm_a_im, ssm_log_dt, ssm_b_re, ssm_b_im, ssm_c_re, ssm_c_im, ssm_d, ssm_w_glu, ssm_b_glu, norm_attn_out, norm_ssm_out, w_out, norm_ffn, router_w_coarse, router_b_coarse, router_w_fine, router_b_fine, w_gate, w_up, w_down, norm_final):
    bsz, seq, d_model = x.shape
    m = bsz * seq
    depth = w_in.shape[0]
    assert depth == 1, "the combine kernel fuses the final norm; stacked layers are not supported"
    layer = 0
    q_w = N_HEADS * HEAD_DIM
    kv_w = N_KV_HEADS * HEAD_DIM
    gate_w = N_HEADS * N_BRANCHES
    cut1, cut2 = q_w, q_w + 6 * kv_w
    cut3 = cut2 + gate_w
    h = x.astype(F32).reshape(m, d_model)
    gate_cols = jnp.array([c if c >= 0 else 0 for c in _gate_column_order()], jnp.int32)
    gate_keep = jnp.array([1.0 if c >= 0 else 0.0 for c in _gate_column_order()], F32)

    w_in_b = w_in[layer].astype(BF16)
    hn = _rmsnorm(h, norm_mix[layer], BF16)
    q = _matmul(hn, w_in_b[:, :cut1], name="proj_q")
    kv = _matmul(hn, w_in_b[:, cut1:cut2], name="proj_kv")
    w_gate_cols = (w_in[layer][:, cut2:cut3][:, gate_cols] * gate_keep[None, :]).astype(BF16)
    gates = _matmul(hn, w_gate_cols, tn=512, name="proj_gate")
    u = _matmul(hn, w_in_b[:, cut3:], name="proj_u")

    stack2 = lambda a, b: jnp.stack([a[layer], b[layer]])
    cmp_kv = _compress(kv, bsz, seq, stack2(cmp_pe_k, cmp_pe_v),
                       stack2(cmp_w1_k, cmp_w1_v).astype(BF16), stack2(cmp_b1_k, cmp_b1_v),
                       stack2(cmp_w2_k, cmp_w2_v).astype(BF16), stack2(cmp_b2_k, cmp_b2_v))
    ks, vs, kw, vw = _kv_prep(kv, seq)
    attn = _nsa_attention(q, gates, cmp_kv[0], cmp_kv[1], ks, vs, kw, vw, bsz, seq)

    ops = _s5_operators(ssm_a_re[layer], ssm_a_im[layer], ssm_log_dt[layer], ssm_b_re[layer],
                        ssm_b_im[layer], ssm_c_re[layer], ssm_c_im[layer])
    y = _s5_scan(u, bsz, seq, ops, ssm_d[layer].astype(F32))
    ssm = _glu_matmul(y, ssm_w_glu[layer].astype(BF16), ssm_b_glu[layer].astype(F32))

    mixed = _mixnorm(attn, ssm, norm_attn_out[layer], norm_ssm_out[layer])
    h = _matmul(mixed, w_out[layer].astype(BF16), residual=h, name="out_proj")

    n_r = N_EXPERT_GROUPS + N_EXPERTS
    w_router = jnp.pad(jnp.concatenate([router_w_coarse[layer], router_w_fine[layer]], axis=1),
                       ((0, 0), (0, LANES - n_r))).astype(F32)
    b_router = jnp.pad(jnp.concatenate([router_b_coarse[layer], router_b_fine[layer]]), (0, LANES - n_r))
    hn2, route = _ffn_norm_router(h, norm_ffn[layer], w_router, b_router.astype(F32))
    expert = route[:, 2:2 + TOP_K].astype(jnp.int32)
    tm_c = 128
    plan, buf_tok, dest, n_blocks = _dispatch_plan(expert, tm_c)
    act = _moe_up(hn2, w_gate[layer], w_up[layer], plan, buf_tok, n_blocks)
    y_buf = _moe_down(act, w_down[layer], plan, n_blocks)
    out = _combine_final(h, route, dest, y_buf, norm_final, tm=tm_c)
    return out.reshape(bsz, seq, d_model).astype(x.dtype)
```

```python
import functools
import math

import jax
import jax.numpy as jnp
from jax import lax
from jax.experimental import pallas as pl
from jax.experimental.pallas import tpu as pltpu

F32 = jnp.float32
BF16 = jnp.bfloat16

HEAD_DIM = 128
N_HEADS = 16
N_KV_HEADS = 4
GQA_GROUP = N_HEADS // N_KV_HEADS
ROT_DIM = HEAD_DIM // 4
ROPE_THETA = 500000.0
CMP_STRIDE = 16
CMP_BLOCK = 32
SLC_BLOCK = 64
N_SELECT = 16
WINDOW = 512
Q_TILE = 256
N_BRANCHES = 3
FORCE_BONUS = 1.0e4
NEG_INF = -1.0e30
MASK_BIG = 1.0e30
SSM_GROUP = 16
SSM_STATE = 64
SSM_CHUNK = 16
SSM_GROUPS_PER_BLOCK = 128 // SSM_GROUP
N_EXPERT_GROUPS = 8
EXPERTS_PER_GROUP = 8
N_EXPERTS = N_EXPERT_GROUPS * EXPERTS_PER_GROUP
TOP_K = 2
MOE_BLOCK = 128
EPS = 1.0e-6
LOG2_E = 1.4426950408889634

V7X_VMEM_LIMIT = 56 * 1024 * 1024
LANES = 128
SCAN_GROUP = 8
SCAN_TABLE_ROWS = 16
DMA_UNROLL = 8
SIDE_DMA_PRIORITY = 1
MOE_X_CAP = 12
W_DMA_SPLIT = 4
GATHER_SLOTS = MOE_X_CAP + 1
DOWN_ACT_CAP = 8
DOWN_OUT_SLOTS = 4


def _cparams(sem, vmem=V7X_VMEM_LIMIT):
    return pltpu.CompilerParams(dimension_semantics=sem, vmem_limit_bytes=vmem)


def _gelu_tanh(x):
    return 0.5 * x * (1.0 + jnp.tanh(math.sqrt(2.0 / math.pi) * (x + 0.044715 * (x * x * x))))


def _sigmoid(x):
    return 1.0 / (1.0 + jnp.exp(-x))


def _rmsnorm_kernel(x_ref, g_ref, o_ref):
    x = x_ref[...]
    ms = jnp.mean(x * x, axis=-1, keepdims=True)
    o_ref[...] = (x * lax.rsqrt(ms + EPS) * g_ref[...]).astype(o_ref.dtype)


def _rmsnorm(x, g, out_dtype, tm=512):
    m, d = x.shape
    return pl.pallas_call(
        _rmsnorm_kernel,
        out_shape=jax.ShapeDtypeStruct((m, d), out_dtype),
        grid=(m // tm,),
        in_specs=[pl.BlockSpec((tm, d), lambda i: (i, 0)),
                  pl.BlockSpec((1, d), lambda i: (0, 0))],
        out_specs=pl.BlockSpec((tm, d), lambda i: (i, 0)),
        compiler_params=_cparams(("parallel",)),
        name="rmsnorm",
    )(x, g.reshape(1, d))


def _mm_kernel(a_ref, w_ref, o_ref):
    o_ref[...] = jnp.dot(a_ref[...], w_ref[...], preferred_element_type=F32)


def _mm_glu_kernel(y_ref, w_ref, b_ref, yg_ref, o_ref):
    z = jnp.dot(y_ref[...].astype(BF16), w_ref[...], preferred_element_type=F32) + b_ref[...]
    o_ref[...] = yg_ref[...] * _sigmoid(z)


def _mm_residual_kernel(a_ref, w_ref, r_ref, o_ref):
    o_ref[...] = r_ref[...] + jnp.dot(a_ref[...], w_ref[...], preferred_element_type=F32)


def _matmul(a, w, *, residual=None, tm=512, tn=1024, name="matmul"):
    m, k = a.shape
    n = w.shape[1]
    tn = min(tn, n)
    assert m % tm == 0 and n % tn == 0
    a_spec = pl.BlockSpec((tm, k), lambda j, i: (i, 0))
    w_spec = pl.BlockSpec((k, tn), lambda j, i: (0, j))
    o_spec = pl.BlockSpec((tm, tn), lambda j, i: (i, j))
    if residual is not None:
        kern, ins, specs = _mm_residual_kernel, (a, w, residual), [a_spec, w_spec, o_spec]
    else:
        kern, ins, specs = _mm_kernel, (a, w), [a_spec, w_spec]
    return pl.pallas_call(
        kern,
        out_shape=jax.ShapeDtypeStruct((m, n), F32),
        grid=(n // tn, m // tm),
        in_specs=specs,
        out_specs=o_spec,
        compiler_params=_cparams(("parallel", "parallel")),
        name=name,
    )(*ins)


def _glu_matmul(y, w, bias, tm=512, tn=1024):
    m, k = y.shape
    n = w.shape[1]
    return pl.pallas_call(
        _mm_glu_kernel,
        out_shape=jax.ShapeDtypeStruct((m, n), F32),
        grid=(n // tn, m // tm),
        in_specs=[pl.BlockSpec((tm, k), lambda j, i: (i, 0)),
                  pl.BlockSpec((k, tn), lambda j, i: (0, j)),
                  pl.BlockSpec((1, tn), lambda j, i: (0, j)),
                  pl.BlockSpec((tm, tn), lambda j, i: (i, j))],
        out_specs=pl.BlockSpec((tm, tn), lambda j, i: (i, j)),
        compiler_params=_cparams(("parallel", "parallel")),
        name="ssm_glu",
    )(y, w, bias.reshape(1, n), y)


def _rope_tables(seq, reps):
    half = ROT_DIM // 2
    inv_freq = ROPE_THETA ** (-jnp.arange(half, dtype=F32) / half)
    ang = jnp.arange(seq).astype(F32)[:, None] * inv_freq[None, :]
    cos, sin = jnp.cos(ang), jnp.sin(ang)
    ones = jnp.ones((seq, HEAD_DIM - ROT_DIM), F32)
    cos_t = jnp.concatenate([cos, cos, ones], axis=-1)
    sin_t = jnp.concatenate([-sin, sin, 0.0 * ones], axis=-1)
    return jnp.tile(cos_t, (1, reps)), jnp.tile(sin_t, (1, reps))


def _rope(x, cos_t, sin_t):
    half = ROT_DIM // 2
    n = x.shape[-1]
    lane = jnp.bitwise_and(lax.broadcasted_iota(jnp.int32, x.shape, x.ndim - 1), HEAD_DIM - 1)
    partner = jnp.where(lane < half, pltpu.roll(x, n - half, x.ndim - 1), pltpu.roll(x, half, x.ndim - 1))
    return x * cos_t + partner * sin_t


def _kv_prep_kernel(slc_ref, win_ref, cos_ref, sin_ref, ks_ref, vs_ref, kw_ref, vw_ref, *, tiles_per_seq):
    tm, w = vs_ref.shape
    cos_t, sin_t = cos_ref[...], sin_ref[...]
    slc = slc_ref[...]
    win = win_ref[...]
    k_slc = _rope(slc[:, :w], cos_t, sin_t).astype(BF16)
    pos = lax.rem(pl.program_id(0), tiles_per_seq) * tm + lax.broadcasted_iota(jnp.int32, (tm, HEAD_DIM), 0)
    lane = lax.broadcasted_iota(jnp.int32, (tm, HEAD_DIM), 1)
    own_blk = jnp.where(lane == lax.shift_right_logical(pos, int(math.log2(SLC_BLOCK))), MASK_BIG, 0.0).astype(BF16)
    for hk in range(N_KV_HEADS):
        ks_ref[:, 2 * hk * HEAD_DIM:(2 * hk + 1) * HEAD_DIM] = k_slc[:, hk * HEAD_DIM:(hk + 1) * HEAD_DIM]
        ks_ref[:, (2 * hk + 1) * HEAD_DIM:(2 * hk + 2) * HEAD_DIM] = own_blk
    vs_ref[...] = slc[:, w:].astype(BF16)
    kw_ref[...] = _rope(win[:, :w], cos_t, sin_t).astype(BF16)
    vw_ref[...] = win[:, w:].astype(BF16)


def _kv_prep(kv, seq, tm=512):
    assert seq // SLC_BLOCK <= HEAD_DIM
    m = kv.shape[0]
    w = N_KV_HEADS * HEAD_DIM
    cos_t, sin_t = _rope_tables(seq, N_KV_HEADS)
    tpb = seq // tm
    out = jax.ShapeDtypeStruct((m, w), BF16)
    o_spec = pl.BlockSpec((tm, w), lambda i: (i, 0))
    t_spec = pl.BlockSpec((tm, w), lambda i: (i % tpb, 0))
    return pl.pallas_call(
        functools.partial(_kv_prep_kernel, tiles_per_seq=tpb),
        out_shape=(jax.ShapeDtypeStruct((m, 2 * w), BF16), out, out, out),
        grid=(m // tm,),
        in_specs=[pl.BlockSpec((tm, 2 * w), lambda i: (i, 1)),
                  pl.BlockSpec((tm, 2 * w), lambda i: (i, 2)),
                  t_spec, t_spec],
        out_specs=(pl.BlockSpec((tm, 2 * w), lambda i: (i, 0)), o_spec, o_spec, o_spec),
        compiler_params=_cparams(("parallel",)),
        name="kv_prep",
    )(kv, kv, cos_t, sin_t)


def _compress_kernel(x_ref, pe_ref, w1_ref, b1_ref, w2_ref, b2_ref, o_ref):
    nc = x_ref.shape[0] // CMP_STRIDE
    hid = w1_ref.shape[-1]
    top = jnp.zeros((nc, hid), F32)
    bot = jnp.zeros((nc, hid), F32)
    for j in range(CMP_STRIDE):
        xj = x_ref[pl.ds(j, nc, stride=CMP_STRIDE), :]
        top += jnp.dot((xj + pe_ref[0, j:j + 1, :]).astype(BF16), w1_ref[0, j],
                       preferred_element_type=F32)
        bot += jnp.dot((xj + pe_ref[0, CMP_STRIDE + j:CMP_STRIDE + j + 1, :]).astype(BF16),
                       w1_ref[0, CMP_STRIDE + j], preferred_element_type=F32)
    h = _gelu_tanh(top + pltpu.roll(bot, nc - 1, 0) + b1_ref[0])
    o_ref[0, 0, 0] = (jnp.dot(h.astype(BF16), w2_ref[0], preferred_element_type=F32) + b2_ref[0]).astype(BF16)


def _compress(kv, bsz, seq, pe, w1, b1, w2, b2):
    nc = seq // CMP_STRIDE
    hid = w1.shape[-1]
    w1r = w1.reshape(2, CMP_BLOCK, HEAD_DIM, hid)
    return pl.pallas_call(
        _compress_kernel,
        out_shape=jax.ShapeDtypeStruct((2, bsz, N_KV_HEADS, nc, HEAD_DIM), BF16),
        grid=(2, bsz, N_KV_HEADS),
        in_specs=[pl.BlockSpec((seq, HEAD_DIM), lambda s, b, h: (b, s * N_KV_HEADS + h)),
                  pl.BlockSpec((1, CMP_BLOCK, HEAD_DIM), lambda s, b, h: (s, 0, 0)),
                  pl.BlockSpec((1, CMP_BLOCK, HEAD_DIM, hid), lambda s, b, h: (s, 0, 0, 0)),
                  pl.BlockSpec((1, 1, hid), lambda s, b, h: (s, 0, 0)),
                  pl.BlockSpec((1, hid, HEAD_DIM), lambda s, b, h: (s, 0, 0)),
                  pl.BlockSpec((1, 1, HEAD_DIM), lambda s, b, h: (s, 0, 0))],
        out_specs=pl.BlockSpec((1, 1, 1, nc, HEAD_DIM), lambda s, b, h: (s, b, h, 0, 0)),
        compiler_params=_cparams(("parallel", "parallel", "parallel")),
        name="nsa_compress",
    )(kv, pe, w1r, b1.reshape(2, 1, hid), w2, b2.reshape(2, 1, HEAD_DIM))


def _attn_kernel(q_ref, gate_ref, cos_ref, sin_ref, kc_ref, vc_ref, c2s_ref,
                 ks_ref, vs_ref, kw_ref, vw_ref, o_ref, *, seq):
    qt = pl.program_id(2)
    s0 = qt * Q_TILE
    n_slc = seq // SLC_BLOCK
    n_cmp = kc_ref.shape[-2]
    scale = HEAD_DIM ** -0.5 * LOG2_E
    rows = GQA_GROUP * Q_TILE
    nt = (((1,), (1,)), ((), ()))

    def tile4(x):
        return jnp.concatenate([x] * GQA_GROUP, axis=0)

    q4 = q_ref[...]
    qp = jnp.concatenate([q4[:, g * HEAD_DIM:(g + 1) * HEAD_DIM] for g in range(GQA_GROUP)], axis=0) * scale
    qr_b = _rope(qp, tile4(cos_ref[...]), tile4(sin_ref[...])).astype(BF16)
    qp_b = qp.astype(BF16)
    t_q = s0 + lax.broadcasted_iota(jnp.int32, (Q_TILE, 1), 0)

    n_idx = lax.broadcasted_iota(jnp.int32, (Q_TILE, n_cmp), 1)
    c_ok = (n_idx * CMP_STRIDE + (CMP_BLOCK - 1)) <= t_q
    s_c = lax.dot_general(qp_b, kc_ref[0, 0], nt, preferred_element_type=F32) + tile4(jnp.where(c_ok, 0.0, NEG_INF))
    e_c = jnp.exp2(s_c - jnp.max(s_c, axis=-1, keepdims=True)) * tile4(jnp.where(c_ok, 1.0, 0.0))
    l_c = jnp.sum(e_c, axis=-1, keepdims=True)
    p_c = e_c / jnp.where(l_c > 0.0, l_c, 1.0)
    o_c = jnp.dot(p_c.astype(BF16), vc_ref[0, 0], preferred_element_type=F32)
    p_sum = (p_c[0:Q_TILE] + p_c[Q_TILE:2 * Q_TILE]) + (p_c[2 * Q_TILE:3 * Q_TILE] + p_c[3 * Q_TILE:])
    p_hi = p_sum.astype(BF16)
    r1 = p_sum - p_hi.astype(F32)
    p_mid = r1.astype(BF16)
    p_lo = (r1 - p_mid.astype(F32)).astype(BF16)
    c2s = c2s_ref[...]
    imp_t = (lax.dot_general(c2s, p_hi, nt, preferred_element_type=F32)
             + lax.dot_general(c2s, p_mid, nt, preferred_element_type=F32)
             + lax.dot_general(c2s, p_lo, nt, preferred_element_type=F32))

    sid = lax.broadcasted_iota(jnp.int32, (n_slc, Q_TILE), 0)
    t_l = s0 + lax.broadcasted_iota(jnp.int32, (n_slc, Q_TILE), 1)
    cur = lax.shift_right_logical(t_l, int(math.log2(SLC_BLOCK)))
    causal_blk = sid <= cur
    forced = jnp.where(sid == 0, 1.0, jnp.where(sid == cur, 1.0, jnp.where(sid == cur - 1, 1.0, 0.0)))
    work = jnp.where(causal_blk, imp_t + FORCE_BONUS * forced, NEG_INF)
    sid_f = sid.astype(F32)
    sel_t = jnp.zeros((n_slc, Q_TILE), F32)
    for _ in range(min(N_SELECT, n_slc)):
        mx = jnp.max(work, axis=0, keepdims=True)
        first = jnp.min(jnp.where(work == mx, sid_f, float(n_slc)), axis=0, keepdims=True)
        pick = sid_f == first
        sel_t = jnp.where(pick, 1.0, sel_t)
        work = jnp.where(pick, -3.0e38, work)
    sel_t = jnp.where(causal_blk, sel_t, 0.0)
    unsel = jnp.concatenate([sel_t - 1.0, jnp.zeros((HEAD_DIM - n_slc, Q_TILE), F32)], axis=0).T
    q_aug = jnp.concatenate([qr_b, tile4(unsel.astype(BF16))], axis=1)

    kt = 512

    def slc_tile(j, carry, diag_bias):
        m, l, acc = carry
        k0 = pl.multiple_of(j * kt, kt)
        s = lax.dot_general(q_aug, ks_ref[pl.ds(k0, kt), :], nt, preferred_element_type=F32)
        if diag_bias is not None:
            s = s + diag_bias
        m_new = jnp.maximum(m, jnp.max(s, axis=-1, keepdims=True))
        alpha = jnp.exp2(m - m_new)
        p = jnp.exp2(s - m_new)
        l = alpha * l + jnp.sum(p, axis=-1, keepdims=True)
        acc = alpha * acc + jnp.dot(p.astype(BF16), vs_ref[pl.ds(k0, kt), :], preferred_element_type=F32)
        return m_new, l, acc

    slab = WINDOW + Q_TILE
    w0 = pl.multiple_of(jnp.maximum(s0 - WINDOW, 0), Q_TILE)
    dist = t_q - (w0 + lax.broadcasted_iota(jnp.int32, (Q_TILE, slab), 1))
    w_bias = jnp.where(dist >= 0, jnp.where(dist < WINDOW, 0.0, NEG_INF), NEG_INF)
    s_w = lax.dot_general(qr_b, kw_ref[pl.ds(w0, slab), :], nt, preferred_element_type=F32) + tile4(w_bias)
    e_w = jnp.exp2(s_w - jnp.max(s_w, axis=-1, keepdims=True))
    o_w = (jnp.dot(e_w.astype(BF16), vw_ref[pl.ds(w0, slab), :], preferred_element_type=F32)
           / jnp.sum(e_w, axis=-1, keepdims=True))

    j_diag = lax.shift_right_logical(s0, int(math.log2(kt)))
    kpos = j_diag * kt + lax.broadcasted_iota(jnp.int32, (Q_TILE, kt), 1)
    init = (jnp.full((rows, 1), NEG_INF, F32), jnp.zeros((rows, 1), F32), jnp.zeros((rows, HEAD_DIM), F32))
    carry = slc_tile(j_diag, init, tile4(jnp.where(kpos <= t_q, 0.0, NEG_INF)))
    carry = lax.fori_loop(0, lax.shift_right_logical(j_diag, 1),
                          lambda p, c: slc_tile(2 * p + 1, slc_tile(2 * p, c, None), None), carry)
    _, l_s, acc_s = lax.cond(jnp.bitwise_and(j_diag, 1) == 1,
                             lambda c: slc_tile(j_diag - 1, c, None), lambda c: c, carry)
    o_s = acc_s / l_s

    sg = _sigmoid(gate_ref[...])
    for g in range(GQA_GROUP):
        r = slice(g * Q_TILE, (g + 1) * Q_TILE)
        o_ref[:, g * HEAD_DIM:(g + 1) * HEAD_DIM] = (
            sg[:, g:g + 1] * o_c[r]
            + sg[:, GQA_GROUP + g:GQA_GROUP + g + 1] * o_s[r]
            + sg[:, 2 * GQA_GROUP + g:2 * GQA_GROUP + g + 1] * o_w[r])


def _cmp_to_slc_t(seq):
    n_cmp = seq // CMP_STRIDE - 1
    n_slc = seq // SLC_BLOCK
    cmp_start = jnp.arange(n_cmp) * CMP_STRIDE
    slc_start = jnp.arange(n_slc) * SLC_BLOCK
    overlap = jnp.clip(jnp.minimum(cmp_start[:, None] + CMP_BLOCK, slc_start[None, :] + SLC_BLOCK)
                       - jnp.maximum(cmp_start[:, None], slc_start[None, :]), 0, None)
    c2s = overlap.astype(F32) / CMP_BLOCK
    return jnp.pad(c2s, ((0, 1), (0, 0))).T.astype(BF16)


def _nsa_attention(q, gates, kc, vc, ks, vs, kw, vw, bsz, seq):
    m = q.shape[0]
    nq = seq // Q_TILE
    nc = seq // CMP_STRIDE
    cos_t, sin_t = _rope_tables(seq, 1)
    c2s_t = _cmp_to_slc_t(seq)
    gw = GQA_GROUP * HEAD_DIM
    kv_spec = pl.BlockSpec((seq, HEAD_DIM), lambda b, h, t: (b, h))
    cmp_spec = pl.BlockSpec((1, 1, nc, HEAD_DIM), lambda b, h, t: (b, h, 0, 0))
    tab_spec = pl.BlockSpec((Q_TILE, HEAD_DIM), lambda b, h, t: (t, 0))
    return pl.pallas_call(
        functools.partial(_attn_kernel, seq=seq),
        out_shape=jax.ShapeDtypeStruct((m, N_HEADS * HEAD_DIM), F32),
        grid=(bsz, N_KV_HEADS, nq),
        in_specs=[pl.BlockSpec((Q_TILE, gw), lambda b, h, t: (b * nq + t, h)),
                  pl.BlockSpec((Q_TILE, LANES), lambda b, h, t: (b * nq + t, h)),
                  tab_spec, tab_spec, cmp_spec, cmp_spec,
                  pl.BlockSpec(c2s_t.shape, lambda b, h, t: (0, 0)),
                  pl.BlockSpec((seq, 2 * HEAD_DIM), lambda b, h, t: (b, h)), kv_spec, kv_spec, kv_spec],
        out_specs=pl.BlockSpec((Q_TILE, gw), lambda b, h, t: (b * nq + t, h)),
        compiler_params=_cparams(("parallel", "parallel", "arbitrary")),
        name="nsa_attention",
    )(q, gates, cos_t, sin_t, kc, vc, c2s_t, ks, vs, kw, vw)


def _s5_operators(a_re, a_im, log_dt, b_re, b_im, c_re, c_im):
    hp = lax.Precision.HIGHEST
    n_g = a_re.shape[0]
    gpb = SSM_GROUPS_PER_BLOCK
    n_gb = n_g // gpb
    t_c, hh, pp = SSM_CHUNK, SSM_GROUP, SSM_STATE
    lam = lax.complex(a_re.astype(F32), a_im.astype(F32))
    dt = jnp.exp(log_dt.astype(F32))[:, None]
    a_bar = jnp.exp(lam * dt)
    b_bar = ((a_bar - 1.0) / lam)[:, :, None] * lax.complex(b_re.astype(F32), b_im.astype(F32))
    c_mat = lax.complex(c_re.astype(F32), c_im.astype(F32))
    steps = jnp.arange(t_c + 1, dtype=F32)[:, None, None]
    pw = jnp.exp((lam * dt)[None] * steps)

    def blockdiag(x, r, c):
        tile = jnp.tile(jnp.eye(c, dtype=BF16), (1, gpb))
        rows = jnp.arange(gpb * r)[:, None] // r
        cols = jnp.arange(gpb * c)[None, :] // c
        tiled = jnp.einsum('...c,cd->...d', x.astype(BF16), tile, preferred_element_type=BF16)
        return jnp.where(rows == cols, tiled, jnp.zeros((), BF16))

    k_lag = jnp.einsum('ghp,kgp,gpi->kgih', c_mat, pw[:t_c], b_bar, precision=hp).real
    w_intra = blockdiag(k_lag.reshape(t_c, n_gb, gpb * hh, hh), hh, hh)
    w_intra = w_intra.transpose(1, 2, 0, 3).reshape(n_gb, LANES, t_c * LANES)
    p_in = jnp.swapaxes(pw[t_c - 1::-1][:, :, :, None] * b_bar[None], -1, -2)
    p_in = p_in.reshape(t_c, n_gb, gpb * hh, pp)
    s_in = jnp.concatenate([blockdiag(p_in.real, hh, pp), blockdiag(p_in.imag, hh, pp)], axis=-1)
    s_in = s_in.transpose(1, 0, 2, 3).reshape(n_gb, t_c * LANES, 2 * gpb * pp)
    m_out = jnp.swapaxes(c_mat[None] * pw[1:, :, None, :], -1, -2)
    m_out = m_out.reshape(t_c, n_gb, gpb * pp, hh)
    s_out = jnp.concatenate([blockdiag(m_out.real, pp, hh), blockdiag(-m_out.imag, pp, hh)], axis=2)
    s_out = s_out.transpose(1, 2, 0, 3).reshape(n_gb, 2 * gpb * pp, t_c * LANES)
    k_steps = jnp.arange(SCAN_TABLE_ROWS, dtype=F32)[:, None, None] * float(t_c)
    a_pow = jnp.exp((lam * dt)[None] * k_steps).reshape(SCAN_TABLE_ROWS, n_gb, gpb * pp).swapaxes(0, 1)
    return s_in, w_intra, s_out, a_pow.real.astype(F32), a_pow.imag.astype(F32)


def _s5_kernel(u_ref, sin_ref, wi_ref, sout_ref, ar_ref, ai_ref, d_ref, o_ref, xin_ref, xprev_ref, wt_ref):
    t_c = SSM_CHUNK
    nc = u_ref.shape[0] // t_c
    half = ar_ref.shape[-1]

    @pl.when(pl.program_id(1) == 0)
    def _():
        wt_ref[...] = jnp.zeros(wt_ref.shape, wt_ref.dtype)
        for tau in range(t_c):
            wt_ref[tau * LANES:(tau + 1) * LANES, tau * LANES:] = wi_ref[0, :, :(t_c - tau) * LANES]

    def step_rows(ref, t):
        return ref.at[pl.ds(t, nc, stride=t_c), :]

    u_flat = jnp.concatenate([step_rows(u_ref, tau)[...].astype(BF16) for tau in range(t_c)], axis=1)
    xin = jnp.dot(u_flat, sin_ref[0], preferred_element_type=F32)
    x_r, x_i = xin[:, :half], xin[:, half:]
    r_in_grp = jnp.bitwise_and(lax.broadcasted_iota(jnp.int32, (nc, 1), 0), SCAN_GROUP - 1)
    d = 1
    while d < SCAN_GROUP:
        keep = jnp.where(r_in_grp >= d, 1.0, 0.0)
        s_r, s_i = pltpu.roll(x_r, d, 0) * keep, pltpu.roll(x_i, d, 0) * keep
        ad_r, ad_i = ar_ref[0, d:d + 1, :], ai_ref[0, d:d + 1, :]
        x_r, x_i = x_r + ad_r * s_r - ad_i * s_i, x_i + ad_r * s_i + ad_i * s_r
        d *= 2
    keep = jnp.where(r_in_grp >= 1, 1.0, 0.0)
    xin_ref[:, :half] = x_r
    xin_ref[:, half:] = x_i
    xprev_ref[:, :half] = pltpu.roll(x_r, 1, 0) * keep
    xprev_ref[:, half:] = pltpu.roll(x_i, 1, 0) * keep
    ak_r, ak_i = ar_ref[0, 0:SCAN_GROUP, :], ai_ref[0, 0:SCAN_GROUP, :]
    ag_r, ag_i = ar_ref[0, SCAN_GROUP:SCAN_GROUP + 1, :], ai_ref[0, SCAN_GROUP:SCAN_GROUP + 1, :]

    def group(g, carry):
        c_r, c_i = carry
        rows = pl.ds(pl.multiple_of(g * SCAN_GROUP, SCAN_GROUP), SCAN_GROUP)
        xprev_ref[rows, :half] = xprev_ref[rows, :half] + ak_r * c_r - ak_i * c_i
        xprev_ref[rows, half:] = xprev_ref[rows, half:] + ak_r * c_i + ak_i * c_r
        last = xin_ref[pl.ds(g * SCAN_GROUP + SCAN_GROUP - 1, 1), :]
        return (last[:, :half] + ag_r * c_r - ag_i * c_i, last[:, half:] + ag_r * c_i + ag_i * c_r)

    zero = jnp.zeros((1, half), F32)
    lax.fori_loop(0, nc // SCAN_GROUP, group, (zero, zero))
    h_w = (t_c // 2) * LANES
    x_prev = xprev_ref[...].astype(BF16)
    y_lo = (jnp.dot(u_flat[:, :h_w], wt_ref[:h_w, :h_w], preferred_element_type=F32)
            + jnp.dot(x_prev, sout_ref[0, :, :h_w], preferred_element_type=F32))
    y_hi = (jnp.dot(u_flat, wt_ref[:, h_w:], preferred_element_type=F32)
            + jnp.dot(x_prev, sout_ref[0, :, h_w:], preferred_element_type=F32))
    for t in range(t_c):
        y_t = (y_lo if t < t_c // 2 else y_hi)[:, (t % (t_c // 2)) * LANES:(t % (t_c // 2) + 1) * LANES]
        step_rows(o_ref, t)[...] = _gelu_tanh(y_t + d_ref[0] * step_rows(u_ref, t)[...])


def _s5_scan(u, bsz, seq, ops, d_skip):
    d_ssm = u.shape[1]
    t_c = SSM_CHUNK
    s_in, w_intra, s_out, a_r, a_i = ops
    n_gb = d_ssm // LANES
    nc = seq // t_c
    st = s_in.shape[-1]
    return pl.pallas_call(
        _s5_kernel,
        out_shape=jax.ShapeDtypeStruct(u.shape, F32),
        grid=(n_gb, bsz),
        in_specs=[pl.BlockSpec((seq, LANES), lambda g, b: (b, g)),
                  pl.BlockSpec((1,) + s_in.shape[1:], lambda g, b: (g, 0, 0)),
                  pl.BlockSpec((1,) + w_intra.shape[1:], lambda g, b: (g, 0, 0)),
                  pl.BlockSpec((1,) + s_out.shape[1:], lambda g, b: (g, 0, 0)),
                  pl.BlockSpec((1, SCAN_TABLE_ROWS, st // 2), lambda g, b: (g, 0, 0)),
                  pl.BlockSpec((1, SCAN_TABLE_ROWS, st // 2), lambda g, b: (g, 0, 0)),
                  pl.BlockSpec((1, 1, LANES), lambda g, b: (g, 0, 0))],
        out_specs=pl.BlockSpec((seq, LANES), lambda g, b: (b, g)),
        scratch_shapes=[pltpu.VMEM((nc, st), F32), pltpu.VMEM((nc, st), F32),
                        pltpu.VMEM((t_c * LANES, t_c * LANES), BF16)],
        compiler_params=_cparams(("arbitrary", "arbitrary")),
        name="s5_scan",
    )(u, s_in, w_intra, s_out, a_r, a_i, d_skip.reshape(n_gb, 1, LANES))


def _mixnorm_kernel(a_ref, s_ref, ga_ref, gs_ref, o_ref):
    da = a_ref.shape[-1]
    a = a_ref[...]
    s = s_ref[...]
    o_ref[:, :da] = (a * lax.rsqrt(jnp.mean(a * a, axis=-1, keepdims=True) + EPS) * ga_ref[...]).astype(BF16)
    o_ref[:, da:] = (s * lax.rsqrt(jnp.mean(s * s, axis=-1, keepdims=True) + EPS) * gs_ref[...]).astype(BF16)


def _mixnorm(attn, ssm, g_attn, g_ssm, tm=512):
    m, da = attn.shape
    ds = ssm.shape[1]
    return pl.pallas_call(
        _mixnorm_kernel,
        out_shape=jax.ShapeDtypeStruct((m, da + ds), BF16),
        grid=(m // tm,),
        in_specs=[pl.BlockSpec((tm, da), lambda i: (i, 0)), pl.BlockSpec((tm, ds), lambda i: (i, 0)),
                  pl.BlockSpec((1, da), lambda i: (0, 0)), pl.BlockSpec((1, ds), lambda i: (0, 0))],
        out_specs=pl.BlockSpec((tm, da + ds), lambda i: (i, 0)),
        compiler_params=_cparams(("parallel",)),
        name="mixnorm",
    )(attn, ssm, g_attn.reshape(1, da), g_ssm.reshape(1, ds))


def _split_bf16(x):
    hi = x.astype(BF16)
    return hi, (x - hi.astype(F32)).astype(BF16)


def _lane_max(x):
    return jnp.max(x, axis=-1, keepdims=True)


def _first_lane_of(cond_val, lane_f):
    return jnp.min(jnp.where(cond_val > 0.0, lane_f, float(LANES)), axis=-1, keepdims=True)


def _ffn_norm_router_kernel(h_ref, g_ref, whi_ref, wlo_ref, b_ref, hn_ref, route_ref):
    x = h_ref[...]
    hn = x * lax.rsqrt(jnp.mean(x * x, axis=-1, keepdims=True) + EPS) * g_ref[...]
    hn_ref[...] = _pack_bf16_pairs(hn)
    hi, lo = _split_bf16(hn)
    logits = (jnp.dot(hi, whi_ref[...], preferred_element_type=F32)
              + jnp.dot(lo, whi_ref[...], preferred_element_type=F32)
              + jnp.dot(hi, wlo_ref[...], preferred_element_type=F32)) + b_ref[...]
    lane = lax.broadcasted_iota(jnp.int32, logits.shape, 1)
    lane_f = lane.astype(F32)
    in_c = jnp.where(lane < N_EXPERT_GROUPS, 1.0, 0.0)
    c_l = jnp.where(in_c > 0.0, logits, NEG_INF)
    c_e = jnp.exp(c_l - _lane_max(c_l)) * in_c
    p_c = c_e / jnp.sum(c_e, axis=-1, keepdims=True)
    w_grp = _lane_max(p_c)
    grp = _first_lane_of(jnp.where(p_c == w_grp, in_c, 0.0), lane_f)
    lo_lane = float(N_EXPERT_GROUPS) + grp * float(EXPERTS_PER_GROUP)
    in_f = jnp.where(lane_f >= lo_lane, jnp.where(lane_f < lo_lane + float(EXPERTS_PER_GROUP), 1.0, 0.0), 0.0)
    f_l = jnp.where(in_f > 0.0, logits, NEG_INF)
    f_e = jnp.exp(f_l - _lane_max(f_l)) * in_f
    p_f = f_e / jnp.sum(f_e, axis=-1, keepdims=True)
    p_f = jnp.where(in_f > 0.0, p_f, -1.0)
    p1 = _lane_max(p_f)
    i1 = _first_lane_of(jnp.where(p_f == p1, in_f, 0.0), lane_f)
    rest = jnp.where(lane_f == i1, -1.0, p_f)
    p2 = _lane_max(rest)
    i2 = _first_lane_of(jnp.where(rest == p2, in_f, 0.0) * jnp.where(lane_f == i1, 0.0, 1.0), lane_f)
    denom = p1 + p2
    vals = (w_grp * p1 / denom, w_grp * p2 / denom,
            i1 - float(N_EXPERT_GROUPS), i2 - float(N_EXPERT_GROUPS))
    out = jnp.zeros(logits.shape, F32)
    for k, v in enumerate(vals):
        out = jnp.where(lane == k, v, out)
    route_ref[...] = out


def _ffn_norm_router(h, g, w_router, b_router, tm=512):
    m, d = h.shape
    n = w_router.shape[1]
    w_hi, w_lo = _split_bf16(w_router)
    return pl.pallas_call(
        _ffn_norm_router_kernel,
        out_shape=(jax.ShapeDtypeStruct((m, d // 2), jnp.uint32), jax.ShapeDtypeStruct((m, n), F32)),
        grid=(m // tm,),
        in_specs=[pl.BlockSpec((tm, d), lambda i: (i, 0)), pl.BlockSpec((1, d), lambda i: (0, 0)),
                  pl.BlockSpec((d, n), lambda i: (0, 0)), pl.BlockSpec((d, n), lambda i: (0, 0)),
                  pl.BlockSpec((1, n), lambda i: (0, 0))],
        out_specs=(pl.BlockSpec((tm, d // 2), lambda i: (i, 0)), pl.BlockSpec((tm, n), lambda i: (i, 0))),
        compiler_params=_cparams(("parallel",)),
        name="ffn_norm_router",
    )(h, g.reshape(1, d), w_hi, w_lo, b_router.reshape(1, n))


def _dispatch_plan(expert, tm):
    n_tok = expert.shape[0]
    n_assign = n_tok * TOP_K
    n_blocks = -(-(n_assign + N_EXPERTS * (MOE_BLOCK - 1)) // MOE_BLOCK)
    n_rows = n_blocks * MOE_BLOCK
    flat = expert.reshape(n_assign)
    ids = jnp.arange(n_assign, dtype=jnp.int32)
    _, order = lax.sort_key_val(flat, ids)
    _, inv = lax.sort_key_val(order, ids)
    e_ids = jnp.arange(N_EXPERTS, dtype=jnp.int32)
    counts = jnp.sum((flat[:, None] == e_ids[None, :]).astype(jnp.int32), axis=0)
    start = jnp.cumsum(counts) - counts
    padded = (counts + MOE_BLOCK - 1) // MOE_BLOCK * MOE_BLOCK
    pad_end = jnp.cumsum(padded)
    pad_start = pad_end - padded
    rows = jnp.arange(n_rows, dtype=jnp.int32)
    blk_row0 = jnp.arange(n_blocks, dtype=jnp.int32) * MOE_BLOCK
    blk_e = jnp.minimum(jnp.sum((pad_end[None, :] <= blk_row0[:, None]).astype(jnp.int32), axis=1), N_EXPERTS - 1)
    shift = pad_start - start
    row_shift = jnp.repeat(shift[blk_e], MOE_BLOCK)
    row_end = jnp.repeat((pad_start + counts)[blk_e], MOE_BLOCK)
    valid = (rows < row_end) & (rows < pad_end[-1])
    src = jnp.clip(rows - row_shift, 0, n_assign - 1)
    buf_tok = jnp.where(valid, order[src] // TOP_K, 0).astype(jnp.int32)
    dest = (inv + shift[flat]).astype(jnp.int32)
    dest = dest.reshape(n_tok // tm, tm, TOP_K).transpose(0, 2, 1).reshape(-1)
    plan = jnp.concatenate([pad_start // MOE_BLOCK, padded // MOE_BLOCK,
                            pad_end[-1:] // MOE_BLOCK]).astype(jnp.int32)
    return plan, buf_tok, dest, n_blocks


def _zero_fill_spare(plan_ref, zero_vmem, out_hbm, col0, width, sem, n_blocks):
    n_used = plan_ref[2 * N_EXPERTS]

    def cp(blk):
        return pltpu.make_async_copy(
            zero_vmem, out_hbm.at[pl.ds(pl.multiple_of(blk * MOE_BLOCK, MOE_BLOCK), MOE_BLOCK), pl.ds(col0, width)],
            sem)

    def start(blk, c):
        cp(blk).start(priority=SIDE_DMA_PRIORITY)
        return c

    def wait(blk, c):
        cp(blk).wait()
        return c

    lax.fori_loop(n_used, n_blocks, start, 0)
    lax.fori_loop(n_used, n_blocks, wait, 0)


def _pack_bf16_pairs(x):
    w = x.shape[1] // 2
    hi = lax.bitcast_convert_type(x[:, :w].astype(BF16).astype(F32), jnp.uint32)
    lo = lax.bitcast_convert_type(x[:, w:].astype(BF16).astype(F32), jnp.uint32)
    return hi | (lo >> 16)


def _unpack_bf16_pairs(p):
    first = lax.bitcast_convert_type(p & jnp.uint32(0xFFFF0000), F32)
    second = lax.bitcast_convert_type(p << 16, F32)
    return jnp.concatenate([first, second], axis=1).astype(BF16)


def _moe_up_kernel(plan_ref, tok_ref, x_hbm, *rest, n_blocks):
    w_refs = rest[:2 * W_DMA_SPLIT]
    act_hbm, wg_b, wu_b, xs, gbuf, obuf, afull, gsem, osem, fsem = rest[2 * W_DMA_SPLIT:]
    wg_refs, wu_refs = w_refs[:W_DMA_SPLIT], w_refs[W_DMA_SPLIT:]
    e = pl.program_id(0)
    c = pl.program_id(1)
    n_c = pl.num_programs(1)
    tn = wg_b.shape[1]
    col0 = pl.multiple_of(c * tn, tn)
    b0 = plan_ref[e]
    nb = plan_ref[N_EXPERTS + e]
    n_res = jnp.minimum(nb, MOE_X_CAP)

    def row_copy(blk, r, slot):
        tok = tok_ref[blk * MOE_BLOCK + r]
        return pltpu.make_async_copy(x_hbm.at[pl.ds(tok, 1), :], gbuf.at[slot, pl.ds(r, 1), :], gsem.at[slot])

    def gather_start(blk, slot):
        lax.fori_loop(0, MOE_BLOCK, lambda r, cr: (row_copy(blk, r, slot).start(priority=SIDE_DMA_PRIORITY), cr)[1], 0, unroll=DMA_UNROLL)

    def gather_wait(blk, slot):
        lax.fori_loop(0, MOE_BLOCK, lambda r, cr: (row_copy(blk, r, slot).wait(), cr)[1], 0, unroll=DMA_UNROLL)

    def gather_expert(first_blk, n_blk):
        lax.fori_loop(0, jnp.minimum(n_blk, MOE_X_CAP), lambda k, cr: (gather_start(first_blk + k, k), cr)[1], 0)

    def gather_now(blk):
        gather_start(blk, MOE_X_CAP)
        gather_wait(blk, MOE_X_CAP)
        return _unpack_bf16_pairs(gbuf[MOE_X_CAP])

    @pl.when(jnp.logical_and(c == 1, e + 1 < N_EXPERTS))
    def _():
        e_next = jnp.minimum(e + 1, N_EXPERTS - 1)
        gather_expert(plan_ref[e_next], plan_ref[N_EXPERTS + e_next])

    def rows_of(blk):
        return pl.ds(pl.multiple_of(blk * MOE_BLOCK, MOE_BLOCK), MOE_BLOCK)

    def out_copy(blk, slot):
        return pltpu.make_async_copy(obuf.at[slot], act_hbm.at[rows_of(blk), pl.ds(col0, tn)], osem.at[slot])

    def full_copy(j):
        return pltpu.make_async_copy(afull.at[j], act_hbm.at[rows_of(b0 + j), :], fsem)

    def act_of(x):
        g = jnp.dot(x, wg_b[...], preferred_element_type=F32)
        u = jnp.dot(x, wu_b[...], preferred_element_type=F32)
        return (g * _sigmoid(g) * u).astype(BF16)

    def emit_resident(x, j):
        act = act_of(x)
        for cc in range(act_hbm.shape[1] // tn):
            @pl.when(c == cc)
            def _():
                afull[j, :, cc * tn:(cc + 1) * tn] = act

    def emit_streamed(x, j):
        i = j - n_res
        slot = lax.rem(i, 2)
        act = act_of(x)

        @pl.when(i >= 2)
        def _():
            out_copy(b0 + j - 2, slot).wait()

        obuf[slot] = act
        out_copy(b0 + j, slot).start(priority=SIDE_DMA_PRIORITY)

    @pl.when(nb > 0)
    def _():
        rows = wg_b.shape[0] // W_DMA_SPLIT
        for k in range(W_DMA_SPLIT):
            wg_b[k * rows:(k + 1) * rows, :] = wg_refs[k][0].astype(BF16)
            wu_b[k * rows:(k + 1) * rows, :] = wu_refs[k][0].astype(BF16)

    @pl.when(c == 0)
    def _():
        @pl.when(e == 0)
        def _():
            gather_expert(b0, nb)

        def landed(j, carry):
            gather_wait(b0 + j, j)
            x = _unpack_bf16_pairs(gbuf[j])
            xs[j] = x
            emit_resident(x, j)
            return carry

        def overflow(j, carry):
            emit_streamed(gather_now(b0 + j), j)
            return carry

        lax.fori_loop(0, n_res, landed, 0)
        lax.fori_loop(n_res, nb, overflow, 0)

    @pl.when(c > 0)
    def _():
        def resident(j, carry):
            emit_resident(xs[j], j)
            return carry

        def overflow(j, carry):
            emit_streamed(gather_now(b0 + j), j)
            return carry

        lax.fori_loop(0, n_res, resident, 0)
        lax.fori_loop(n_res, nb, overflow, 0)

    n_str = nb - n_res

    @pl.when(n_str >= 2)
    def _():
        out_copy(b0 + nb - 2, lax.rem(n_str, 2)).wait()

    @pl.when(n_str >= 1)
    def _():
        out_copy(b0 + nb - 1, lax.rem(n_str + 1, 2)).wait()

    @pl.when(c == n_c - 1)
    def _():
        lax.fori_loop(0, n_res, lambda j, cr: (full_copy(j).start(priority=SIDE_DMA_PRIORITY), cr)[1], 0)
        lax.fori_loop(0, n_res, lambda j, cr: (full_copy(j).wait(), cr)[1], 0)

    @pl.when(jnp.logical_and(e == N_EXPERTS - 1, c == n_c - 1))
    def _():
        afull[0] = jnp.zeros(afull.shape[1:], afull.dtype)
        _zero_fill_spare(plan_ref, afull.at[0], act_hbm, 0, act_hbm.shape[1], fsem, n_blocks)


def _moe_up(hn_packed, w_gate, w_up, plan, buf_tok, n_blocks, n_chunks=4):
    d = w_gate.shape[-2]
    d_e = w_gate.shape[-1]
    tn = d_e // n_chunks
    w_specs = [pl.BlockSpec((1, d // W_DMA_SPLIT, tn), functools.partial(lambda e, c, plan, tok, k: (e, k, c), k=k))
               for k in range(W_DMA_SPLIT)]
    return pl.pallas_call(
        functools.partial(_moe_up_kernel, n_blocks=n_blocks),
        out_shape=jax.ShapeDtypeStruct((n_blocks * MOE_BLOCK, d_e), BF16),
        grid_spec=pltpu.PrefetchScalarGridSpec(
            num_scalar_prefetch=2,
            grid=(N_EXPERTS, n_chunks),
            in_specs=[pl.BlockSpec(memory_space=pl.ANY)] + w_specs + w_specs,
            out_specs=pl.BlockSpec(memory_space=pl.ANY),
            scratch_shapes=[pltpu.VMEM((d, tn), BF16), pltpu.VMEM((d, tn), BF16),
                            pltpu.VMEM((MOE_X_CAP, MOE_BLOCK, d), BF16),
                            pltpu.VMEM((GATHER_SLOTS, MOE_BLOCK, d // 2), jnp.uint32),
                            pltpu.VMEM((2, MOE_BLOCK, tn), BF16),
                            pltpu.VMEM((MOE_X_CAP, MOE_BLOCK, d_e), BF16),
                            pltpu.SemaphoreType.DMA((GATHER_SLOTS,)), pltpu.SemaphoreType.DMA((2,)),
                            pltpu.SemaphoreType.DMA(())]),
        compiler_params=_cparams(("arbitrary", "arbitrary")),
        name="moe_up",
    )(plan, buf_tok, hn_packed, *([w_gate] * W_DMA_SPLIT), *([w_up] * W_DMA_SPLIT))


def _moe_down_kernel(plan_ref, act_hbm, *rest, n_blocks):
    wd_refs, (y_hbm, wd_b, abuf, obuf, asem, osem) = rest[:W_DMA_SPLIT], rest[W_DMA_SPLIT:]
    e = pl.program_id(0)
    b0 = plan_ref[e]
    nb = plan_ref[N_EXPERTS + e]
    n_res = jnp.minimum(nb, DOWN_ACT_CAP)
    par = lax.rem(e, 2)

    def rows_of(blk):
        return pl.ds(pl.multiple_of(blk * MOE_BLOCK, MOE_BLOCK), MOE_BLOCK)

    def in_copy(blk, half, slot):
        return pltpu.make_async_copy(act_hbm.at[rows_of(blk), :], abuf.at[half, slot], asem.at[half, slot])

    def out_copy(blk, slot):
        return pltpu.make_async_copy(obuf.at[slot], y_hbm.at[rows_of(blk), :], osem.at[slot])

    def request_expert(first_blk, n_blk, half):
        lax.fori_loop(0, jnp.minimum(n_blk, DOWN_ACT_CAP),
                      lambda k, cr: (in_copy(first_blk + k, half, k).start(priority=SIDE_DMA_PRIORITY), cr)[1], 0)

    @pl.when(e == 0)
    def _():
        request_expert(b0, nb, 0)

    @pl.when(e + 1 < N_EXPERTS)
    def _():
        e_next = jnp.minimum(e + 1, N_EXPERTS - 1)
        request_expert(plan_ref[e_next], plan_ref[N_EXPERTS + e_next], 1 - par)

    @pl.when(nb > 0)
    def _():
        rows = wd_b.shape[0] // W_DMA_SPLIT
        for k in range(W_DMA_SPLIT):
            wd_b[k * rows:(k + 1) * rows, :] = wd_refs[k][0].astype(BF16)

    def emit(a, j):
        slot = lax.rem(j, DOWN_OUT_SLOTS)
        y = jnp.dot(a, wd_b[...], preferred_element_type=F32)

        @pl.when(j >= DOWN_OUT_SLOTS)
        def _():
            out_copy(b0 + j - DOWN_OUT_SLOTS, slot).wait()

        obuf[slot] = _pack_bf16_pairs(y)
        out_copy(b0 + j, slot).start(priority=SIDE_DMA_PRIORITY)

    def landed(j, carry):
        in_copy(b0 + j, par, j).wait()
        emit(abuf[par, j], j)
        return carry

    def overflow(j, carry):
        in_copy(b0 + j, par, 0).start(priority=SIDE_DMA_PRIORITY)
        in_copy(b0 + j, par, 0).wait()
        emit(abuf[par, 0], j)
        return carry

    lax.fori_loop(0, n_res, landed, 0)
    lax.fori_loop(n_res, nb, overflow, 0)
    lax.fori_loop(jnp.maximum(nb - DOWN_OUT_SLOTS, 0), nb,
                  lambda j, cr: (out_copy(b0 + j, lax.rem(j, DOWN_OUT_SLOTS)).wait(), cr)[1], 0)

    @pl.when(e == N_EXPERTS - 1)
    def _():
        obuf[0] = jnp.zeros(obuf.shape[1:], obuf.dtype)
        _zero_fill_spare(plan_ref, obuf.at[0], y_hbm, 0, y_hbm.shape[1], osem.at[0], n_blocks)


def _moe_down(act, w_down, plan, n_blocks):
    n_rows, d_e = act.shape
    d = w_down.shape[-1]
    return pl.pallas_call(
        functools.partial(_moe_down_kernel, n_blocks=n_blocks),
        out_shape=jax.ShapeDtypeStruct((n_rows, d // 2), jnp.uint32),
        grid_spec=pltpu.PrefetchScalarGridSpec(
            num_scalar_prefetch=1,
            grid=(N_EXPERTS,),
            in_specs=[pl.BlockSpec(memory_space=pl.ANY)] + [
                pl.BlockSpec((1, d_e // W_DMA_SPLIT, d), functools.partial(lambda e, plan, k: (e, k, 0), k=k))
                for k in range(W_DMA_SPLIT)],
            out_specs=pl.BlockSpec(memory_space=pl.ANY),
            scratch_shapes=[pltpu.VMEM((d_e, d), BF16),
                            pltpu.VMEM((2, DOWN_ACT_CAP, MOE_BLOCK, d_e), BF16),
                            pltpu.VMEM((DOWN_OUT_SLOTS, MOE_BLOCK, d // 2), jnp.uint32),
                            pltpu.SemaphoreType.DMA((2, DOWN_ACT_CAP)), pltpu.SemaphoreType.DMA((DOWN_OUT_SLOTS,))]),
        compiler_params=_cparams(("arbitrary",)),
        name="moe_down",
    )(plan, act, *([w_down] * W_DMA_SPLIT))


def _combine_kernel(dest_ref, h_ref, route_ref, gn_ref, y_hbm, o_ref, rows_ref, sem):
    tm = h_ref.shape[0]
    n_rows = tm * TOP_K
    i = pl.program_id(0)
    n = pl.num_programs(0)

    def row_copy(step, r, slot):
        src = dest_ref[step * n_rows + r]
        return pltpu.make_async_copy(y_hbm.at[pl.ds(src, 1), :], rows_ref.at[slot, pl.ds(r, 1), :], sem.at[slot])

    def gather_start(step, slot):
        lax.fori_loop(0, n_rows, lambda r, cr: (row_copy(step, r, slot).start(priority=SIDE_DMA_PRIORITY), cr)[1], 0, unroll=DMA_UNROLL)

    def gather_wait(step, slot):
        lax.fori_loop(0, n_rows, lambda r, cr: (row_copy(step, r, slot).wait(), cr)[1], 0, unroll=DMA_UNROLL)

    slot = lax.rem(i, 2)

    @pl.when(i == 0)
    def _():
        gather_start(0, 0)

    nxt = jnp.minimum(i + 1, n - 1)
    for r in range(n_rows):
        row_copy(nxt, r, 1 - slot).start(priority=SIDE_DMA_PRIORITY)

    gather_wait(i, slot)
    route = route_ref[...]
    rows = rows_ref[slot]
    g0, g1 = route[:, 0:1], route[:, 1:2]
    r0, r1 = rows[0:tm, :], rows[tm:2 * tm, :]
    half = rows.shape[1]

    def first(p):
        return lax.bitcast_convert_type(p & jnp.uint32(0xFFFF0000), F32)

    def second(p):
        return lax.bitcast_convert_type(p << 16, F32)

    h = jnp.concatenate([h_ref[:, :half] + g0 * first(r0) + g1 * first(r1),
                         h_ref[:, half:] + g0 * second(r0) + g1 * second(r1)], axis=1)
    o_ref[...] = h * lax.rsqrt(jnp.mean(h * h, axis=-1, keepdims=True) + EPS) * gn_ref[...]

    @pl.when(i == n - 1)
    def _():
        gather_wait(nxt, 1 - slot)


def _combine_final(h, route, dest, y_buf, g_final, tm=128):
    m, d = h.shape
    return pl.pallas_call(
        _combine_kernel,
        out_shape=jax.ShapeDtypeStruct((m, d), F32),
        grid_spec=pltpu.PrefetchScalarGridSpec(
            num_scalar_prefetch=1,
            grid=(m // tm,),
            in_specs=[pl.BlockSpec((tm, d), lambda i, dst: (i, 0)),
                      pl.BlockSpec((tm, LANES), lambda i, dst: (i, 0)),
                      pl.BlockSpec((1, d), lambda i, dst: (0, 0)),
                      pl.BlockSpec(memory_space=pl.ANY)],
            out_specs=pl.BlockSpec((tm, d), lambda i, dst: (i, 0)),
            scratch_shapes=[pltpu.VMEM((2, TOP_K * tm, d // 2), jnp.uint32), pltpu.SemaphoreType.DMA((2,))]),
        compiler_params=_cparams(("arbitrary",)),
        name="moe_combine_final",
    )(dest, h, route, g_final.reshape(1, d), y_buf)


def _gate_column_order():
    cols = []
    for hk in range(N_KV_HEADS):
        blk = [-1] * LANES
        for br in range(N_BRANCHES):
            for g in range(GQA_GROUP):
                blk[br * GQA_GROUP + g] = (hk * GQA_GROUP + g) * N_BRANCHES + br
        cols.extend(blk)
    return cols


def kernel(x, norm_mix, w_in, cmp_pe_k, cmp_w1_k, cmp_b1_k, cmp_w2_k, cmp_b2_k, cmp_pe_v, cmp_w1_v, cmp_b1_v, cmp_w2_v, cmp_b2_v, ssm_a_re, ssm_a_im, ssm_log_dt, ssm_b_re, ssm_b_im, ssm_c_re, ssm_c_im, ssm_d, ssm_w_glu, ssm_b_glu, norm_attn_out, norm_ssm_out, w_out, norm_ffn, router_w_coarse, router_b_coarse, router_w_fine, router_b_fine, w_gate, w_up, w_down, norm_final):
    bsz, seq, d_model = x.shape
    m = bsz * seq
    depth = w_in.shape[0]
    assert depth == 1, "the combine kernel fuses the final norm; stacked layers are not supported"
    layer = 0
    q_w = N_HEADS * HEAD_DIM
    kv_w = N_KV_HEADS * HEAD_DIM
    gate_w = N_HEADS * N_BRANCHES
    cut1, cut2 = q_w, q_w + 6 * kv_w
    cut3 = cut2 + gate_w
    h = x.astype(F32).reshape(m, d_model)
    gate_cols = jnp.array([c if c >= 0 else 0 for c in _gate_column_order()], jnp.int32)
    gate_keep = jnp.array([1.0 if c >= 0 else 0.0 for c in _gate_column_order()], F32)

    w_in_b = w_in[layer].astype(BF16)
    hn = _rmsnorm(h, norm_mix[layer], BF16)
    q = _matmul(hn, w_in_b[:, :cut1], name="proj_q")
    kv = _matmul(hn, w_in_b[:, cut1:cut2], name="proj_kv")
    w_gate_cols = (w_in[layer][:, cut2:cut3][:, gate_cols] * gate_keep[None, :]).astype(BF16)
    gates = _matmul(hn, w_gate_cols, tn=512, name="proj_gate")
    u = _matmul(hn, w_in_b[:, cut3:], name="proj_u")

    stack2 = lambda a, b: jnp.stack([a[layer], b[layer]])
    cmp_kv = _compress(kv, bsz, seq, stack2(cmp_pe_k, cmp_pe_v),
                       stack2(cmp_w1_k, cmp_w1_v).astype(BF16), stack2(cmp_b1_k, cmp_b1_v),
                       stack2(cmp_w2_k, cmp_w2_v).astype(BF16), stack2(cmp_b2_k, cmp_b2_v))
    ks, vs, kw, vw = _kv_prep(kv, seq)
    attn = _nsa_attention(q, gates, cmp_kv[0], cmp_kv[1], ks, vs, kw, vw, bsz, seq)

    ops = _s5_operators(ssm_a_re[layer], ssm_a_im[layer], ssm_log_dt[layer], ssm_b_re[layer],
                        ssm_b_im[layer], ssm_c_re[layer], ssm_c_im[layer])
    y = _s5_scan(u, bsz, seq, ops, ssm_d[layer].astype(F32))
    ssm = _glu_matmul(y, ssm_w_glu[layer].astype(BF16), ssm_b_glu[layer].astype(F32))

    mixed = _mixnorm(attn, ssm, norm_attn_out[layer], norm_ssm_out[layer])
    h = _matmul(mixed, w_out[layer].astype(BF16), residual=h, name="out_proj")

    n_r = N_EXPERT_GROUPS + N_EXPERTS
    w_router = jnp.pad(jnp.concatenate([router_w_coarse[layer], router_w_fine[layer]], axis=1),
                       ((0, 0), (0, LANES - n_r))).astype(F32)
    b_router = jnp.pad(jnp.concatenate([router_b_coarse[layer], router_b_fine[layer]]), (0, LANES - n_r))
    hn2, route = _ffn_norm_router(h, norm_ffn[layer], w_router, b_router.astype(F32))
    expert = route[:, 2:2 + TOP_K].astype(jnp.int32)
    tm_c = 128
    plan, buf_tok, dest, n_blocks = _dispatch_plan(expert, tm_c)
    act = _moe_up(hn2, w_gate[layer], w_up[layer], plan, buf_tok, n_blocks)
    y_buf = _moe_down(act, w_down[layer], plan, n_blocks)
    out = _combine_final(h, route, dest, y_buf, norm_final, tm=tm_c)
    return out.reshape(bsz, seq, d_model).astype(x.dtype)
```
